```python
import math
import jax
import jax.numpy as jnp
from jax import lax
import numpy as np


D_MODEL = 1024
BATCH = 8
SEQ = 2048
DEPTH = 2

PLE_DIM = 256
D_FF = 2816
EPS = 1e-6
Q_BLOCK = 128

A_HEADS = 8
A_HEAD_DIM = 64
MOBA_BLOCK = 256
MOBA_TOPK = 3
MOBA_Q_CHUNK = 16

B_HEADS = 8
MLA_Q_RANK = 256
MLA_KV_RANK = 128
MLA_NOPE = 64
MLA_ROPE = 32
MLA_V = 64
ROPE_THETA = 10000.0

T5_BUCKETS = 32
T5_MAX_DIST = 128

C_HEADS = 16
C_HEAD_DIM = 64

A_WIDTH = A_HEADS * A_HEAD_DIM
B_WIDTH = B_HEADS * MLA_V
AB_IN = 3 * A_WIDTH + MLA_Q_RANK + MLA_KV_RANK + MLA_ROPE
AB_OUT = A_WIDTH + B_WIDTH
C_WIDTH = C_HEADS * C_HEAD_DIM
C_IN = 3 * C_WIDTH + C_HEADS
N_EVEN = (DEPTH + 1) // 2
N_ODD = DEPTH // 2

kernel_name = 'hybrid_moba_mla_fox_macaron'

F32 = jnp.float32


def rmsnorm(x, g):
    xf = x.astype(F32)
    y = xf * lax.rsqrt(jnp.mean(xf * xf, axis=-1, keepdims=True) + EPS)
    return (y * g.astype(F32)).astype(x.dtype)


def swiglu(h, w_in, w_out):
    a, u = jnp.split(h @ w_in, 2, axis=-1)
    return (jax.nn.silu(a) * u) @ w_out


def split_heads(t, n_heads):
    b, s, _ = t.shape
    return t.reshape(b, s, n_heads, -1).transpose(0, 2, 1, 3)


def merge_heads(t):
    b, h, s, d = t.shape
    return t.transpose(0, 2, 1, 3).reshape(b, s, h * d)


def rope(x, pos):
    half = x.shape[-1] // 2
    inv = ROPE_THETA ** (-jnp.arange(half, dtype=F32) / half)
    ang = pos[:, None] * inv[None, :]
    cos, sin = jnp.cos(ang), jnp.sin(ang)
    xf = x.astype(F32)
    x1, x2 = xf[..., :half], xf[..., half:]
    return jnp.concatenate([x1 * cos - x2 * sin, x1 * sin + x2 * cos], axis=-1).astype(x.dtype)


def t5_bucket(dist):
    dist = jnp.maximum(dist, 0)
    max_exact = T5_BUCKETS // 2
    d = jnp.maximum(dist.astype(F32), 1.0)
    large = max_exact + (jnp.log(d / max_exact) / math.log(T5_MAX_DIST / max_exact)
                         * (T5_BUCKETS - max_exact)).astype(jnp.int32)
    large = jnp.minimum(large, T5_BUCKETS - 1)
    return jnp.where(dist < max_exact, dist, large)


def causal_block_attention(q, k, v, scale, log_decay_cum=None):
    b, h, s, dq = q.shape
    dv = v.shape[-1]
    nq = s // Q_BLOCK
    qb = q.reshape(b, h, nq, Q_BLOCK, dq).transpose(2, 0, 1, 3, 4)
    key_pos = jnp.arange(s)

    def block_fn(xs):
        i, q_i = xs[0], xs[1]
        logits = jnp.einsum('bhqd,bhkd->bhqk', q_i, k).astype(F32) * scale
        if log_decay_cum is not None:
            c_i = xs[2]
            logits = logits + (c_i[..., :, None] - log_decay_cum[..., None, :])
        q_pos = i * Q_BLOCK + jnp.arange(Q_BLOCK)
        mask = key_pos[None, :] <= q_pos[:, None]
        probs = jax.nn.softmax(jnp.where(mask, logits, -jnp.inf), axis=-1)
        return jnp.einsum('bhqk,bhkd->bhqd', probs.astype(v.dtype), v)

    xs = (jnp.arange(nq), qb)
    if log_decay_cum is not None:
        xs = xs + (log_decay_cum.reshape(b, h, nq, Q_BLOCK).transpose(2, 0, 1, 3),)
    out = lax.map(block_fn, xs)
    return out.transpose(1, 2, 0, 3, 4).reshape(b, h, s, dv)


def moba_attention(q, k, v, t5_table):
    b, h, s, dh = q.shape
    nb = -(-s // MOBA_BLOCK)
    pad = nb * MOBA_BLOCK - s
    k_blk = jnp.pad(k, ((0, 0), (0, 0), (0, pad), (0, 0))).reshape(b, h, nb, MOBA_BLOCK, dh)
    v_blk = jnp.pad(v, ((0, 0), (0, 0), (0, pad), (0, 0))).reshape(b, h, nb, MOBA_BLOCK, dh)
    pos = jnp.arange(s)
    own = (pos // MOBA_BLOCK).astype(jnp.int32)
    n_sel = min(MOBA_TOPK, nb - 1)
    own_idx = jnp.broadcast_to(own[:, None], (b, h, s, 1))
    if n_sel > 0:
        k_mean = k_blk.astype(F32).mean(axis=3)
        gate = jnp.einsum('bhsd,bhnd->bhsn', q.astype(F32), k_mean)
        past = jnp.arange(nb)[None, :] < own[:, None]
        gate = jnp.where(past, gate, -jnp.inf)
        _, top_idx = lax.top_k(gate, n_sel)
        sel_idx = jnp.concatenate([top_idx.astype(jnp.int32), own_idx], axis=-1)
        sel_valid = jnp.concatenate([jnp.arange(n_sel)[None, :] < own[:, None],
                                     jnp.ones((s, 1), dtype=bool)], axis=-1)
    else:
        sel_idx = own_idx
        sel_valid = jnp.ones((s, 1), dtype=bool)
    n_slots = n_sel + 1
    nqc = s // MOBA_Q_CHUNK
    qc = q.reshape(b, h, nqc, MOBA_Q_CHUNK, dh).transpose(2, 0, 1, 3, 4)
    idxc = sel_idx.reshape(b, h, nqc, MOBA_Q_CHUNK, n_slots).transpose(2, 0, 1, 3, 4)
    validc = sel_valid.reshape(nqc, MOBA_Q_CHUNK, n_slots)
    table_h = t5_table.T.astype(F32)
    bi = jnp.arange(b)[:, None, None, None]
    hi4 = jnp.arange(h)[None, :, None, None]
    hi5 = jnp.arange(h)[None, :, None, None, None]
    scale = dh ** -0.5

    def chunk_fn(xs):
        ci, q_i, idx_i, valid_i = xs
        k_g = k_blk[bi, hi4, idx_i]
        v_g = v_blk[bi, hi4, idx_i]
        logits = jnp.einsum('bhqd,bhqnkd->bhqnk', q_i, k_g).astype(F32) * scale
        q_pos = ci * MOBA_Q_CHUNK + jnp.arange(MOBA_Q_CHUNK)
        k_pos = idx_i[..., None] * MOBA_BLOCK + jnp.arange(MOBA_BLOCK)
        dist = q_pos[None, None, :, None, None] - k_pos
        bias = table_h[hi5, t5_bucket(dist)]
        mask = (dist >= 0) & valid_i[None, None, :, :, None]
        logits = jnp.where(mask, logits + bias, -jnp.inf)
        probs = jax.nn.softmax(logits.reshape(b, h, MOBA_Q_CHUNK, n_slots * MOBA_BLOCK), axis=-1)
        probs = probs.reshape(b, h, MOBA_Q_CHUNK, n_slots, MOBA_BLOCK)
        return jnp.einsum('bhqnk,bhqnkd->bhqd', probs.astype(v.dtype), v_g)

    out = lax.map(chunk_fn, (jnp.arange(nqc), qc, idxc, validc))
    return out.transpose(1, 2, 0, 3, 4).reshape(b, h, s, dh)


def mla_attention(cq, ckv, kr, q_norm, w_uq, kv_norm, w_ukv):
    b, s, _ = cq.shape
    pos = jnp.arange(s, dtype=F32)
    q = split_heads(rmsnorm(cq, q_norm) @ w_uq, B_HEADS)
    kv = split_heads(rmsnorm(ckv, kv_norm) @ w_ukv, B_HEADS)
    q_full = jnp.concatenate([q[..., :MLA_NOPE], rope(q[..., MLA_NOPE:], pos)], axis=-1)
    k_rope = jnp.broadcast_to(rope(kr, pos)[:, None], (b, B_HEADS, s, MLA_ROPE))
    k_full = jnp.concatenate([kv[..., :MLA_NOPE], k_rope], axis=-1)
    v = kv[..., MLA_NOPE:]
    return causal_block_attention(q_full, k_full, v, (MLA_NOPE + MLA_ROPE) ** -0.5)


def moba_mla_mixer(h, w_in, t5_table, q_norm, w_uq, kv_norm, w_ukv, w_out):
    u = h @ w_in
    cuts = np.cumsum([A_WIDTH, A_WIDTH, A_WIDTH, MLA_Q_RANK, MLA_KV_RANK]).tolist()
    qa, ka, va, cq, ckv, kr = jnp.split(u, cuts, axis=-1)
    oa = moba_attention(split_heads(qa, A_HEADS), split_heads(ka, A_HEADS),
                        split_heads(va, A_HEADS), t5_table)
    ob = mla_attention(cq, ckv, kr, q_norm, w_uq, kv_norm, w_ukv)
    return jnp.concatenate([merge_heads(oa), merge_heads(ob)], axis=-1) @ w_out


def fox_mixer(h, w_in, b_f, w_out):
    u = h @ w_in
    q, k, v, f = jnp.split(u, [C_WIDTH, 2 * C_WIDTH, 3 * C_WIDTH], axis=-1)
    log_f = jax.nn.log_sigmoid((f + b_f).astype(F32))
    cum = jnp.cumsum(log_f, axis=1).transpose(0, 2, 1)
    o = causal_block_attention(split_heads(q, C_HEADS), split_heads(k, C_HEADS),
                               split_heads(v, C_HEADS), C_HEAD_DIM ** -0.5, cum)
    return merge_heads(o) @ w_out


def setup_inputs(seed: int = 0) -> dict:
    key = jax.random.key(seed)
    ks = iter(jax.random.split(key, 32))

    def w(shape, fan_in):
        return jax.random.normal(next(ks), shape, F32) * fan_in ** -0.5

    def gain(shape):
        return 1.0 + 0.02 * jax.random.normal(next(ks), shape, F32)

    return {
        'x': jax.random.normal(next(ks), (BATCH, SEQ, D_MODEL), F32),
        'p': jax.random.normal(next(ks), (DEPTH, BATCH, SEQ, PLE_DIM), F32),
        't5_bias': 0.5 * jax.random.normal(next(ks), (T5_BUCKETS, A_HEADS), F32),
        'ff1_norm': gain((DEPTH, D_MODEL)),
        'ff1_w_in': w((DEPTH, D_MODEL, 2 * D_FF), D_MODEL),
        'ff1_w_out': w((DEPTH, D_FF, D_MODEL), D_FF),
        'mix_norm': gain((DEPTH, D_MODEL)),
        'ff2_norm': gain((DEPTH, D_MODEL)),
        'ff2_w_in': w((DEPTH, D_MODEL, 2 * D_FF), D_MODEL),
        'ff2_w_out': w((DEPTH, D_FF, D_MODEL), D_FF),
        'ple_norm': gain((DEPTH, D_MODEL)),
        'ple_w_gate': w((DEPTH, D_MODEL, D_MODEL), D_MODEL),
        'ple_w_proj': w((DEPTH, PLE_DIM, D_MODEL), PLE_DIM),
        'ab_w_in': w((N_EVEN, D_MODEL, AB_IN), D_MODEL),
        'mla_q_norm': gain((N_EVEN, MLA_Q_RANK)),
        'mla_w_uq': w((N_EVEN, MLA_Q_RANK, B_HEADS * (MLA_NOPE + MLA_ROPE)), MLA_Q_RANK),
        'mla_kv_norm': gain((N_EVEN, MLA_KV_RANK)),
        'mla_w_ukv': w((N_EVEN, MLA_KV_RANK, B_HEADS * (MLA_NOPE + MLA_V)), MLA_KV_RANK),
        'ab_w_out': w((N_EVEN, AB_OUT, D_MODEL), AB_OUT),
        'fox_w_in': w((N_ODD, D_MODEL, C_IN), D_MODEL),
        'fox_b_f': 3.0 + 0.5 * jax.random.normal(next(ks), (N_ODD, C_HEADS), F32),
        'fox_w_out': w((N_ODD, C_WIDTH, D_MODEL), C_WIDTH),
        'final_norm': gain((D_MODEL,)),
    }


def reference(x, p, t5_bias, ff1_norm, ff1_w_in, ff1_w_out, mix_norm, ff2_norm, ff2_w_in,
              ff2_w_out, ple_norm, ple_w_gate, ple_w_proj, ab_w_in, mla_q_norm, mla_w_uq,
              mla_kv_norm, mla_w_ukv, ab_w_out, fox_w_in, fox_b_f, fox_w_out, final_norm):
    for i in range(DEPTH):
        x = x + 0.5 * swiglu(rmsnorm(x, ff1_norm[i]), ff1_w_in[i], ff1_w_out[i])
        h = rmsnorm(x, mix_norm[i])
        j = i // 2
        if i % 2 == 0:
            x = x + moba_mla_mixer(h, ab_w_in[j], t5_bias, mla_q_norm[j], mla_w_uq[j],
                                   mla_kv_norm[j], mla_w_ukv[j], ab_w_out[j])
        else:
            x = x + fox_mixer(h, fox_w_in[j], fox_b_f[j], fox_w_out[j])
        x = x + 0.5 * swiglu(rmsnorm(x, ff2_norm[i]), ff2_w_in[i], ff2_w_out[i])
        g = jax.nn.sigmoid(rmsnorm(x, ple_norm[i]) @ ple_w_gate[i])
        x = x + g * (p[i] @ ple_w_proj[i])
    return rmsnorm(x, final_norm)
```

```python
import functools
import math

import jax
import jax.numpy as jnp
import numpy as np
from jax import lax
from jax.experimental import pallas as pl
from jax.experimental.pallas import tpu as pltpu

F32 = jnp.float32
BF16 = jnp.bfloat16

D_MODEL = 1024
BATCH = 8
SEQ = 2048
DEPTH = 2
PLE_DIM = 256
D_FF = 2816
EPS = 1e-6

A_HEADS = 8
A_HEAD_DIM = 64
MOBA_BLOCK = 256
MOBA_TOPK = 3

B_HEADS = 8
MLA_Q_RANK = 256
MLA_KV_RANK = 128
MLA_NOPE = 64
MLA_ROPE = 32
MLA_V = 64
ROPE_THETA = 10000.0

T5_BUCKETS = 32
T5_MAX_DIST = 128

C_HEADS = 16
C_HEAD_DIM = 64

A_WIDTH = A_HEADS * A_HEAD_DIM
C_WIDTH = C_HEADS * C_HEAD_DIM

TOKENS = BATCH * SEQ
LANES = 128
Q_TILE = MOBA_BLOCK
N_KBLK = SEQ // Q_TILE
TOKEN_TILE = 256
FF_CHUNK = D_FF // 2
VMEM_LIMIT = 56 * 1024 * 1024
NEG_INF = float("-inf")


def _wspec(shape):
    nd = len(shape)
    return pl.BlockSpec(shape, lambda *_: (0,) * nd, pipeline_mode=pl.Buffered(1))


def _dot(a, b):
    return jnp.dot(a, b, preferred_element_type=F32)


def _dot_nt(a, b):
    return lax.dot_general(a, b, (((1,), (1,)), ((), ())), preferred_element_type=F32)


def _rms(x, g):
    return x * lax.rsqrt(jnp.mean(x * x, axis=-1, keepdims=True) + EPS) * g


def _ffn(x, g, wa_ref, wu_ref, wo_ref):
    h = _rms(x, g).astype(BF16)
    y = jnp.zeros_like(x)
    for c in range(D_FF // FF_CHUNK):
        sl = slice(c * FF_CHUNK, (c + 1) * FF_CHUNK)
        a = _dot(h, wa_ref[:, sl])
        u = _dot(h, wu_ref[:, sl])
        act = (a * jax.nn.sigmoid(a) * u).astype(BF16)
        y = y + _dot(act, wo_ref[sl, :])
    return x + 0.5 * y


def _rope_tables(inv_lane, tile_idx):
    pos0 = (tile_idx * TOKEN_TILE) % SEQ
    pos = (pos0 + lax.broadcasted_iota(jnp.int32, (TOKEN_TILE, LANES), 0)).astype(F32)
    lane = lax.broadcasted_iota(jnp.int32, (TOKEN_TILE, LANES), 1)
    ang = pos * inv_lane
    is_x1 = (lane >= MLA_NOPE) & (lane < MLA_NOPE + MLA_ROPE // 2)
    is_x2 = (lane >= MLA_NOPE + MLA_ROPE // 2) & (lane < MLA_NOPE + MLA_ROPE)
    cos_t = jnp.where(is_x1 | is_x2, jnp.cos(ang), 1.0)
    sin = jnp.sin(ang)
    sin_t = jnp.where(is_x1, -sin, jnp.where(is_x2, sin, 0.0))
    return cos_t, sin_t, is_x1


def _rope_block(xb, cos_t, sin_t, is_x1):
    half = MLA_ROPE // 2
    partner = jnp.where(is_x1, pltpu.roll(xb, LANES - half, 1), pltpu.roll(xb, half, 1))
    return xb * cos_t + partner * sin_t


def _pre0_kernel(x_ref, g1_ref, wa_ref, wu_ref, wo_ref, gmix_ref, wqkv_ref, wc_ref,
                 qn_ref, wuq_ref, kvn_ref, wukvk_ref, wukvv_ref, inv_ref,
                 x1_ref, qkv_ref, qf_ref, km_ref, qm_ref, kmla_ref, vm_ref):
    x1 = _ffn(x_ref[...], g1_ref[...], wa_ref, wu_ref, wo_ref)
    x1_ref[...] = x1
    h = _rms(x1, gmix_ref[...]).astype(BF16)
    qkv = _dot(h, wqkv_ref[...])
    qkv_ref[...] = qkv.astype(BF16)
    qf_ref[...] = qkv[:, :A_WIDTH]
    km_ref[0] = jnp.mean(qkv[:, A_WIDTH:2 * A_WIDTH], axis=0, keepdims=True)

    c = _dot(h, wc_ref[...])
    cq = c[:, :MLA_Q_RANK]
    ckv = c[:, MLA_Q_RANK:MLA_Q_RANK + MLA_KV_RANK]
    kr = c[:, MLA_Q_RANK + MLA_KV_RANK:]
    cqn = _rms(cq, qn_ref[...]).astype(BF16)
    ckvn = _rms(ckv, kvn_ref[...]).astype(BF16)
    qm = _dot(cqn, wuq_ref[...])
    kn = _dot(ckvn, wukvk_ref[...])
    vm_ref[...] = _dot(ckvn, wukvv_ref[...]).astype(BF16)

    cos_t, sin_t, is_x1 = _rope_tables(inv_ref[...], pl.program_id(0))
    krr = _rope_block(kr, cos_t, sin_t, is_x1)
    for hb in range(B_HEADS):
        sl = slice(hb * LANES, (hb + 1) * LANES)
        qm_ref[:, sl] = _rope_block(qm[:, sl], cos_t, sin_t, is_x1).astype(BF16)
        kmla_ref[:, sl] = (kn[:, sl] + krr).astype(BF16)


def _pre1_kernel(x_ref, g1_ref, wa_ref, wu_ref, wo_ref, gmix_ref, wqkv_ref, wf_ref,
                 x1_ref, qkv_ref, f_ref):
    x1 = _ffn(x_ref[...], g1_ref[...], wa_ref, wu_ref, wo_ref)
    x1_ref[...] = x1
    h = _rms(x1, gmix_ref[...]).astype(BF16)
    qkv_ref[...] = _dot(h, wqkv_ref[...]).astype(BF16)
    f_ref[...] = _dot(h, wf_ref[...])


def _tok_spec(width):
    return pl.BlockSpec((TOKEN_TILE, width), lambda i: (i, 0))


def _token_call(body, ins, in_specs, outs, out_specs):
    return pl.pallas_call(
        body,
        grid=(TOKENS // TOKEN_TILE,),
        in_specs=in_specs,
        out_specs=out_specs,
        out_shape=outs,
        compiler_params=pltpu.CompilerParams(
            dimension_semantics=("arbitrary",), vmem_limit_bytes=VMEM_LIMIT),
    )(*ins)


def _post_kernel(*refs, n_mix, final):
    x_ref = refs[0]
    o_refs = refs[1:1 + n_mix]
    w_refs = refs[1 + n_mix:1 + 2 * n_mix]
    (g2_ref, wa_ref, wu_ref, wo_ref, gple_ref, wg_ref, p_ref, wp_ref, gfin_ref,
     out_ref) = refs[1 + 2 * n_mix:]
    x = x_ref[...]
    for o_ref, w_ref in zip(o_refs, w_refs):
        x = x + _dot(o_ref[...], w_ref[...])
    x = _ffn(x, g2_ref[...], wa_ref, wu_ref, wo_ref)
    gate = jax.nn.sigmoid(_dot(_rms(x, gple_ref[...]).astype(BF16), wg_ref[...]))
    x = x + gate * _dot(p_ref[...].astype(BF16), wp_ref[...])
    if final:
        x = _rms(x, gfin_ref[...])
    out_ref[...] = x


def _split_bf16(x):
    hi = x.astype(BF16)
    lo = (x - hi.astype(F32)).astype(BF16)
    return hi, lo


def _router_kernel(q_ref, km_ref, pen_ref):
    own = pl.program_id(1)
    rows = A_HEADS * N_KBLK
    km = km_ref[0]
    gt = jnp.concatenate([km] * A_HEADS, axis=0)
    r = lax.broadcasted_iota(jnp.int32, (rows, A_WIDTH), 0)
    c = lax.broadcasted_iota(jnp.int32, (rows, A_WIDTH), 1)
    gt = jnp.where((r // N_KBLK) == (c // A_HEAD_DIM), gt, 0.0)
    g_hi, g_lo = _split_bf16(gt)
    q_hi, q_lo = _split_bf16(q_ref[...])
    gate = _dot_nt(g_hi, q_hi) + _dot_nt(g_hi, q_lo) + _dot_nt(g_lo, q_hi)

    n_idx = lax.broadcasted_iota(jnp.int32, (N_KBLK, Q_TILE), 0)
    pen_rows = []
    for h in range(A_HEADS):
        gh = gate[h * N_KBLK:(h + 1) * N_KBLK]
        rank = jnp.zeros((N_KBLK, Q_TILE), jnp.int32)
        for m in range(N_KBLK):
            gm = gh[m:m + 1]
            beats = (gm > gh) | ((gm == gh) & (m < n_idx))
            rank = rank + jnp.where(beats & (m < own), 1, 0)
        sel = (n_idx < own) & (rank < MOBA_TOPK)
        pen_n = jnp.where(sel, 0.0, NEG_INF)
        pen_d = jnp.full((N_KBLK, Q_TILE), NEG_INF, F32)
        for n in range(N_KBLK):
            pen_d = jnp.where(n_idx == own - n, pen_n[n:n + 1], pen_d)
        pen_rows.append(pen_d)
    pad = jnp.zeros((LANES - 2 * N_KBLK, Q_TILE), F32)
    for hp in range(A_HEADS // 2):
        blk = jnp.concatenate([pen_rows[2 * hp], pen_rows[2 * hp + 1], pad], axis=0)
        pen_ref[:, hp * LANES:(hp + 1) * LANES] = blk.T.astype(BF16)


def _t5_bias_kernel(tbl_ref, o_ref):
    h = pl.program_id(0)
    r = lax.broadcasted_iota(jnp.int32, (Q_TILE, Q_TILE), 0)
    c = lax.broadcasted_iota(jnp.int32, (Q_TILE, Q_TILE), 1)
    max_exact = T5_BUCKETS // 2
    for dd in range(3):
        dist = dd * Q_TILE + r - c
        dc = jnp.maximum(dist, 0)
        df = jnp.maximum(dc.astype(F32), 1.0)
        large = max_exact + (jnp.log(df / max_exact) / math.log(T5_MAX_DIST / max_exact)
                             * (T5_BUCKETS - max_exact)).astype(jnp.int32)
        large = jnp.minimum(large, T5_BUCKETS - 1)
        bucket = jnp.where(dc < max_exact, dc, large)
        bias = jnp.zeros((Q_TILE, Q_TILE), F32)
        for b in range(T5_BUCKETS):
            bias = jnp.where(bucket == b, tbl_ref[b, h], bias)
        if dd == 0:
            bias = jnp.where(dist >= 0, bias, NEG_INF)
        o_ref[0, dd] = bias


def _fox_gate_kernel(f_ref, b_ref, o_ref, carry_ref):
    i = pl.program_id(1)

    @pl.when(i == 0)
    def _():
        carry_ref[...] = jnp.zeros_like(carry_ref)

    lane = lax.broadcasted_iota(jnp.int32, (Q_TILE, LANES), 1)
    z = f_ref[...] + b_ref[...]
    logf = jnp.minimum(z, 0.0) - jnp.log1p(jnp.exp(-jnp.abs(z)))
    logf = jnp.where(lane < C_HEADS, logf, 0.0)
    r = lax.broadcasted_iota(jnp.int32, (Q_TILE, Q_TILE), 0)
    c = lax.broadcasted_iota(jnp.int32, (Q_TILE, Q_TILE), 1)
    tri = jnp.where(c <= r, 1.0, 0.0).astype(BF16)
    l1 = logf.astype(BF16)
    rem = logf - l1.astype(F32)
    l2 = rem.astype(BF16)
    l3 = (rem - l2.astype(F32)).astype(BF16)
    cum = _dot(tri, l1) + _dot(tri, l2) + _dot(tri, l3) + carry_ref[...]
    carry_ref[...] = cum[Q_TILE - 1:Q_TILE, :]

    c1 = cum.astype(BF16)
    rem = cum - c1.astype(F32)
    c2 = rem.astype(BF16)
    c3 = (rem - c2.astype(F32)).astype(BF16)
    rin = lax.broadcasted_iota(jnp.int32, (LANES, LANES), 0)
    lout = lax.broadcasted_iota(jnp.int32, (LANES, LANES), 1)
    out = jnp.zeros((Q_TILE, LANES), F32)
    for part, cp in enumerate((c1, c2, c3)):
        place = jnp.where((rin < C_HEADS) & (lout == rin * 8 + part), 1.0, 0.0).astype(BF16)
        out = out + _dot(cp, place)
    o_ref[...] = out.astype(BF16)


def _attn_kernel(*refs, kind):
    if kind == "moba":
        q_ref, k_ref, v_ref, pen_ref, bias_ref, o_ref, m_s, l_s, acc_s = refs
    elif kind == "mla":
        q_ref, k_ref, v_ref, o_ref, m_s, l_s, acc_s = refs
    else:
        q_ref, k_ref, v_ref, cq_ref, ck_ref, o_ref, m_s, l_s, acc_s, kp_s = refs
    hp = pl.program_id(1)
    i = pl.program_id(2)
    lane = lax.broadcasted_iota(jnp.int32, (1, LANES), 1)
    low = lane < A_HEAD_DIM
    row = lax.broadcasted_iota(jnp.int32, (Q_TILE, Q_TILE), 0)
    col = lax.broadcasted_iota(jnp.int32, (Q_TILE, Q_TILE), 1)

    if kind == "fox":
        rin = lax.broadcasted_iota(jnp.int32, (LANES, LANES), 0)
        lout = lax.broadcasted_iota(jnp.int32, (LANES, LANES), 1)
        base_a = 2 * hp * 8
        base_b = base_a + 8

        def place(off_a, off_b, val):
            sel_a = (lout >= off_a) & (lout < off_a + 3) & (rin == base_a + lout - off_a)
            sel_b = (lout >= off_b) & (lout < off_b + 3) & (rin == base_b + lout - off_b)
            return jnp.where(sel_a | sel_b, val, 0.0).astype(BF16)

        def ones(off_a, off_b):
            in_a = (lane >= off_a) & (lane < off_a + 3)
            in_b = (lane >= off_b) & (lane < off_b + 3)
            return jnp.where(in_a | in_b, 1.0, 0.0)

        @pl.when(i == 0)
        def _():
            ak = (_dot(ck_ref[...], place(A_HEAD_DIM + 3, 3, -1.0))
                  + ones(A_HEAD_DIM, 0)).astype(BF16)
            k = k_ref[...]
            kp_s[0] = jnp.where(low, k, ak)
            kp_s[1] = jnp.where(low, ak, k)

        aq = (_dot(cq_ref[...], place(A_HEAD_DIM, 0, 1.0))
              + ones(A_HEAD_DIM + 3, 3)).astype(BF16)
        qs = q_ref[...] * 0.125
        q_heads = (jnp.where(low, qs, aq), jnp.where(low, aq, qs))
    elif kind == "moba":
        qs = q_ref[...] * 0.125
        zero = jnp.zeros_like(qs)
        q_heads = (jnp.where(low, qs, zero), jnp.where(low, zero, qs))
        pen = pen_ref[...].astype(F32)
    else:
        q_heads = (q_ref[:, :LANES], q_ref[:, LANES:])
    mla_scale = (MLA_NOPE + MLA_ROPE) ** -0.5

    def logits(hh, d):
        start = pl.multiple_of((i - d) * Q_TILE, Q_TILE)
        if kind == "fox":
            kb = kp_s[hh, pl.ds(start, Q_TILE), :]
        elif kind == "mla":
            kb = k_ref[pl.ds(start, Q_TILE), hh * LANES:(hh + 1) * LANES]
        else:
            kb = k_ref[pl.ds(start, Q_TILE), :]
        s = _dot_nt(q_heads[hh], kb)
        if kind == "mla":
            s = s * mla_scale
        if kind == "moba":
            s = s + bias_ref[hh, min(d, 2)]
            if d > 0:
                s = s + pen[:, hh * N_KBLK + d:hh * N_KBLK + d + 1]
        elif d == 0:
            s = jnp.where(col <= row, s, NEG_INF)
        return s, v_ref[pl.ds(start, Q_TILE), :]

    for hh in range(2):
        s, vb = logits(hh, 0)
        m = jnp.max(s, axis=1, keepdims=True)
        p = jnp.exp(s - m)
        m_s[hh] = m
        l_s[hh] = jnp.sum(p, axis=1, keepdims=True)
        acc_s[hh] = _dot(p.astype(BF16), vb)

        for d in range(1, N_KBLK):
            @pl.when(d <= i)
            def _(hh=hh, d=d):
                s, vb = logits(hh, d)
                m_old = m_s[hh]
                m_new = jnp.maximum(m_old, jnp.max(s, axis=1, keepdims=True))
                alpha = jnp.exp(m_old - m_new)
                p = jnp.exp(s - m_new)
                l_s[hh] = alpha * l_s[hh] + jnp.sum(p, axis=1, keepdims=True)
                acc_s[hh] = alpha * acc_s[hh] + _dot(p.astype(BF16), vb)
                m_s[hh] = m_new

    o0 = acc_s[0] / l_s[0]
    o1 = acc_s[1] / l_s[1]
    o_ref[...] = jnp.where(low, o0, o1).astype(BF16)


def _attention(kind, ins, in_specs, n_pairs, extra_scratch=()):
    scratch = [pltpu.VMEM((2, Q_TILE, 1), F32), pltpu.VMEM((2, Q_TILE, 1), F32),
               pltpu.VMEM((2, Q_TILE, LANES), F32)] + list(extra_scratch)
    return pl.pallas_call(
        functools.partial(_attn_kernel, kind=kind),
        grid=(BATCH, n_pairs, N_KBLK),
        in_specs=in_specs,
        out_specs=pl.BlockSpec((Q_TILE, LANES), lambda b, h, i: (b * N_KBLK + i, h)),
        out_shape=jax.ShapeDtypeStruct((TOKENS, n_pairs * LANES), BF16),
        scratch_shapes=scratch,
        compiler_params=pltpu.CompilerParams(
            dimension_semantics=("arbitrary", "arbitrary", "arbitrary"),
            vmem_limit_bytes=VMEM_LIMIT),
    )(*ins)


def _q_spec(width, col0):
    return pl.BlockSpec((Q_TILE, width), lambda b, h, i: (b * N_KBLK + i, col0 + h))


def _kv_spec(width, col0):
    return pl.BlockSpec((SEQ, width), lambda b, h, i: (b, col0 + h))


def _place_heads(w, n_heads, src_stride, src_off, width):
    out = jnp.zeros((w.shape[0], n_heads * LANES), w.dtype)
    for h in range(n_heads):
        src = w[:, h * src_stride + src_off:h * src_stride + src_off + width]
        out = out.at[:, h * LANES:h * LANES + width].set(src)
    return out


def _row(v, width=None):
    v = v.reshape(1, -1).astype(F32)
    if width is not None and v.shape[1] < width:
        v = jnp.pad(v, ((0, 0), (0, width - v.shape[1])))
    return v


def kernel(x, p, t5_bias, ff1_norm, ff1_w_in, ff1_w_out, mix_norm, ff2_norm, ff2_w_in, ff2_w_out,
           ple_norm, ple_w_gate, ple_w_proj, ab_w_in, mla_q_norm, mla_w_uq, mla_kv_norm, mla_w_ukv,
           ab_w_out, fox_w_in, fox_b_f, fox_w_out, final_norm):
    nt = TOKENS // TOKEN_TILE
    xt = x.reshape(TOKENS, D_MODEL)

    def ffn_args(norm, w_in, w_out):
        w_in = w_in.astype(BF16)
        args = [_row(norm), w_in[:, :D_FF], w_in[:, D_FF:], w_out.astype(BF16)]
        specs = [_wspec((1, D_MODEL)), _wspec((D_MODEL, D_FF)), _wspec((D_MODEL, D_FF)),
                 _wspec((D_FF, D_MODEL))]
        return args, specs

    def post(xin, mixes, w_outs, layer, final):
        fa, fs = ffn_args(ff2_norm[layer], ff2_w_in[layer], ff2_w_out[layer])
        ws = [w.astype(BF16) for w in w_outs]
        ins = ([xin] + list(mixes) + ws + fa
               + [_row(ple_norm[layer]), ple_w_gate[layer].astype(BF16),
                  p[layer].reshape(TOKENS, PLE_DIM), ple_w_proj[layer].astype(BF16),
                  _row(final_norm)])
        specs = ([_tok_spec(D_MODEL)] + [_tok_spec(m.shape[1]) for m in mixes]
                 + [_wspec(w.shape) for w in ws] + fs
                 + [_wspec((1, D_MODEL)), _wspec((D_MODEL, D_MODEL)), _tok_spec(PLE_DIM),
                    _wspec((PLE_DIM, D_MODEL)), _wspec((1, D_MODEL))])
        return _token_call(
            functools.partial(_post_kernel, n_mix=len(mixes), final=final), ins, specs,
            jax.ShapeDtypeStruct((TOKENS, D_MODEL), F32), _tok_spec(D_MODEL))

    w_ab = ab_w_in[0]
    w_qkv = w_ab[:, :3 * A_WIDTH].astype(BF16)
    c0 = 3 * A_WIDTH
    w_c = jnp.zeros((D_MODEL, 4 * LANES), F32)
    w_c = w_c.at[:, :MLA_Q_RANK + MLA_KV_RANK].set(w_ab[:, c0:c0 + MLA_Q_RANK + MLA_KV_RANK])
    kr0 = MLA_Q_RANK + MLA_KV_RANK
    w_c = w_c.at[:, kr0 + MLA_NOPE:kr0 + MLA_NOPE + MLA_ROPE].set(w_ab[:, c0 + kr0:])
    w_c = w_c.astype(BF16)
    w_uq = _place_heads(mla_w_uq[0], B_HEADS, MLA_NOPE + MLA_ROPE, 0, MLA_NOPE + MLA_ROPE).astype(BF16)
    w_ukv_k = _place_heads(mla_w_ukv[0], B_HEADS, MLA_NOPE + MLA_V, 0, MLA_NOPE).astype(BF16)
    w_ukv = mla_w_ukv[0].reshape(MLA_KV_RANK, B_HEADS, MLA_NOPE + MLA_V)
    w_ukv_v = w_ukv[:, :, MLA_NOPE:].reshape(MLA_KV_RANK, B_HEADS * MLA_V).astype(BF16)
    half = MLA_ROPE // 2
    inv = ROPE_THETA ** (-np.arange(half, dtype=np.float64) / half)
    inv_lane = np.zeros((1, LANES), np.float32)
    inv_lane[0, MLA_NOPE:MLA_NOPE + half] = inv
    inv_lane[0, MLA_NOPE + half:MLA_NOPE + MLA_ROPE] = inv
    inv_lane = jnp.asarray(inv_lane)

    fa, fs = ffn_args(ff1_norm[0], ff1_w_in[0], ff1_w_out[0])
    ins = ([xt] + fa + [_row(mix_norm[0]), w_qkv, w_c, _row(mla_q_norm[0]), w_uq,
                        _row(mla_kv_norm[0]), w_ukv_k, w_ukv_v, inv_lane])
    specs = ([_tok_spec(D_MODEL)] + fs
             + [_wspec((1, D_MODEL)), _wspec(w_qkv.shape), _wspec(w_c.shape),
                _wspec((1, MLA_Q_RANK)), _wspec(w_uq.shape), _wspec((1, MLA_KV_RANK)),
                _wspec(w_ukv_k.shape), _wspec(w_ukv_v.shape), _wspec((1, LANES))])
    outs = (jax.ShapeDtypeStruct((TOKENS, D_MODEL), F32),
            jax.ShapeDtypeStruct((TOKENS, 3 * A_WIDTH), BF16),
            jax.ShapeDtypeStruct((TOKENS, A_WIDTH), F32),
            jax.ShapeDtypeStruct((nt, 1, A_WIDTH), F32),
            jax.ShapeDtypeStruct((TOKENS, B_HEADS * LANES), BF16),
            jax.ShapeDtypeStruct((TOKENS, B_HEADS * LANES), BF16),
            jax.ShapeDtypeStruct((TOKENS, B_HEADS * MLA_V), BF16))
    out_specs = (_tok_spec(D_MODEL), _tok_spec(3 * A_WIDTH), _tok_spec(A_WIDTH),
                 pl.BlockSpec((1, 1, A_WIDTH), lambda i: (i, 0, 0)),
                 _tok_spec(B_HEADS * LANES), _tok_spec(B_HEADS * LANES), _tok_spec(B_HEADS * MLA_V))
    x1, qkv_a, q_f32, k_mean, q_mla, k_mla, v_mla = _token_call(_pre0_kernel, ins, specs, outs, out_specs)

    pen = pl.pallas_call(
        _router_kernel,
        grid=(BATCH, N_KBLK),
        in_specs=[pl.BlockSpec((Q_TILE, A_WIDTH), lambda b, i: (b * N_KBLK + i, 0)),
                  pl.BlockSpec((1, N_KBLK, A_WIDTH), lambda b, i: (b, 0, 0))],
        out_specs=pl.BlockSpec((Q_TILE, A_HEADS // 2 * LANES), lambda b, i: (b * N_KBLK + i, 0)),
        out_shape=jax.ShapeDtypeStruct((TOKENS, A_HEADS // 2 * LANES), BF16),
    )(q_f32, k_mean.reshape(BATCH, N_KBLK, A_WIDTH))

    bias = pl.pallas_call(
        _t5_bias_kernel,
        grid=(A_HEADS,),
        in_specs=[pl.BlockSpec(memory_space=pltpu.SMEM)],
        out_specs=pl.BlockSpec((1, 3, Q_TILE, Q_TILE), lambda h: (h, 0, 0, 0)),
        out_shape=jax.ShapeDtypeStruct((A_HEADS, 3, Q_TILE, Q_TILE), F32),
    )(t5_bias.astype(F32))

    na = A_HEADS // 2
    o_a = _attention(
        "moba", [qkv_a, qkv_a, qkv_a, pen, bias],
        [_q_spec(LANES, 0), _kv_spec(LANES, na), _kv_spec(LANES, 2 * na), _q_spec(LANES, 0),
         pl.BlockSpec((2, 3, Q_TILE, Q_TILE), lambda b, h, i: (h, 0, 0, 0))], na)
    o_b = _attention(
        "mla", [q_mla, k_mla, v_mla],
        [_q_spec(2 * LANES, 0), _kv_spec(2 * LANES, 0), _kv_spec(LANES, 0)], B_HEADS // 2)
    w_o = ab_w_out[0]
    xt = post(x1, [o_a, o_b], [w_o[:A_WIDTH], w_o[A_WIDTH:]], 0, DEPTH == 1)

    w_fox = fox_w_in[0]
    w_qkv = w_fox[:, :3 * C_WIDTH].astype(BF16)
    w_f = jnp.pad(w_fox[:, 3 * C_WIDTH:], ((0, 0), (0, LANES - C_HEADS))).astype(BF16)
    fa, fs = ffn_args(ff1_norm[1], ff1_w_in[1], ff1_w_out[1])
    ins = [xt] + fa + [_row(mix_norm[1]), w_qkv, w_f]
    specs = ([_tok_spec(D_MODEL)] + fs
             + [_wspec((1, D_MODEL)), _wspec(w_qkv.shape), _wspec(w_f.shape)])
    outs = (jax.ShapeDtypeStruct((TOKENS, D_MODEL), F32),
            jax.ShapeDtypeStruct((TOKENS, 3 * C_WIDTH), BF16),
            jax.ShapeDtypeStruct((TOKENS, LANES), F32))
    out_specs = (_tok_spec(D_MODEL), _tok_spec(3 * C_WIDTH), _tok_spec(LANES))
    x1, qkv_c, f_gate = _token_call(_pre1_kernel, ins, specs, outs, out_specs)

    caug = pl.pallas_call(
        _fox_gate_kernel,
        grid=(BATCH, N_KBLK),
        in_specs=[pl.BlockSpec((Q_TILE, LANES), lambda b, i: (b * N_KBLK + i, 0)),
                  pl.BlockSpec((1, LANES), lambda b, i: (0, 0))],
        out_specs=pl.BlockSpec((Q_TILE, LANES), lambda b, i: (b * N_KBLK + i, 0)),
        out_shape=jax.ShapeDtypeStruct((TOKENS, LANES), BF16),
        scratch_shapes=[pltpu.VMEM((1, LANES), F32)],
        compiler_params=pltpu.CompilerParams(dimension_semantics=("arbitrary", "arbitrary")),
    )(f_gate, _row(fox_b_f[0], LANES))

    nc = C_HEADS // 2
    o_c = _attention(
        "fox", [qkv_c, qkv_c, qkv_c, caug, caug],
        [_q_spec(LANES, 0), _kv_spec(LANES, nc), _kv_spec(LANES, 2 * nc),
         pl.BlockSpec((Q_TILE, LANES), lambda b, h, i: (b * N_KBLK + i, 0)),
         pl.BlockSpec((SEQ, LANES), lambda b, h, i: (b, 0))],
        nc, extra_scratch=[pltpu.VMEM((2, SEQ, LANES), BF16)])
    xt = post(x1, [o_c], [fox_w_out[0]], 1, True)
    return xt.reshape(BATCH, SEQ, D_MODEL)
```

```python
import functools
import math

import jax
import jax.numpy as jnp
import numpy as np
from jax import lax
from jax.experimental import pallas as pl
from jax.experimental.pallas import tpu as pltpu

F32 = jnp.float32
BF16 = jnp.bfloat16

D_MODEL = 1024
BATCH = 8
SEQ = 2048
DEPTH = 2
PLE_DIM = 256
D_FF = 2816
EPS = 1e-6

A_HEADS = 8
A_HEAD_DIM = 64
MOBA_BLOCK = 256
MOBA_TOPK = 3

B_HEADS = 8
MLA_Q_RANK = 256
MLA_KV_RANK = 128
MLA_NOPE = 64
MLA_ROPE = 32
MLA_V = 64
ROPE_THETA = 10000.0

T5_BUCKETS = 32
T5_MAX_DIST = 128

C_HEADS = 16
C_HEAD_DIM = 64

A_WIDTH = A_HEADS * A_HEAD_DIM
C_WIDTH = C_HEADS * C_HEAD_DIM

TOKENS = BATCH * SEQ
LANES = 128
Q_TILE = MOBA_BLOCK
N_KBLK = SEQ // Q_TILE
TOKEN_TILE = 256
FF_CHUNK = D_FF // 2
VMEM_LIMIT = 56 * 1024 * 1024
NEG_INF = float("-inf")


def _wspec(shape):
    nd = len(shape)
    return pl.BlockSpec(shape, lambda *_: (0,) * nd, pipeline_mode=pl.Buffered(1))


def _dot(a, b):
    return jnp.dot(a, b, preferred_element_type=F32)


def _dot_nt(a, b):
    return lax.dot_general(a, b, (((1,), (1,)), ((), ())), preferred_element_type=F32)


def _rms(x, g):
    return x * lax.rsqrt(jnp.mean(x * x, axis=-1, keepdims=True) + EPS) * g


def _ffn(x, g, wa_ref, wu_ref, wo_ref):
    h = _rms(x, g).astype(BF16)
    y = jnp.zeros_like(x)
    for c in range(D_FF // FF_CHUNK):
        sl = slice(c * FF_CHUNK, (c + 1) * FF_CHUNK)
        a = _dot(h, wa_ref[:, sl])
        u = _dot(h, wu_ref[:, sl])
        act = (a * jax.nn.sigmoid(a) * u).astype(BF16)
        y = y + _dot(act, wo_ref[sl, :])
    return x + 0.5 * y


def _rope_tables(inv_lane, tile_idx):
    pos0 = (tile_idx * TOKEN_TILE) % SEQ
    pos = (pos0 + lax.broadcasted_iota(jnp.int32, (TOKEN_TILE, LANES), 0)).astype(F32)
    lane = lax.broadcasted_iota(jnp.int32, (TOKEN_TILE, LANES), 1)
    ang = pos * inv_lane
    is_x1 = (lane >= MLA_NOPE) & (lane < MLA_NOPE + MLA_ROPE // 2)
    is_x2 = (lane >= MLA_NOPE + MLA_ROPE // 2) & (lane < MLA_NOPE + MLA_ROPE)
    cos_t = jnp.where(is_x1 | is_x2, jnp.cos(ang), 1.0)
    sin = jnp.sin(ang)
    sin_t = jnp.where(is_x1, -sin, jnp.where(is_x2, sin, 0.0))
    return cos_t, sin_t, is_x1


def _rope_block(xb, cos_t, sin_t, is_x1):
    half = MLA_ROPE // 2
    partner = jnp.where(is_x1, pltpu.roll(xb, LANES - half, 1), pltpu.roll(xb, half, 1))
    return xb * cos_t + partner * sin_t


def _pre0_kernel(x_ref, g1_ref, wa_ref, wu_ref, wo_ref, gmix_ref, wqkv_ref, wc_ref,
                 qn_ref, wuq_ref, kvn_ref, wukvk_ref, wukvv_ref, inv_ref,
                 x1_ref, qkv_ref, qf_ref, km_ref, qm_ref, kmla_ref, vm_ref):
    x1 = _ffn(x_ref[...], g1_ref[...], wa_ref, wu_ref, wo_ref)
    x1_ref[...] = x1
    h = _rms(x1, gmix_ref[...]).astype(BF16)
    qkv = _dot(h, wqkv_ref[...])
    qkv_ref[...] = qkv.astype(BF16)
    qf_ref[...] = qkv[:, :A_WIDTH]
    km_ref[0] = jnp.mean(qkv[:, A_WIDTH:2 * A_WIDTH], axis=0, keepdims=True)

    c = _dot(h, wc_ref[...])
    cq = c[:, :MLA_Q_RANK]
    ckv = c[:, MLA_Q_RANK:MLA_Q_RANK + MLA_KV_RANK]
    kr = c[:, MLA_Q_RANK + MLA_KV_RANK:]
    cqn = _rms(cq, qn_ref[...]).astype(BF16)
    ckvn = _rms(ckv, kvn_ref[...]).astype(BF16)
    qm = _dot(cqn, wuq_ref[...])
    kn = _dot(ckvn, wukvk_ref[...])
    vm_ref[...] = _dot(ckvn, wukvv_ref[...]).astype(BF16)

    cos_t, sin_t, is_x1 = _rope_tables(inv_ref[...], pl.program_id(0))
    krr = _rope_block(kr, cos_t, sin_t, is_x1)
    for hb in range(B_HEADS):
        sl = slice(hb * LANES, (hb + 1) * LANES)
        qm_ref[:, sl] = _rope_block(qm[:, sl], cos_t, sin_t, is_x1).astype(BF16)
        kmla_ref[:, sl] = (kn[:, sl] + krr).astype(BF16)


def _pre1_kernel(x_ref, g1_ref, wa_ref, wu_ref, wo_ref, gmix_ref, wqkv_ref, wf_ref,
                 x1_ref, qkv_ref, f_ref):
    x1 = _ffn(x_ref[...], g1_ref[...], wa_ref, wu_ref, wo_ref)
    x1_ref[...] = x1
    h = _rms(x1, gmix_ref[...]).astype(BF16)
    qkv_ref[...] = _dot(h, wqkv_ref[...]).astype(BF16)
    f_ref[...] = _dot(h, wf_ref[...])


def _tok_spec(width):
    return pl.BlockSpec((TOKEN_TILE, width), lambda i: (i, 0))


def _token_call(name, body, ins, in_specs, outs, out_specs):
    return pl.pallas_call(
        body,
        grid=(TOKENS // TOKEN_TILE,),
        in_specs=in_specs,
        out_specs=out_specs,
        out_shape=outs,
        compiler_params=pltpu.CompilerParams(
            dimension_semantics=("arbitrary",), vmem_limit_bytes=VMEM_LIMIT),
        name=name,
    )(*ins)


def _post_kernel(*refs, n_mix, final):
    x_ref = refs[0]
    o_refs = refs[1:1 + n_mix]
    w_refs = refs[1 + n_mix:1 + 2 * n_mix]
    (g2_ref, wa_ref, wu_ref, wo_ref, gple_ref, wg_ref, p_ref, wp_ref, gfin_ref,
     out_ref) = refs[1 + 2 * n_mix:]
    x = x_ref[...]
    for o_ref, w_ref in zip(o_refs, w_refs):
        x = x + _dot(o_ref[...], w_ref[...])
    x = _ffn(x, g2_ref[...], wa_ref, wu_ref, wo_ref)
    gate = jax.nn.sigmoid(_dot(_rms(x, gple_ref[...]).astype(BF16), wg_ref[...]))
    x = x + gate * _dot(p_ref[...].astype(BF16), wp_ref[...])
    if final:
        x = _rms(x, gfin_ref[...])
    out_ref[...] = x


def _split_bf16(x):
    hi = x.astype(BF16)
    lo = (x - hi.astype(F32)).astype(BF16)
    return hi, lo


def _router_kernel(q_ref, km_ref, pen_ref):
    own = pl.program_id(1)
    rows = A_HEADS * N_KBLK
    km = km_ref[0]
    gt = jnp.concatenate([km] * A_HEADS, axis=0)
    r = lax.broadcasted_iota(jnp.int32, (rows, A_WIDTH), 0)
    c = lax.broadcasted_iota(jnp.int32, (rows, A_WIDTH), 1)
    gt = jnp.where((r // N_KBLK) == (c // A_HEAD_DIM), gt, 0.0)
    g_hi, g_lo = _split_bf16(gt)
    q_hi, q_lo = _split_bf16(q_ref[...])
    gate = _dot_nt(g_hi, q_hi) + _dot_nt(g_hi, q_lo) + _dot_nt(g_lo, q_hi)

    n_idx = lax.broadcasted_iota(jnp.int32, (N_KBLK, Q_TILE), 0)
    pen_rows = []
    for h in range(A_HEADS):
        gh = gate[h * N_KBLK:(h + 1) * N_KBLK]
        rank = jnp.zeros((N_KBLK, Q_TILE), jnp.int32)
        for m in range(N_KBLK):
            gm = gh[m:m + 1]
            beats = (gm > gh) | ((gm == gh) & (m < n_idx))
            rank = rank + jnp.where(beats & (m < own), 1, 0)
        sel = (n_idx < own) & (rank < MOBA_TOPK)
        pen_n = jnp.where(sel, 0.0, NEG_INF)
        pen_d = jnp.full((N_KBLK, Q_TILE), NEG_INF, F32)
        for n in range(N_KBLK):
            pen_d = jnp.where(n_idx == own - n, pen_n[n:n + 1], pen_d)
        pen_rows.append(pen_d)
    pad = jnp.zeros((LANES - 2 * N_KBLK, Q_TILE), F32)
    for hp in range(A_HEADS // 2):
        blk = jnp.concatenate([pen_rows[2 * hp], pen_rows[2 * hp + 1], pad], axis=0)
        pen_ref[:, hp * LANES:(hp + 1) * LANES] = blk.T.astype(BF16)


def _t5_bias_kernel(tbl_ref, o_ref):
    h = pl.program_id(0)
    r = lax.broadcasted_iota(jnp.int32, (Q_TILE, Q_TILE), 0)
    c = lax.broadcasted_iota(jnp.int32, (Q_TILE, Q_TILE), 1)
    max_exact = T5_BUCKETS // 2
    for dd in range(3):
        dist = dd * Q_TILE + r - c
        dc = jnp.maximum(dist, 0)
        df = jnp.maximum(dc.astype(F32), 1.0)
        large = max_exact + (jnp.log(df / max_exact) / math.log(T5_MAX_DIST / max_exact)
                             * (T5_BUCKETS - max_exact)).astype(jnp.int32)
        large = jnp.minimum(large, T5_BUCKETS - 1)
        bucket = jnp.where(dc < max_exact, dc, large)
        bias = jnp.zeros((Q_TILE, Q_TILE), F32)
        for b in range(T5_BUCKETS):
            bias = jnp.where(bucket == b, tbl_ref[b, h], bias)
        if dd == 0:
            bias = jnp.where(dist >= 0, bias, NEG_INF)
        o_ref[0, dd] = bias


def _fox_gate_kernel(f_ref, b_ref, o_ref, carry_ref):
    i = pl.program_id(1)

    @pl.when(i == 0)
    def _():
        carry_ref[...] = jnp.zeros_like(carry_ref)

    lane = lax.broadcasted_iota(jnp.int32, (Q_TILE, LANES), 1)
    z = f_ref[...] + b_ref[...]
    logf = jnp.minimum(z, 0.0) - jnp.log1p(jnp.exp(-jnp.abs(z)))
    logf = jnp.where(lane < C_HEADS, logf, 0.0)
    r = lax.broadcasted_iota(jnp.int32, (Q_TILE, Q_TILE), 0)
    c = lax.broadcasted_iota(jnp.int32, (Q_TILE, Q_TILE), 1)
    tri = jnp.where(c <= r, 1.0, 0.0).astype(BF16)
    l1 = logf.astype(BF16)
    rem = logf - l1.astype(F32)
    l2 = rem.astype(BF16)
    l3 = (rem - l2.astype(F32)).astype(BF16)
    cum = _dot(tri, l1) + _dot(tri, l2) + _dot(tri, l3) + carry_ref[...]
    carry_ref[...] = cum[Q_TILE - 1:Q_TILE, :]

    c1 = cum.astype(BF16)
    rem = cum - c1.astype(F32)
    c2 = rem.astype(BF16)
    c3 = (rem - c2.astype(F32)).astype(BF16)
    rin = lax.broadcasted_iota(jnp.int32, (LANES, LANES), 0)
    lout = lax.broadcasted_iota(jnp.int32, (LANES, LANES), 1)
    out = jnp.zeros((Q_TILE, LANES), F32)
    for part, cp in enumerate((c1, c2, c3)):
        place = jnp.where((rin < C_HEADS) & (lout == rin * 8 + part), 1.0, 0.0).astype(BF16)
        out = out + _dot(cp, place)
    o_ref[...] = out.astype(BF16)


def _attn_kernel(*refs, kind):
    if kind == "moba":
        q_ref, k_ref, v_ref, pen_ref, bias_ref, o_ref = refs
    elif kind == "mla":
        q_ref, k_ref, v_ref, o_ref = refs
    else:
        q_ref, k_ref, v_ref, cq_ref, ck_ref, o_ref, kp_s = refs
    hp = pl.program_id(1)
    i = pl.program_id(2)
    lane = lax.broadcasted_iota(jnp.int32, (1, LANES), 1)
    low = lane < A_HEAD_DIM
    row = lax.broadcasted_iota(jnp.int32, (Q_TILE, Q_TILE), 0)
    col = lax.broadcasted_iota(jnp.int32, (Q_TILE, Q_TILE), 1)

    if kind == "fox":
        rin = lax.broadcasted_iota(jnp.int32, (LANES, LANES), 0)
        lout = lax.broadcasted_iota(jnp.int32, (LANES, LANES), 1)
        base_a = 2 * hp * 8
        base_b = base_a + 8

        def place(off_a, off_b, val):
            sel_a = (lout >= off_a) & (lout < off_a + 3) & (rin == base_a + lout - off_a)
            sel_b = (lout >= off_b) & (lout < off_b + 3) & (rin == base_b + lout - off_b)
            return jnp.where(sel_a | sel_b, val, 0.0).astype(BF16)

        def ones(off_a, off_b):
            in_a = (lane >= off_a) & (lane < off_a + 3)
            in_b = (lane >= off_b) & (lane < off_b + 3)
            return jnp.where(in_a | in_b, 1.0, 0.0)

        @pl.when(i == 0)
        def _():
            ak = (_dot(ck_ref[...], place(A_HEAD_DIM + 3, 3, -1.0))
                  + ones(A_HEAD_DIM, 0)).astype(BF16)
            k = k_ref[...]
            kp_s[0] = jnp.where(low, k, ak)
            kp_s[1] = jnp.where(low, ak, k)

        aq = (_dot(cq_ref[...], place(A_HEAD_DIM, 0, 1.0))
              + ones(A_HEAD_DIM + 3, 3)).astype(BF16)
        qs = q_ref[...] * 0.125
        q_heads = (jnp.where(low, qs, aq), jnp.where(low, aq, qs))
    elif kind == "moba":
        qs = q_ref[...] * 0.125
        zero = jnp.zeros_like(qs)
        q_heads = (jnp.where(low, qs, zero), jnp.where(low, zero, qs))
        pen = pen_ref[...].astype(F32)
    else:
        q_heads = (q_ref[:, :LANES], q_ref[:, LANES:])
    mla_scale = (MLA_NOPE + MLA_ROPE) ** -0.5

    def head_out(hh, nb):
        nk = nb * Q_TILE
        if kind == "fox":
            k_all = kp_s[hh, :nk, :]
        elif kind == "mla":
            k_all = k_ref[:nk, hh * LANES:(hh + 1) * LANES]
        else:
            k_all = k_ref[:nk, :]
        s = _dot_nt(q_heads[hh], k_all)
        blocks = []
        for n in range(nb):
            d = nb - 1 - n
            sn = s[:, n * Q_TILE:(n + 1) * Q_TILE]
            if kind == "mla":
                sn = sn * mla_scale
            if kind == "moba":
                sn = sn + bias_ref[hh, min(d, 2)]
                if d > 0:
                    sn = sn + pen[:, hh * N_KBLK + d:hh * N_KBLK + d + 1]
            elif d == 0:
                sn = jnp.where(col <= row, sn, NEG_INF)
            blocks.append(sn)
        m = functools.reduce(jnp.maximum, blocks)
        m = jnp.max(m, axis=1, keepdims=True)
        probs = [jnp.exp(b - m) for b in blocks]
        l = jnp.sum(functools.reduce(jnp.add, probs), axis=1, keepdims=True)
        p = jnp.concatenate([x.astype(BF16) for x in probs], axis=1)
        return _dot(p, v_ref[:nk, :]) / l

    for ii in range(N_KBLK):
        @pl.when(i == ii)
        def _(ii=ii):
            o0 = head_out(0, ii + 1)
            o1 = head_out(1, ii + 1)
            o_ref[...] = jnp.where(low, o0, o1).astype(BF16)


def _attention(kind, ins, in_specs, n_pairs, extra_scratch=()):
    return pl.pallas_call(
        functools.partial(_attn_kernel, kind=kind),
        grid=(BATCH, n_pairs, N_KBLK),
        in_specs=in_specs,
        out_specs=pl.BlockSpec((Q_TILE, LANES), lambda b, h, i: (b * N_KBLK + i, h)),
        out_shape=jax.ShapeDtypeStruct((TOKENS, n_pairs * LANES), BF16),
        scratch_shapes=list(extra_scratch),
        compiler_params=pltpu.CompilerParams(
            dimension_semantics=("arbitrary", "arbitrary", "arbitrary"),
            vmem_limit_bytes=VMEM_LIMIT),
        name="attn_" + kind,
    )(*ins)


def _q_spec(width, col0):
    return pl.BlockSpec((Q_TILE, width), lambda b, h, i: (b * N_KBLK + i, col0 + h))


def _kv_spec(width, col0):
    return pl.BlockSpec((SEQ, width), lambda b, h, i: (b, col0 + h))


def _place_heads(w, n_heads, src_stride, src_off, width):
    out = jnp.zeros((w.shape[0], n_heads * LANES), w.dtype)
    for h in range(n_heads):
        src = w[:, h * src_stride + src_off:h * src_stride + src_off + width]
        out = out.at[:, h * LANES:h * LANES + width].set(src)
    return out


def _row(v, width=None):
    v = v.reshape(1, -1).astype(F32)
    if width is not None and v.shape[1] < width:
        v = jnp.pad(v, ((0, 0), (0, width - v.shape[1])))
    return v


def kernel(x, p, t5_bias, ff1_norm, ff1_w_in, ff1_w_out, mix_norm, ff2_norm, ff2_w_in, ff2_w_out,
           ple_norm, ple_w_gate, ple_w_proj, ab_w_in, mla_q_norm, mla_w_uq, mla_kv_norm, mla_w_ukv,
           ab_w_out, fox_w_in, fox_b_f, fox_w_out, final_norm):
    nt = TOKENS // TOKEN_TILE
    xt = x.reshape(TOKENS, D_MODEL)

    def ffn_args(norm, w_in, w_out):
        w_in = w_in.astype(BF16)
        args = [_row(norm), w_in[:, :D_FF], w_in[:, D_FF:], w_out.astype(BF16)]
        specs = [_wspec((1, D_MODEL)), _wspec((D_MODEL, D_FF)), _wspec((D_MODEL, D_FF)),
                 _wspec((D_FF, D_MODEL))]
        return args, specs

    def post(xin, mixes, w_outs, layer, final):
        fa, fs = ffn_args(ff2_norm[layer], ff2_w_in[layer], ff2_w_out[layer])
        ws = [w.astype(BF16) for w in w_outs]
        ins = ([xin] + list(mixes) + ws + fa
               + [_row(ple_norm[layer]), ple_w_gate[layer].astype(BF16),
                  p[layer].reshape(TOKENS, PLE_DIM), ple_w_proj[layer].astype(BF16),
                  _row(final_norm)])
        specs = ([_tok_spec(D_MODEL)] + [_tok_spec(m.shape[1]) for m in mixes]
                 + [_wspec(w.shape) for w in ws] + fs
                 + [_wspec((1, D_MODEL)), _wspec((D_MODEL, D_MODEL)), _tok_spec(PLE_DIM),
                    _wspec((PLE_DIM, D_MODEL)), _wspec((1, D_MODEL))])
        return _token_call(
            "post%d" % layer,
            functools.partial(_post_kernel, n_mix=len(mixes), final=final), ins, specs,
            jax.ShapeDtypeStruct((TOKENS, D_MODEL), F32), _tok_spec(D_MODEL))

    w_ab = ab_w_in[0]
    w_qkv = w_ab[:, :3 * A_WIDTH].astype(BF16)
    c0 = 3 * A_WIDTH
    w_c = jnp.zeros((D_MODEL, 4 * LANES), F32)
    w_c = w_c.at[:, :MLA_Q_RANK + MLA_KV_RANK].set(w_ab[:, c0:c0 + MLA_Q_RANK + MLA_KV_RANK])
    kr0 = MLA_Q_RANK + MLA_KV_RANK
    w_c = w_c.at[:, kr0 + MLA_NOPE:kr0 + MLA_NOPE + MLA_ROPE].set(w_ab[:, c0 + kr0:])
    w_c = w_c.astype(BF16)
    w_uq = _place_heads(mla_w_uq[0], B_HEADS, MLA_NOPE + MLA_ROPE, 0, MLA_NOPE + MLA_ROPE).astype(BF16)
    w_ukv_k = _place_heads(mla_w_ukv[0], B_HEADS, MLA_NOPE + MLA_V, 0, MLA_NOPE).astype(BF16)
    w_ukv = mla_w_ukv[0].reshape(MLA_KV_RANK, B_HEADS, MLA_NOPE + MLA_V)
    w_ukv_v = w_ukv[:, :, MLA_NOPE:].reshape(MLA_KV_RANK, B_HEADS * MLA_V).astype(BF16)
    half = MLA_ROPE // 2
    inv = ROPE_THETA ** (-np.arange(half, dtype=np.float64) / half)
    inv_lane = np.zeros((1, LANES), np.float32)
    inv_lane[0, MLA_NOPE:MLA_NOPE + half] = inv
    inv_lane[0, MLA_NOPE + half:MLA_NOPE + MLA_ROPE] = inv
    inv_lane = jnp.asarray(inv_lane)

    fa, fs = ffn_args(ff1_norm[0], ff1_w_in[0], ff1_w_out[0])
    ins = ([xt] + fa + [_row(mix_norm[0]), w_qkv, w_c, _row(mla_q_norm[0]), w_uq,
                        _row(mla_kv_norm[0]), w_ukv_k, w_ukv_v, inv_lane])
    specs = ([_tok_spec(D_MODEL)] + fs
             + [_wspec((1, D_MODEL)), _wspec(w_qkv.shape), _wspec(w_c.shape),
                _wspec((1, MLA_Q_RANK)), _wspec(w_uq.shape), _wspec((1, MLA_KV_RANK)),
                _wspec(w_ukv_k.shape), _wspec(w_ukv_v.shape), _wspec((1, LANES))])
    outs = (jax.ShapeDtypeStruct((TOKENS, D_MODEL), F32),
            jax.ShapeDtypeStruct((TOKENS, 3 * A_WIDTH), BF16),
            jax.ShapeDtypeStruct((TOKENS, A_WIDTH), F32),
            jax.ShapeDtypeStruct((nt, 1, A_WIDTH), F32),
            jax.ShapeDtypeStruct((TOKENS, B_HEADS * LANES), BF16),
            jax.ShapeDtypeStruct((TOKENS, B_HEADS * LANES), BF16),
            jax.ShapeDtypeStruct((TOKENS, B_HEADS * MLA_V), BF16))
    out_specs = (_tok_spec(D_MODEL), _tok_spec(3 * A_WIDTH), _tok_spec(A_WIDTH),
                 pl.BlockSpec((1, 1, A_WIDTH), lambda i: (i, 0, 0)),
                 _tok_spec(B_HEADS * LANES), _tok_spec(B_HEADS * LANES), _tok_spec(B_HEADS * MLA_V))
    x1, qkv_a, q_f32, k_mean, q_mla, k_mla, v_mla = _token_call(
        "pre0", _pre0_kernel, ins, specs, outs, out_specs)

    pen = pl.pallas_call(
        _router_kernel,
        grid=(BATCH, N_KBLK),
        in_specs=[pl.BlockSpec((Q_TILE, A_WIDTH), lambda b, i: (b * N_KBLK + i, 0)),
                  pl.BlockSpec((1, N_KBLK, A_WIDTH), lambda b, i: (b, 0, 0))],
        out_specs=pl.BlockSpec((Q_TILE, A_HEADS // 2 * LANES), lambda b, i: (b * N_KBLK + i, 0)),
        out_shape=jax.ShapeDtypeStruct((TOKENS, A_HEADS // 2 * LANES), BF16),
        name="moba_router",
    )(q_f32, k_mean.reshape(BATCH, N_KBLK, A_WIDTH))

    bias = pl.pallas_call(
        _t5_bias_kernel,
        grid=(A_HEADS,),
        in_specs=[pl.BlockSpec(memory_space=pltpu.SMEM)],
        out_specs=pl.BlockSpec((1, 3, Q_TILE, Q_TILE), lambda h: (h, 0, 0, 0)),
        out_shape=jax.ShapeDtypeStruct((A_HEADS, 3, Q_TILE, Q_TILE), F32),
        name="t5_bias_tiles",
    )(t5_bias.astype(F32))

    na = A_HEADS // 2
    o_a = _attention(
        "moba", [qkv_a, qkv_a, qkv_a, pen, bias],
        [_q_spec(LANES, 0), _kv_spec(LANES, na), _kv_spec(LANES, 2 * na), _q_spec(LANES, 0),
         pl.BlockSpec((2, 3, Q_TILE, Q_TILE), lambda b, h, i: (h, 0, 0, 0))], na)
    o_b = _attention(
        "mla", [q_mla, k_mla, v_mla],
        [_q_spec(2 * LANES, 0), _kv_spec(2 * LANES, 0), _kv_spec(LANES, 0)], B_HEADS // 2)
    w_o = ab_w_out[0]
    xt = post(x1, [o_a, o_b], [w_o[:A_WIDTH], w_o[A_WIDTH:]], 0, DEPTH == 1)

    w_fox = fox_w_in[0]
    w_qkv = w_fox[:, :3 * C_WIDTH].astype(BF16)
    w_f = jnp.pad(w_fox[:, 3 * C_WIDTH:], ((0, 0), (0, LANES - C_HEADS))).astype(BF16)
    fa, fs = ffn_args(ff1_norm[1], ff1_w_in[1], ff1_w_out[1])
    ins = [xt] + fa + [_row(mix_norm[1]), w_qkv, w_f]
    specs = ([_tok_spec(D_MODEL)] + fs
             + [_wspec((1, D_MODEL)), _wspec(w_qkv.shape), _wspec(w_f.shape)])
    outs = (jax.ShapeDtypeStruct((TOKENS, D_MODEL), F32),
            jax.ShapeDtypeStruct((TOKENS, 3 * C_WIDTH), BF16),
            jax.ShapeDtypeStruct((TOKENS, LANES), F32))
    out_specs = (_tok_spec(D_MODEL), _tok_spec(3 * C_WIDTH), _tok_spec(LANES))
    x1, qkv_c, f_gate = _token_call("pre1", _pre1_kernel, ins, specs, outs, out_specs)

    caug = pl.pallas_call(
        _fox_gate_kernel,
        grid=(BATCH, N_KBLK),
        in_specs=[pl.BlockSpec((Q_TILE, LANES), lambda b, i: (b * N_KBLK + i, 0)),
                  pl.BlockSpec((1, LANES), lambda b, i: (0, 0))],
        out_specs=pl.BlockSpec((Q_TILE, LANES), lambda b, i: (b * N_KBLK + i, 0)),
        out_shape=jax.ShapeDtypeStruct((TOKENS, LANES), BF16),
        scratch_shapes=[pltpu.VMEM((1, LANES), F32)],
        compiler_params=pltpu.CompilerParams(dimension_semantics=("arbitrary", "arbitrary")),
        name="fox_gate",
    )(f_gate, _row(fox_b_f[0], LANES))

    nc = C_HEADS // 2
    o_c = _attention(
        "fox", [qkv_c, qkv_c, qkv_c, caug, caug],
        [_q_spec(LANES, 0), _kv_spec(LANES, nc), _kv_spec(LANES, 2 * nc),
         pl.BlockSpec((Q_TILE, LANES), lambda b, h, i: (b * N_KBLK + i, 0)),
         pl.BlockSpec((SEQ, LANES), lambda b, h, i: (b, 0))],
        nc, extra_scratch=[pltpu.VMEM((2, SEQ, LANES), BF16)])
    xt = post(x1, [o_c], [fox_w_out[0]], 1, True)
    return xt.reshape(BATCH, SEQ, D_MODEL)
```

```python
import functools
import math

import jax
import jax.numpy as jnp
import numpy as np
from jax import lax
from jax.experimental import pallas as pl
from jax.experimental.pallas import tpu as pltpu

F32 = jnp.float32
BF16 = jnp.bfloat16

D_MODEL = 1024
BATCH = 8
SEQ = 2048
DEPTH = 2
PLE_DIM = 256
D_FF = 2816
EPS = 1e-6

A_HEADS = 8
A_HEAD_DIM = 64
MOBA_BLOCK = 256
MOBA_TOPK = 3

B_HEADS = 8
MLA_Q_RANK = 256
MLA_KV_RANK = 128
MLA_NOPE = 64
MLA_ROPE = 32
MLA_V = 64
ROPE_THETA = 10000.0

T5_BUCKETS = 32
T5_MAX_DIST = 128

C_HEADS = 16
C_HEAD_DIM = 64

A_WIDTH = A_HEADS * A_HEAD_DIM
C_WIDTH = C_HEADS * C_HEAD_DIM

TOKENS = BATCH * SEQ
LANES = 128
BF16_ROWS = 16
PAIR_GROUP = 2
Q_TILE = MOBA_BLOCK
N_KBLK = SEQ // Q_TILE
TOKEN_TILE = 256
FF_CHUNK = D_FF // 2
VMEM_LIMIT = 56 * 1024 * 1024
NEG_INF = float("-inf")
LOG2E = math.log2(math.e)
QSCALE_64 = A_HEAD_DIM ** -0.5 * LOG2E
QSCALE_MLA = (MLA_NOPE + MLA_ROPE) ** -0.5 * LOG2E


def _wspec(shape):
    nd = len(shape)
    return pl.BlockSpec(shape, lambda *_: (0,) * nd, pipeline_mode=pl.Buffered(1))


def _dot(a, b):
    return jnp.dot(a, b, preferred_element_type=F32)


def _dot_nt(a, b):
    return lax.dot_general(a, b, (((1,), (1,)), ((), ())), preferred_element_type=F32)


def _rms(x, g):
    return x * lax.rsqrt(jnp.mean(x * x, axis=-1, keepdims=True) + EPS) * g


def _ffn(x, g, wa_ref, wu_ref, wo_ref):
    h = _rms(x, g).astype(BF16)
    y = jnp.zeros_like(x)
    for c in range(D_FF // FF_CHUNK):
        sl = slice(c * FF_CHUNK, (c + 1) * FF_CHUNK)
        a = _dot(h, wa_ref[:, sl])
        u = _dot(h, wu_ref[:, sl])
        act = (a * jax.nn.sigmoid(a) * u).astype(BF16)
        y = y + _dot(act, wo_ref[sl, :])
    return x + 0.5 * y


def _rope_tables(inv_lane, tile_idx):
    pos0 = (tile_idx * TOKEN_TILE) % SEQ
    pos = (pos0 + lax.broadcasted_iota(jnp.int32, (TOKEN_TILE, LANES), 0)).astype(F32)
    lane = lax.broadcasted_iota(jnp.int32, (TOKEN_TILE, LANES), 1)
    ang = pos * inv_lane
    is_x1 = (lane >= MLA_NOPE) & (lane < MLA_NOPE + MLA_ROPE // 2)
    is_x2 = (lane >= MLA_NOPE + MLA_ROPE // 2) & (lane < MLA_NOPE + MLA_ROPE)
    cos_t = jnp.where(is_x1 | is_x2, jnp.cos(ang), 1.0)
    sin = jnp.sin(ang)
    sin_t = jnp.where(is_x1, -sin, jnp.where(is_x2, sin, 0.0))
    return cos_t, sin_t, is_x1


def _rope_block(xb, cos_t, sin_t, is_x1):
    half = MLA_ROPE // 2
    partner = jnp.where(is_x1, pltpu.roll(xb, LANES - half, 1), pltpu.roll(xb, half, 1))
    return xb * cos_t + partner * sin_t


def _pre0_kernel(x_ref, g1_ref, wa_ref, wu_ref, wo_ref, gmix_ref, wqk_ref, wvt_ref, wc_ref,
                 qn_ref, wuq_ref, kvn_ref, wukvk_ref, wukvvt_ref, inv_ref,
                 x1_ref, qk_ref, vt_ref, qf_ref, km_ref, qm_ref, kmla_ref, vmt_ref):
    x1 = _ffn(x_ref[...], g1_ref[...], wa_ref, wu_ref, wo_ref)
    x1_ref[...] = x1
    h = _rms(x1, gmix_ref[...]).astype(BF16)
    qk = _dot(h, wqk_ref[...])
    q = qk[:, :A_WIDTH]
    k = qk[:, A_WIDTH:]
    qk_ref[:, :A_WIDTH] = (q * QSCALE_64).astype(BF16)
    qk_ref[:, A_WIDTH:] = k.astype(BF16)
    vt_ref[...] = _dot_nt(wvt_ref[...], h).astype(BF16)
    qf_ref[...] = q
    km_ref[0] = jnp.mean(k, axis=0, keepdims=True)

    c = _dot(h, wc_ref[...])
    cq = c[:, :MLA_Q_RANK]
    ckv = c[:, MLA_Q_RANK:MLA_Q_RANK + MLA_KV_RANK]
    kr = c[:, MLA_Q_RANK + MLA_KV_RANK:]
    cqn = _rms(cq, qn_ref[...]).astype(BF16)
    ckvn = _rms(ckv, kvn_ref[...]).astype(BF16)
    qm = _dot(cqn, wuq_ref[...])
    kn = _dot(ckvn, wukvk_ref[...])
    vmt_ref[...] = _dot_nt(wukvvt_ref[...], ckvn).astype(BF16)

    cos_t, sin_t, is_x1 = _rope_tables(inv_ref[...], pl.program_id(0))
    cos_q = cos_t * QSCALE_MLA
    sin_q = sin_t * QSCALE_MLA
    krr = _rope_block(kr, cos_t, sin_t, is_x1)
    for hb in range(B_HEADS):
        sl = slice(hb * LANES, (hb + 1) * LANES)
        qm_ref[:, sl] = _rope_block(qm[:, sl], cos_q, sin_q, is_x1).astype(BF16)
        kmla_ref[:, sl] = (kn[:, sl] + krr).astype(BF16)


def _pre1_kernel(x_ref, g1_ref, wa_ref, wu_ref, wo_ref, gmix_ref, wqk_ref, wvt_ref, wf_ref,
                 x1_ref, qk_ref, vt_ref, f_ref):
    x1 = _ffn(x_ref[...], g1_ref[...], wa_ref, wu_ref, wo_ref)
    x1_ref[...] = x1
    h = _rms(x1, gmix_ref[...]).astype(BF16)
    qk = _dot(h, wqk_ref[...])
    qk_ref[:, :C_WIDTH] = (qk[:, :C_WIDTH] * QSCALE_64).astype(BF16)
    qk_ref[:, C_WIDTH:] = qk[:, C_WIDTH:].astype(BF16)
    vt_ref[...] = _dot_nt(wvt_ref[...], h).astype(BF16)
    f_ref[...] = _dot(h, wf_ref[...])


def _tok_spec(width):
    return pl.BlockSpec((TOKEN_TILE, width), lambda i: (i, 0))


def _tok_t_spec(height):
    return pl.BlockSpec((height, TOKEN_TILE), lambda i: (0, i))


def _token_call(name, body, ins, in_specs, outs, out_specs):
    return pl.pallas_call(
        body,
        grid=(TOKENS // TOKEN_TILE,),
        in_specs=in_specs,
        out_specs=out_specs,
        out_shape=outs,
        compiler_params=pltpu.CompilerParams(
            dimension_semantics=("arbitrary",), vmem_limit_bytes=VMEM_LIMIT),
        name=name,
    )(*ins)


def _post_kernel(*refs, n_mix, final):
    x_ref = refs[0]
    o_refs = refs[1:1 + n_mix]
    w_refs = refs[1 + n_mix:1 + 2 * n_mix]
    (g2_ref, wa_ref, wu_ref, wo_ref, gple_ref, wg_ref, p_ref, wp_ref, gfin_ref,
     out_ref) = refs[1 + 2 * n_mix:]
    x = x_ref[...]
    for o_ref, w_ref in zip(o_refs, w_refs):
        x = x + _dot(o_ref[...], w_ref[...])
    x = _ffn(x, g2_ref[...], wa_ref, wu_ref, wo_ref)
    gate = jax.nn.sigmoid(_dot(_rms(x, gple_ref[...]).astype(BF16), wg_ref[...]))
    x = x + gate * _dot(p_ref[...].astype(BF16), wp_ref[...])
    if final:
        x = _rms(x, gfin_ref[...])
    out_ref[...] = x


def _split_bf16(x):
    hi = x.astype(BF16)
    lo = (x - hi.astype(F32)).astype(BF16)
    return hi, lo


def _router_kernel(q_ref, km_ref, pen_ref):
    own = pl.program_id(1)
    rows = A_HEADS * N_KBLK
    km = km_ref[0]
    gt = jnp.concatenate([km] * A_HEADS, axis=0)
    r = lax.broadcasted_iota(jnp.int32, (rows, A_WIDTH), 0)
    c = lax.broadcasted_iota(jnp.int32, (rows, A_WIDTH), 1)
    gt = jnp.where((r // N_KBLK) == (c // A_HEAD_DIM), gt, 0.0)
    g_hi, g_lo = _split_bf16(gt)
    q_hi, q_lo = _split_bf16(q_ref[...])
    gate = _dot_nt(g_hi, q_hi) + _dot_nt(g_hi, q_lo) + _dot_nt(g_lo, q_hi)

    n_idx = lax.broadcasted_iota(jnp.int32, (N_KBLK, Q_TILE), 0)
    pen_rows = []
    for h in range(A_HEADS):
        gh = gate[h * N_KBLK:(h + 1) * N_KBLK]
        rank = jnp.zeros((N_KBLK, Q_TILE), jnp.int32)
        for m in range(N_KBLK):
            gm = gh[m:m + 1]
            beats = (gm > gh) | ((gm == gh) & (m < n_idx))
            rank = rank + jnp.where(beats & (m < own), 1, 0)
        sel = (n_idx < own) & (rank < MOBA_TOPK)
        pen_n = jnp.where(sel, 0.0, NEG_INF)
        pen_d = jnp.full((N_KBLK, Q_TILE), NEG_INF, F32)
        for n in range(N_KBLK):
            pen_d = jnp.where(n_idx == own - n, pen_n[n:n + 1], pen_d)
        pen_rows.append(pen_d)
    pairs = [jnp.concatenate(pen_rows[2 * hp:2 * hp + 2], axis=1) for hp in range(A_HEADS // 2)]
    pen_ref[0] = jnp.concatenate(pairs, axis=0)


def _t5_bias_kernel(tbl_ref, o_ref):
    hp = pl.program_id(0)
    r = lax.broadcasted_iota(jnp.int32, (Q_TILE, Q_TILE), 0)
    c = lax.broadcasted_iota(jnp.int32, (Q_TILE, Q_TILE), 1)
    max_exact = T5_BUCKETS // 2
    for dd in range(3):
        dist = dd * Q_TILE + c - r
        dc = jnp.maximum(dist, 0)
        df = jnp.maximum(dc.astype(F32), 1.0)
        large = max_exact + (jnp.log(df / max_exact) / math.log(T5_MAX_DIST / max_exact)
                             * (T5_BUCKETS - max_exact)).astype(jnp.int32)
        large = jnp.minimum(large, T5_BUCKETS - 1)
        bucket = jnp.where(dc < max_exact, dc, large)
        for hh in range(2):
            bias = jnp.zeros((Q_TILE, Q_TILE), F32)
            for b in range(T5_BUCKETS):
                bias = jnp.where(bucket == b, tbl_ref[b, 2 * hp + hh], bias)
            bias = bias * LOG2E
            if dd == 0:
                bias = jnp.where(dist >= 0, bias, NEG_INF)
            o_ref[0, dd, :, hh * Q_TILE:(hh + 1) * Q_TILE] = bias


def _fox_gate_kernel(f_ref, b_ref, o_ref, carry_ref):
    i = pl.program_id(1)

    @pl.when(i == 0)
    def _():
        carry_ref[...] = jnp.zeros_like(carry_ref)

    lane = lax.broadcasted_iota(jnp.int32, (Q_TILE, LANES), 1)
    z = f_ref[...] + b_ref[...]
    logf = jnp.minimum(z, 0.0) - jnp.log1p(jnp.exp(-jnp.abs(z)))
    logf = jnp.where(lane < C_HEADS, logf, 0.0)
    r = lax.broadcasted_iota(jnp.int32, (Q_TILE, Q_TILE), 0)
    c = lax.broadcasted_iota(jnp.int32, (Q_TILE, Q_TILE), 1)
    tri = jnp.where(c <= r, 1.0, 0.0).astype(BF16)
    l1 = logf.astype(BF16)
    rem = logf - l1.astype(F32)
    l2 = rem.astype(BF16)
    l3 = (rem - l2.astype(F32)).astype(BF16)
    cum = _dot(tri, l1) + _dot(tri, l2) + _dot(tri, l3) + carry_ref[...]
    carry_ref[...] = cum[Q_TILE - 1:Q_TILE, :]

    cs = cum * LOG2E
    c1 = cs.astype(BF16)
    rem = cs - c1.astype(F32)
    c2 = rem.astype(BF16)
    c3 = (rem - c2.astype(F32)).astype(BF16)
    rin = lax.broadcasted_iota(jnp.int32, (LANES, LANES), 0)
    lout = lax.broadcasted_iota(jnp.int32, (LANES, LANES), 1)
    out = jnp.zeros((Q_TILE, LANES), F32)
    for part, cp in enumerate((c1, c2, c3)):
        place = jnp.where((rin < C_HEADS) & (lout == rin * 8 + part), 1.0, 0.0).astype(BF16)
        out = out + _dot(cp, place)
    o_ref[...] = out.astype(BF16)


def _attn_kernel(*refs, kind):
    if kind == "moba":
        q_ref, k_ref, vt_ref, pen_ref, bias_ref, o_ref = refs
    elif kind == "mla":
        q_ref, k_ref, vt_ref, o_ref = refs
    else:
        q_ref, k_ref, vt_ref, cq_ref, ck_ref, o_ref, kp_s = refs
    grp = pl.program_id(1)
    i = pl.program_id(2)
    lane = lax.broadcasted_iota(jnp.int32, (1, LANES), 1)
    low = lane < A_HEAD_DIM
    key = lax.broadcasted_iota(jnp.int32, (Q_TILE, 2 * Q_TILE), 0)
    qry = lax.broadcasted_iota(jnp.int32, (Q_TILE, 2 * Q_TILE), 1) % Q_TILE
    kw = LANES if kind == "moba" else 2 * LANES
    k_cat = kp_s if kind == "fox" else k_ref

    if kind == "fox":
        rin = lax.broadcasted_iota(jnp.int32, (LANES, LANES), 0)
        lout = lax.broadcasted_iota(jnp.int32, (LANES, LANES), 1)

        def place(pair, off_a, off_b, val):
            base_a = 2 * pair * 8
            base_b = base_a + 8
            sel_a = (lout >= off_a) & (lout < off_a + 3) & (rin == base_a + lout - off_a)
            sel_b = (lout >= off_b) & (lout < off_b + 3) & (rin == base_b + lout - off_b)
            return jnp.where(sel_a | sel_b, val, 0.0).astype(BF16)

        def ones(off_a, off_b):
            in_a = (lane >= off_a) & (lane < off_a + 3)
            in_b = (lane >= off_b) & (lane < off_b + 3)
            return jnp.where(in_a | in_b, 1.0, 0.0)

        @pl.when(i == 0)
        def _():
            for pi in range(PAIR_GROUP):
                ak = (_dot(ck_ref[...], place(grp * PAIR_GROUP + pi, A_HEAD_DIM + 3, 3, -1.0))
                      + ones(A_HEAD_DIM, 0)).astype(BF16)
                k = k_ref[:, pi * LANES:(pi + 1) * LANES]
                kp_s[:, pi * kw:pi * kw + LANES] = jnp.where(low, k, ak)
                kp_s[:, pi * kw + LANES:(pi + 1) * kw] = jnp.where(low, ak, k)

    q_bd = []
    for pi in range(PAIR_GROUP):
        zero = jnp.zeros((Q_TILE, LANES), BF16)
        if kind == "fox":
            aq = (_dot(cq_ref[...], place(grp * PAIR_GROUP + pi, A_HEAD_DIM, 0, 1.0))
                  + ones(A_HEAD_DIM + 3, 3)).astype(BF16)
            q = q_ref[:, pi * LANES:(pi + 1) * LANES]
            q_bd.append(jnp.concatenate(
                [jnp.concatenate([jnp.where(low, q, aq), zero], axis=1),
                 jnp.concatenate([zero, jnp.where(low, aq, q)], axis=1)], axis=0))
        elif kind == "moba":
            q = q_ref[:, pi * LANES:(pi + 1) * LANES]
            q_bd.append(jnp.concatenate([jnp.where(low, q, zero), jnp.where(low, zero, q)], axis=0))
        else:
            q = q_ref[:, pi * kw:(pi + 1) * kw]
            q_bd.append(jnp.concatenate(
                [jnp.concatenate([q[:, :LANES], zero], axis=1),
                 jnp.concatenate([zero, q[:, LANES:]], axis=1)], axis=0))

    def group_out(nb):
        ones_rows = jnp.ones((BF16_ROWS, Q_TILE), BF16)

        def rows_of(d):
            return slice((nb - 1 - d) * Q_TILE, (nb - d) * Q_TILE)

        def qk(pi, d):
            return _dot_nt(k_cat[rows_of(d), pi * kw:(pi + 1) * kw], q_bd[pi])

        s_next = [qk(pi, 0) for pi in range(PAIR_GROUP)]
        m = [None] * PAIR_GROUP
        acc = [None] * PAIR_GROUP
        for d in range(nb):
            for pi in range(PAIR_GROUP):
                sn = s_next[pi]
                if d + 1 < nb:
                    s_next[pi] = qk(pi, d + 1)
                if kind == "moba":
                    sn = sn + bias_ref[pi, min(d, 2)]
                    if d > 0:
                        sn = sn + pen_ref[0, pi * N_KBLK + d:pi * N_KBLK + d + 1, :]
                elif d == 0:
                    sn = jnp.where(key <= qry, sn, NEG_INF)
                bm = jnp.max(sn, axis=0, keepdims=True)
                v_aug = jnp.concatenate([vt_ref[pi * LANES:(pi + 1) * LANES, rows_of(d)], ones_rows], axis=0)
                if d == 0:
                    m[pi] = bm
                    acc[pi] = _dot(v_aug, jnp.exp2(sn - bm).astype(BF16))
                else:
                    m_new = jnp.maximum(m[pi], bm)
                    acc[pi] = (acc[pi] * jnp.exp2(m[pi] - m_new)
                               + _dot(v_aug, jnp.exp2(sn - m_new).astype(BF16)))
                    m[pi] = m_new
        return [a[:LANES] / a[LANES:LANES + 1] for a in acc]

    for ii in range(N_KBLK):
        @pl.when(i == ii)
        def _(ii=ii):
            for pi, ot2 in enumerate(group_out(ii + 1)):
                ot = jnp.concatenate([ot2[:A_HEAD_DIM, :Q_TILE], ot2[A_HEAD_DIM:, Q_TILE:]], axis=0)
                o_ref[:, pi * LANES:(pi + 1) * LANES] = ot.T.astype(BF16)


def _attention(kind, ins, in_specs, n_pairs, extra_scratch=()):
    width = PAIR_GROUP * LANES
    return pl.pallas_call(
        functools.partial(_attn_kernel, kind=kind),
        grid=(BATCH, n_pairs // PAIR_GROUP, N_KBLK),
        in_specs=in_specs,
        out_specs=pl.BlockSpec((Q_TILE, width), lambda b, g, i: (b * N_KBLK + i, g)),
        out_shape=jax.ShapeDtypeStruct((TOKENS, n_pairs * LANES), BF16),
        scratch_shapes=list(extra_scratch),
        compiler_params=pltpu.CompilerParams(
            dimension_semantics=("arbitrary", "arbitrary", "arbitrary"),
            vmem_limit_bytes=VMEM_LIMIT),
        name="attn_" + kind,
    )(*ins)


def _q_spec(pair_width, col0):
    return pl.BlockSpec((Q_TILE, PAIR_GROUP * pair_width), lambda b, g, i: (b * N_KBLK + i, col0 + g))


def _kv_spec(pair_width, col0):
    return pl.BlockSpec((SEQ, PAIR_GROUP * pair_width), lambda b, g, i: (b, col0 + g))


def _vt_spec():
    return pl.BlockSpec((PAIR_GROUP * LANES, SEQ), lambda b, g, i: (g, b))


def _place_heads(w, n_heads, src_stride, src_off, width):
    out = jnp.zeros((w.shape[0], n_heads * LANES), w.dtype)
    for h in range(n_heads):
        src = w[:, h * src_stride + src_off:h * src_stride + src_off + width]
        out = out.at[:, h * LANES:h * LANES + width].set(src)
    return out


def _row(v, width=None):
    v = v.reshape(1, -1).astype(F32)
    if width is not None and v.shape[1] < width:
        v = jnp.pad(v, ((0, 0), (0, width - v.shape[1])))
    return v


def kernel(x, p, t5_bias, ff1_norm, ff1_w_in, ff1_w_out, mix_norm, ff2_norm, ff2_w_in, ff2_w_out,
           ple_norm, ple_w_gate, ple_w_proj, ab_w_in, mla_q_norm, mla_w_uq, mla_kv_norm, mla_w_ukv,
           ab_w_out, fox_w_in, fox_b_f, fox_w_out, final_norm):
    nt = TOKENS // TOKEN_TILE
    xt = x.reshape(TOKENS, D_MODEL)

    def ffn_args(norm, w_in, w_out):
        w_in = w_in.astype(BF16)
        args = [_row(norm), w_in[:, :D_FF], w_in[:, D_FF:], w_out.astype(BF16)]
        specs = [_wspec((1, D_MODEL)), _wspec((D_MODEL, D_FF)), _wspec((D_MODEL, D_FF)),
                 _wspec((D_FF, D_MODEL))]
        return args, specs

    def post(xin, mixes, w_outs, layer, final):
        fa, fs = ffn_args(ff2_norm[layer], ff2_w_in[layer], ff2_w_out[layer])
        ws = [w.astype(BF16) for w in w_outs]
        ins = ([xin] + list(mixes) + ws + fa
               + [_row(ple_norm[layer]), ple_w_gate[layer].astype(BF16),
                  p[layer].reshape(TOKENS, PLE_DIM), ple_w_proj[layer].astype(BF16),
                  _row(final_norm)])
        specs = ([_tok_spec(D_MODEL)] + [_tok_spec(m.shape[1]) for m in mixes]
                 + [_wspec(w.shape) for w in ws] + fs
                 + [_wspec((1, D_MODEL)), _wspec((D_MODEL, D_MODEL)), _tok_spec(PLE_DIM),
                    _wspec((PLE_DIM, D_MODEL)), _wspec((1, D_MODEL))])
        return _token_call(
            "post%d" % layer,
            functools.partial(_post_kernel, n_mix=len(mixes), final=final), ins, specs,
            jax.ShapeDtypeStruct((TOKENS, D_MODEL), F32), _tok_spec(D_MODEL))

    w_ab = ab_w_in[0]
    w_qk = w_ab[:, :2 * A_WIDTH].astype(BF16)
    w_vt = w_ab[:, 2 * A_WIDTH:3 * A_WIDTH].T.astype(BF16)
    c0 = 3 * A_WIDTH
    w_c = jnp.zeros((D_MODEL, 4 * LANES), F32)
    w_c = w_c.at[:, :MLA_Q_RANK + MLA_KV_RANK].set(w_ab[:, c0:c0 + MLA_Q_RANK + MLA_KV_RANK])
    kr0 = MLA_Q_RANK + MLA_KV_RANK
    w_c = w_c.at[:, kr0 + MLA_NOPE:kr0 + MLA_NOPE + MLA_ROPE].set(w_ab[:, c0 + kr0:])
    w_c = w_c.astype(BF16)
    w_uq = _place_heads(mla_w_uq[0], B_HEADS, MLA_NOPE + MLA_ROPE, 0, MLA_NOPE + MLA_ROPE).astype(BF16)
    w_ukv_k = _place_heads(mla_w_ukv[0], B_HEADS, MLA_NOPE + MLA_V, 0, MLA_NOPE).astype(BF16)
    w_ukv = mla_w_ukv[0].reshape(MLA_KV_RANK, B_HEADS, MLA_NOPE + MLA_V)
    w_ukv_vt = w_ukv[:, :, MLA_NOPE:].reshape(MLA_KV_RANK, B_HEADS * MLA_V).T.astype(BF16)
    half = MLA_ROPE // 2
    inv = ROPE_THETA ** (-np.arange(half, dtype=np.float64) / half)
    inv_lane = np.zeros((1, LANES), np.float32)
    inv_lane[0, MLA_NOPE:MLA_NOPE + half] = inv
    inv_lane[0, MLA_NOPE + half:MLA_NOPE + MLA_ROPE] = inv
    inv_lane = jnp.asarray(inv_lane)

    fa, fs = ffn_args(ff1_norm[0], ff1_w_in[0], ff1_w_out[0])
    ins = ([xt] + fa + [_row(mix_norm[0]), w_qk, w_vt, w_c, _row(mla_q_norm[0]), w_uq,
                        _row(mla_kv_norm[0]), w_ukv_k, w_ukv_vt, inv_lane])
    specs = ([_tok_spec(D_MODEL)] + fs
             + [_wspec((1, D_MODEL)), _wspec(w_qk.shape), _wspec(w_vt.shape), _wspec(w_c.shape),
                _wspec((1, MLA_Q_RANK)), _wspec(w_uq.shape), _wspec((1, MLA_KV_RANK)),
                _wspec(w_ukv_k.shape), _wspec(w_ukv_vt.shape), _wspec((1, LANES))])
    outs = (jax.ShapeDtypeStruct((TOKENS, D_MODEL), F32),
            jax.ShapeDtypeStruct((TOKENS, 2 * A_WIDTH), BF16),
            jax.ShapeDtypeStruct((A_WIDTH, TOKENS), BF16),
            jax.ShapeDtypeStruct((TOKENS, A_WIDTH), F32),
            jax.ShapeDtypeStruct((nt, 1, A_WIDTH), F32),
            jax.ShapeDtypeStruct((TOKENS, B_HEADS * LANES), BF16),
            jax.ShapeDtypeStruct((TOKENS, B_HEADS * LANES), BF16),
            jax.ShapeDtypeStruct((B_HEADS * MLA_V, TOKENS), BF16))
    out_specs = (_tok_spec(D_MODEL), _tok_spec(2 * A_WIDTH), _tok_t_spec(A_WIDTH), _tok_spec(A_WIDTH),
                 pl.BlockSpec((1, 1, A_WIDTH), lambda i: (i, 0, 0)),
                 _tok_spec(B_HEADS * LANES), _tok_spec(B_HEADS * LANES),
                 _tok_t_spec(B_HEADS * MLA_V))
    x1, qk_a, vt_a, q_f32, k_mean, q_mla, k_mla, vt_mla = _token_call(
        "pre0", _pre0_kernel, ins, specs, outs, out_specs)

    pen = pl.pallas_call(
        _router_kernel,
        grid=(BATCH, N_KBLK),
        in_specs=[pl.BlockSpec((Q_TILE, A_WIDTH), lambda b, i: (b * N_KBLK + i, 0)),
                  pl.BlockSpec((1, N_KBLK, A_WIDTH), lambda b, i: (b, 0, 0))],
        out_specs=pl.BlockSpec((1, A_HEADS // 2 * N_KBLK, 2 * Q_TILE),
                               lambda b, i: (b * N_KBLK + i, 0, 0)),
        out_shape=jax.ShapeDtypeStruct((BATCH * N_KBLK, A_HEADS // 2 * N_KBLK, 2 * Q_TILE), F32),
        name="moba_router",
    )(q_f32, k_mean.reshape(BATCH, N_KBLK, A_WIDTH))

    bias = pl.pallas_call(
        _t5_bias_kernel,
        grid=(A_HEADS // 2,),
        in_specs=[pl.BlockSpec(memory_space=pltpu.SMEM)],
        out_specs=pl.BlockSpec((1, 3, Q_TILE, 2 * Q_TILE), lambda h: (h, 0, 0, 0)),
        out_shape=jax.ShapeDtypeStruct((A_HEADS // 2, 3, Q_TILE, 2 * Q_TILE), F32),
        name="t5_bias_tiles",
    )(t5_bias.astype(F32))

    na = A_HEADS // 2
    o_a = _attention(
        "moba", [qk_a, qk_a, vt_a, pen, bias],
        [_q_spec(LANES, 0), _kv_spec(LANES, na // PAIR_GROUP), _vt_spec(),
         pl.BlockSpec((1, PAIR_GROUP * N_KBLK, 2 * Q_TILE), lambda b, g, i: (b * N_KBLK + i, g, 0)),
         pl.BlockSpec((PAIR_GROUP, 3, Q_TILE, 2 * Q_TILE), lambda b, g, i: (g, 0, 0, 0))], na)
    o_b = _attention(
        "mla", [q_mla, k_mla, vt_mla],
        [_q_spec(2 * LANES, 0), _kv_spec(2 * LANES, 0), _vt_spec()], B_HEADS // 2)
    w_o = ab_w_out[0]
    xt = post(x1, [o_a, o_b], [w_o[:A_WIDTH], w_o[A_WIDTH:]], 0, DEPTH == 1)

    w_fox = fox_w_in[0]
    w_qk = w_fox[:, :2 * C_WIDTH].astype(BF16)
    w_vt = w_fox[:, 2 * C_WIDTH:3 * C_WIDTH].T.astype(BF16)
    w_f = jnp.pad(w_fox[:, 3 * C_WIDTH:], ((0, 0), (0, LANES - C_HEADS))).astype(BF16)
    fa, fs = ffn_args(ff1_norm[1], ff1_w_in[1], ff1_w_out[1])
    ins = [xt] + fa + [_row(mix_norm[1]), w_qk, w_vt, w_f]
    specs = ([_tok_spec(D_MODEL)] + fs
             + [_wspec((1, D_MODEL)), _wspec(w_qk.shape), _wspec(w_vt.shape), _wspec(w_f.shape)])
    outs = (jax.ShapeDtypeStruct((TOKENS, D_MODEL), F32),
            jax.ShapeDtypeStruct((TOKENS, 2 * C_WIDTH), BF16),
            jax.ShapeDtypeStruct((C_WIDTH, TOKENS), BF16),
            jax.ShapeDtypeStruct((TOKENS, LANES), F32))
    out_specs = (_tok_spec(D_MODEL), _tok_spec(2 * C_WIDTH), _tok_t_spec(C_WIDTH), _tok_spec(LANES))
    x1, qk_c, vt_c, f_gate = _token_call("pre1", _pre1_kernel, ins, specs, outs, out_specs)

    caug = pl.pallas_call(
        _fox_gate_kernel,
        grid=(BATCH, N_KBLK),
        in_specs=[pl.BlockSpec((Q_TILE, LANES), lambda b, i: (b * N_KBLK + i, 0)),
                  pl.BlockSpec((1, LANES), lambda b, i: (0, 0))],
        out_specs=pl.BlockSpec((Q_TILE, LANES), lambda b, i: (b * N_KBLK + i, 0)),
        out_shape=jax.ShapeDtypeStruct((TOKENS, LANES), BF16),
        scratch_shapes=[pltpu.VMEM((1, LANES), F32)],
        compiler_params=pltpu.CompilerParams(dimension_semantics=("arbitrary", "arbitrary")),
        name="fox_gate",
    )(f_gate, _row(fox_b_f[0], LANES))

    nc = C_HEADS // 2
    o_c = _attention(
        "fox", [qk_c, qk_c, vt_c, caug, caug],
        [_q_spec(LANES, 0), _kv_spec(LANES, nc // PAIR_GROUP), _vt_spec(),
         pl.BlockSpec((Q_TILE, LANES), lambda b, g, i: (b * N_KBLK + i, 0)),
         pl.BlockSpec((SEQ, LANES), lambda b, g, i: (b, 0))],
        nc, extra_scratch=[pltpu.VMEM((SEQ, PAIR_GROUP * 2 * LANES), BF16)])
    xt = post(x1, [o_c], [fox_w_out[0]], 1, True)
    return xt.reshape(BATCH, SEQ, D_MODEL)
```

```python
import functools
import math

import jax
import jax.numpy as jnp
import numpy as np
from jax import lax
from jax.experimental import pallas as pl
from jax.experimental.pallas import tpu as pltpu

F32 = jnp.float32
BF16 = jnp.bfloat16

D_MODEL = 1024
BATCH = 8
SEQ = 2048
DEPTH = 2
PLE_DIM = 256
D_FF = 2816
EPS = 1e-6

A_HEADS = 8
A_HEAD_DIM = 64
MOBA_BLOCK = 256
MOBA_TOPK = 3

B_HEADS = 8
MLA_Q_RANK = 256
MLA_KV_RANK = 128
MLA_NOPE = 64
MLA_ROPE = 32
MLA_V = 64
ROPE_THETA = 10000.0

T5_BUCKETS = 32
T5_MAX_DIST = 128

C_HEADS = 16
C_HEAD_DIM = 64

A_WIDTH = A_HEADS * A_HEAD_DIM
C_WIDTH = C_HEADS * C_HEAD_DIM

TOKENS = BATCH * SEQ
LANES = 128
BF16_ROWS = 16
PAIR_GROUP = 2
Q_TILE = MOBA_BLOCK
N_KBLK = SEQ // Q_TILE
TOKEN_TILE = Q_TILE
FF_CHUNK = D_FF // 2
VMEM_LIMIT = 56 * 1024 * 1024
NEG_INF = float("-inf")
LOG2E = math.log2(math.e)
QSCALE_64 = A_HEAD_DIM ** -0.5 * LOG2E
QSCALE_MLA = (MLA_NOPE + MLA_ROPE) ** -0.5 * LOG2E


def _wspec(shape):
    nd = len(shape)
    return pl.BlockSpec(shape, lambda *_: (0,) * nd, pipeline_mode=pl.Buffered(1))


def _dot(a, b):
    return jnp.dot(a, b, preferred_element_type=F32)


def _dot_nt(a, b):
    return lax.dot_general(a, b, (((1,), (1,)), ((), ())), preferred_element_type=F32)


def _rms(x, g):
    return x * lax.rsqrt(jnp.mean(x * x, axis=-1, keepdims=True) + EPS) * g


def _ffn(x, g, wa_ref, wu_ref, wo_ref):
    h = _rms(x, g).astype(BF16)
    y = jnp.zeros_like(x)
    for c in range(D_FF // FF_CHUNK):
        sl = slice(c * FF_CHUNK, (c + 1) * FF_CHUNK)
        a = _dot(h, wa_ref[:, sl])
        u = _dot(h, wu_ref[:, sl])
        act = (a * jax.nn.sigmoid(a) * u).astype(BF16)
        y = y + _dot(act, wo_ref[sl, :])
    return x + 0.5 * y


def _rope_tables(inv_lane, tile_idx):
    pos0 = (tile_idx * TOKEN_TILE) % SEQ
    pos = (pos0 + lax.broadcasted_iota(jnp.int32, (TOKEN_TILE, LANES), 0)).astype(F32)
    lane = lax.broadcasted_iota(jnp.int32, (TOKEN_TILE, LANES), 1)
    ang = pos * inv_lane
    is_x1 = (lane >= MLA_NOPE) & (lane < MLA_NOPE + MLA_ROPE // 2)
    is_x2 = (lane >= MLA_NOPE + MLA_ROPE // 2) & (lane < MLA_NOPE + MLA_ROPE)
    cos_t = jnp.where(is_x1 | is_x2, jnp.cos(ang), 1.0)
    sin = jnp.sin(ang)
    sin_t = jnp.where(is_x1, -sin, jnp.where(is_x2, sin, 0.0))
    return cos_t, sin_t, is_x1


def _rope_block(xb, cos_t, sin_t, is_x1):
    half = MLA_ROPE // 2
    partner = jnp.where(is_x1, pltpu.roll(xb, LANES - half, 1), pltpu.roll(xb, half, 1))
    return xb * cos_t + partner * sin_t


def _pre0_kernel(x_ref, g1_ref, wa_ref, wu_ref, wo_ref, gmix_ref, wqk_ref, wvt_ref, wc_ref,
                 qn_ref, wuq_ref, kvn_ref, wukvk_ref, wukvvt_ref, inv_ref,
                 x1_ref, qk_ref, vt_ref, qf_ref, km_ref, qm_ref, kmla_ref, vmt_ref):
    x1 = _ffn(x_ref[...], g1_ref[...], wa_ref, wu_ref, wo_ref)
    x1_ref[...] = x1
    h = _rms(x1, gmix_ref[...]).astype(BF16)
    qk = _dot(h, wqk_ref[...])
    q = qk[:, :A_WIDTH]
    k = qk[:, A_WIDTH:]
    qk_ref[:, :A_WIDTH] = (q * QSCALE_64).astype(BF16)
    qk_ref[:, A_WIDTH:] = k.astype(BF16)
    vt_ref[...] = _dot_nt(wvt_ref[...], h).astype(BF16)
    qf_ref[...] = q
    km_ref[0] = jnp.mean(k, axis=0, keepdims=True)

    c = _dot(h, wc_ref[...])
    cq = c[:, :MLA_Q_RANK]
    ckv = c[:, MLA_Q_RANK:MLA_Q_RANK + MLA_KV_RANK]
    kr = c[:, MLA_Q_RANK + MLA_KV_RANK:]
    cqn = _rms(cq, qn_ref[...]).astype(BF16)
    ckvn = _rms(ckv, kvn_ref[...]).astype(BF16)
    qm = _dot(cqn, wuq_ref[...])
    kn = _dot(ckvn, wukvk_ref[...])
    vmt_ref[...] = _dot_nt(wukvvt_ref[...], ckvn).astype(BF16)

    cos_t, sin_t, is_x1 = _rope_tables(inv_ref[...], pl.program_id(0))
    cos_q = cos_t * QSCALE_MLA
    sin_q = sin_t * QSCALE_MLA
    krr = _rope_block(kr, cos_t, sin_t, is_x1)
    for hb in range(B_HEADS):
        sl = slice(hb * LANES, (hb + 1) * LANES)
        qm_ref[:, sl] = _rope_block(qm[:, sl], cos_q, sin_q, is_x1).astype(BF16)
        kmla_ref[:, sl] = (kn[:, sl] + krr).astype(BF16)


def _pre1_kernel(x_ref, g1_ref, wa_ref, wu_ref, wo_ref, gmix_ref, wqk_ref, wvt_ref, wf_ref,
                 x1_ref, qk_ref, vt_ref, f_ref):
    x1 = _ffn(x_ref[...], g1_ref[...], wa_ref, wu_ref, wo_ref)
    x1_ref[...] = x1
    h = _rms(x1, gmix_ref[...]).astype(BF16)
    qk = _dot(h, wqk_ref[...])
    qk_ref[:, :C_WIDTH] = (qk[:, :C_WIDTH] * QSCALE_64).astype(BF16)
    qk_ref[:, C_WIDTH:] = qk[:, C_WIDTH:].astype(BF16)
    vt_ref[...] = _dot_nt(wvt_ref[...], h).astype(BF16)
    f_ref[...] = _dot(h, wf_ref[...])


def _tok_spec(width):
    return pl.BlockSpec((TOKEN_TILE, width), lambda i: (i, 0))


def _tok_t_spec(height):
    return pl.BlockSpec((height, TOKEN_TILE), lambda i: (0, i))


def _token_call(name, body, ins, in_specs, outs, out_specs):
    return pl.pallas_call(
        body,
        grid=(TOKENS // TOKEN_TILE,),
        in_specs=in_specs,
        out_specs=out_specs,
        out_shape=outs,
        compiler_params=pltpu.CompilerParams(
            dimension_semantics=("arbitrary",), vmem_limit_bytes=VMEM_LIMIT),
        name=name,
    )(*ins)


def _post_kernel(*refs, n_mix, final):
    x_ref = refs[0]
    o_refs = refs[1:1 + 2 * n_mix]
    w_refs = refs[1 + 2 * n_mix:1 + 3 * n_mix]
    (g2_ref, wa_ref, wu_ref, wo_ref, gple_ref, wg_ref, p_ref, wp_ref, gfin_ref,
     out_ref) = refs[1 + 3 * n_mix:]
    x = x_ref[...]
    in_lo = (pl.program_id(0) % N_KBLK) < N_KBLK // 2
    for mi, w_ref in enumerate(w_refs):
        o = jnp.where(in_lo, o_refs[2 * mi][0, 0], o_refs[2 * mi + 1][0, 0])
        x = x + _dot(o, w_ref[...])
    x = _ffn(x, g2_ref[...], wa_ref, wu_ref, wo_ref)
    gate = jax.nn.sigmoid(_dot(_rms(x, gple_ref[...]).astype(BF16), wg_ref[...]))
    x = x + gate * _dot(p_ref[...].astype(BF16), wp_ref[...])
    if final:
        x = _rms(x, gfin_ref[...])
    out_ref[...] = x


def _split_bf16(x):
    hi = x.astype(BF16)
    lo = (x - hi.astype(F32)).astype(BF16)
    return hi, lo


def _router_kernel(q_ref, km_ref, pen_ref):
    own = pl.program_id(1)
    rows = A_HEADS * N_KBLK
    km = km_ref[0]
    gt = jnp.concatenate([km] * A_HEADS, axis=0)
    r = lax.broadcasted_iota(jnp.int32, (rows, A_WIDTH), 0)
    c = lax.broadcasted_iota(jnp.int32, (rows, A_WIDTH), 1)
    gt = jnp.where((r // N_KBLK) == (c // A_HEAD_DIM), gt, 0.0)
    g_hi, g_lo = _split_bf16(gt)
    q_hi, q_lo = _split_bf16(q_ref[...])
    gate = _dot_nt(g_hi, q_hi) + _dot_nt(g_hi, q_lo) + _dot_nt(g_lo, q_hi)

    n_idx = lax.broadcasted_iota(jnp.int32, (N_KBLK, Q_TILE), 0)
    pen_rows = []
    for h in range(A_HEADS):
        gh = gate[h * N_KBLK:(h + 1) * N_KBLK]
        rank = jnp.zeros((N_KBLK, Q_TILE), jnp.int32)
        for m in range(N_KBLK):
            gm = gh[m:m + 1]
            beats = (gm > gh) | ((gm == gh) & (m < n_idx))
            rank = rank + jnp.where(beats & (m < own), 1, 0)
        sel = (n_idx < own) & (rank < MOBA_TOPK)
        pen_n = jnp.where(sel, 0.0, NEG_INF)
        pen_d = jnp.full((N_KBLK, Q_TILE), NEG_INF, F32)
        for n in range(N_KBLK):
            pen_d = jnp.where(n_idx == own - n, pen_n[n:n + 1], pen_d)
        pen_rows.append(pen_d)
    pairs = [jnp.concatenate(pen_rows[2 * hp:2 * hp + 2], axis=1) for hp in range(A_HEADS // 2)]
    pen_ref[0, 0] = jnp.concatenate(pairs, axis=0)


def _t5_bias_kernel(tbl_ref, o_ref):
    hp = pl.program_id(0)
    r = lax.broadcasted_iota(jnp.int32, (Q_TILE, Q_TILE), 0)
    c = lax.broadcasted_iota(jnp.int32, (Q_TILE, Q_TILE), 1)
    max_exact = T5_BUCKETS // 2
    for dd in range(3):
        dist = dd * Q_TILE + c - r
        dc = jnp.maximum(dist, 0)
        df = jnp.maximum(dc.astype(F32), 1.0)
        large = max_exact + (jnp.log(df / max_exact) / math.log(T5_MAX_DIST / max_exact)
                             * (T5_BUCKETS - max_exact)).astype(jnp.int32)
        large = jnp.minimum(large, T5_BUCKETS - 1)
        bucket = jnp.where(dc < max_exact, dc, large)
        for hh in range(2):
            bias = jnp.zeros((Q_TILE, Q_TILE), F32)
            for b in range(T5_BUCKETS):
                bias = jnp.where(bucket == b, tbl_ref[b, 2 * hp + hh], bias)
            bias = bias * LOG2E
            if dd == 0:
                bias = jnp.where(dist >= 0, bias, NEG_INF)
            o_ref[0, dd, :, hh * Q_TILE:(hh + 1) * Q_TILE] = bias


def _fox_gate_kernel(f_ref, b_ref, o_ref, carry_ref):
    i = pl.program_id(1)

    @pl.when(i == 0)
    def _():
        carry_ref[...] = jnp.zeros_like(carry_ref)

    lane = lax.broadcasted_iota(jnp.int32, (Q_TILE, LANES), 1)
    z = f_ref[...] + b_ref[...]
    logf = jnp.minimum(z, 0.0) - jnp.log1p(jnp.exp(-jnp.abs(z)))
    logf = jnp.where(lane < C_HEADS, logf, 0.0)
    r = lax.broadcasted_iota(jnp.int32, (Q_TILE, Q_TILE), 0)
    c = lax.broadcasted_iota(jnp.int32, (Q_TILE, Q_TILE), 1)
    tri = jnp.where(c <= r, 1.0, 0.0).astype(BF16)
    l1 = logf.astype(BF16)
    rem = logf - l1.astype(F32)
    l2 = rem.astype(BF16)
    l3 = (rem - l2.astype(F32)).astype(BF16)
    cum = _dot(tri, l1) + _dot(tri, l2) + _dot(tri, l3) + carry_ref[...]
    carry_ref[...] = cum[Q_TILE - 1:Q_TILE, :]

    cs = cum * LOG2E
    c1 = cs.astype(BF16)
    rem = cs - c1.astype(F32)
    c2 = rem.astype(BF16)
    c3 = (rem - c2.astype(F32)).astype(BF16)
    rin = lax.broadcasted_iota(jnp.int32, (LANES, LANES), 0)
    lout = lax.broadcasted_iota(jnp.int32, (LANES, LANES), 1)
    out = jnp.zeros((Q_TILE, LANES), F32)
    for part, cp in enumerate((c1, c2, c3)):
        place = jnp.where((rin < C_HEADS) & (lout == rin * 8 + part), 1.0, 0.0).astype(BF16)
        out = out + _dot(cp, place)
    o_ref[...] = out.astype(BF16)


def _attn_kernel(*refs, kind):
    if kind == "moba":
        q_lo, q_hi, k_ref, vt_ref, pen_lo, pen_hi, bias_ref, o_lo, o_hi = refs
    elif kind == "mla":
        q_lo, q_hi, k_ref, vt_ref, o_lo, o_hi = refs
    else:
        q_lo, q_hi, k_ref, vt_ref, cq_lo, cq_hi, ck_ref, o_lo, o_hi, kp_s = refs
    grp = pl.program_id(1)
    j = pl.program_id(2)
    lane = lax.broadcasted_iota(jnp.int32, (1, LANES), 1)
    low = lane < A_HEAD_DIM
    key = lax.broadcasted_iota(jnp.int32, (Q_TILE, 2 * Q_TILE), 0)
    qry = lax.broadcasted_iota(jnp.int32, (Q_TILE, 2 * Q_TILE), 1) % Q_TILE
    kw = LANES if kind == "moba" else 2 * LANES
    qw = 2 * LANES if kind == "mla" else LANES
    k_cat = kp_s if kind == "fox" else k_ref

    if kind == "fox":
        rin = lax.broadcasted_iota(jnp.int32, (LANES, LANES), 0)
        lout = lax.broadcasted_iota(jnp.int32, (LANES, LANES), 1)

        def place(pair, off_a, off_b, val):
            base_a = 2 * pair * 8
            base_b = base_a + 8
            sel_a = (lout >= off_a) & (lout < off_a + 3) & (rin == base_a + lout - off_a)
            sel_b = (lout >= off_b) & (lout < off_b + 3) & (rin == base_b + lout - off_b)
            return jnp.where(sel_a | sel_b, val, 0.0).astype(BF16)

        def ones(off_a, off_b):
            in_a = (lane >= off_a) & (lane < off_a + 3)
            in_b = (lane >= off_b) & (lane < off_b + 3)
            return jnp.where(in_a | in_b, 1.0, 0.0)

        @pl.when(j == 0)
        def _():
            for pi in range(PAIR_GROUP):
                ak = (_dot(ck_ref[...], place(grp * PAIR_GROUP + pi, A_HEAD_DIM + 3, 3, -1.0))
                      + ones(A_HEAD_DIM, 0)).astype(BF16)
                k = k_ref[:, pi * LANES:(pi + 1) * LANES]
                kp_s[:, pi * kw:pi * kw + LANES] = jnp.where(low, k, ak)
                kp_s[:, pi * kw + LANES:(pi + 1) * kw] = jnp.where(low, ak, k)

    def q_operand(q_ref, cq_ref, pi):
        zero = jnp.zeros((Q_TILE, LANES), BF16)
        q = q_ref[0, 0, :, pi * qw:(pi + 1) * qw]
        if kind == "moba":
            return jnp.concatenate([jnp.where(low, q, zero), jnp.where(low, zero, q)], axis=0)
        if kind == "fox":
            aq = (_dot(cq_ref[0, 0], place(grp * PAIR_GROUP + pi, A_HEAD_DIM, 0, 1.0))
                  + ones(A_HEAD_DIM + 3, 3)).astype(BF16)
            qa, qb = jnp.where(low, q, aq), jnp.where(low, aq, q)
        else:
            qa, qb = q[:, :LANES], q[:, LANES:]
        return jnp.concatenate([jnp.concatenate([qa, zero], axis=1),
                                jnp.concatenate([zero, qb], axis=1)], axis=0)

    def run(chains):
        ones_rows = jnp.ones((BF16_ROWS, Q_TILE), BF16)
        n = len(chains)

        def rows_of(c, d):
            nb = chains[c][1]
            return slice((nb - 1 - d) * Q_TILE, (nb - d) * Q_TILE)

        def qk(c, d):
            pi = chains[c][0]
            return _dot_nt(k_cat[rows_of(c, d), pi * kw:(pi + 1) * kw], chains[c][2])

        s_next = [qk(c, 0) for c in range(n)]
        m = [None] * n
        acc = [None] * n
        for d in range(max(c[1] for c in chains)):
            live = [c for c in range(n) if d < chains[c][1]]
            s_cur = list(s_next)
            for c in live:
                if d + 1 < chains[c][1]:
                    s_next[c] = qk(c, d + 1)
            for c in live:
                pi, _, _, pen_ref = chains[c]
                sn = s_cur[c]
                shift = None
                if kind == "moba":
                    if d < 2:
                        sn = sn + bias_ref[pi, d]
                    else:
                        shift = bias_ref[pi, 2, 0:1, :]
                    if d > 0:
                        pen = pen_ref[0, 0, pi * N_KBLK + d:pi * N_KBLK + d + 1, :]
                        shift = pen if shift is None else shift + pen
                elif d == 0:
                    sn = jnp.where(key <= qry, sn, NEG_INF)
                bm = jnp.max(sn, axis=0, keepdims=True)
                if shift is not None:
                    bm = bm + shift
                m_new = bm if d == 0 else jnp.maximum(m[c], bm)
                off = m_new if shift is None else m_new - shift
                p = jnp.exp2(sn - off).astype(BF16)
                v_aug = jnp.concatenate(
                    [vt_ref[pi * LANES:(pi + 1) * LANES, rows_of(c, d)], ones_rows], axis=0)
                pv = _dot(v_aug, p)
                acc[c] = pv if d == 0 else acc[c] * jnp.exp2(m[c] - m_new) + pv
                m[c] = m_new
        return [a[:LANES] / a[LANES:LANES + 1] for a in acc]

    for jj in range(N_KBLK // 2):
        @pl.when(j == jj)
        def _(jj=jj):
            chains = []
            for pi in range(PAIR_GROUP):
                for q_ref, cq_ref, pen_ref, nb in (
                        (q_hi, cq_hi if kind == "fox" else None,
                         pen_hi if kind == "moba" else None, N_KBLK - jj),
                        (q_lo, cq_lo if kind == "fox" else None,
                         pen_lo if kind == "moba" else None, jj + 1)):
                    chains.append((pi, nb, q_operand(q_ref, cq_ref, pi), pen_ref))
            outs = run(chains)
            for c, ot2 in enumerate(outs):
                pi = chains[c][0]
                o_ref = o_hi if c % 2 == 0 else o_lo
                ot = jnp.concatenate([ot2[:A_HEAD_DIM, :Q_TILE], ot2[A_HEAD_DIM:, Q_TILE:]], axis=0)
                o_ref[0, 0, :, pi * LANES:(pi + 1) * LANES] = ot.T.astype(BF16)


def _tiles4(a):
    return a.reshape(BATCH, N_KBLK, Q_TILE, a.shape[-1])


def _tile_pair_specs(width, col0=0):
    return [pl.BlockSpec((1, 1, Q_TILE, width), lambda b, g, j: (b, j, 0, col0 + g)),
            pl.BlockSpec((1, 1, Q_TILE, width), lambda b, g, j: (b, N_KBLK - 1 - j, 0, col0 + g))]


def _attention(kind, ins, in_specs, n_pairs, extra_scratch=()):
    width = PAIR_GROUP * LANES
    half = N_KBLK // 2
    shape = jax.ShapeDtypeStruct((BATCH, half, Q_TILE, n_pairs * LANES), BF16)
    return pl.pallas_call(
        functools.partial(_attn_kernel, kind=kind),
        grid=(BATCH, n_pairs // PAIR_GROUP, half),
        in_specs=in_specs,
        out_specs=(pl.BlockSpec((1, 1, Q_TILE, width), lambda b, g, j: (b, j, 0, g)),
                   pl.BlockSpec((1, 1, Q_TILE, width), lambda b, g, j: (b, half - 1 - j, 0, g))),
        out_shape=(shape, shape),
        scratch_shapes=list(extra_scratch),
        compiler_params=pltpu.CompilerParams(
            dimension_semantics=("arbitrary", "arbitrary", "arbitrary"),
            vmem_limit_bytes=VMEM_LIMIT),
        name="attn_" + kind,
    )(*ins)


def _kv_spec(pair_width, col0):
    return pl.BlockSpec((SEQ, PAIR_GROUP * pair_width), lambda b, g, j: (b, col0 + g))


def _vt_spec():
    return pl.BlockSpec((PAIR_GROUP * LANES, SEQ), lambda b, g, j: (g, b))


def _place_heads(w, n_heads, src_stride, src_off, width):
    out = jnp.zeros((w.shape[0], n_heads * LANES), w.dtype)
    for h in range(n_heads):
        src = w[:, h * src_stride + src_off:h * src_stride + src_off + width]
        out = out.at[:, h * LANES:h * LANES + width].set(src)
    return out


def _row(v, width=None):
    v = v.reshape(1, -1).astype(F32)
    if width is not None and v.shape[1] < width:
        v = jnp.pad(v, ((0, 0), (0, width - v.shape[1])))
    return v


def kernel(x, p, t5_bias, ff1_norm, ff1_w_in, ff1_w_out, mix_norm, ff2_norm, ff2_w_in, ff2_w_out,
           ple_norm, ple_w_gate, ple_w_proj, ab_w_in, mla_q_norm, mla_w_uq, mla_kv_norm, mla_w_ukv,
           ab_w_out, fox_w_in, fox_b_f, fox_w_out, final_norm):
    nt = TOKENS // TOKEN_TILE
    xt = x.reshape(TOKENS, D_MODEL)

    def ffn_args(norm, w_in, w_out):
        w_in = w_in.astype(BF16)
        args = [_row(norm), w_in[:, :D_FF], w_in[:, D_FF:], w_out.astype(BF16)]
        specs = [_wspec((1, D_MODEL)), _wspec((D_MODEL, D_FF)), _wspec((D_MODEL, D_FF)),
                 _wspec((D_FF, D_MODEL))]
        return args, specs

    def post(xin, mixes, w_outs, layer, final):
        fa, fs = ffn_args(ff2_norm[layer], ff2_w_in[layer], ff2_w_out[layer])
        ws = [w.astype(BF16) for w in w_outs]
        flat = [half for mix in mixes for half in mix]
        ins = ([xin] + flat + ws + fa
               + [_row(ple_norm[layer]), ple_w_gate[layer].astype(BF16),
                  p[layer].reshape(TOKENS, PLE_DIM), ple_w_proj[layer].astype(BF16),
                  _row(final_norm)])
        half = N_KBLK // 2
        mix_specs = []
        for lo, hi in mixes:
            blk = (1, 1, Q_TILE, lo.shape[-1])
            mix_specs.append(pl.BlockSpec(
                blk, lambda t: (t // N_KBLK, jnp.minimum(t % N_KBLK, half - 1), 0, 0)))
            mix_specs.append(pl.BlockSpec(
                blk, lambda t: (t // N_KBLK, jnp.maximum(t % N_KBLK - half, 0), 0, 0)))
        specs = ([_tok_spec(D_MODEL)] + mix_specs
                 + [_wspec(w.shape) for w in ws] + fs
                 + [_wspec((1, D_MODEL)), _wspec((D_MODEL, D_MODEL)), _tok_spec(PLE_DIM),
                    _wspec((PLE_DIM, D_MODEL)), _wspec((1, D_MODEL))])
        return _token_call(
            "post%d" % layer,
            functools.partial(_post_kernel, n_mix=len(mixes), final=final), ins, specs,
            jax.ShapeDtypeStruct((TOKENS, D_MODEL), F32), _tok_spec(D_MODEL))

    w_ab = ab_w_in[0]
    w_qk = w_ab[:, :2 * A_WIDTH].astype(BF16)
    w_vt = w_ab[:, 2 * A_WIDTH:3 * A_WIDTH].T.astype(BF16)
    c0 = 3 * A_WIDTH
    w_c = jnp.zeros((D_MODEL, 4 * LANES), F32)
    w_c = w_c.at[:, :MLA_Q_RANK + MLA_KV_RANK].set(w_ab[:, c0:c0 + MLA_Q_RANK + MLA_KV_RANK])
    kr0 = MLA_Q_RANK + MLA_KV_RANK
    w_c = w_c.at[:, kr0 + MLA_NOPE:kr0 + MLA_NOPE + MLA_ROPE].set(w_ab[:, c0 + kr0:])
    w_c = w_c.astype(BF16)
    w_uq = _place_heads(mla_w_uq[0], B_HEADS, MLA_NOPE + MLA_ROPE, 0, MLA_NOPE + MLA_ROPE).astype(BF16)
    w_ukv_k = _place_heads(mla_w_ukv[0], B_HEADS, MLA_NOPE + MLA_V, 0, MLA_NOPE).astype(BF16)
    w_ukv = mla_w_ukv[0].reshape(MLA_KV_RANK, B_HEADS, MLA_NOPE + MLA_V)
    w_ukv_vt = w_ukv[:, :, MLA_NOPE:].reshape(MLA_KV_RANK, B_HEADS * MLA_V).T.astype(BF16)
    half = MLA_ROPE // 2
    inv = ROPE_THETA ** (-np.arange(half, dtype=np.float64) / half)
    inv_lane = np.zeros((1, LANES), np.float32)
    inv_lane[0, MLA_NOPE:MLA_NOPE + half] = inv
    inv_lane[0, MLA_NOPE + half:MLA_NOPE + MLA_ROPE] = inv
    inv_lane = jnp.asarray(inv_lane)

    fa, fs = ffn_args(ff1_norm[0], ff1_w_in[0], ff1_w_out[0])
    ins = ([xt] + fa + [_row(mix_norm[0]), w_qk, w_vt, w_c, _row(mla_q_norm[0]), w_uq,
                        _row(mla_kv_norm[0]), w_ukv_k, w_ukv_vt, inv_lane])
    specs = ([_tok_spec(D_MODEL)] + fs
             + [_wspec((1, D_MODEL)), _wspec(w_qk.shape), _wspec(w_vt.shape), _wspec(w_c.shape),
                _wspec((1, MLA_Q_RANK)), _wspec(w_uq.shape), _wspec((1, MLA_KV_RANK)),
                _wspec(w_ukv_k.shape), _wspec(w_ukv_vt.shape), _wspec((1, LANES))])
    outs = (jax.ShapeDtypeStruct((TOKENS, D_MODEL), F32),
            jax.ShapeDtypeStruct((TOKENS, 2 * A_WIDTH), BF16),
            jax.ShapeDtypeStruct((A_WIDTH, TOKENS), BF16),
            jax.ShapeDtypeStruct((TOKENS, A_WIDTH), F32),
            jax.ShapeDtypeStruct((nt, 1, A_WIDTH), F32),
            jax.ShapeDtypeStruct((TOKENS, B_HEADS * LANES), BF16),
            jax.ShapeDtypeStruct((TOKENS, B_HEADS * LANES), BF16),
            jax.ShapeDtypeStruct((B_HEADS * MLA_V, TOKENS), BF16))
    out_specs = (_tok_spec(D_MODEL), _tok_spec(2 * A_WIDTH), _tok_t_spec(A_WIDTH), _tok_spec(A_WIDTH),
                 pl.BlockSpec((1, 1, A_WIDTH), lambda i: (i, 0, 0)),
                 _tok_spec(B_HEADS * LANES), _tok_spec(B_HEADS * LANES),
                 _tok_t_spec(B_HEADS * MLA_V))
    x1, qk_a, vt_a, q_f32, k_mean, q_mla, k_mla, vt_mla = _token_call(
        "pre0", _pre0_kernel, ins, specs, outs, out_specs)

    pen = pl.pallas_call(
        _router_kernel,
        grid=(BATCH, N_KBLK),
        in_specs=[pl.BlockSpec((Q_TILE, A_WIDTH), lambda b, i: (b * N_KBLK + i, 0)),
                  pl.BlockSpec((1, N_KBLK, A_WIDTH), lambda b, i: (b, 0, 0))],
        out_specs=pl.BlockSpec((1, 1, A_HEADS // 2 * N_KBLK, 2 * Q_TILE),
                               lambda b, i: (b, i, 0, 0)),
        out_shape=jax.ShapeDtypeStruct((BATCH, N_KBLK, A_HEADS // 2 * N_KBLK, 2 * Q_TILE), F32),
        name="moba_router",
    )(q_f32, k_mean.reshape(BATCH, N_KBLK, A_WIDTH))

    bias = pl.pallas_call(
        _t5_bias_kernel,
        grid=(A_HEADS // 2,),
        in_specs=[pl.BlockSpec(memory_space=pltpu.SMEM)],
        out_specs=pl.BlockSpec((1, 3, Q_TILE, 2 * Q_TILE), lambda h: (h, 0, 0, 0)),
        out_shape=jax.ShapeDtypeStruct((A_HEADS // 2, 3, Q_TILE, 2 * Q_TILE), F32),
        name="t5_bias_tiles",
    )(t5_bias.astype(F32))

    na = A_HEADS // 2
    o_a = _attention(
        "moba", [_tiles4(qk_a), _tiles4(qk_a), qk_a, vt_a, pen, pen, bias],
        _tile_pair_specs(PAIR_GROUP * LANES)
        + [_kv_spec(LANES, na // PAIR_GROUP), _vt_spec()]
        + [pl.BlockSpec((1, 1, PAIR_GROUP * N_KBLK, 2 * Q_TILE), lambda b, g, j: (b, j, g, 0)),
           pl.BlockSpec((1, 1, PAIR_GROUP * N_KBLK, 2 * Q_TILE),
                        lambda b, g, j: (b, N_KBLK - 1 - j, g, 0)),
           pl.BlockSpec((PAIR_GROUP, 3, Q_TILE, 2 * Q_TILE), lambda b, g, j: (g, 0, 0, 0))], na)
    o_b = _attention(
        "mla", [_tiles4(q_mla), _tiles4(q_mla), k_mla, vt_mla],
        _tile_pair_specs(PAIR_GROUP * 2 * LANES) + [_kv_spec(2 * LANES, 0), _vt_spec()],
        B_HEADS // 2)
    w_o = ab_w_out[0]
    xt = post(x1, [o_a, o_b], [w_o[:A_WIDTH], w_o[A_WIDTH:]], 0, DEPTH == 1)

    w_fox = fox_w_in[0]
    w_qk = w_fox[:, :2 * C_WIDTH].astype(BF16)
    w_vt = w_fox[:, 2 * C_WIDTH:3 * C_WIDTH].T.astype(BF16)
    w_f = jnp.pad(w_fox[:, 3 * C_WIDTH:], ((0, 0), (0, LANES - C_HEADS))).astype(BF16)
    fa, fs = ffn_args(ff1_norm[1], ff1_w_in[1], ff1_w_out[1])
    ins = [xt] + fa + [_row(mix_norm[1]), w_qk, w_vt, w_f]
    specs = ([_tok_spec(D_MODEL)] + fs
             + [_wspec((1, D_MODEL)), _wspec(w_qk.shape), _wspec(w_vt.shape), _wspec(w_f.shape)])
    outs = (jax.ShapeDtypeStruct((TOKENS, D_MODEL), F32),
            jax.ShapeDtypeStruct((TOKENS, 2 * C_WIDTH), BF16),
            jax.ShapeDtypeStruct((C_WIDTH, TOKENS), BF16),
            jax.ShapeDtypeStruct((TOKENS, LANES), F32))
    out_specs = (_tok_spec(D_MODEL), _tok_spec(2 * C_WIDTH), _tok_t_spec(C_WIDTH), _tok_spec(LANES))
    x1, qk_c, vt_c, f_gate = _token_call("pre1", _pre1_kernel, ins, specs, outs, out_specs)

    caug = pl.pallas_call(
        _fox_gate_kernel,
        grid=(BATCH, N_KBLK),
        in_specs=[pl.BlockSpec((Q_TILE, LANES), lambda b, i: (b * N_KBLK + i, 0)),
                  pl.BlockSpec((1, LANES), lambda b, i: (0, 0))],
        out_specs=pl.BlockSpec((Q_TILE, LANES), lambda b, i: (b * N_KBLK + i, 0)),
        out_shape=jax.ShapeDtypeStruct((TOKENS, LANES), BF16),
        scratch_shapes=[pltpu.VMEM((1, LANES), F32)],
        compiler_params=pltpu.CompilerParams(dimension_semantics=("arbitrary", "arbitrary")),
        name="fox_gate",
    )(f_gate, _row(fox_b_f[0], LANES))

    nc = C_HEADS // 2
    o_c = _attention(
        "fox", [_tiles4(qk_c), _tiles4(qk_c), qk_c, vt_c, _tiles4(caug), _tiles4(caug), caug],
        _tile_pair_specs(PAIR_GROUP * LANES)
        + [_kv_spec(LANES, nc // PAIR_GROUP), _vt_spec(),
           pl.BlockSpec((1, 1, Q_TILE, LANES), lambda b, g, j: (b, j, 0, 0)),
           pl.BlockSpec((1, 1, Q_TILE, LANES), lambda b, g, j: (b, N_KBLK - 1 - j, 0, 0)),
           pl.BlockSpec((SEQ, LANES), lambda b, g, j: (b, 0))],
        nc, extra_scratch=[pltpu.VMEM((SEQ, PAIR_GROUP * 2 * LANES), BF16)])
    xt = post(x1, [o_c], [fox_w_out[0]], 1, True)
    return xt.reshape(BATCH, SEQ, D_MODEL)
```

```python
import functools
import math

import jax
import jax.numpy as jnp
import numpy as np
from jax import lax
from jax.experimental import pallas as pl
from jax.experimental.pallas import tpu as pltpu

F32 = jnp.float32
BF16 = jnp.bfloat16

D_MODEL = 1024
BATCH = 8
SEQ = 2048
DEPTH = 2
PLE_DIM = 256
D_FF = 2816
EPS = 1e-6

A_HEADS = 8
A_HEAD_DIM = 64
MOBA_BLOCK = 256
MOBA_TOPK = 3

B_HEADS = 8
MLA_Q_RANK = 256
MLA_KV_RANK = 128
MLA_NOPE = 64
MLA_ROPE = 32
MLA_V = 64
ROPE_THETA = 10000.0

T5_BUCKETS = 32
T5_MAX_DIST = 128

C_HEADS = 16
C_HEAD_DIM = 64

A_WIDTH = A_HEADS * A_HEAD_DIM
C_WIDTH = C_HEADS * C_HEAD_DIM

TOKENS = BATCH * SEQ
LANES = 128
BF16_ROWS = 16
PAIR_GROUP = 2
Q_TILE = MOBA_BLOCK
N_KBLK = SEQ // Q_TILE
TOKEN_TILE = Q_TILE
FF_CHUNK = D_FF // 2
VMEM_LIMIT = 56 * 1024 * 1024
NEG_INF = float("-inf")
LOG2E = math.log2(math.e)
QSCALE_64 = A_HEAD_DIM ** -0.5 * LOG2E
QSCALE_MLA = (MLA_NOPE + MLA_ROPE) ** -0.5 * LOG2E


def _wspec(shape):
    nd = len(shape)
    return pl.BlockSpec(shape, lambda *_: (0,) * nd, pipeline_mode=pl.Buffered(1))


def _dot(a, b):
    return jnp.dot(a, b, preferred_element_type=F32)


def _dot_nt(a, b):
    return lax.dot_general(a, b, (((1,), (1,)), ((), ())), preferred_element_type=F32)


def _rms(x, g):
    return x * lax.rsqrt(jnp.mean(x * x, axis=-1, keepdims=True) + EPS) * g


def _ffn(x, g, win_ref, wo_ref):
    h = _rms(x, g).astype(BF16)
    y = jnp.zeros_like(x)
    for c in range(D_FF // FF_CHUNK):
        lo, hi = c * FF_CHUNK, (c + 1) * FF_CHUNK
        a = _dot(h, win_ref[:, lo:hi])
        u = _dot(h, win_ref[:, D_FF + lo:D_FF + hi])
        act = (a * jax.nn.sigmoid(a) * u).astype(BF16)
        y = y + _dot(act, wo_ref[lo:hi, :])
    return x + 0.5 * y


def _rope_tables(inv_lane, tile_idx):
    pos0 = (tile_idx * TOKEN_TILE) % SEQ
    pos = (pos0 + lax.broadcasted_iota(jnp.int32, (TOKEN_TILE, LANES), 0)).astype(F32)
    lane = lax.broadcasted_iota(jnp.int32, (TOKEN_TILE, LANES), 1)
    ang = pos * inv_lane
    is_x1 = (lane >= MLA_NOPE) & (lane < MLA_NOPE + MLA_ROPE // 2)
    is_x2 = (lane >= MLA_NOPE + MLA_ROPE // 2) & (lane < MLA_NOPE + MLA_ROPE)
    cos_t = jnp.where(is_x1 | is_x2, jnp.cos(ang), 1.0)
    sin = jnp.sin(ang)
    sin_t = jnp.where(is_x1, -sin, jnp.where(is_x2, sin, 0.0))
    return cos_t, sin_t, is_x1


def _rope_block(xb, cos_t, sin_t, is_x1):
    half = MLA_ROPE // 2
    partner = jnp.where(is_x1, pltpu.roll(xb, LANES - half, 1), pltpu.roll(xb, half, 1))
    return xb * cos_t + partner * sin_t


def _pre0_kernel(x_ref, g1_ref, win_ref, wo_ref, gmix_ref, wqk_ref, wvt_ref, wc_ref,
                 qn_ref, wuq_ref, kvn_ref, wukvk_ref, wukvvt_ref, inv_ref,
                 x1_ref, qk_ref, vt_ref, pen_ref, qm_ref, kmla_ref, vmt_ref, km_s):
    t = pl.program_id(0)
    x1 = _ffn(x_ref[...], g1_ref[...], win_ref, wo_ref)
    x1_ref[...] = x1
    h = _rms(x1, gmix_ref[...]).astype(BF16)
    qk = _dot(h, wqk_ref[...])
    q = qk[:, :A_WIDTH]
    k = qk[:, A_WIDTH:]
    qk_ref[:, :A_WIDTH] = (q * QSCALE_64).astype(BF16)
    qk_ref[:, A_WIDTH:] = k.astype(BF16)

    @pl.when(t == 0)
    def _():
        km_s[...] = jnp.zeros_like(km_s)

    own = t % N_KBLK
    km_s[pl.ds(own, 1), :] = jnp.mean(k, axis=0, keepdims=True)
    gate = _moba_gate(q, km_s[...])
    vt_ref[...] = _dot_nt(wvt_ref[...], h).astype(BF16)
    pen_ref[0, 0] = _moba_select(gate, own)

    c = _dot(h, wc_ref[...])
    cq = c[:, :MLA_Q_RANK]
    ckv = c[:, MLA_Q_RANK:MLA_Q_RANK + MLA_KV_RANK]
    kr = c[:, MLA_Q_RANK + MLA_KV_RANK:]
    cqn = _rms(cq, qn_ref[...]).astype(BF16)
    ckvn = _rms(ckv, kvn_ref[...]).astype(BF16)
    qm = _dot(cqn, wuq_ref[...])
    kn = _dot(ckvn, wukvk_ref[...])
    vmt_ref[...] = _dot_nt(wukvvt_ref[...], ckvn).astype(BF16)

    cos_t, sin_t, is_x1 = _rope_tables(inv_ref[...], t)
    cos_q = cos_t * QSCALE_MLA
    sin_q = sin_t * QSCALE_MLA
    krr = _rope_block(kr, cos_t, sin_t, is_x1)
    for hb in range(B_HEADS):
        sl = slice(hb * LANES, (hb + 1) * LANES)
        qm_ref[:, sl] = _rope_block(qm[:, sl], cos_q, sin_q, is_x1).astype(BF16)
        kmla_ref[:, sl] = (kn[:, sl] + krr).astype(BF16)


def _pre1_kernel(x_ref, g1_ref, win_ref, wo_ref, gmix_ref, wqk_ref, wvt_ref, wf_ref, bf_ref,
                 x1_ref, qk_ref, vt_ref, caug_ref, carry_s):
    t = pl.program_id(0)
    x1 = _ffn(x_ref[...], g1_ref[...], win_ref, wo_ref)
    x1_ref[...] = x1
    h = _rms(x1, gmix_ref[...]).astype(BF16)

    @pl.when(t % N_KBLK == 0)
    def _():
        carry_s[...] = jnp.zeros_like(carry_s)

    z = _dot(h, wf_ref[...]) + bf_ref[...]
    qk = _dot(h, wqk_ref[...])
    qk_ref[:, :C_WIDTH] = (qk[:, :C_WIDTH] * QSCALE_64).astype(BF16)
    qk_ref[:, C_WIDTH:] = qk[:, C_WIDTH:].astype(BF16)
    cum = _fox_cumsum(z, carry_s)
    vt_ref[...] = _dot_nt(wvt_ref[...], h).astype(BF16)
    caug_ref[...] = _fox_decay_parts(cum)


def _tok_spec(width):
    return pl.BlockSpec((TOKEN_TILE, width), lambda i: (i, 0))


def _tok_t_spec(height):
    return pl.BlockSpec((height, TOKEN_TILE), lambda i: (0, i))


def _token_call(name, body, ins, in_specs, outs, out_specs, scratch=()):
    return pl.pallas_call(
        body,
        grid=(TOKENS // TOKEN_TILE,),
        in_specs=in_specs,
        out_specs=out_specs,
        out_shape=outs,
        scratch_shapes=list(scratch),
        compiler_params=pltpu.CompilerParams(
            dimension_semantics=("arbitrary",), vmem_limit_bytes=VMEM_LIMIT),
        name=name,
    )(*ins)


def _post_kernel(*refs, n_mix, final):
    x_ref = refs[0]
    o_refs = refs[1:1 + 2 * n_mix]
    w_refs = refs[1 + 2 * n_mix:1 + 3 * n_mix]
    (g2_ref, win_ref, wo_ref, gple_ref, wg_ref, p_ref, wp_ref, gfin_ref,
     out_ref) = refs[1 + 3 * n_mix:]
    x = x_ref[...]
    in_lo = (pl.program_id(0) % N_KBLK) < N_KBLK // 2
    for mi, w_ref in enumerate(w_refs):
        o = jnp.where(in_lo, o_refs[2 * mi][0, 0], o_refs[2 * mi + 1][0, 0])
        x = x + _dot(o, w_ref[...])
    x = _ffn(x, g2_ref[...], win_ref, wo_ref)
    gate = jax.nn.sigmoid(_dot(_rms(x, gple_ref[...]).astype(BF16), wg_ref[...]))
    x = x + gate * _dot(p_ref[...].astype(BF16), wp_ref[...])
    if final:
        x = _rms(x, gfin_ref[...])
    out_ref[...] = x


def _split_bf16(x):
    hi = x.astype(BF16)
    lo = (x - hi.astype(F32)).astype(BF16)
    return hi, lo


def _moba_gate(q, km):
    rows = A_HEADS * N_KBLK
    gt = jnp.concatenate([km] * A_HEADS, axis=0)
    r = lax.broadcasted_iota(jnp.int32, (rows, A_WIDTH), 0)
    c = lax.broadcasted_iota(jnp.int32, (rows, A_WIDTH), 1)
    gt = jnp.where((r // N_KBLK) == (c // A_HEAD_DIM), gt, 0.0)
    g_hi, g_lo = _split_bf16(gt)
    q_hi, q_lo = _split_bf16(q)
    return _dot_nt(g_hi, q_hi) + _dot_nt(g_hi, q_lo) + _dot_nt(g_lo, q_hi)


def _moba_select(gate, own):
    n_idx = lax.broadcasted_iota(jnp.int32, (N_KBLK, Q_TILE), 0)
    pen_rows = []
    for h in range(A_HEADS):
        gh = gate[h * N_KBLK:(h + 1) * N_KBLK]
        rank = jnp.zeros((N_KBLK, Q_TILE), jnp.int32)
        for m in range(N_KBLK):
            gm = gh[m:m + 1]
            beats = (gm > gh) | ((gm == gh) & (m < n_idx))
            rank = rank + jnp.where(beats & (m < own), 1, 0)
        sel = (n_idx < own) & (rank < MOBA_TOPK)
        pen_n = jnp.where(sel, 0.0, NEG_INF)
        pen_d = jnp.full((N_KBLK, Q_TILE), NEG_INF, F32)
        for n in range(N_KBLK):
            pen_d = jnp.where(n_idx == own - n, pen_n[n:n + 1], pen_d)
        pen_rows.append(pen_d)
    pairs = [jnp.concatenate(pen_rows[2 * hp:2 * hp + 2], axis=1) for hp in range(A_HEADS // 2)]
    return jnp.concatenate(pairs, axis=0)


def _t5_bias_kernel(tbl_ref, o_ref):
    hp = pl.program_id(0)
    r = lax.broadcasted_iota(jnp.int32, (Q_TILE, Q_TILE), 0)
    c = lax.broadcasted_iota(jnp.int32, (Q_TILE, Q_TILE), 1)
    max_exact = T5_BUCKETS // 2
    for dd in range(3):
        dist = dd * Q_TILE + c - r
        dc = jnp.maximum(dist, 0)
        df = jnp.maximum(dc.astype(F32), 1.0)
        large = max_exact + (jnp.log(df / max_exact) / math.log(T5_MAX_DIST / max_exact)
                             * (T5_BUCKETS - max_exact)).astype(jnp.int32)
        large = jnp.minimum(large, T5_BUCKETS - 1)
        bucket = jnp.where(dc < max_exact, dc, large)
        for hh in range(2):
            bias = jnp.zeros((Q_TILE, Q_TILE), F32)
            for b in range(T5_BUCKETS):
                bias = jnp.where(bucket == b, tbl_ref[b, 2 * hp + hh], bias)
            bias = bias * LOG2E
            if dd == 0:
                bias = jnp.where(dist >= 0, bias, NEG_INF)
            o_ref[0, dd, :, hh * Q_TILE:(hh + 1) * Q_TILE] = bias


def _fox_cumsum(z, carry_ref):
    lane = lax.broadcasted_iota(jnp.int32, (Q_TILE, LANES), 1)
    logf = jnp.minimum(z, 0.0) - jnp.log1p(jnp.exp(-jnp.abs(z)))
    logf = jnp.where(lane < C_HEADS, logf, 0.0)
    r = lax.broadcasted_iota(jnp.int32, (Q_TILE, Q_TILE), 0)
    c = lax.broadcasted_iota(jnp.int32, (Q_TILE, Q_TILE), 1)
    tri = jnp.where(c <= r, 1.0, 0.0).astype(BF16)
    l1 = logf.astype(BF16)
    rem = logf - l1.astype(F32)
    l2 = rem.astype(BF16)
    l3 = (rem - l2.astype(F32)).astype(BF16)
    cum = _dot(tri, l1) + _dot(tri, l2) + _dot(tri, l3) + carry_ref[...]
    carry_ref[...] = cum[Q_TILE - 1:Q_TILE, :]
    return cum


def _fox_decay_parts(cum):
    cs = cum * LOG2E
    c1 = cs.astype(BF16)
    rem = cs - c1.astype(F32)
    c2 = rem.astype(BF16)
    c3 = (rem - c2.astype(F32)).astype(BF16)
    rin = lax.broadcasted_iota(jnp.int32, (LANES, LANES), 0)
    lout = lax.broadcasted_iota(jnp.int32, (LANES, LANES), 1)
    out = jnp.zeros((Q_TILE, LANES), F32)
    for part, cp in enumerate((c1, c2, c3)):
        place = jnp.where((rin < C_HEADS) & (lout == rin * 8 + part), 1.0, 0.0).astype(BF16)
        out = out + _dot(cp, place)
    return out.astype(BF16)


def _attn_kernel(*refs, kind):
    if kind == "moba":
        q_lo, q_hi, k_ref, vt_ref, pen_lo, pen_hi, bias_ref, o_lo, o_hi = refs
    elif kind == "mla":
        q_lo, q_hi, k_ref, vt_ref, o_lo, o_hi = refs
    else:
        q_lo, q_hi, k_ref, vt_ref, cq_lo, cq_hi, ck_ref, o_lo, o_hi, kp_s = refs
    grp = pl.program_id(1)
    j = pl.program_id(2)
    lane = lax.broadcasted_iota(jnp.int32, (1, LANES), 1)
    low = lane < A_HEAD_DIM
    key = lax.broadcasted_iota(jnp.int32, (Q_TILE, 2 * Q_TILE), 0)
    qry = lax.broadcasted_iota(jnp.int32, (Q_TILE, 2 * Q_TILE), 1) % Q_TILE
    kw = LANES if kind == "moba" else 2 * LANES
    qw = 2 * LANES if kind == "mla" else LANES
    k_cat = kp_s if kind == "fox" else k_ref

    if kind == "fox":
        rin = lax.broadcasted_iota(jnp.int32, (LANES, LANES), 0)
        lout = lax.broadcasted_iota(jnp.int32, (LANES, LANES), 1)

        def place(pair, off_a, off_b, val):
            base_a = 2 * pair * 8
            base_b = base_a + 8
            sel_a = (lout >= off_a) & (lout < off_a + 3) & (rin == base_a + lout - off_a)
            sel_b = (lout >= off_b) & (lout < off_b + 3) & (rin == base_b + lout - off_b)
            return jnp.where(sel_a | sel_b, val, 0.0).astype(BF16)

        def ones(off_a, off_b):
            in_a = (lane >= off_a) & (lane < off_a + 3)
            in_b = (lane >= off_b) & (lane < off_b + 3)
            return jnp.where(in_a | in_b, 1.0, 0.0)

        @pl.when(j == 0)
        def _():
            for pi in range(PAIR_GROUP):
                ak = (_dot(ck_ref[...], place(grp * PAIR_GROUP + pi, A_HEAD_DIM + 3, 3, -1.0))
                      + ones(A_HEAD_DIM, 0)).astype(BF16)
                k = k_ref[:, pi * LANES:(pi + 1) * LANES]
                kp_s[:, pi * kw:pi * kw + LANES] = jnp.where(low, k, ak)
                kp_s[:, pi * kw + LANES:(pi + 1) * kw] = jnp.where(low, ak, k)

    def q_operand(q_ref, cq_ref, pi):
        zero = jnp.zeros((Q_TILE, LANES), BF16)
        q = q_ref[0, 0, :, pi * qw:(pi + 1) * qw]
        if kind == "moba":
            return jnp.concatenate([jnp.where(low, q, zero), jnp.where(low, zero, q)], axis=0)
        if kind == "fox":
            aq = (_dot(cq_ref[0, 0], place(grp * PAIR_GROUP + pi, A_HEAD_DIM, 0, 1.0))
                  + ones(A_HEAD_DIM + 3, 3)).astype(BF16)
            qa, qb = jnp.where(low, q, aq), jnp.where(low, aq, q)
        else:
            qa, qb = q[:, :LANES], q[:, LANES:]
        return jnp.concatenate([jnp.concatenate([qa, zero], axis=1),
                                jnp.concatenate([zero, qb], axis=1)], axis=0)

    def run(chains):
        ones_rows = jnp.ones((BF16_ROWS, Q_TILE), BF16)
        n = len(chains)

        def rows_of(c, d):
            nb = chains[c][1]
            return slice((nb - 1 - d) * Q_TILE, (nb - d) * Q_TILE)

        def qk(c, d):
            pi = chains[c][0]
            return _dot_nt(k_cat[rows_of(c, d), pi * kw:(pi + 1) * kw], chains[c][2])

        s_next = [qk(c, 0) for c in range(n)]
        m = [None] * n
        acc = [None] * n
        for d in range(max(c[1] for c in chains)):
            live = [c for c in range(n) if d < chains[c][1]]
            s_cur = list(s_next)
            for c in live:
                if d + 1 < chains[c][1]:
                    s_next[c] = qk(c, d + 1)
            for c in live:
                pi, _, _, pen_ref = chains[c]
                sn = s_cur[c]
                shift = None
                if kind == "moba":
                    if d < 2:
                        sn = sn + bias_ref[pi, d]
                    else:
                        shift = bias_ref[pi, 2, 0:1, :]
                    if d > 0:
                        pen = pen_ref[0, 0, pi * N_KBLK + d:pi * N_KBLK + d + 1, :]
                        shift = pen if shift is None else shift + pen
                elif d == 0:
                    sn = jnp.where(key <= qry, sn, NEG_INF)
                bm = jnp.max(sn, axis=0, keepdims=True)
                if shift is not None:
                    bm = bm + shift
                m_new = bm if d == 0 else jnp.maximum(m[c], bm)
                off = m_new if shift is None else m_new - shift
                p = jnp.exp2(sn - off).astype(BF16)
                v_aug = jnp.concatenate(
                    [vt_ref[pi * LANES:(pi + 1) * LANES, rows_of(c, d)], ones_rows], axis=0)
                pv = _dot(v_aug, p)
                acc[c] = pv if d == 0 else acc[c] * jnp.exp2(m[c] - m_new) + pv
                m[c] = m_new
        return [a[:LANES] / a[LANES:LANES + 1] for a in acc]

    for jj in range(N_KBLK // 2):
        @pl.when(j == jj)
        def _(jj=jj):
            chains = []
            for pi in range(PAIR_GROUP):
                for q_ref, cq_ref, pen_ref, nb in (
                        (q_hi, cq_hi if kind == "fox" else None,
                         pen_hi if kind == "moba" else None, N_KBLK - jj),
                        (q_lo, cq_lo if kind == "fox" else None,
                         pen_lo if kind == "moba" else None, jj + 1)):
                    chains.append((pi, nb, q_operand(q_ref, cq_ref, pi), pen_ref))
            outs = run(chains)
            for c, ot2 in enumerate(outs):
                pi = chains[c][0]
                o_ref = o_hi if c % 2 == 0 else o_lo
                ot = jnp.concatenate([ot2[:A_HEAD_DIM, :Q_TILE], ot2[A_HEAD_DIM:, Q_TILE:]], axis=0)
                o_ref[0, 0, :, pi * LANES:(pi + 1) * LANES] = ot.T.astype(BF16)


def _tiles4(a):
    return a.reshape(BATCH, N_KBLK, Q_TILE, a.shape[-1])


def _tile_pair_specs(width, col0=0):
    return [pl.BlockSpec((1, 1, Q_TILE, width), lambda b, g, j: (b, j, 0, col0 + g)),
            pl.BlockSpec((1, 1, Q_TILE, width), lambda b, g, j: (b, N_KBLK - 1 - j, 0, col0 + g))]


def _attention(kind, ins, in_specs, n_pairs, extra_scratch=()):
    width = PAIR_GROUP * LANES
    half = N_KBLK // 2
    shape = jax.ShapeDtypeStruct((BATCH, half, Q_TILE, n_pairs * LANES), BF16)
    return pl.pallas_call(
        functools.partial(_attn_kernel, kind=kind),
        grid=(BATCH, n_pairs // PAIR_GROUP, half),
        in_specs=in_specs,
        out_specs=(pl.BlockSpec((1, 1, Q_TILE, width), lambda b, g, j: (b, j, 0, g)),
                   pl.BlockSpec((1, 1, Q_TILE, width), lambda b, g, j: (b, half - 1 - j, 0, g))),
        out_shape=(shape, shape),
        scratch_shapes=list(extra_scratch),
        compiler_params=pltpu.CompilerParams(
            dimension_semantics=("arbitrary", "arbitrary", "arbitrary"),
            vmem_limit_bytes=VMEM_LIMIT),
        name="attn_" + kind,
    )(*ins)


def _kv_spec(pair_width, col0):
    return pl.BlockSpec((SEQ, PAIR_GROUP * pair_width), lambda b, g, j: (b, col0 + g))


def _vt_spec():
    return pl.BlockSpec((PAIR_GROUP * LANES, SEQ), lambda b, g, j: (g, b))


def _place_heads(w, n_heads, src_stride, src_off, width):
    per_head = w.reshape(w.shape[0], n_heads, src_stride)[:, :, src_off:src_off + width]
    per_head = jnp.pad(per_head, ((0, 0), (0, 0), (0, LANES - width)))
    return per_head.reshape(w.shape[0], n_heads * LANES)


def _row(v, width=None):
    v = v.reshape(1, -1).astype(F32)
    if width is not None and v.shape[1] < width:
        v = jnp.pad(v, ((0, 0), (0, width - v.shape[1])))
    return v


def kernel(x, p, t5_bias, ff1_norm, ff1_w_in, ff1_w_out, mix_norm, ff2_norm, ff2_w_in, ff2_w_out,
           ple_norm, ple_w_gate, ple_w_proj, ab_w_in, mla_q_norm, mla_w_uq, mla_kv_norm, mla_w_ukv,
           ab_w_out, fox_w_in, fox_b_f, fox_w_out, final_norm):
    xt = x.reshape(TOKENS, D_MODEL)

    def ffn_args(norm, w_in, w_out):
        args = [_row(norm), w_in.astype(BF16), w_out.astype(BF16)]
        specs = [_wspec((1, D_MODEL)), _wspec((D_MODEL, 2 * D_FF)), _wspec((D_FF, D_MODEL))]
        return args, specs

    def post(xin, mixes, w_outs, layer, final):
        fa, fs = ffn_args(ff2_norm[layer], ff2_w_in[layer], ff2_w_out[layer])
        ws = [w.astype(BF16) for w in w_outs]
        flat = [half for mix in mixes for half in mix]
        ins = ([xin] + flat + ws + fa
               + [_row(ple_norm[layer]), ple_w_gate[layer].astype(BF16),
                  p[layer].reshape(TOKENS, PLE_DIM), ple_w_proj[layer].astype(BF16),
                  _row(final_norm)])
        half = N_KBLK // 2
        mix_specs = []
        for lo, hi in mixes:
            blk = (1, 1, Q_TILE, lo.shape[-1])
            mix_specs.append(pl.BlockSpec(
                blk, lambda t: (t // N_KBLK, jnp.minimum(t % N_KBLK, half - 1), 0, 0)))
            mix_specs.append(pl.BlockSpec(
                blk, lambda t: (t // N_KBLK, jnp.maximum(t % N_KBLK - half, 0), 0, 0)))
        specs = ([_tok_spec(D_MODEL)] + mix_specs
                 + [_wspec(w.shape) for w in ws] + fs
                 + [_wspec((1, D_MODEL)), _wspec((D_MODEL, D_MODEL)), _tok_spec(PLE_DIM),
                    _wspec((PLE_DIM, D_MODEL)), _wspec((1, D_MODEL))])
        return _token_call(
            "post%d" % layer,
            functools.partial(_post_kernel, n_mix=len(mixes), final=final), ins, specs,
            jax.ShapeDtypeStruct((TOKENS, D_MODEL), F32), _tok_spec(D_MODEL))

    w_ab = ab_w_in[0]
    w_qk = w_ab[:, :2 * A_WIDTH].astype(BF16)
    w_vt = w_ab[:, 2 * A_WIDTH:3 * A_WIDTH].T.astype(BF16)
    c0 = 3 * A_WIDTH
    kr0 = MLA_Q_RANK + MLA_KV_RANK
    w_c = jnp.concatenate(
        [w_ab[:, c0:c0 + kr0], jnp.zeros((D_MODEL, MLA_NOPE), F32), w_ab[:, c0 + kr0:],
         jnp.zeros((D_MODEL, LANES - MLA_NOPE - MLA_ROPE), F32)], axis=1).astype(BF16)
    w_uq = _place_heads(mla_w_uq[0], B_HEADS, MLA_NOPE + MLA_ROPE, 0, MLA_NOPE + MLA_ROPE).astype(BF16)
    w_ukv_k = _place_heads(mla_w_ukv[0], B_HEADS, MLA_NOPE + MLA_V, 0, MLA_NOPE).astype(BF16)
    w_ukv = mla_w_ukv[0].reshape(MLA_KV_RANK, B_HEADS, MLA_NOPE + MLA_V)
    w_ukv_vt = w_ukv[:, :, MLA_NOPE:].reshape(MLA_KV_RANK, B_HEADS * MLA_V).T.astype(BF16)
    half = MLA_ROPE // 2
    inv = ROPE_THETA ** (-np.arange(half, dtype=np.float64) / half)
    inv_lane = np.zeros((1, LANES), np.float32)
    inv_lane[0, MLA_NOPE:MLA_NOPE + half] = inv
    inv_lane[0, MLA_NOPE + half:MLA_NOPE + MLA_ROPE] = inv
    inv_lane = jnp.asarray(inv_lane)

    fa, fs = ffn_args(ff1_norm[0], ff1_w_in[0], ff1_w_out[0])
    ins = ([xt] + fa + [_row(mix_norm[0]), w_qk, w_vt, w_c, _row(mla_q_norm[0]), w_uq,
                        _row(mla_kv_norm[0]), w_ukv_k, w_ukv_vt, inv_lane])
    specs = ([_tok_spec(D_MODEL)] + fs
             + [_wspec((1, D_MODEL)), _wspec(w_qk.shape), _wspec(w_vt.shape), _wspec(w_c.shape),
                _wspec((1, MLA_Q_RANK)), _wspec(w_uq.shape), _wspec((1, MLA_KV_RANK)),
                _wspec(w_ukv_k.shape), _wspec(w_ukv_vt.shape), _wspec((1, LANES))])
    pen_rows = A_HEADS // 2 * N_KBLK
    outs = (jax.ShapeDtypeStruct((TOKENS, D_MODEL), F32),
            jax.ShapeDtypeStruct((TOKENS, 2 * A_WIDTH), BF16),
            jax.ShapeDtypeStruct((A_WIDTH, TOKENS), BF16),
            jax.ShapeDtypeStruct((BATCH, N_KBLK, pen_rows, 2 * Q_TILE), F32),
            jax.ShapeDtypeStruct((TOKENS, B_HEADS * LANES), BF16),
            jax.ShapeDtypeStruct((TOKENS, B_HEADS * LANES), BF16),
            jax.ShapeDtypeStruct((B_HEADS * MLA_V, TOKENS), BF16))
    out_specs = (_tok_spec(D_MODEL), _tok_spec(2 * A_WIDTH), _tok_t_spec(A_WIDTH),
                 pl.BlockSpec((1, 1, pen_rows, 2 * Q_TILE), lambda t: (t // N_KBLK, t % N_KBLK, 0, 0)),
                 _tok_spec(B_HEADS * LANES), _tok_spec(B_HEADS * LANES),
                 _tok_t_spec(B_HEADS * MLA_V))
    x1, qk_a, vt_a, pen, q_mla, k_mla, vt_mla = _token_call(
        "pre0", _pre0_kernel, ins, specs, outs, out_specs,
        scratch=[pltpu.VMEM((N_KBLK, A_WIDTH), F32)])

    bias = pl.pallas_call(
        _t5_bias_kernel,
        grid=(A_HEADS // 2,),
        in_specs=[pl.BlockSpec(memory_space=pltpu.SMEM)],
        out_specs=pl.BlockSpec((1, 3, Q_TILE, 2 * Q_TILE), lambda h: (h, 0, 0, 0)),
        out_shape=jax.ShapeDtypeStruct((A_HEADS // 2, 3, Q_TILE, 2 * Q_TILE), F32),
        name="t5_bias_tiles",
    )(t5_bias.astype(F32))

    na = A_HEADS // 2
    o_a = _attention(
        "moba", [_tiles4(qk_a), _tiles4(qk_a), qk_a, vt_a, pen, pen, bias],
        _tile_pair_specs(PAIR_GROUP * LANES)
        + [_kv_spec(LANES, na // PAIR_GROUP), _vt_spec()]
        + [pl.BlockSpec((1, 1, PAIR_GROUP * N_KBLK, 2 * Q_TILE), lambda b, g, j: (b, j, g, 0)),
           pl.BlockSpec((1, 1, PAIR_GROUP * N_KBLK, 2 * Q_TILE),
                        lambda b, g, j: (b, N_KBLK - 1 - j, g, 0)),
           pl.BlockSpec((PAIR_GROUP, 3, Q_TILE, 2 * Q_TILE), lambda b, g, j: (g, 0, 0, 0))], na)
    o_b = _attention(
        "mla", [_tiles4(q_mla), _tiles4(q_mla), k_mla, vt_mla],
        _tile_pair_specs(PAIR_GROUP * 2 * LANES) + [_kv_spec(2 * LANES, 0), _vt_spec()],
        B_HEADS // 2)
    w_o = ab_w_out[0]
    xt = post(x1, [o_a, o_b], [w_o[:A_WIDTH], w_o[A_WIDTH:]], 0, DEPTH == 1)

    w_fox = fox_w_in[0]
    w_qk = w_fox[:, :2 * C_WIDTH].astype(BF16)
    w_vt = w_fox[:, 2 * C_WIDTH:3 * C_WIDTH].T.astype(BF16)
    w_f = jnp.pad(w_fox[:, 3 * C_WIDTH:], ((0, 0), (0, LANES - C_HEADS))).astype(BF16)
    fa, fs = ffn_args(ff1_norm[1], ff1_w_in[1], ff1_w_out[1])
    ins = [xt] + fa + [_row(mix_norm[1]), w_qk, w_vt, w_f, _row(fox_b_f[0], LANES)]
    specs = ([_tok_spec(D_MODEL)] + fs
             + [_wspec((1, D_MODEL)), _wspec(w_qk.shape), _wspec(w_vt.shape), _wspec(w_f.shape),
                _wspec((1, LANES))])
    outs = (jax.ShapeDtypeStruct((TOKENS, D_MODEL), F32),
            jax.ShapeDtypeStruct((TOKENS, 2 * C_WIDTH), BF16),
            jax.ShapeDtypeStruct((C_WIDTH, TOKENS), BF16),
            jax.ShapeDtypeStruct((TOKENS, LANES), BF16))
    out_specs = (_tok_spec(D_MODEL), _tok_spec(2 * C_WIDTH), _tok_t_spec(C_WIDTH), _tok_spec(LANES))
    x1, qk_c, vt_c, caug = _token_call("pre1", _pre1_kernel, ins, specs, outs, out_specs,
                                       scratch=[pltpu.VMEM((1, LANES), F32)])

    nc = C_HEADS // 2
    o_c = _attention(
        "fox", [_tiles4(qk_c), _tiles4(qk_c), qk_c, vt_c, _tiles4(caug), _tiles4(caug), caug],
        _tile_pair_specs(PAIR_GROUP * LANES)
        + [_kv_spec(LANES, nc // PAIR_GROUP), _vt_spec(),
           pl.BlockSpec((1, 1, Q_TILE, LANES), lambda b, g, j: (b, j, 0, 0)),
           pl.BlockSpec((1, 1, Q_TILE, LANES), lambda b, g, j: (b, N_KBLK - 1 - j, 0, 0)),
           pl.BlockSpec((SEQ, LANES), lambda b, g, j: (b, 0))],
        nc, extra_scratch=[pltpu.VMEM((SEQ, PAIR_GROUP * 2 * LANES), BF16)])
    xt = post(x1, [o_c], [fox_w_out[0]], 1, True)
    return xt.reshape(BATCH, SEQ, D_MODEL)
```

```python
import functools
import math

import jax
import jax.numpy as jnp
import numpy as np
from jax import lax
from jax.experimental import pallas as pl
from jax.experimental.pallas import tpu as pltpu

F32 = jnp.float32
BF16 = jnp.bfloat16

D_MODEL = 1024
BATCH = 8
SEQ = 2048
DEPTH = 2
PLE_DIM = 256
D_FF = 2816
EPS = 1e-6

A_HEADS = 8
A_HEAD_DIM = 64
MOBA_BLOCK = 256
MOBA_TOPK = 3

B_HEADS = 8
MLA_Q_RANK = 256
MLA_KV_RANK = 128
MLA_NOPE = 64
MLA_ROPE = 32
MLA_V = 64
ROPE_THETA = 10000.0

T5_BUCKETS = 32
T5_MAX_DIST = 128

C_HEADS = 16
C_HEAD_DIM = 64

A_WIDTH = A_HEADS * A_HEAD_DIM
C_WIDTH = C_HEADS * C_HEAD_DIM

TOKENS = BATCH * SEQ
LANES = 128
BF16_ROWS = 16
PAIR_GROUP = 2
Q_TILE = MOBA_BLOCK
N_KBLK = SEQ // Q_TILE
TOKEN_TILE = 2 * Q_TILE
TILE_R = TOKEN_TILE // Q_TILE
FF_CHUNK = D_FF // 2
VMEM_LIMIT = 56 * 1024 * 1024
NEG_INF = float("-inf")
LOG2E = math.log2(math.e)
QSCALE_64 = A_HEAD_DIM ** -0.5 * LOG2E
QSCALE_MLA = (MLA_NOPE + MLA_ROPE) ** -0.5 * LOG2E


def _wspec(shape, layer=None):
    nd = len(shape)
    if layer is None:
        return pl.BlockSpec(shape, lambda *_: (0,) * nd, pipeline_mode=pl.Buffered(1))
    return pl.BlockSpec((None,) + tuple(shape), lambda *_: (layer,) + (0,) * nd,
                        pipeline_mode=pl.Buffered(1))


def _dot(a, b):
    return jnp.dot(a, b, preferred_element_type=F32)


def _dot_nt(a, b):
    return lax.dot_general(a, b, (((1,), (1,)), ((), ())), preferred_element_type=F32)


def _rms(x, g):
    return x * lax.rsqrt(jnp.mean(x * x, axis=-1, keepdims=True) + EPS) * g


def _ffn(x, g, win_ref, wo_ref):
    h = _rms(x, g).astype(BF16)
    y = jnp.zeros_like(x)
    for c in range(D_FF // FF_CHUNK):
        lo, hi = c * FF_CHUNK, (c + 1) * FF_CHUNK
        a = _dot(h, win_ref[:, lo:hi])
        u = _dot(h, win_ref[:, D_FF + lo:D_FF + hi])
        act = (a * jax.nn.sigmoid(a) * u).astype(BF16)
        y = y + _dot(act, wo_ref[lo:hi, :])
    return x + 0.5 * y


def _rope_tables(inv_lane, tile_idx):
    pos0 = (tile_idx * TOKEN_TILE) % SEQ
    pos = (pos0 + lax.broadcasted_iota(jnp.int32, (TOKEN_TILE, LANES), 0)).astype(F32)
    lane = lax.broadcasted_iota(jnp.int32, (TOKEN_TILE, LANES), 1)
    ang = pos * inv_lane
    is_x1 = (lane >= MLA_NOPE) & (lane < MLA_NOPE + MLA_ROPE // 2)
    is_x2 = (lane >= MLA_NOPE + MLA_ROPE // 2) & (lane < MLA_NOPE + MLA_ROPE)
    cos_t = jnp.where(is_x1 | is_x2, jnp.cos(ang), 1.0)
    sin = jnp.sin(ang)
    sin_t = jnp.where(is_x1, -sin, jnp.where(is_x2, sin, 0.0))
    return cos_t, sin_t, is_x1


def _rope_block(xb, cos_t, sin_t, is_x1):
    half = MLA_ROPE // 2
    partner = jnp.where(is_x1, pltpu.roll(xb, LANES - half, 1), pltpu.roll(xb, half, 1))
    return xb * cos_t + partner * sin_t


def _pre0_kernel(x_ref, g1_ref, win_ref, wo_ref, gmix_ref, wqk_ref, wvt_ref, wc_ref,
                 qn_ref, wuq_ref, kvn_ref, wukvk_ref, wukvvt_ref, inv_ref,
                 x1_ref, qk_ref, vt_ref, pen_ref, qm_ref, kmla_ref, vmt_ref, km_s):
    t = pl.program_id(0)
    x1 = _ffn(x_ref[...], g1_ref[...], win_ref, wo_ref)
    x1_ref[...] = x1
    h = _rms(x1, gmix_ref[...]).astype(BF16)
    qk = _dot(h, wqk_ref[...])
    q = qk[:, :A_WIDTH]
    k = qk[:, A_WIDTH:]
    qk_ref[:, :A_WIDTH] = (q * QSCALE_64).astype(BF16)
    qk_ref[:, A_WIDTH:] = k.astype(BF16)

    @pl.when(t == 0)
    def _():
        km_s[...] = jnp.zeros_like(km_s)

    gates = []
    for r in range(TILE_R):
        own = (t * TILE_R + r) % N_KBLK
        part = slice(r * Q_TILE, (r + 1) * Q_TILE)
        km_s[pl.ds(own, 1), :] = jnp.mean(k[part], axis=0, keepdims=True)
        gates.append((own, _moba_gate(q[part], km_s[...])))
    vt_ref[...] = _dot_nt(wvt_ref[...], h).astype(BF16)
    for r, (own, gate) in enumerate(gates):
        pen_ref[0, r] = _moba_select(gate, own)

    c = _dot(h, wc_ref[...])
    cq = c[:, :MLA_Q_RANK]
    ckv = c[:, MLA_Q_RANK:MLA_Q_RANK + MLA_KV_RANK]
    kr = c[:, MLA_Q_RANK + MLA_KV_RANK:]
    cqn = _rms(cq, qn_ref[...]).astype(BF16)
    ckvn = _rms(ckv, kvn_ref[...]).astype(BF16)
    qm = _dot(cqn, wuq_ref[...])
    kn = _dot(ckvn, wukvk_ref[...])
    vmt_ref[...] = _dot_nt(wukvvt_ref[...], ckvn).astype(BF16)

    cos_t, sin_t, is_x1 = _rope_tables(inv_ref[...], t)
    cos_q = cos_t * QSCALE_MLA
    sin_q = sin_t * QSCALE_MLA
    krr = _rope_block(kr, cos_t, sin_t, is_x1)
    for hb in range(B_HEADS):
        sl = slice(hb * LANES, (hb + 1) * LANES)
        qm_ref[:, sl] = _rope_block(qm[:, sl], cos_q, sin_q, is_x1).astype(BF16)
        kmla_ref[:, sl] = (kn[:, sl] + krr).astype(BF16)


def _pre1_kernel(x_ref, g1_ref, win_ref, wo_ref, gmix_ref, wqk_ref, wvt_ref, wf_ref, bf_ref,
                 x1_ref, qk_ref, vt_ref, caug_ref, carry_s):
    t = pl.program_id(0)
    x1 = _ffn(x_ref[...], g1_ref[...], win_ref, wo_ref)
    x1_ref[...] = x1
    h = _rms(x1, gmix_ref[...]).astype(BF16)

    @pl.when(t % (N_KBLK // TILE_R) == 0)
    def _():
        carry_s[...] = jnp.zeros_like(carry_s)

    z = _dot(h, wf_ref[...]) + bf_ref[...]
    qk = _dot(h, wqk_ref[...])
    qk_ref[:, :C_WIDTH] = (qk[:, :C_WIDTH] * QSCALE_64).astype(BF16)
    qk_ref[:, C_WIDTH:] = qk[:, C_WIDTH:].astype(BF16)
    cums = [_fox_cumsum(z[r * Q_TILE:(r + 1) * Q_TILE], carry_s) for r in range(TILE_R)]
    vt_ref[...] = _dot_nt(wvt_ref[...], h).astype(BF16)
    for r, cum in enumerate(cums):
        caug_ref[r * Q_TILE:(r + 1) * Q_TILE, :] = _fox_decay_parts(cum)


def _tok_spec(width):
    return pl.BlockSpec((TOKEN_TILE, width), lambda i: (i, 0))


def _tok_t_spec(height):
    return pl.BlockSpec((height, TOKEN_TILE), lambda i: (0, i))


def _token_call(name, body, ins, in_specs, outs, out_specs, scratch=()):
    return pl.pallas_call(
        body,
        grid=(TOKENS // TOKEN_TILE,),
        in_specs=in_specs,
        out_specs=out_specs,
        out_shape=outs,
        scratch_shapes=list(scratch),
        compiler_params=pltpu.CompilerParams(
            dimension_semantics=("arbitrary",), vmem_limit_bytes=VMEM_LIMIT),
        name=name,
    )(*ins)


def _post_kernel(*refs, n_mix, final):
    x_ref = refs[0]
    o_refs = refs[1:1 + 2 * n_mix]
    w_refs = refs[1 + 2 * n_mix:1 + 3 * n_mix]
    (g2_ref, win_ref, wo_ref, gple_ref, wg_ref, p_ref, wp_ref, gfin_ref,
     out_ref) = refs[1 + 3 * n_mix:]
    x = x_ref[...]
    per_seq = N_KBLK // TILE_R
    in_lo = (pl.program_id(0) % per_seq) < per_seq // 2
    for mi, w_ref in enumerate(w_refs):
        o = jnp.where(in_lo, o_refs[2 * mi][0], o_refs[2 * mi + 1][0])
        x = x + _dot(o.reshape(TOKEN_TILE, o.shape[-1]), w_ref[...])
    x = _ffn(x, g2_ref[...], win_ref, wo_ref)
    gate = jax.nn.sigmoid(_dot(_rms(x, gple_ref[...]).astype(BF16), wg_ref[...]))
    x = x + gate * _dot(p_ref[...].astype(BF16), wp_ref[...])
    if final:
        x = _rms(x, gfin_ref[...])
    out_ref[...] = x


def _split_bf16(x):
    hi = x.astype(BF16)
    lo = (x - hi.astype(F32)).astype(BF16)
    return hi, lo


def _moba_gate(q, km):
    rows = A_HEADS * N_KBLK
    gt = jnp.concatenate([km] * A_HEADS, axis=0)
    r = lax.broadcasted_iota(jnp.int32, (rows, A_WIDTH), 0)
    c = lax.broadcasted_iota(jnp.int32, (rows, A_WIDTH), 1)
    gt = jnp.where((r // N_KBLK) == (c // A_HEAD_DIM), gt, 0.0)
    g_hi, g_lo = _split_bf16(gt)
    q_hi, q_lo = _split_bf16(q)
    return _dot_nt(g_hi, q_hi) + _dot_nt(g_hi, q_lo) + _dot_nt(g_lo, q_hi)


def _moba_select(gate, own):
    n_idx = lax.broadcasted_iota(jnp.int32, (N_KBLK, Q_TILE), 0)
    pen_rows = []
    for h in range(A_HEADS):
        gh = gate[h * N_KBLK:(h + 1) * N_KBLK]
        rank = jnp.zeros((N_KBLK, Q_TILE), jnp.int32)
        for m in range(N_KBLK):
            gm = gh[m:m + 1]
            beats = (gm > gh) | ((gm == gh) & (m < n_idx))
            rank = rank + jnp.where(beats & (m < own), 1, 0)
        sel = (n_idx < own) & (rank < MOBA_TOPK)
        pen_n = jnp.where(sel, 0.0, NEG_INF)
        pen_d = jnp.full((N_KBLK, Q_TILE), NEG_INF, F32)
        for n in range(N_KBLK):
            pen_d = jnp.where(n_idx == own - n, pen_n[n:n + 1], pen_d)
        pen_rows.append(pen_d)
    pairs = [jnp.concatenate(pen_rows[2 * hp:2 * hp + 2], axis=1) for hp in range(A_HEADS // 2)]
    return jnp.concatenate(pairs, axis=0)


def _t5_bias_kernel(tbl_ref, o_ref):
    hp = pl.program_id(0)
    r = lax.broadcasted_iota(jnp.int32, (Q_TILE, Q_TILE), 0)
    c = lax.broadcasted_iota(jnp.int32, (Q_TILE, Q_TILE), 1)
    max_exact = T5_BUCKETS // 2
    for dd in range(3):
        dist = dd * Q_TILE + c - r
        dc = jnp.maximum(dist, 0)
        df = jnp.maximum(dc.astype(F32), 1.0)
        large = max_exact + (jnp.log(df / max_exact) / math.log(T5_MAX_DIST / max_exact)
                             * (T5_BUCKETS - max_exact)).astype(jnp.int32)
        large = jnp.minimum(large, T5_BUCKETS - 1)
        bucket = jnp.where(dc < max_exact, dc, large)
        for hh in range(2):
            bias = jnp.zeros((Q_TILE, Q_TILE), F32)
            for b in range(T5_BUCKETS):
                bias = jnp.where(bucket == b, tbl_ref[b, 2 * hp + hh], bias)
            bias = bias * LOG2E
            if dd == 0:
                bias = jnp.where(dist >= 0, bias, NEG_INF)
            o_ref[0, dd, :, hh * Q_TILE:(hh + 1) * Q_TILE] = bias


def _fox_cumsum(z, carry_ref):
    lane = lax.broadcasted_iota(jnp.int32, (Q_TILE, LANES), 1)
    logf = jnp.minimum(z, 0.0) - jnp.log1p(jnp.exp(-jnp.abs(z)))
    logf = jnp.where(lane < C_HEADS, logf, 0.0)
    r = lax.broadcasted_iota(jnp.int32, (Q_TILE, Q_TILE), 0)
    c = lax.broadcasted_iota(jnp.int32, (Q_TILE, Q_TILE), 1)
    tri = jnp.where(c <= r, 1.0, 0.0).astype(BF16)
    l1 = logf.astype(BF16)
    rem = logf - l1.astype(F32)
    l2 = rem.astype(BF16)
    l3 = (rem - l2.astype(F32)).astype(BF16)
    cum = _dot(tri, l1) + _dot(tri, l2) + _dot(tri, l3) + carry_ref[...]
    carry_ref[...] = cum[Q_TILE - 1:Q_TILE, :]
    return cum


def _fox_decay_parts(cum):
    cs = cum * LOG2E
    c1 = cs.astype(BF16)
    rem = cs - c1.astype(F32)
    c2 = rem.astype(BF16)
    c3 = (rem - c2.astype(F32)).astype(BF16)
    rin = lax.broadcasted_iota(jnp.int32, (LANES, LANES), 0)
    lout = lax.broadcasted_iota(jnp.int32, (LANES, LANES), 1)
    out = jnp.zeros((Q_TILE, LANES), F32)
    for part, cp in enumerate((c1, c2, c3)):
        place = jnp.where((rin < C_HEADS) & (lout == rin * 8 + part), 1.0, 0.0).astype(BF16)
        out = out + _dot(cp, place)
    return out.astype(BF16)


def _attn_kernel(*refs, kind):
    if kind == "moba":
        q_lo, q_hi, k_ref, vt_ref, pen_lo, pen_hi, bias_ref, o_lo, o_hi = refs
    elif kind == "mla":
        q_lo, q_hi, k_ref, vt_ref, o_lo, o_hi = refs
    else:
        q_lo, q_hi, k_ref, vt_ref, cq_lo, cq_hi, ck_ref, o_lo, o_hi, kp_s = refs
    grp = pl.program_id(1)
    j = pl.program_id(2)
    lane = lax.broadcasted_iota(jnp.int32, (1, LANES), 1)
    low = lane < A_HEAD_DIM
    key = lax.broadcasted_iota(jnp.int32, (Q_TILE, 2 * Q_TILE), 0)
    qry = lax.broadcasted_iota(jnp.int32, (Q_TILE, 2 * Q_TILE), 1) % Q_TILE
    kw = LANES if kind == "moba" else 2 * LANES
    qw = 2 * LANES if kind == "mla" else LANES
    k_cat = kp_s if kind == "fox" else k_ref

    if kind == "fox":
        rin = lax.broadcasted_iota(jnp.int32, (LANES, LANES), 0)
        lout = lax.broadcasted_iota(jnp.int32, (LANES, LANES), 1)

        def place(pair, off_a, off_b, val):
            base_a = 2 * pair * 8
            base_b = base_a + 8
            sel_a = (lout >= off_a) & (lout < off_a + 3) & (rin == base_a + lout - off_a)
            sel_b = (lout >= off_b) & (lout < off_b + 3) & (rin == base_b + lout - off_b)
            return jnp.where(sel_a | sel_b, val, 0.0).astype(BF16)

        def ones(off_a, off_b):
            in_a = (lane >= off_a) & (lane < off_a + 3)
            in_b = (lane >= off_b) & (lane < off_b + 3)
            return jnp.where(in_a | in_b, 1.0, 0.0)

        @pl.when(j == 0)
        def _():
            for pi in range(PAIR_GROUP):
                ak = (_dot(ck_ref[...], place(grp * PAIR_GROUP + pi, A_HEAD_DIM + 3, 3, -1.0))
                      + ones(A_HEAD_DIM, 0)).astype(BF16)
                k = k_ref[:, pi * LANES:(pi + 1) * LANES]
                kp_s[:, pi * kw:pi * kw + LANES] = jnp.where(low, k, ak)
                kp_s[:, pi * kw + LANES:(pi + 1) * kw] = jnp.where(low, ak, k)

    def q_operand(q_ref, cq_ref, pi):
        zero = jnp.zeros((Q_TILE, LANES), BF16)
        q = q_ref[0, 0, :, pi * qw:(pi + 1) * qw]
        if kind == "moba":
            return jnp.concatenate([jnp.where(low, q, zero), jnp.where(low, zero, q)], axis=0)
        if kind == "fox":
            aq = (_dot(cq_ref[0, 0], place(grp * PAIR_GROUP + pi, A_HEAD_DIM, 0, 1.0))
                  + ones(A_HEAD_DIM + 3, 3)).astype(BF16)
            qa, qb = jnp.where(low, q, aq), jnp.where(low, aq, q)
        else:
            qa, qb = q[:, :LANES], q[:, LANES:]
        return jnp.concatenate([jnp.concatenate([qa, zero], axis=1),
                                jnp.concatenate([zero, qb], axis=1)], axis=0)

    def run(chains):
        ones_rows = jnp.ones((BF16_ROWS, Q_TILE), BF16)
        n = len(chains)

        def rows_of(c, d):
            nb = chains[c][1]
            return slice((nb - 1 - d) * Q_TILE, (nb - d) * Q_TILE)

        def qk(c, d):
            pi = chains[c][0]
            return _dot_nt(k_cat[rows_of(c, d), pi * kw:(pi + 1) * kw], chains[c][2])

        s_next = [qk(c, 0) for c in range(n)]
        m = [None] * n
        acc = [None] * n
        for d in range(max(c[1] for c in chains)):
            live = [c for c in range(n) if d < chains[c][1]]
            s_cur = list(s_next)
            for c in live:
                if d + 1 < chains[c][1]:
                    s_next[c] = qk(c, d + 1)
            for c in live:
                pi, _, _, pen_ref = chains[c]
                sn = s_cur[c]
                shift = None
                if kind == "moba":
                    if d < 2:
                        sn = sn + bias_ref[pi, d]
                    else:
                        shift = bias_ref[pi, 2, 0:1, :]
                    if d > 0:
                        pen = pen_ref[0, 0, pi * N_KBLK + d:pi * N_KBLK + d + 1, :]
                        shift = pen if shift is None else shift + pen
                elif d == 0:
                    sn = jnp.where(key <= qry, sn, NEG_INF)
                bm = jnp.max(sn, axis=0, keepdims=True)
                if shift is not None:
                    bm = bm + shift
                m_new = bm if d == 0 else jnp.maximum(m[c], bm)
                off = m_new if shift is None else m_new - shift
                p = jnp.exp2(sn - off).astype(BF16)
                v_aug = jnp.concatenate(
                    [vt_ref[pi * LANES:(pi + 1) * LANES, rows_of(c, d)], ones_rows], axis=0)
                pv = _dot(v_aug, p)
                acc[c] = pv if d == 0 else acc[c] * jnp.exp2(m[c] - m_new) + pv
                m[c] = m_new
        return [a[:LANES] / a[LANES:LANES + 1] for a in acc]

    for jj in range(N_KBLK // 2):
        @pl.when(j == jj)
        def _(jj=jj):
            chains = []
            for pi in range(PAIR_GROUP):
                for q_ref, cq_ref, pen_ref, nb in (
                        (q_hi, cq_hi if kind == "fox" else None,
                         pen_hi if kind == "moba" else None, N_KBLK - jj),
                        (q_lo, cq_lo if kind == "fox" else None,
                         pen_lo if kind == "moba" else None, jj + 1)):
                    chains.append((pi, nb, q_operand(q_ref, cq_ref, pi), pen_ref))
            outs = run(chains)
            for c, ot2 in enumerate(outs):
                pi = chains[c][0]
                o_ref = o_hi if c % 2 == 0 else o_lo
                ot = jnp.concatenate([ot2[:A_HEAD_DIM, :Q_TILE], ot2[A_HEAD_DIM:, Q_TILE:]], axis=0)
                o_ref[0, 0, :, pi * LANES:(pi + 1) * LANES] = ot.T.astype(BF16)


def _tiles4(a):
    return a.reshape(BATCH, N_KBLK, Q_TILE, a.shape[-1])


def _tile_pair_specs(width, col0=0):
    return [pl.BlockSpec((1, 1, Q_TILE, width), lambda b, g, j: (b, j, 0, col0 + g)),
            pl.BlockSpec((1, 1, Q_TILE, width), lambda b, g, j: (b, N_KBLK - 1 - j, 0, col0 + g))]


def _attention(kind, ins, in_specs, n_pairs, extra_scratch=()):
    width = PAIR_GROUP * LANES
    half = N_KBLK // 2
    shape = jax.ShapeDtypeStruct((BATCH, half, Q_TILE, n_pairs * LANES), BF16)
    return pl.pallas_call(
        functools.partial(_attn_kernel, kind=kind),
        grid=(BATCH, n_pairs // PAIR_GROUP, half),
        in_specs=in_specs,
        out_specs=(pl.BlockSpec((1, 1, Q_TILE, width), lambda b, g, j: (b, j, 0, g)),
                   pl.BlockSpec((1, 1, Q_TILE, width), lambda b, g, j: (b, half - 1 - j, 0, g))),
        out_shape=(shape, shape),
        scratch_shapes=list(extra_scratch),
        compiler_params=pltpu.CompilerParams(
            dimension_semantics=("arbitrary", "arbitrary", "arbitrary"),
            vmem_limit_bytes=VMEM_LIMIT),
        name="attn_" + kind,
    )(*ins)


def _kv_spec(pair_width, col0):
    return pl.BlockSpec((SEQ, PAIR_GROUP * pair_width), lambda b, g, j: (b, col0 + g))


def _vt_spec():
    return pl.BlockSpec((PAIR_GROUP * LANES, SEQ), lambda b, g, j: (g, b))


def _place_heads(w, n_heads, src_stride, src_off, width):
    per_head = w.reshape(w.shape[0], n_heads, src_stride)[:, :, src_off:src_off + width]
    per_head = jnp.pad(per_head, ((0, 0), (0, 0), (0, LANES - width)))
    return per_head.reshape(w.shape[0], n_heads * LANES)


def _row(v, width=None):
    v = v.reshape(1, -1).astype(F32)
    if width is not None and v.shape[1] < width:
        v = jnp.pad(v, ((0, 0), (0, width - v.shape[1])))
    return v


def kernel(x, p, t5_bias, ff1_norm, ff1_w_in, ff1_w_out, mix_norm, ff2_norm, ff2_w_in, ff2_w_out,
           ple_norm, ple_w_gate, ple_w_proj, ab_w_in, mla_q_norm, mla_w_uq, mla_kv_norm, mla_w_ukv,
           ab_w_out, fox_w_in, fox_b_f, fox_w_out, final_norm):
    xt = x.reshape(TOKENS, D_MODEL)

    ffw = {1: (ff1_norm, ff1_w_in.astype(BF16), ff1_w_out.astype(BF16)),
           2: (ff2_norm, ff2_w_in.astype(BF16), ff2_w_out.astype(BF16))}

    def ffn_args(which, layer):
        norm, w_in, w_out = ffw[which]
        args = [_row(norm[layer]), w_in, w_out]
        specs = [_wspec((1, D_MODEL)), _wspec((D_MODEL, 2 * D_FF), layer), _wspec((D_FF, D_MODEL), layer)]
        return args, specs

    ple_gate_bf = ple_w_gate.astype(BF16)
    ple_proj_bf = ple_w_proj.astype(BF16)

    def post(xin, mixes, w_outs, layer, final):
        fa, fs = ffn_args(2, layer)
        ws = [w.astype(BF16) for w in w_outs]
        flat = [half for mix in mixes for half in mix]
        ins = ([xin] + flat + ws + fa
               + [_row(ple_norm[layer]), ple_gate_bf, p.reshape(DEPTH, TOKENS, PLE_DIM), ple_proj_bf,
                  _row(final_norm)])
        per_seq = N_KBLK // TILE_R
        half = per_seq // 2
        mix_specs = []
        for lo, hi in mixes:
            blk = (1, TILE_R, Q_TILE, lo.shape[-1])
            mix_specs.append(pl.BlockSpec(
                blk, lambda t: (t // per_seq, jnp.minimum(t % per_seq, half - 1), 0, 0)))
            mix_specs.append(pl.BlockSpec(
                blk, lambda t: (t // per_seq, jnp.maximum(t % per_seq - half, 0), 0, 0)))
        specs = ([_tok_spec(D_MODEL)] + mix_specs
                 + [_wspec(w.shape) for w in ws] + fs
                 + [_wspec((1, D_MODEL)), _wspec((D_MODEL, D_MODEL), layer),
                    pl.BlockSpec((None, TOKEN_TILE, PLE_DIM), lambda t: (layer, t, 0)),
                    _wspec((PLE_DIM, D_MODEL), layer), _wspec((1, D_MODEL))])
        return _token_call(
            "post%d" % layer,
            functools.partial(_post_kernel, n_mix=len(mixes), final=final), ins, specs,
            jax.ShapeDtypeStruct((TOKENS, D_MODEL), F32), _tok_spec(D_MODEL))

    w_ab = ab_w_in[0]
    w_qk = w_ab[:, :2 * A_WIDTH].astype(BF16)
    w_vt = w_ab[:, 2 * A_WIDTH:3 * A_WIDTH].T.astype(BF16)
    c0 = 3 * A_WIDTH
    kr0 = MLA_Q_RANK + MLA_KV_RANK
    w_c = jnp.concatenate(
        [w_ab[:, c0:c0 + kr0], jnp.zeros((D_MODEL, MLA_NOPE), F32), w_ab[:, c0 + kr0:],
         jnp.zeros((D_MODEL, LANES - MLA_NOPE - MLA_ROPE), F32)], axis=1).astype(BF16)
    w_uq = _place_heads(mla_w_uq[0], B_HEADS, MLA_NOPE + MLA_ROPE, 0, MLA_NOPE + MLA_ROPE).astype(BF16)
    w_ukv_k = _place_heads(mla_w_ukv[0], B_HEADS, MLA_NOPE + MLA_V, 0, MLA_NOPE).astype(BF16)
    w_ukv = mla_w_ukv[0].reshape(MLA_KV_RANK, B_HEADS, MLA_NOPE + MLA_V)
    w_ukv_vt = w_ukv[:, :, MLA_NOPE:].reshape(MLA_KV_RANK, B_HEADS * MLA_V).T.astype(BF16)
    half = MLA_ROPE // 2
    inv = ROPE_THETA ** (-np.arange(half, dtype=np.float64) / half)
    inv_lane = np.zeros((1, LANES), np.float32)
    inv_lane[0, MLA_NOPE:MLA_NOPE + half] = inv
    inv_lane[0, MLA_NOPE + half:MLA_NOPE + MLA_ROPE] = inv
    inv_lane = jnp.asarray(inv_lane)

    fa, fs = ffn_args(1, 0)
    ins = ([xt] + fa + [_row(mix_norm[0]), w_qk, w_vt, w_c, _row(mla_q_norm[0]), w_uq,
                        _row(mla_kv_norm[0]), w_ukv_k, w_ukv_vt, inv_lane])
    specs = ([_tok_spec(D_MODEL)] + fs
             + [_wspec((1, D_MODEL)), _wspec(w_qk.shape), _wspec(w_vt.shape), _wspec(w_c.shape),
                _wspec((1, MLA_Q_RANK)), _wspec(w_uq.shape), _wspec((1, MLA_KV_RANK)),
                _wspec(w_ukv_k.shape), _wspec(w_ukv_vt.shape), _wspec((1, LANES))])
    pen_rows = A_HEADS // 2 * N_KBLK
    outs = (jax.ShapeDtypeStruct((TOKENS, D_MODEL), F32),
            jax.ShapeDtypeStruct((TOKENS, 2 * A_WIDTH), BF16),
            jax.ShapeDtypeStruct((A_WIDTH, TOKENS), BF16),
            jax.ShapeDtypeStruct((BATCH, N_KBLK, pen_rows, 2 * Q_TILE), F32),
            jax.ShapeDtypeStruct((TOKENS, B_HEADS * LANES), BF16),
            jax.ShapeDtypeStruct((TOKENS, B_HEADS * LANES), BF16),
            jax.ShapeDtypeStruct((B_HEADS * MLA_V, TOKENS), BF16))
    out_specs = (_tok_spec(D_MODEL), _tok_spec(2 * A_WIDTH), _tok_t_spec(A_WIDTH),
                 pl.BlockSpec((1, TILE_R, pen_rows, 2 * Q_TILE),
                              lambda t: (t // (N_KBLK // TILE_R), t % (N_KBLK // TILE_R), 0, 0)),
                 _tok_spec(B_HEADS * LANES), _tok_spec(B_HEADS * LANES),
                 _tok_t_spec(B_HEADS * MLA_V))
    x1, qk_a, vt_a, pen, q_mla, k_mla, vt_mla = _token_call(
        "pre0", _pre0_kernel, ins, specs, outs, out_specs,
        scratch=[pltpu.VMEM((N_KBLK, A_WIDTH), F32)])

    bias = pl.pallas_call(
        _t5_bias_kernel,
        grid=(A_HEADS // 2,),
        in_specs=[pl.BlockSpec(memory_space=pltpu.SMEM)],
        out_specs=pl.BlockSpec((1, 3, Q_TILE, 2 * Q_TILE), lambda h: (h, 0, 0, 0)),
        out_shape=jax.ShapeDtypeStruct((A_HEADS // 2, 3, Q_TILE, 2 * Q_TILE), F32),
        name="t5_bias_tiles",
    )(t5_bias.astype(F32))

    na = A_HEADS // 2
    o_a = _attention(
        "moba", [_tiles4(qk_a), _tiles4(qk_a), qk_a, vt_a, pen, pen, bias],
        _tile_pair_specs(PAIR_GROUP * LANES)
        + [_kv_spec(LANES, na // PAIR_GROUP), _vt_spec()]
        + [pl.BlockSpec((1, 1, PAIR_GROUP * N_KBLK, 2 * Q_TILE), lambda b, g, j: (b, j, g, 0)),
           pl.BlockSpec((1, 1, PAIR_GROUP * N_KBLK, 2 * Q_TILE),
                        lambda b, g, j: (b, N_KBLK - 1 - j, g, 0)),
           pl.BlockSpec((PAIR_GROUP, 3, Q_TILE, 2 * Q_TILE), lambda b, g, j: (g, 0, 0, 0))], na)
    o_b = _attention(
        "mla", [_tiles4(q_mla), _tiles4(q_mla), k_mla, vt_mla],
        _tile_pair_specs(PAIR_GROUP * 2 * LANES) + [_kv_spec(2 * LANES, 0), _vt_spec()],
        B_HEADS // 2)
    w_o = ab_w_out[0]
    xt = post(x1, [o_a, o_b], [w_o[:A_WIDTH], w_o[A_WIDTH:]], 0, DEPTH == 1)

    w_fox = fox_w_in[0]
    w_qk = w_fox[:, :2 * C_WIDTH].astype(BF16)
    w_vt = w_fox[:, 2 * C_WIDTH:3 * C_WIDTH].T.astype(BF16)
    w_f = jnp.pad(w_fox[:, 3 * C_WIDTH:], ((0, 0), (0, LANES - C_HEADS))).astype(BF16)
    fa, fs = ffn_args(1, 1)
    ins = [xt] + fa + [_row(mix_norm[1]), w_qk, w_vt, w_f, _row(fox_b_f[0], LANES)]
    specs = ([_tok_spec(D_MODEL)] + fs
             + [_wspec((1, D_MODEL)), _wspec(w_qk.shape), _wspec(w_vt.shape), _wspec(w_f.shape),
                _wspec((1, LANES))])
    outs = (jax.ShapeDtypeStruct((TOKENS, D_MODEL), F32),
            jax.ShapeDtypeStruct((TOKENS, 2 * C_WIDTH), BF16),
            jax.ShapeDtypeStruct((C_WIDTH, TOKENS), BF16),
            jax.ShapeDtypeStruct((TOKENS, LANES), BF16))
    out_specs = (_tok_spec(D_MODEL), _tok_spec(2 * C_WIDTH), _tok_t_spec(C_WIDTH), _tok_spec(LANES))
    x1, qk_c, vt_c, caug = _token_call("pre1", _pre1_kernel, ins, specs, outs, out_specs,
                                       scratch=[pltpu.VMEM((1, LANES), F32)])

    nc = C_HEADS // 2
    o_c = _attention(
        "fox", [_tiles4(qk_c), _tiles4(qk_c), qk_c, vt_c, _tiles4(caug), _tiles4(caug), caug],
        _tile_pair_specs(PAIR_GROUP * LANES)
        + [_kv_spec(LANES, nc // PAIR_GROUP), _vt_spec(),
           pl.BlockSpec((1, 1, Q_TILE, LANES), lambda b, g, j: (b, j, 0, 0)),
           pl.BlockSpec((1, 1, Q_TILE, LANES), lambda b, g, j: (b, N_KBLK - 1 - j, 0, 0)),
           pl.BlockSpec((SEQ, LANES), lambda b, g, j: (b, 0))],
        nc, extra_scratch=[pltpu.VMEM((SEQ, PAIR_GROUP * 2 * LANES), BF16)])
    xt = post(x1, [o_c], [fox_w_out[0]], 1, True)
    return xt.reshape(BATCH, SEQ, D_MODEL)
```

```python
import functools
import math

import jax
import jax.numpy as jnp
import numpy as np
from jax import lax
from jax.experimental import pallas as pl
from jax.experimental.pallas import tpu as pltpu

F32 = jnp.float32
BF16 = jnp.bfloat16

D_MODEL = 1024
BATCH = 8
SEQ = 2048
DEPTH = 2
PLE_DIM = 256
D_FF = 2816
EPS = 1e-6

A_HEADS = 8
A_HEAD_DIM = 64
MOBA_BLOCK = 256
MOBA_TOPK = 3

B_HEADS = 8
MLA_Q_RANK = 256
MLA_KV_RANK = 128
MLA_NOPE = 64
MLA_ROPE = 32
MLA_V = 64
ROPE_THETA = 10000.0

T5_BUCKETS = 32
T5_MAX_DIST = 128

C_HEADS = 16
C_HEAD_DIM = 64

A_WIDTH = A_HEADS * A_HEAD_DIM
C_WIDTH = C_HEADS * C_HEAD_DIM

TOKENS = BATCH * SEQ
LANES = 128
BF16_ROWS = 16
PAIR_GROUP = 2
Q_TILE = MOBA_BLOCK
N_KBLK = SEQ // Q_TILE
TOKEN_TILE = 2 * Q_TILE
TILE_R = TOKEN_TILE // Q_TILE
FF_CHUNK = D_FF // 2
VMEM_LIMIT = 56 * 1024 * 1024
NEG_INF = float("-inf")
LOG2E = math.log2(math.e)
QSCALE_64 = A_HEAD_DIM ** -0.5 * LOG2E
QSCALE_MLA = (MLA_NOPE + MLA_ROPE) ** -0.5 * LOG2E


def _wspec(shape, layer=None):
    nd = len(shape)
    if layer is None:
        return pl.BlockSpec(shape, lambda *_: (0,) * nd, pipeline_mode=pl.Buffered(1))
    return pl.BlockSpec((None,) + tuple(shape), lambda *_: (layer,) + (0,) * nd,
                        pipeline_mode=pl.Buffered(1))


def _dot(a, b):
    return jnp.dot(a, b, preferred_element_type=F32)


def _dot_nt(a, b):
    return lax.dot_general(a, b, (((1,), (1,)), ((), ())), preferred_element_type=F32)


def _rms(x, g):
    return x * lax.rsqrt(jnp.mean(x * x, axis=-1, keepdims=True) + EPS) * g


def _ffn(x, g, win_ref, wo_ref):
    h = _rms(x, g).astype(BF16)
    y = jnp.zeros_like(x)
    for c in range(D_FF // FF_CHUNK):
        lo, hi = c * FF_CHUNK, (c + 1) * FF_CHUNK
        a = _dot(h, win_ref[:, lo:hi])
        u = _dot(h, win_ref[:, D_FF + lo:D_FF + hi])
        act = (a * jax.nn.sigmoid(a) * u).astype(BF16)
        y = y + _dot(act, wo_ref[lo:hi, :])
    return x + 0.5 * y


def _rope_tables(inv_lane, tile_idx):
    pos0 = (tile_idx * TOKEN_TILE) % SEQ
    pos = (pos0 + lax.broadcasted_iota(jnp.int32, (TOKEN_TILE, LANES), 0)).astype(F32)
    lane = lax.broadcasted_iota(jnp.int32, (TOKEN_TILE, LANES), 1)
    ang = pos * inv_lane
    is_x1 = (lane >= MLA_NOPE) & (lane < MLA_NOPE + MLA_ROPE // 2)
    is_x2 = (lane >= MLA_NOPE + MLA_ROPE // 2) & (lane < MLA_NOPE + MLA_ROPE)
    cos_t = jnp.where(is_x1 | is_x2, jnp.cos(ang), 1.0)
    sin = jnp.sin(ang)
    sin_t = jnp.where(is_x1, -sin, jnp.where(is_x2, sin, 0.0))
    return cos_t, sin_t, is_x1


def _rope_block(xb, cos_t, sin_t, is_x1):
    half = MLA_ROPE // 2
    partner = jnp.where(is_x1, pltpu.roll(xb, LANES - half, 1), pltpu.roll(xb, half, 1))
    return xb * cos_t + partner * sin_t


def _pre0_kernel(x_ref, g1_ref, win_ref, wo_ref, gmix_ref, wqk_ref, wvt_ref, wc_ref,
                 qn_ref, wuq_ref, kvn_ref, wukvk_ref, wukvvt_ref, inv_ref,
                 x1_ref, qk_ref, vt_ref, pen_ref, qm_ref, kmla_ref, vmt_ref, km_s):
    t = pl.program_id(0)
    x1 = _ffn(x_ref[...], g1_ref[...], win_ref, wo_ref)
    x1_ref[...] = x1
    h = _rms(x1, gmix_ref[...]).astype(BF16)
    qk = _dot(h, wqk_ref[...])
    q = qk[:, :A_WIDTH]
    k = qk[:, A_WIDTH:]
    qk_ref[:, :A_WIDTH] = (q * QSCALE_64).astype(BF16)
    qk_ref[:, A_WIDTH:] = k.astype(BF16)

    @pl.when(t == 0)
    def _():
        km_s[...] = jnp.zeros_like(km_s)

    gates = []
    for r in range(TILE_R):
        own = (t * TILE_R + r) % N_KBLK
        part = slice(r * Q_TILE, (r + 1) * Q_TILE)
        km_s[pl.ds(own, 1), :] = jnp.mean(k[part], axis=0, keepdims=True)
        gates.append((own, _moba_gate(q[part], km_s[...])))
    vt_ref[...] = _dot_nt(wvt_ref[...], h).astype(BF16)
    for r, (own, gate) in enumerate(gates):
        pen_ref[0, r] = _moba_select(gate, own)

    c = _dot(h, wc_ref[...])
    cq = c[:, :MLA_Q_RANK]
    ckv = c[:, MLA_Q_RANK:MLA_Q_RANK + MLA_KV_RANK]
    kr = c[:, MLA_Q_RANK + MLA_KV_RANK:]
    cqn = _rms(cq, qn_ref[...]).astype(BF16)
    ckvn = _rms(ckv, kvn_ref[...]).astype(BF16)
    qm = _dot(cqn, wuq_ref[...])
    kn = _dot(ckvn, wukvk_ref[...])
    vmt_ref[...] = _dot_nt(wukvvt_ref[...], ckvn).astype(BF16)

    cos_t, sin_t, is_x1 = _rope_tables(inv_ref[...], t)
    cos_q = cos_t * QSCALE_MLA
    sin_q = sin_t * QSCALE_MLA
    krr = _rope_block(kr, cos_t, sin_t, is_x1)
    for hb in range(B_HEADS):
        sl = slice(hb * LANES, (hb + 1) * LANES)
        qm_ref[:, sl] = _rope_block(qm[:, sl], cos_q, sin_q, is_x1).astype(BF16)
        kmla_ref[:, sl] = (kn[:, sl] + krr).astype(BF16)


def _pre1_kernel(x_ref, g1_ref, win_ref, wo_ref, gmix_ref, wqk_ref, wvt_ref, wf_ref, bf_ref,
                 x1_ref, qk_ref, vt_ref, caug_ref, carry_s):
    t = pl.program_id(0)
    x1 = _ffn(x_ref[...], g1_ref[...], win_ref, wo_ref)
    x1_ref[...] = x1
    h = _rms(x1, gmix_ref[...]).astype(BF16)

    @pl.when(t % (N_KBLK // TILE_R) == 0)
    def _():
        carry_s[...] = jnp.zeros_like(carry_s)

    z = _dot(h, wf_ref[...]) + bf_ref[...]
    qk = _dot(h, wqk_ref[...])
    qk_ref[:, :C_WIDTH] = (qk[:, :C_WIDTH] * QSCALE_64).astype(BF16)
    qk_ref[:, C_WIDTH:] = qk[:, C_WIDTH:].astype(BF16)
    cums = [_fox_cumsum(z[r * Q_TILE:(r + 1) * Q_TILE], carry_s) for r in range(TILE_R)]
    vt_ref[...] = _dot_nt(wvt_ref[...], h).astype(BF16)
    for r, cum in enumerate(cums):
        caug_ref[r * Q_TILE:(r + 1) * Q_TILE, :] = _fox_decay_parts(cum)


def _tok_spec(width):
    return pl.BlockSpec((TOKEN_TILE, width), lambda i: (i, 0))


def _tok_t_spec(height):
    return pl.BlockSpec((height, TOKEN_TILE), lambda i: (0, i))


def _token_call(name, body, ins, in_specs, outs, out_specs, scratch=()):
    return pl.pallas_call(
        body,
        grid=(TOKENS // TOKEN_TILE,),
        in_specs=in_specs,
        out_specs=out_specs,
        out_shape=outs,
        scratch_shapes=list(scratch),
        compiler_params=pltpu.CompilerParams(
            dimension_semantics=("arbitrary",), vmem_limit_bytes=VMEM_LIMIT),
        name=name,
    )(*ins)


def _post_kernel(*refs, n_mix, final):
    x_ref = refs[0]
    o_refs = refs[1:1 + n_mix]
    w_refs = refs[1 + n_mix:1 + 2 * n_mix]
    (g2_ref, win_ref, wo_ref, gple_ref, wg_ref, p_ref, wp_ref, gfin_ref,
     out_ref) = refs[1 + 2 * n_mix:]
    x = x_ref[...]
    for o_ref, w_ref in zip(o_refs, w_refs):
        x = x + _dot(o_ref[...], w_ref[...])
    x = _ffn(x, g2_ref[...], win_ref, wo_ref)
    gate = jax.nn.sigmoid(_dot(_rms(x, gple_ref[...]).astype(BF16), wg_ref[...]))
    x = x + gate * _dot(p_ref[...].astype(BF16), wp_ref[...])
    if final:
        x = _rms(x, gfin_ref[...])
    out_ref[...] = x


def _split_bf16(x):
    hi = x.astype(BF16)
    lo = (x - hi.astype(F32)).astype(BF16)
    return hi, lo


def _moba_gate(q, km):
    rows = A_HEADS * N_KBLK
    gt = jnp.concatenate([km] * A_HEADS, axis=0)
    r = lax.broadcasted_iota(jnp.int32, (rows, A_WIDTH), 0)
    c = lax.broadcasted_iota(jnp.int32, (rows, A_WIDTH), 1)
    gt = jnp.where((r // N_KBLK) == (c // A_HEAD_DIM), gt, 0.0)
    g_hi, g_lo = _split_bf16(gt)
    q_hi, q_lo = _split_bf16(q)
    return _dot_nt(g_hi, q_hi) + _dot_nt(g_hi, q_lo) + _dot_nt(g_lo, q_hi)


def _moba_select(gate, own):
    n_idx = lax.broadcasted_iota(jnp.int32, (N_KBLK, Q_TILE), 0)
    pen_rows = []
    for h in range(A_HEADS):
        gh = gate[h * N_KBLK:(h + 1) * N_KBLK]
        rank = jnp.zeros((N_KBLK, Q_TILE), jnp.int32)
        for m in range(N_KBLK):
            gm = gh[m:m + 1]
            beats = (gm > gh) | ((gm == gh) & (m < n_idx))
            rank = rank + jnp.where(beats & (m < own), 1, 0)
        sel = (n_idx < own) & (rank < MOBA_TOPK)
        pen_n = jnp.where(sel, 0.0, NEG_INF)
        pen_d = jnp.full((N_KBLK, Q_TILE), NEG_INF, F32)
        for n in range(N_KBLK):
            pen_d = jnp.where(n_idx == own - n, pen_n[n:n + 1], pen_d)
        pen_rows.append(pen_d)
    pairs = [jnp.concatenate(pen_rows[2 * hp:2 * hp + 2], axis=1) for hp in range(A_HEADS // 2)]
    return jnp.concatenate(pairs, axis=0)


def _t5_bias_kernel(tbl_ref, o_ref):
    hp = pl.program_id(0)
    r = lax.broadcasted_iota(jnp.int32, (Q_TILE, Q_TILE), 0)
    c = lax.broadcasted_iota(jnp.int32, (Q_TILE, Q_TILE), 1)
    max_exact = T5_BUCKETS // 2
    for dd in range(3):
        dist = dd * Q_TILE + c - r
        dc = jnp.maximum(dist, 0)
        df = jnp.maximum(dc.astype(F32), 1.0)
        large = max_exact + (jnp.log(df / max_exact) / math.log(T5_MAX_DIST / max_exact)
                             * (T5_BUCKETS - max_exact)).astype(jnp.int32)
        large = jnp.minimum(large, T5_BUCKETS - 1)
        bucket = jnp.where(dc < max_exact, dc, large)
        for hh in range(2):
            bias = jnp.zeros((Q_TILE, Q_TILE), F32)
            for b in range(T5_BUCKETS):
                bias = jnp.where(bucket == b, tbl_ref[b, 2 * hp + hh], bias)
            bias = bias * LOG2E
            if dd == 0:
                bias = jnp.where(dist >= 0, bias, NEG_INF)
            o_ref[0, dd, :, hh * Q_TILE:(hh + 1) * Q_TILE] = bias


def _fox_cumsum(z, carry_ref):
    lane = lax.broadcasted_iota(jnp.int32, (Q_TILE, LANES), 1)
    logf = jnp.minimum(z, 0.0) - jnp.log1p(jnp.exp(-jnp.abs(z)))
    logf = jnp.where(lane < C_HEADS, logf, 0.0)
    r = lax.broadcasted_iota(jnp.int32, (Q_TILE, Q_TILE), 0)
    c = lax.broadcasted_iota(jnp.int32, (Q_TILE, Q_TILE), 1)
    tri = jnp.where(c <= r, 1.0, 0.0).astype(BF16)
    l1 = logf.astype(BF16)
    rem = logf - l1.astype(F32)
    l2 = rem.astype(BF16)
    l3 = (rem - l2.astype(F32)).astype(BF16)
    cum = _dot(tri, l1) + _dot(tri, l2) + _dot(tri, l3) + carry_ref[...]
    carry_ref[...] = cum[Q_TILE - 1:Q_TILE, :]
    return cum


def _fox_decay_parts(cum):
    cs = cum * LOG2E
    c1 = cs.astype(BF16)
    rem = cs - c1.astype(F32)
    c2 = rem.astype(BF16)
    c3 = (rem - c2.astype(F32)).astype(BF16)
    rin = lax.broadcasted_iota(jnp.int32, (LANES, LANES), 0)
    lout = lax.broadcasted_iota(jnp.int32, (LANES, LANES), 1)
    out = jnp.zeros((Q_TILE, LANES), F32)
    for part, cp in enumerate((c1, c2, c3)):
        place = jnp.where((rin < C_HEADS) & (lout == rin * 8 + part), 1.0, 0.0).astype(BF16)
        out = out + _dot(cp, place)
    return out.astype(BF16)


def _attn_kernel(*refs, kind):
    if kind == "moba":
        q_ref, k_ref, vt_ref, pen_ref, bias_ref, o_ref = refs
    elif kind == "mla":
        q_ref, k_ref, vt_ref, o_ref = refs
    else:
        q_ref, k_ref, vt_ref, c_ref, o_ref, kp_s = refs
    grp = pl.program_id(1)
    lane = lax.broadcasted_iota(jnp.int32, (1, LANES), 1)
    low = lane < A_HEAD_DIM
    key = lax.broadcasted_iota(jnp.int32, (Q_TILE, 2 * Q_TILE), 0)
    qry = lax.broadcasted_iota(jnp.int32, (Q_TILE, 2 * Q_TILE), 1) % Q_TILE
    kw = LANES if kind == "moba" else 2 * LANES
    qw = 2 * LANES if kind == "mla" else LANES
    k_cat = kp_s if kind == "fox" else k_ref

    if kind == "fox":
        rin = lax.broadcasted_iota(jnp.int32, (LANES, LANES), 0)
        lout = lax.broadcasted_iota(jnp.int32, (LANES, LANES), 1)

        def place(pair, off_a, off_b, val):
            base_a = 2 * pair * 8
            base_b = base_a + 8
            sel_a = (lout >= off_a) & (lout < off_a + 3) & (rin == base_a + lout - off_a)
            sel_b = (lout >= off_b) & (lout < off_b + 3) & (rin == base_b + lout - off_b)
            return jnp.where(sel_a | sel_b, val, 0.0).astype(BF16)

        def ones(off_a, off_b):
            in_a = (lane >= off_a) & (lane < off_a + 3)
            in_b = (lane >= off_b) & (lane < off_b + 3)
            return jnp.where(in_a | in_b, 1.0, 0.0)

        for pi in range(PAIR_GROUP):
            ak = (_dot(c_ref[...], place(grp * PAIR_GROUP + pi, A_HEAD_DIM + 3, 3, -1.0))
                  + ones(A_HEAD_DIM, 0)).astype(BF16)
            k = k_ref[:, pi * LANES:(pi + 1) * LANES]
            kp_s[:, pi * kw:pi * kw + LANES] = jnp.where(low, k, ak)
            kp_s[:, pi * kw + LANES:(pi + 1) * kw] = jnp.where(low, ak, k)

    def q_operand(tile, pi):
        zero = jnp.zeros((Q_TILE, LANES), BF16)
        rows = slice(tile * Q_TILE, (tile + 1) * Q_TILE)
        q = q_ref[rows, pi * qw:(pi + 1) * qw]
        if kind == "moba":
            return jnp.concatenate([jnp.where(low, q, zero), jnp.where(low, zero, q)], axis=0)
        if kind == "fox":
            aq = (_dot(c_ref[rows, :], place(grp * PAIR_GROUP + pi, A_HEAD_DIM, 0, 1.0))
                  + ones(A_HEAD_DIM + 3, 3)).astype(BF16)
            qa, qb = jnp.where(low, q, aq), jnp.where(low, aq, q)
        else:
            qa, qb = q[:, :LANES], q[:, LANES:]
        return jnp.concatenate([jnp.concatenate([qa, zero], axis=1),
                                jnp.concatenate([zero, qb], axis=1)], axis=0)

    def run(chains):
        ones_rows = jnp.ones((BF16_ROWS, Q_TILE), BF16)
        n = len(chains)

        def rows_of(c, d):
            nb = chains[c][1]
            return slice((nb - 1 - d) * Q_TILE, (nb - d) * Q_TILE)

        def qk(c, d):
            pi = chains[c][0]
            return _dot_nt(k_cat[rows_of(c, d), pi * kw:(pi + 1) * kw], chains[c][2])

        s_next = [qk(c, 0) for c in range(n)]
        m = [None] * n
        acc = [None] * n
        for d in range(max(c[1] for c in chains)):
            live = [c for c in range(n) if d < chains[c][1]]
            s_cur = list(s_next)
            for c in live:
                if d + 1 < chains[c][1]:
                    s_next[c] = qk(c, d + 1)
            for c in live:
                pi, nb, _ = chains[c]
                sn = s_cur[c]
                shift = None
                if kind == "moba":
                    if d < 2:
                        sn = sn + bias_ref[pi, d]
                    else:
                        shift = bias_ref[pi, 2, 0:1, :]
                    if d > 0:
                        pen = pen_ref[0, nb - 1, pi * N_KBLK + d:pi * N_KBLK + d + 1, :]
                        shift = pen if shift is None else shift + pen
                elif d == 0:
                    sn = jnp.where(key <= qry, sn, NEG_INF)
                bm = jnp.max(sn, axis=0, keepdims=True)
                if shift is not None:
                    bm = bm + shift
                m_new = bm if d == 0 else jnp.maximum(m[c], bm)
                off = m_new if shift is None else m_new - shift
                p = jnp.exp2(sn - off).astype(BF16)
                v_aug = jnp.concatenate(
                    [vt_ref[pi * LANES:(pi + 1) * LANES, rows_of(c, d)], ones_rows], axis=0)
                pv = _dot(v_aug, p)
                acc[c] = pv if d == 0 else acc[c] * jnp.exp2(m[c] - m_new) + pv
                m[c] = m_new
        return [a[:LANES] / a[LANES:LANES + 1] for a in acc]

    for jj in range(N_KBLK // 2):
        tiles = (N_KBLK - 1 - jj, jj)
        chains = [(pi, tile + 1, q_operand(tile, pi)) for pi in range(PAIR_GROUP) for tile in tiles]
        for (pi, nb, _), ot2 in zip(chains, run(chains)):
            ot = jnp.concatenate([ot2[:A_HEAD_DIM, :Q_TILE], ot2[A_HEAD_DIM:, Q_TILE:]], axis=0)
            o_ref[(nb - 1) * Q_TILE:nb * Q_TILE, pi * LANES:(pi + 1) * LANES] = ot.T.astype(BF16)


def _attention(kind, ins, in_specs, n_pairs, extra_scratch=()):
    width = PAIR_GROUP * LANES
    return pl.pallas_call(
        functools.partial(_attn_kernel, kind=kind),
        grid=(BATCH, n_pairs // PAIR_GROUP),
        in_specs=in_specs,
        out_specs=pl.BlockSpec((SEQ, width), lambda b, g: (b, g)),
        out_shape=jax.ShapeDtypeStruct((TOKENS, n_pairs * LANES), BF16),
        scratch_shapes=list(extra_scratch),
        compiler_params=pltpu.CompilerParams(
            dimension_semantics=("arbitrary", "arbitrary"), vmem_limit_bytes=VMEM_LIMIT),
        name="attn_" + kind,
    )(*ins)


def _seq_spec(pair_width, col0):
    return pl.BlockSpec((SEQ, PAIR_GROUP * pair_width), lambda b, g: (b, col0 + g))


def _vt_spec():
    return pl.BlockSpec((PAIR_GROUP * LANES, SEQ), lambda b, g: (g, b))


def _place_heads(w, n_heads, src_stride, src_off, width):
    per_head = w.reshape(w.shape[0], n_heads, src_stride)[:, :, src_off:src_off + width]
    per_head = jnp.pad(per_head, ((0, 0), (0, 0), (0, LANES - width)))
    return per_head.reshape(w.shape[0], n_heads * LANES)


def _row(v, width=None):
    v = v.reshape(1, -1).astype(F32)
    if width is not None and v.shape[1] < width:
        v = jnp.pad(v, ((0, 0), (0, width - v.shape[1])))
    return v


def kernel(x, p, t5_bias, ff1_norm, ff1_w_in, ff1_w_out, mix_norm, ff2_norm, ff2_w_in, ff2_w_out,
           ple_norm, ple_w_gate, ple_w_proj, ab_w_in, mla_q_norm, mla_w_uq, mla_kv_norm, mla_w_ukv,
           ab_w_out, fox_w_in, fox_b_f, fox_w_out, final_norm):
    xt = x.reshape(TOKENS, D_MODEL)

    ffw = {1: (ff1_norm, ff1_w_in.astype(BF16), ff1_w_out.astype(BF16)),
           2: (ff2_norm, ff2_w_in.astype(BF16), ff2_w_out.astype(BF16))}

    def ffn_args(which, layer):
        norm, w_in, w_out = ffw[which]
        args = [_row(norm[layer]), w_in, w_out]
        specs = [_wspec((1, D_MODEL)), _wspec((D_MODEL, 2 * D_FF), layer), _wspec((D_FF, D_MODEL), layer)]
        return args, specs

    ple_gate_bf = ple_w_gate.astype(BF16)
    ple_proj_bf = ple_w_proj.astype(BF16)

    def post(xin, mixes, w_outs, layer, final):
        fa, fs = ffn_args(2, layer)
        ws = [w.astype(BF16) for w in w_outs]
        ins = ([xin] + list(mixes) + ws + fa
               + [_row(ple_norm[layer]), ple_gate_bf, p.reshape(DEPTH, TOKENS, PLE_DIM), ple_proj_bf,
                  _row(final_norm)])
        specs = ([_tok_spec(D_MODEL)] + [_tok_spec(m.shape[1]) for m in mixes]
                 + [_wspec(w.shape) for w in ws] + fs
                 + [_wspec((1, D_MODEL)), _wspec((D_MODEL, D_MODEL), layer),
                    pl.BlockSpec((None, TOKEN_TILE, PLE_DIM), lambda t: (layer, t, 0)),
                    _wspec((PLE_DIM, D_MODEL), layer), _wspec((1, D_MODEL))])
        return _token_call(
            "post%d" % layer,
            functools.partial(_post_kernel, n_mix=len(mixes), final=final), ins, specs,
            jax.ShapeDtypeStruct((TOKENS, D_MODEL), F32), _tok_spec(D_MODEL))

    w_ab = ab_w_in[0]
    w_qk = w_ab[:, :2 * A_WIDTH].astype(BF16)
    w_vt = w_ab[:, 2 * A_WIDTH:3 * A_WIDTH].T.astype(BF16)
    c0 = 3 * A_WIDTH
    kr0 = MLA_Q_RANK + MLA_KV_RANK
    w_c = jnp.concatenate(
        [w_ab[:, c0:c0 + kr0], jnp.zeros((D_MODEL, MLA_NOPE), F32), w_ab[:, c0 + kr0:],
         jnp.zeros((D_MODEL, LANES - MLA_NOPE - MLA_ROPE), F32)], axis=1).astype(BF16)
    w_uq = _place_heads(mla_w_uq[0], B_HEADS, MLA_NOPE + MLA_ROPE, 0, MLA_NOPE + MLA_ROPE).astype(BF16)
    w_ukv_k = _place_heads(mla_w_ukv[0], B_HEADS, MLA_NOPE + MLA_V, 0, MLA_NOPE).astype(BF16)
    w_ukv = mla_w_ukv[0].reshape(MLA_KV_RANK, B_HEADS, MLA_NOPE + MLA_V)
    w_ukv_vt = w_ukv[:, :, MLA_NOPE:].reshape(MLA_KV_RANK, B_HEADS * MLA_V).T.astype(BF16)
    half = MLA_ROPE // 2
    inv = ROPE_THETA ** (-np.arange(half, dtype=np.float64) / half)
    inv_lane = np.zeros((1, LANES), np.float32)
    inv_lane[0, MLA_NOPE:MLA_NOPE + half] = inv
    inv_lane[0, MLA_NOPE + half:MLA_NOPE + MLA_ROPE] = inv
    inv_lane = jnp.asarray(inv_lane)

    fa, fs = ffn_args(1, 0)
    ins = ([xt] + fa + [_row(mix_norm[0]), w_qk, w_vt, w_c, _row(mla_q_norm[0]), w_uq,
                        _row(mla_kv_norm[0]), w_ukv_k, w_ukv_vt, inv_lane])
    specs = ([_tok_spec(D_MODEL)] + fs
             + [_wspec((1, D_MODEL)), _wspec(w_qk.shape), _wspec(w_vt.shape), _wspec(w_c.shape),
                _wspec((1, MLA_Q_RANK)), _wspec(w_uq.shape), _wspec((1, MLA_KV_RANK)),
                _wspec(w_ukv_k.shape), _wspec(w_ukv_vt.shape), _wspec((1, LANES))])
    pen_rows = A_HEADS // 2 * N_KBLK
    outs = (jax.ShapeDtypeStruct((TOKENS, D_MODEL), F32),
            jax.ShapeDtypeStruct((TOKENS, 2 * A_WIDTH), BF16),
            jax.ShapeDtypeStruct((A_WIDTH, TOKENS), BF16),
            jax.ShapeDtypeStruct((BATCH, N_KBLK, pen_rows, 2 * Q_TILE), F32),
            jax.ShapeDtypeStruct((TOKENS, B_HEADS * LANES), BF16),
            jax.ShapeDtypeStruct((TOKENS, B_HEADS * LANES), BF16),
            jax.ShapeDtypeStruct((B_HEADS * MLA_V, TOKENS), BF16))
    out_specs = (_tok_spec(D_MODEL), _tok_spec(2 * A_WIDTH), _tok_t_spec(A_WIDTH),
                 pl.BlockSpec((1, TILE_R, pen_rows, 2 * Q_TILE),
                              lambda t: (t // (N_KBLK // TILE_R), t % (N_KBLK // TILE_R), 0, 0)),
                 _tok_spec(B_HEADS * LANES), _tok_spec(B_HEADS * LANES),
                 _tok_t_spec(B_HEADS * MLA_V))
    x1, qk_a, vt_a, pen, q_mla, k_mla, vt_mla = _token_call(
        "pre0", _pre0_kernel, ins, specs, outs, out_specs,
        scratch=[pltpu.VMEM((N_KBLK, A_WIDTH), F32)])

    bias = pl.pallas_call(
        _t5_bias_kernel,
        grid=(A_HEADS // 2,),
        in_specs=[pl.BlockSpec(memory_space=pltpu.SMEM)],
        out_specs=pl.BlockSpec((1, 3, Q_TILE, 2 * Q_TILE), lambda h: (h, 0, 0, 0)),
        out_shape=jax.ShapeDtypeStruct((A_HEADS // 2, 3, Q_TILE, 2 * Q_TILE), F32),
        name="t5_bias_tiles",
    )(t5_bias.astype(F32))

    na = A_HEADS // 2
    o_a = _attention(
        "moba", [qk_a, qk_a, vt_a, pen, bias],
        [_seq_spec(LANES, 0), _seq_spec(LANES, na // PAIR_GROUP), _vt_spec(),
         pl.BlockSpec((1, N_KBLK, PAIR_GROUP * N_KBLK, 2 * Q_TILE), lambda b, g: (b, 0, g, 0)),
         pl.BlockSpec((PAIR_GROUP, 3, Q_TILE, 2 * Q_TILE), lambda b, g: (g, 0, 0, 0))], na)
    o_b = _attention(
        "mla", [q_mla, k_mla, vt_mla],
        [_seq_spec(2 * LANES, 0), _seq_spec(2 * LANES, 0), _vt_spec()], B_HEADS // 2)
    w_o = ab_w_out[0]
    xt = post(x1, [o_a, o_b], [w_o[:A_WIDTH], w_o[A_WIDTH:]], 0, DEPTH == 1)

    w_fox = fox_w_in[0]
    w_qk = w_fox[:, :2 * C_WIDTH].astype(BF16)
    w_vt = w_fox[:, 2 * C_WIDTH:3 * C_WIDTH].T.astype(BF16)
    w_f = jnp.pad(w_fox[:, 3 * C_WIDTH:], ((0, 0), (0, LANES - C_HEADS))).astype(BF16)
    fa, fs = ffn_args(1, 1)
    ins = [xt] + fa + [_row(mix_norm[1]), w_qk, w_vt, w_f, _row(fox_b_f[0], LANES)]
    specs = ([_tok_spec(D_MODEL)] + fs
             + [_wspec((1, D_MODEL)), _wspec(w_qk.shape), _wspec(w_vt.shape), _wspec(w_f.shape),
                _wspec((1, LANES))])
    outs = (jax.ShapeDtypeStruct((TOKENS, D_MODEL), F32),
            jax.ShapeDtypeStruct((TOKENS, 2 * C_WIDTH), BF16),
            jax.ShapeDtypeStruct((C_WIDTH, TOKENS), BF16),
            jax.ShapeDtypeStruct((TOKENS, LANES), BF16))
    out_specs = (_tok_spec(D_MODEL), _tok_spec(2 * C_WIDTH), _tok_t_spec(C_WIDTH), _tok_spec(LANES))
    x1, qk_c, vt_c, caug = _token_call("pre1", _pre1_kernel, ins, specs, outs, out_specs,
                                       scratch=[pltpu.VMEM((1, LANES), F32)])

    nc = C_HEADS // 2
    o_c = _attention(
        "fox", [qk_c, qk_c, vt_c, caug],
        [_seq_spec(LANES, 0), _seq_spec(LANES, nc // PAIR_GROUP), _vt_spec(),
         pl.BlockSpec((SEQ, LANES), lambda b, g: (b, 0))],
        nc, extra_scratch=[pltpu.VMEM((SEQ, PAIR_GROUP * 2 * LANES), BF16)])
    xt = post(x1, [o_c], [fox_w_out[0]], 1, True)
    return xt.reshape(BATCH, SEQ, D_MODEL)
```

```python
import functools
import math

import jax
import jax.numpy as jnp
import numpy as np
from jax import lax
from jax.experimental import pallas as pl
from jax.experimental.pallas import tpu as pltpu

F32 = jnp.float32
BF16 = jnp.bfloat16

D_MODEL = 1024
BATCH = 8
SEQ = 2048
DEPTH = 2
PLE_DIM = 256
D_FF = 2816
EPS = 1e-6

A_HEADS = 8
A_HEAD_DIM = 64
MOBA_BLOCK = 256
MOBA_TOPK = 3

B_HEADS = 8
MLA_Q_RANK = 256
MLA_KV_RANK = 128
MLA_NOPE = 64
MLA_ROPE = 32
MLA_V = 64
ROPE_THETA = 10000.0

T5_BUCKETS = 32
T5_MAX_DIST = 128

C_HEADS = 16
C_HEAD_DIM = 64

A_WIDTH = A_HEADS * A_HEAD_DIM
C_WIDTH = C_HEADS * C_HEAD_DIM

TOKENS = BATCH * SEQ
LANES = 128
BF16_ROWS = 16
PAIR_GROUP = 2
Q_TILE = MOBA_BLOCK
N_KBLK = SEQ // Q_TILE
TOKEN_TILE = 2 * Q_TILE
TILE_R = TOKEN_TILE // Q_TILE
FF_CHUNK = D_FF // 2
VMEM_LIMIT = 56 * 1024 * 1024
NEG_INF = float("-inf")
LOG2E = math.log2(math.e)
QSCALE_64 = A_HEAD_DIM ** -0.5 * LOG2E
QSCALE_MLA = (MLA_NOPE + MLA_ROPE) ** -0.5 * LOG2E


def _wspec(shape, layer=None):
    nd = len(shape)
    if layer is None:
        return pl.BlockSpec(shape, lambda *_: (0,) * nd, pipeline_mode=pl.Buffered(1))
    return pl.BlockSpec((None,) + tuple(shape), lambda *_: (layer,) + (0,) * nd,
                        pipeline_mode=pl.Buffered(1))


def _dot(a, b):
    return jnp.dot(a, b, preferred_element_type=F32)


def _dot_nt(a, b):
    return lax.dot_general(a, b, (((1,), (1,)), ((), ())), preferred_element_type=F32)


def _rms(x, g):
    return x * lax.rsqrt(jnp.mean(x * x, axis=-1, keepdims=True) + EPS) * g


def _ffn(x, g, win_ref, wo_ref):
    h = _rms(x, g).astype(BF16)
    acts = []
    for c in range(D_FF // FF_CHUNK):
        lo, hi = c * FF_CHUNK, (c + 1) * FF_CHUNK
        a = _dot(h, win_ref[:, lo:hi])
        u = _dot(h, win_ref[:, D_FF + lo:D_FF + hi])
        acts.append((a * jax.nn.sigmoid(a) * u).astype(BF16))
    return x + 0.5 * _dot(jnp.concatenate(acts, axis=1), wo_ref[...])


def _rope_tables(inv_lane, tile_idx):
    pos0 = (tile_idx * TOKEN_TILE) % SEQ
    pos = (pos0 + lax.broadcasted_iota(jnp.int32, (TOKEN_TILE, LANES), 0)).astype(F32)
    lane = lax.broadcasted_iota(jnp.int32, (TOKEN_TILE, LANES), 1)
    ang = pos * inv_lane
    is_x1 = (lane >= MLA_NOPE) & (lane < MLA_NOPE + MLA_ROPE // 2)
    is_x2 = (lane >= MLA_NOPE + MLA_ROPE // 2) & (lane < MLA_NOPE + MLA_ROPE)
    cos_t = jnp.where(is_x1 | is_x2, jnp.cos(ang), 1.0)
    sin = jnp.sin(ang)
    sin_t = jnp.where(is_x1, -sin, jnp.where(is_x2, sin, 0.0))
    return cos_t, sin_t, is_x1


def _rope_block(xb, cos_t, sin_t, is_x1):
    half = MLA_ROPE // 2
    partner = jnp.where(is_x1, pltpu.roll(xb, LANES - half, 1), pltpu.roll(xb, half, 1))
    return xb * cos_t + partner * sin_t


def _pre0_kernel(x_ref, g1_ref, win_ref, wo_ref, gmix_ref, wqk_ref, wvt_ref, wc_ref,
                 qn_ref, wuq_ref, kvn_ref, wukvk_ref, wukvvt_ref, inv_ref,
                 x1_ref, qk_ref, vt_ref, pen_ref, qm_ref, kmla_ref, vmt_ref, km_s):
    t = pl.program_id(0)
    x1 = _ffn(x_ref[...], g1_ref[...], win_ref, wo_ref)
    x1_ref[...] = x1
    h = _rms(x1, gmix_ref[...]).astype(BF16)
    qk = _dot(h, wqk_ref[...])
    q = qk[:, :A_WIDTH]
    k = qk[:, A_WIDTH:]
    qk_ref[:, :A_WIDTH] = (q * QSCALE_64).astype(BF16)
    qk_ref[:, A_WIDTH:] = k.astype(BF16)

    @pl.when(t == 0)
    def _():
        km_s[...] = jnp.zeros_like(km_s)

    c = _dot(h, wc_ref[...])
    gates = []
    for r in range(TILE_R):
        own = (t * TILE_R + r) % N_KBLK
        part = slice(r * Q_TILE, (r + 1) * Q_TILE)
        km_s[pl.ds(own, 1), :] = jnp.mean(k[part], axis=0, keepdims=True)
        gates.append((own, _moba_gate(q[part], km_s[...])))
    vt_ref[...] = _dot_nt(wvt_ref[...], h).astype(BF16)
    for r, (own, gate) in enumerate(gates):
        pen_ref[0, r] = _moba_select(gate, own)

    cq = c[:, :MLA_Q_RANK]
    ckv = c[:, MLA_Q_RANK:MLA_Q_RANK + MLA_KV_RANK]
    kr = c[:, MLA_Q_RANK + MLA_KV_RANK:]
    cqn = _rms(cq, qn_ref[...]).astype(BF16)
    ckvn = _rms(ckv, kvn_ref[...]).astype(BF16)
    qm = _dot(cqn, wuq_ref[...])
    kn = _dot(ckvn, wukvk_ref[...])
    vmt_ref[...] = _dot_nt(wukvvt_ref[...], ckvn).astype(BF16)

    cos_t, sin_t, is_x1 = _rope_tables(inv_ref[...], t)
    cos_q = cos_t * QSCALE_MLA
    sin_q = sin_t * QSCALE_MLA
    krr = _rope_block(kr, cos_t, sin_t, is_x1)
    for hb in range(B_HEADS):
        sl = slice(hb * LANES, (hb + 1) * LANES)
        qm_ref[:, sl] = _rope_block(qm[:, sl], cos_q, sin_q, is_x1).astype(BF16)
        kmla_ref[:, sl] = (kn[:, sl] + krr).astype(BF16)


def _pre1_kernel(x_ref, g1_ref, win_ref, wo_ref, gmix_ref, wqk_ref, wvt_ref, wf_ref, bf_ref,
                 x1_ref, qk_ref, vt_ref, caug_ref, carry_s):
    t = pl.program_id(0)
    x1 = _ffn(x_ref[...], g1_ref[...], win_ref, wo_ref)
    x1_ref[...] = x1
    h = _rms(x1, gmix_ref[...]).astype(BF16)

    @pl.when(t % (N_KBLK // TILE_R) == 0)
    def _():
        carry_s[...] = jnp.zeros_like(carry_s)

    z = _dot(h, wf_ref[...]) + bf_ref[...]
    qk = _dot(h, wqk_ref[...])
    qk_ref[:, :C_WIDTH] = (qk[:, :C_WIDTH] * QSCALE_64).astype(BF16)
    qk_ref[:, C_WIDTH:] = qk[:, C_WIDTH:].astype(BF16)
    cums = [_fox_cumsum(z[r * Q_TILE:(r + 1) * Q_TILE], carry_s) for r in range(TILE_R)]
    vt_ref[...] = _dot_nt(wvt_ref[...], h).astype(BF16)
    for r, cum in enumerate(cums):
        caug_ref[r * Q_TILE:(r + 1) * Q_TILE, :] = _fox_decay_parts(cum)


def _tok_spec(width):
    return pl.BlockSpec((TOKEN_TILE, width), lambda i: (i, 0))


def _tok_t_spec(height):
    return pl.BlockSpec((height, TOKEN_TILE), lambda i: (0, i))


def _token_call(name, body, ins, in_specs, outs, out_specs, scratch=()):
    return pl.pallas_call(
        body,
        grid=(TOKENS // TOKEN_TILE,),
        in_specs=in_specs,
        out_specs=out_specs,
        out_shape=outs,
        scratch_shapes=list(scratch),
        compiler_params=pltpu.CompilerParams(
            dimension_semantics=("arbitrary",), vmem_limit_bytes=VMEM_LIMIT),
        name=name,
    )(*ins)


def _post_kernel(*refs, n_mix, final):
    x_ref = refs[0]
    o_refs = refs[1:1 + n_mix]
    w_refs = refs[1 + n_mix:1 + 2 * n_mix]
    (g2_ref, win_ref, wo_ref, gple_ref, wg_ref, p_ref, wp_ref, gfin_ref,
     out_ref) = refs[1 + 2 * n_mix:]
    x = x_ref[...]
    for o_ref, w_ref in zip(o_refs, w_refs):
        x = x + _dot(o_ref[...], w_ref[...])
    x = _ffn(x, g2_ref[...], win_ref, wo_ref)
    gate = jax.nn.sigmoid(_dot(_rms(x, gple_ref[...]).astype(BF16), wg_ref[...]))
    x = x + gate * _dot(p_ref[...].astype(BF16), wp_ref[...])
    if final:
        x = _rms(x, gfin_ref[...])
    out_ref[...] = x


def _split_bf16(x):
    hi = x.astype(BF16)
    lo = (x - hi.astype(F32)).astype(BF16)
    return hi, lo


def _moba_gate(q, km):
    rows = A_HEADS * N_KBLK
    gt = jnp.concatenate([km] * A_HEADS, axis=0)
    r = lax.broadcasted_iota(jnp.int32, (rows, A_WIDTH), 0)
    c = lax.broadcasted_iota(jnp.int32, (rows, A_WIDTH), 1)
    gt = jnp.where((r // N_KBLK) == (c // A_HEAD_DIM), gt, 0.0)
    g_hi, g_lo = _split_bf16(gt)
    q_hi, q_lo = _split_bf16(q)
    return _dot_nt(g_hi, q_hi) + _dot_nt(g_hi, q_lo) + _dot_nt(g_lo, q_hi)


def _moba_select(gate, own):
    n_idx = lax.broadcasted_iota(jnp.int32, (N_KBLK, Q_TILE), 0)
    pen_rows = []
    for h in range(A_HEADS):
        gh = gate[h * N_KBLK:(h + 1) * N_KBLK]
        rank = jnp.zeros((N_KBLK, Q_TILE), jnp.int32)
        for m in range(N_KBLK):
            gm = gh[m:m + 1]
            beats = (gm > gh) | ((gm == gh) & (m < n_idx))
            rank = rank + jnp.where(beats & (m < own), 1, 0)
        sel = (n_idx < own) & (rank < MOBA_TOPK)
        pen_n = jnp.where(sel, 0.0, NEG_INF)
        pen_d = jnp.full((N_KBLK, Q_TILE), NEG_INF, F32)
        for n in range(N_KBLK):
            pen_d = jnp.where(n_idx == own - n, pen_n[n:n + 1], pen_d)
        pen_rows.append(pen_d)
    pairs = [jnp.concatenate(pen_rows[2 * hp:2 * hp + 2], axis=1) for hp in range(A_HEADS // 2)]
    return jnp.concatenate(pairs, axis=0)


def _t5_bias_kernel(tbl_ref, o_ref):
    hp = pl.program_id(0)
    r = lax.broadcasted_iota(jnp.int32, (Q_TILE, Q_TILE), 0)
    c = lax.broadcasted_iota(jnp.int32, (Q_TILE, Q_TILE), 1)
    max_exact = T5_BUCKETS // 2
    for dd in range(3):
        dist = dd * Q_TILE + c - r
        dc = jnp.maximum(dist, 0)
        df = jnp.maximum(dc.astype(F32), 1.0)
        large = max_exact + (jnp.log(df / max_exact) / math.log(T5_MAX_DIST / max_exact)
                             * (T5_BUCKETS - max_exact)).astype(jnp.int32)
        large = jnp.minimum(large, T5_BUCKETS - 1)
        bucket = jnp.where(dc < max_exact, dc, large)
        for hh in range(2):
            bias = jnp.zeros((Q_TILE, Q_TILE), F32)
            for b in range(T5_BUCKETS):
                bias = jnp.where(bucket == b, tbl_ref[b, 2 * hp + hh], bias)
            bias = bias * LOG2E
            if dd == 0:
                bias = jnp.where(dist >= 0, bias, NEG_INF)
            o_ref[0, dd, :, hh * Q_TILE:(hh + 1) * Q_TILE] = bias


def _fox_cumsum(z, carry_ref):
    lane = lax.broadcasted_iota(jnp.int32, (Q_TILE, LANES), 1)
    logf = jnp.minimum(z, 0.0) - jnp.log1p(jnp.exp(-jnp.abs(z)))
    logf = jnp.where(lane < C_HEADS, logf, 0.0)
    r = lax.broadcasted_iota(jnp.int32, (Q_TILE, Q_TILE), 0)
    c = lax.broadcasted_iota(jnp.int32, (Q_TILE, Q_TILE), 1)
    tri = jnp.where(c <= r, 1.0, 0.0).astype(BF16)
    l1 = logf.astype(BF16)
    rem = logf - l1.astype(F32)
    l2 = rem.astype(BF16)
    l3 = (rem - l2.astype(F32)).astype(BF16)
    cum = _dot(tri, l1) + _dot(tri, l2) + _dot(tri, l3) + carry_ref[...]
    carry_ref[...] = cum[Q_TILE - 1:Q_TILE, :]
    return cum


def _fox_decay_parts(cum):
    cs = cum * LOG2E
    c1 = cs.astype(BF16)
    rem = cs - c1.astype(F32)
    c2 = rem.astype(BF16)
    c3 = (rem - c2.astype(F32)).astype(BF16)
    rin = lax.broadcasted_iota(jnp.int32, (LANES, LANES), 0)
    lout = lax.broadcasted_iota(jnp.int32, (LANES, LANES), 1)
    out = jnp.zeros((Q_TILE, LANES), F32)
    for part, cp in enumerate((c1, c2, c3)):
        place = jnp.where((rin < C_HEADS) & (lout == rin * 8 + part), 1.0, 0.0).astype(BF16)
        out = out + _dot(cp, place)
    return out.astype(BF16)


def _attn_kernel(*refs, kind):
    if kind == "moba":
        q_ref, k_ref, vt_ref, pen_ref, bias_ref, o_ref = refs
    elif kind == "mla":
        q_ref, k_ref, vt_ref, o_ref = refs
    else:
        q_ref, k_ref, vt_ref, c_ref, o_ref, kp_s = refs
    grp = pl.program_id(1)
    lane = lax.broadcasted_iota(jnp.int32, (1, LANES), 1)
    low = lane < A_HEAD_DIM
    key = lax.broadcasted_iota(jnp.int32, (Q_TILE, 2 * Q_TILE), 0)
    qry = lax.broadcasted_iota(jnp.int32, (Q_TILE, 2 * Q_TILE), 1) % Q_TILE
    kw = LANES if kind == "moba" else 2 * LANES
    qw = 2 * LANES if kind == "mla" else LANES
    k_cat = kp_s if kind == "fox" else k_ref

    if kind == "fox":
        rin = lax.broadcasted_iota(jnp.int32, (LANES, LANES), 0)
        lout = lax.broadcasted_iota(jnp.int32, (LANES, LANES), 1)

        def place(pair, off_a, off_b, val):
            base_a = 2 * pair * 8
            base_b = base_a + 8
            sel_a = (lout >= off_a) & (lout < off_a + 3) & (rin == base_a + lout - off_a)
            sel_b = (lout >= off_b) & (lout < off_b + 3) & (rin == base_b + lout - off_b)
            return jnp.where(sel_a | sel_b, val, 0.0).astype(BF16)

        def ones(off_a, off_b):
            in_a = (lane >= off_a) & (lane < off_a + 3)
            in_b = (lane >= off_b) & (lane < off_b + 3)
            return jnp.where(in_a | in_b, 1.0, 0.0)

        for pi in range(PAIR_GROUP):
            ak = (_dot(c_ref[...], place(grp * PAIR_GROUP + pi, A_HEAD_DIM + 3, 3, -1.0))
                  + ones(A_HEAD_DIM, 0)).astype(BF16)
            k = k_ref[:, pi * LANES:(pi + 1) * LANES]
            kp_s[:, pi * kw:pi * kw + LANES] = jnp.where(low, k, ak)
            kp_s[:, pi * kw + LANES:(pi + 1) * kw] = jnp.where(low, ak, k)

    def q_operand(tile, pi):
        zero = jnp.zeros((Q_TILE, LANES), BF16)
        rows = slice(tile * Q_TILE, (tile + 1) * Q_TILE)
        q = q_ref[rows, pi * qw:(pi + 1) * qw]
        if kind == "moba":
            return jnp.concatenate([jnp.where(low, q, zero), jnp.where(low, zero, q)], axis=0)
        if kind == "fox":
            aq = (_dot(c_ref[rows, :], place(grp * PAIR_GROUP + pi, A_HEAD_DIM, 0, 1.0))
                  + ones(A_HEAD_DIM + 3, 3)).astype(BF16)
            qa, qb = jnp.where(low, q, aq), jnp.where(low, aq, q)
        else:
            qa, qb = q[:, :LANES], q[:, LANES:]
        return jnp.concatenate([jnp.concatenate([qa, zero], axis=1),
                                jnp.concatenate([zero, qb], axis=1)], axis=0)

    def run(chains):
        ones_rows = jnp.ones((BF16_ROWS, Q_TILE), BF16)
        n = len(chains)

        def rows_of(c, d):
            nb = chains[c][1]
            return slice((nb - 1 - d) * Q_TILE, (nb - d) * Q_TILE)

        def qk(c, d):
            pi = chains[c][0]
            return _dot_nt(k_cat[rows_of(c, d), pi * kw:(pi + 1) * kw], chains[c][2])

        s_next = [qk(c, 0) for c in range(n)]
        m = [None] * n
        acc = [None] * n
        for d in range(max(c[1] for c in chains)):
            live = [c for c in range(n) if d < chains[c][1]]
            s_cur = list(s_next)
            for c in live:
                if d + 1 < chains[c][1]:
                    s_next[c] = qk(c, d + 1)
            for c in live:
                pi, nb, _ = chains[c]
                sn = s_cur[c]
                shift = None
                if kind == "moba":
                    if d < 2:
                        sn = sn + bias_ref[pi, d]
                    else:
                        shift = bias_ref[pi, 2, 0:1, :]
                    if d > 0:
                        pen = pen_ref[0, nb - 1, pi * N_KBLK + d:pi * N_KBLK + d + 1, :]
                        shift = pen if shift is None else shift + pen
                elif d == 0:
                    sn = jnp.where(key <= qry, sn, NEG_INF)
                bm = jnp.max(sn, axis=0, keepdims=True)
                if shift is not None:
                    bm = bm + shift
                m_new = bm if d == 0 else jnp.maximum(m[c], bm)
                off = m_new if shift is None else m_new - shift
                p = jnp.exp2(sn - off).astype(BF16)
                v_aug = jnp.concatenate(
                    [vt_ref[pi * LANES:(pi + 1) * LANES, rows_of(c, d)], ones_rows], axis=0)
                pv = _dot(v_aug, p)
                acc[c] = pv if d == 0 else acc[c] * jnp.exp2(m[c] - m_new) + pv
                m[c] = m_new
        return [a[:LANES] / a[LANES:LANES + 1] for a in acc]

    for jj in range(N_KBLK // 2):
        tiles = (N_KBLK - 1 - jj, jj)
        chains = [(pi, tile + 1, q_operand(tile, pi)) for pi in range(PAIR_GROUP) for tile in tiles]
        for (pi, nb, _), ot2 in zip(chains, run(chains)):
            ot = jnp.concatenate([ot2[:A_HEAD_DIM, :Q_TILE], ot2[A_HEAD_DIM:, Q_TILE:]], axis=0)
            o_ref[(nb - 1) * Q_TILE:nb * Q_TILE, pi * LANES:(pi + 1) * LANES] = ot.T.astype(BF16)


def _attention(kind, ins, in_specs, n_pairs, extra_scratch=()):
    width = PAIR_GROUP * LANES
    return pl.pallas_call(
        functools.partial(_attn_kernel, kind=kind),
        grid=(BATCH, n_pairs // PAIR_GROUP),
        in_specs=in_specs,
        out_specs=pl.BlockSpec((SEQ, width), lambda b, g: (b, g)),
        out_shape=jax.ShapeDtypeStruct((TOKENS, n_pairs * LANES), BF16),
        scratch_shapes=list(extra_scratch),
        compiler_params=pltpu.CompilerParams(
            dimension_semantics=("arbitrary", "arbitrary"), vmem_limit_bytes=VMEM_LIMIT),
        name="attn_" + kind,
    )(*ins)


def _seq_spec(pair_width, col0):
    return pl.BlockSpec((SEQ, PAIR_GROUP * pair_width), lambda b, g: (b, col0 + g))


def _vt_spec():
    return pl.BlockSpec((PAIR_GROUP * LANES, SEQ), lambda b, g: (g, b))


def _place_heads(w, n_heads, src_stride, src_off, width):
    per_head = w.reshape(w.shape[0], n_heads, src_stride)[:, :, src_off:src_off + width]
    per_head = jnp.pad(per_head, ((0, 0), (0, 0), (0, LANES - width)))
    return per_head.reshape(w.shape[0], n_heads * LANES)


def _row(v, width=None):
    v = v.reshape(1, -1).astype(F32)
    if width is not None and v.shape[1] < width:
        v = jnp.pad(v, ((0, 0), (0, width - v.shape[1])))
    return v


def kernel(x, p, t5_bias, ff1_norm, ff1_w_in, ff1_w_out, mix_norm, ff2_norm, ff2_w_in, ff2_w_out,
           ple_norm, ple_w_gate, ple_w_proj, ab_w_in, mla_q_norm, mla_w_uq, mla_kv_norm, mla_w_ukv,
           ab_w_out, fox_w_in, fox_b_f, fox_w_out, final_norm):
    xt = x.reshape(TOKENS, D_MODEL)

    ffw = {1: (ff1_norm, ff1_w_in.astype(BF16), ff1_w_out.astype(BF16)),
           2: (ff2_norm, ff2_w_in.astype(BF16), ff2_w_out.astype(BF16))}

    def ffn_args(which, layer):
        norm, w_in, w_out = ffw[which]
        args = [_row(norm[layer]), w_in, w_out]
        specs = [_wspec((1, D_MODEL)), _wspec((D_MODEL, 2 * D_FF), layer), _wspec((D_FF, D_MODEL), layer)]
        return args, specs

    ple_gate_bf = ple_w_gate.astype(BF16)
    ple_proj_bf = ple_w_proj.astype(BF16)

    def post(xin, mixes, w_outs, layer, final):
        fa, fs = ffn_args(2, layer)
        ws = [w.astype(BF16) for w in w_outs]
        ins = ([xin] + list(mixes) + ws + fa
               + [_row(ple_norm[layer]), ple_gate_bf, p.reshape(DEPTH, TOKENS, PLE_DIM), ple_proj_bf,
                  _row(final_norm)])
        specs = ([_tok_spec(D_MODEL)] + [_tok_spec(m.shape[1]) for m in mixes]
                 + [_wspec(w.shape) for w in ws] + fs
                 + [_wspec((1, D_MODEL)), _wspec((D_MODEL, D_MODEL), layer),
                    pl.BlockSpec((None, TOKEN_TILE, PLE_DIM), lambda t: (layer, t, 0)),
                    _wspec((PLE_DIM, D_MODEL), layer), _wspec((1, D_MODEL))])
        return _token_call(
            "post%d" % layer,
            functools.partial(_post_kernel, n_mix=len(mixes), final=final), ins, specs,
            jax.ShapeDtypeStruct((TOKENS, D_MODEL), F32), _tok_spec(D_MODEL))

    w_ab = ab_w_in[0]
    w_qk = w_ab[:, :2 * A_WIDTH].astype(BF16)
    w_vt = w_ab[:, 2 * A_WIDTH:3 * A_WIDTH].T.astype(BF16)
    c0 = 3 * A_WIDTH
    kr0 = MLA_Q_RANK + MLA_KV_RANK
    w_c = jnp.concatenate(
        [w_ab[:, c0:c0 + kr0], jnp.zeros((D_MODEL, MLA_NOPE), F32), w_ab[:, c0 + kr0:],
         jnp.zeros((D_MODEL, LANES - MLA_NOPE - MLA_ROPE), F32)], axis=1).astype(BF16)
    w_uq = _place_heads(mla_w_uq[0], B_HEADS, MLA_NOPE + MLA_ROPE, 0, MLA_NOPE + MLA_ROPE).astype(BF16)
    w_ukv_k = _place_heads(mla_w_ukv[0], B_HEADS, MLA_NOPE + MLA_V, 0, MLA_NOPE).astype(BF16)
    w_ukv = mla_w_ukv[0].reshape(MLA_KV_RANK, B_HEADS, MLA_NOPE + MLA_V)
    w_ukv_vt = w_ukv[:, :, MLA_NOPE:].reshape(MLA_KV_RANK, B_HEADS * MLA_V).T.astype(BF16)
    half = MLA_ROPE // 2
    inv = ROPE_THETA ** (-np.arange(half, dtype=np.float64) / half)
    inv_lane = np.zeros((1, LANES), np.float32)
    inv_lane[0, MLA_NOPE:MLA_NOPE + half] = inv
    inv_lane[0, MLA_NOPE + half:MLA_NOPE + MLA_ROPE] = inv
    inv_lane = jnp.asarray(inv_lane)

    fa, fs = ffn_args(1, 0)
    ins = ([xt] + fa + [_row(mix_norm[0]), w_qk, w_vt, w_c, _row(mla_q_norm[0]), w_uq,
                        _row(mla_kv_norm[0]), w_ukv_k, w_ukv_vt, inv_lane])
    specs = ([_tok_spec(D_MODEL)] + fs
             + [_wspec((1, D_MODEL)), _wspec(w_qk.shape), _wspec(w_vt.shape), _wspec(w_c.shape),
                _wspec((1, MLA_Q_RANK)), _wspec(w_uq.shape), _wspec((1, MLA_KV_RANK)),
                _wspec(w_ukv_k.shape), _wspec(w_ukv_vt.shape), _wspec((1, LANES))])
    pen_rows = A_HEADS // 2 * N_KBLK
    outs = (jax.ShapeDtypeStruct((TOKENS, D_MODEL), F32),
            jax.ShapeDtypeStruct((TOKENS, 2 * A_WIDTH), BF16),
            jax.ShapeDtypeStruct((A_WIDTH, TOKENS), BF16),
            jax.ShapeDtypeStruct((BATCH, N_KBLK, pen_rows, 2 * Q_TILE), F32),
            jax.ShapeDtypeStruct((TOKENS, B_HEADS * LANES), BF16),
            jax.ShapeDtypeStruct((TOKENS, B_HEADS * LANES), BF16),
            jax.ShapeDtypeStruct((B_HEADS * MLA_V, TOKENS), BF16))
    out_specs = (_tok_spec(D_MODEL), _tok_spec(2 * A_WIDTH), _tok_t_spec(A_WIDTH),
                 pl.BlockSpec((1, TILE_R, pen_rows, 2 * Q_TILE),
                              lambda t: (t // (N_KBLK // TILE_R), t % (N_KBLK // TILE_R), 0, 0)),
                 _tok_spec(B_HEADS * LANES), _tok_spec(B_HEADS * LANES),
                 _tok_t_spec(B_HEADS * MLA_V))
    x1, qk_a, vt_a, pen, q_mla, k_mla, vt_mla = _token_call(
        "pre0", _pre0_kernel, ins, specs, outs, out_specs,
        scratch=[pltpu.VMEM((N_KBLK, A_WIDTH), F32)])

    bias = pl.pallas_call(
        _t5_bias_kernel,
        grid=(A_HEADS // 2,),
        in_specs=[pl.BlockSpec(memory_space=pltpu.SMEM)],
        out_specs=pl.BlockSpec((1, 3, Q_TILE, 2 * Q_TILE), lambda h: (h, 0, 0, 0)),
        out_shape=jax.ShapeDtypeStruct((A_HEADS // 2, 3, Q_TILE, 2 * Q_TILE), F32),
        name="t5_bias_tiles",
    )(t5_bias.astype(F32))

    na = A_HEADS // 2
    o_a = _attention(
        "moba", [qk_a, qk_a, vt_a, pen, bias],
        [_seq_spec(LANES, 0), _seq_spec(LANES, na // PAIR_GROUP), _vt_spec(),
         pl.BlockSpec((1, N_KBLK, PAIR_GROUP * N_KBLK, 2 * Q_TILE), lambda b, g: (b, 0, g, 0)),
         pl.BlockSpec((PAIR_GROUP, 3, Q_TILE, 2 * Q_TILE), lambda b, g: (g, 0, 0, 0))], na)
    o_b = _attention(
        "mla", [q_mla, k_mla, vt_mla],
        [_seq_spec(2 * LANES, 0), _seq_spec(2 * LANES, 0), _vt_spec()], B_HEADS // 2)
    w_o = ab_w_out[0]
    xt = post(x1, [o_a, o_b], [w_o[:A_WIDTH], w_o[A_WIDTH:]], 0, DEPTH == 1)

    w_fox = fox_w_in[0]
    w_qk = w_fox[:, :2 * C_WIDTH].astype(BF16)
    w_vt = w_fox[:, 2 * C_WIDTH:3 * C_WIDTH].T.astype(BF16)
    w_f = jnp.pad(w_fox[:, 3 * C_WIDTH:], ((0, 0), (0, LANES - C_HEADS))).astype(BF16)
    fa, fs = ffn_args(1, 1)
    ins = [xt] + fa + [_row(mix_norm[1]), w_qk, w_vt, w_f, _row(fox_b_f[0], LANES)]
    specs = ([_tok_spec(D_MODEL)] + fs
             + [_wspec((1, D_MODEL)), _wspec(w_qk.shape), _wspec(w_vt.shape), _wspec(w_f.shape),
                _wspec((1, LANES))])
    outs = (jax.ShapeDtypeStruct((TOKENS, D_MODEL), F32),
            jax.ShapeDtypeStruct((TOKENS, 2 * C_WIDTH), BF16),
            jax.ShapeDtypeStruct((C_WIDTH, TOKENS), BF16),
            jax.ShapeDtypeStruct((TOKENS, LANES), BF16))
    out_specs = (_tok_spec(D_MODEL), _tok_spec(2 * C_WIDTH), _tok_t_spec(C_WIDTH), _tok_spec(LANES))
    x1, qk_c, vt_c, caug = _token_call("pre1", _pre1_kernel, ins, specs, outs, out_specs,
                                       scratch=[pltpu.VMEM((1, LANES), F32)])

    nc = C_HEADS // 2
    o_c = _attention(
        "fox", [qk_c, qk_c, vt_c, caug],
        [_seq_spec(LANES, 0), _seq_spec(LANES, nc // PAIR_GROUP), _vt_spec(),
         pl.BlockSpec((SEQ, LANES), lambda b, g: (b, 0))],
        nc, extra_scratch=[pltpu.VMEM((SEQ, PAIR_GROUP * 2 * LANES), BF16)])
    xt = post(x1, [o_c], [fox_w_out[0]], 1, True)
    return xt.reshape(BATCH, SEQ, D_MODEL)
```

```python
import functools
import math

import jax
import jax.numpy as jnp
import numpy as np
from jax import lax
from jax.experimental import pallas as pl
from jax.experimental.pallas import tpu as pltpu

F32 = jnp.float32
BF16 = jnp.bfloat16

D_MODEL = 1024
BATCH = 8
SEQ = 2048
DEPTH = 2
PLE_DIM = 256
D_FF = 2816
EPS = 1e-6

A_HEADS = 8
A_HEAD_DIM = 64
MOBA_BLOCK = 256
MOBA_TOPK = 3

B_HEADS = 8
MLA_Q_RANK = 256
MLA_KV_RANK = 128
MLA_NOPE = 64
MLA_ROPE = 32
MLA_V = 64
ROPE_THETA = 10000.0

T5_BUCKETS = 32
T5_MAX_DIST = 128

C_HEADS = 16
C_HEAD_DIM = 64

A_WIDTH = A_HEADS * A_HEAD_DIM
C_WIDTH = C_HEADS * C_HEAD_DIM

TOKENS = BATCH * SEQ
LANES = 128
BF16_ROWS = 16
PAIR_GROUP = 2
Q_TILE = MOBA_BLOCK
N_KBLK = SEQ // Q_TILE
TOKEN_TILE = 2 * Q_TILE
TILE_R = TOKEN_TILE // Q_TILE
FF_CHUNK = D_FF // 2
VMEM_LIMIT = 56 * 1024 * 1024
NEG_INF = float("-inf")
LOG2E = math.log2(math.e)
QSCALE_64 = A_HEAD_DIM ** -0.5 * LOG2E
QSCALE_MLA = (MLA_NOPE + MLA_ROPE) ** -0.5 * LOG2E


def _wspec(shape, layer=None):
    nd = len(shape)
    if layer is None:
        return pl.BlockSpec(shape, lambda *_: (0,) * nd, pipeline_mode=pl.Buffered(1))
    return pl.BlockSpec((None,) + tuple(shape), lambda *_: (layer,) + (0,) * nd,
                        pipeline_mode=pl.Buffered(1))


def _dot(a, b):
    return jnp.dot(a, b, preferred_element_type=F32)


def _dot_nt(a, b):
    return lax.dot_general(a, b, (((1,), (1,)), ((), ())), preferred_element_type=F32)


def _rms(x, g):
    return x * lax.rsqrt(jnp.mean(x * x, axis=-1, keepdims=True) + EPS) * g


def _ffn(x, g, win_ref, wo_ref):
    h = _rms(x, g).astype(BF16)
    acts = []
    for c in range(D_FF // FF_CHUNK):
        lo, hi = c * FF_CHUNK, (c + 1) * FF_CHUNK
        a = _dot(h, win_ref[:, lo:hi])
        u = _dot(h, win_ref[:, D_FF + lo:D_FF + hi])
        acts.append((a * jax.nn.sigmoid(a) * u).astype(BF16))
    return x + 0.5 * _dot(jnp.concatenate(acts, axis=1), wo_ref[...])


def _rope_tables(inv_lane, tile_idx):
    pos0 = (tile_idx * TOKEN_TILE) % SEQ
    pos = (pos0 + lax.broadcasted_iota(jnp.int32, (TOKEN_TILE, LANES), 0)).astype(F32)
    lane = lax.broadcasted_iota(jnp.int32, (TOKEN_TILE, LANES), 1)
    ang = pos * inv_lane
    is_x1 = (lane >= MLA_NOPE) & (lane < MLA_NOPE + MLA_ROPE // 2)
    is_x2 = (lane >= MLA_NOPE + MLA_ROPE // 2) & (lane < MLA_NOPE + MLA_ROPE)
    cos_t = jnp.where(is_x1 | is_x2, jnp.cos(ang), 1.0)
    sin = jnp.sin(ang)
    sin_t = jnp.where(is_x1, -sin, jnp.where(is_x2, sin, 0.0))
    return cos_t, sin_t, is_x1


def _rope_block(xb, cos_t, sin_t, is_x1):
    half = MLA_ROPE // 2
    partner = jnp.where(is_x1, pltpu.roll(xb, LANES - half, 1), pltpu.roll(xb, half, 1))
    return xb * cos_t + partner * sin_t


def _pre0_kernel(x_ref, g1_ref, win_ref, wo_ref, gmix_ref, wqk_ref, wvt_ref, wc_ref,
                 qn_ref, wuq_ref, kvn_ref, wukvk_ref, wukvvt_ref, inv_ref,
                 x1_ref, qk_ref, vt_ref, pen_ref, qm_ref, kmla_ref, vmt_ref, km_s):
    t = pl.program_id(0)
    x1 = _ffn(x_ref[...], g1_ref[...], win_ref, wo_ref)
    x1_ref[...] = x1
    h = _rms(x1, gmix_ref[...]).astype(BF16)
    qk = _dot(h, wqk_ref[...])
    q = qk[:, :A_WIDTH]
    k = qk[:, A_WIDTH:]
    qk_ref[:, :A_WIDTH] = (q * QSCALE_64).astype(BF16)
    qk_ref[:, A_WIDTH:] = k.astype(BF16)

    @pl.when(t == 0)
    def _():
        km_s[...] = jnp.zeros_like(km_s)

    c = _dot(h, wc_ref[...])
    gates = []
    for r in range(TILE_R):
        own = (t * TILE_R + r) % N_KBLK
        part = slice(r * Q_TILE, (r + 1) * Q_TILE)
        km_s[pl.ds(own, 1), :] = jnp.mean(k[part], axis=0, keepdims=True)
        gates.append((own, _moba_gate(q[part], km_s[...])))
    vt_ref[...] = _dot_nt(wvt_ref[...], h).astype(BF16)
    for r, (own, gate) in enumerate(gates):
        pen_ref[0, r] = _moba_select(gate, own)

    cq = c[:, :MLA_Q_RANK]
    ckv = c[:, MLA_Q_RANK:MLA_Q_RANK + MLA_KV_RANK]
    kr = c[:, MLA_Q_RANK + MLA_KV_RANK:]
    cqn = _rms(cq, qn_ref[...]).astype(BF16)
    ckvn = _rms(ckv, kvn_ref[...]).astype(BF16)
    qm = _dot(cqn, wuq_ref[...])
    kn = _dot(ckvn, wukvk_ref[...])
    vmt_ref[...] = _dot_nt(wukvvt_ref[...], ckvn).astype(BF16)

    cos_t, sin_t, is_x1 = _rope_tables(inv_ref[...], t)
    cos_q = cos_t * QSCALE_MLA
    sin_q = sin_t * QSCALE_MLA
    krr = _rope_block(kr, cos_t, sin_t, is_x1)
    for hb in range(B_HEADS):
        sl = slice(hb * LANES, (hb + 1) * LANES)
        qm_ref[:, sl] = _rope_block(qm[:, sl], cos_q, sin_q, is_x1).astype(BF16)
        kmla_ref[:, sl] = (kn[:, sl] + krr).astype(BF16)


def _pre1_kernel(x_ref, g1_ref, win_ref, wo_ref, gmix_ref, wqk_ref, wvt_ref, wf_ref, bf_ref,
                 x1_ref, qk_ref, vt_ref, caug_ref, carry_s):
    t = pl.program_id(0)
    x1 = _ffn(x_ref[...], g1_ref[...], win_ref, wo_ref)
    x1_ref[...] = x1
    h = _rms(x1, gmix_ref[...]).astype(BF16)

    @pl.when(t % (N_KBLK // TILE_R) == 0)
    def _():
        carry_s[...] = jnp.zeros_like(carry_s)

    z = _dot(h, wf_ref[...]) + bf_ref[...]
    qk = _dot(h, wqk_ref[...])
    qk_ref[:, :C_WIDTH] = (qk[:, :C_WIDTH] * QSCALE_64).astype(BF16)
    qk_ref[:, C_WIDTH:] = qk[:, C_WIDTH:].astype(BF16)
    cums = [_fox_cumsum(z[r * Q_TILE:(r + 1) * Q_TILE], carry_s) for r in range(TILE_R)]
    vt_ref[...] = _dot_nt(wvt_ref[...], h).astype(BF16)
    for r, cum in enumerate(cums):
        caug_ref[r * Q_TILE:(r + 1) * Q_TILE, :] = _fox_decay_parts(cum)


def _tok_spec(width):
    return pl.BlockSpec((TOKEN_TILE, width), lambda i: (i, 0))


def _tok_t_spec(height):
    return pl.BlockSpec((height, TOKEN_TILE), lambda i: (0, i))


def _token_call(name, body, ins, in_specs, outs, out_specs, scratch=()):
    return pl.pallas_call(
        body,
        grid=(TOKENS // TOKEN_TILE,),
        in_specs=in_specs,
        out_specs=out_specs,
        out_shape=outs,
        scratch_shapes=list(scratch),
        compiler_params=pltpu.CompilerParams(
            dimension_semantics=("arbitrary",), vmem_limit_bytes=VMEM_LIMIT),
        name=name,
    )(*ins)


def _post_kernel(*refs, n_mix, final):
    x_ref = refs[0]
    o_refs = refs[1:1 + n_mix]
    w_refs = refs[1 + n_mix:1 + 2 * n_mix]
    (g2_ref, win_ref, wo_ref, gple_ref, wg_ref, p_ref, wp_ref, gfin_ref,
     out_ref) = refs[1 + 2 * n_mix:]
    x = x_ref[...]
    for o_ref, w_ref in zip(o_refs, w_refs):
        x = x + _dot(o_ref[...], w_ref[...])
    x = _ffn(x, g2_ref[...], win_ref, wo_ref)
    gate = jax.nn.sigmoid(_dot(_rms(x, gple_ref[...]).astype(BF16), wg_ref[...]))
    x = x + gate * _dot(p_ref[...].astype(BF16), wp_ref[...])
    if final:
        x = _rms(x, gfin_ref[...])
    out_ref[...] = x


def _split_bf16(x):
    hi = x.astype(BF16)
    lo = (x - hi.astype(F32)).astype(BF16)
    return hi, lo


def _moba_gate(q, km):
    rows = A_HEADS * N_KBLK
    gt = jnp.concatenate([km] * A_HEADS, axis=0)
    r = lax.broadcasted_iota(jnp.int32, (rows, A_WIDTH), 0)
    c = lax.broadcasted_iota(jnp.int32, (rows, A_WIDTH), 1)
    gt = jnp.where((r // N_KBLK) == (c // A_HEAD_DIM), gt, 0.0)
    g_hi, g_lo = _split_bf16(gt)
    q_hi, q_lo = _split_bf16(q)
    return _dot_nt(g_hi, q_hi) + _dot_nt(g_hi, q_lo) + _dot_nt(g_lo, q_hi)


def _moba_select(gate, own):
    n_idx = lax.broadcasted_iota(jnp.int32, (N_KBLK, Q_TILE), 0)
    pen_rows = []
    for h in range(A_HEADS):
        gh = gate[h * N_KBLK:(h + 1) * N_KBLK]
        rank = jnp.zeros((N_KBLK, Q_TILE), jnp.int32)
        for m in range(N_KBLK):
            gm = gh[m:m + 1]
            beats = (gm > gh) | ((gm == gh) & (m < n_idx))
            rank = rank + jnp.where(beats & (m < own), 1, 0)
        sel = (n_idx < own) & (rank < MOBA_TOPK)
        pen_n = jnp.where(sel, 0.0, NEG_INF)
        pen_d = jnp.full((N_KBLK, Q_TILE), NEG_INF, F32)
        for n in range(N_KBLK):
            pen_d = jnp.where(n_idx == own - n, pen_n[n:n + 1], pen_d)
        pen_rows.append(pen_d)
    pairs = [jnp.concatenate(pen_rows[2 * hp:2 * hp + 2], axis=1) for hp in range(A_HEADS // 2)]
    return jnp.concatenate(pairs, axis=0)


def _t5_bias_kernel(tbl_ref, o_ref):
    hp = pl.program_id(0)
    r = lax.broadcasted_iota(jnp.int32, (Q_TILE, Q_TILE), 0)
    c = lax.broadcasted_iota(jnp.int32, (Q_TILE, Q_TILE), 1)
    max_exact = T5_BUCKETS // 2
    for dd in range(3):
        dist = dd * Q_TILE + c - r
        dc = jnp.maximum(dist, 0)
        df = jnp.maximum(dc.astype(F32), 1.0)
        large = max_exact + (jnp.log(df / max_exact) / math.log(T5_MAX_DIST / max_exact)
                             * (T5_BUCKETS - max_exact)).astype(jnp.int32)
        large = jnp.minimum(large, T5_BUCKETS - 1)
        bucket = jnp.where(dc < max_exact, dc, large)
        for hh in range(2):
            bias = jnp.zeros((Q_TILE, Q_TILE), F32)
            for b in range(T5_BUCKETS):
                bias = jnp.where(bucket == b, tbl_ref[b, 2 * hp + hh], bias)
            bias = bias * LOG2E
            if dd == 0:
                bias = jnp.where(dist >= 0, bias, NEG_INF)
            o_ref[0, dd, :, hh * Q_TILE:(hh + 1) * Q_TILE] = bias


def _fox_cumsum(z, carry_ref):
    lane = lax.broadcasted_iota(jnp.int32, (Q_TILE, LANES), 1)
    logf = jnp.minimum(z, 0.0) - jnp.log1p(jnp.exp(-jnp.abs(z)))
    logf = jnp.where(lane < C_HEADS, logf, 0.0)
    r = lax.broadcasted_iota(jnp.int32, (Q_TILE, Q_TILE), 0)
    c = lax.broadcasted_iota(jnp.int32, (Q_TILE, Q_TILE), 1)
    tri = jnp.where(c <= r, 1.0, 0.0).astype(BF16)
    l1 = logf.astype(BF16)
    rem = logf - l1.astype(F32)
    l2 = rem.astype(BF16)
    l3 = (rem - l2.astype(F32)).astype(BF16)
    cum = _dot(tri, l1) + _dot(tri, l2) + _dot(tri, l3) + carry_ref[...]
    carry_ref[...] = cum[Q_TILE - 1:Q_TILE, :]
    return cum


def _fox_decay_parts(cum):
    cs = cum * LOG2E
    c1 = cs.astype(BF16)
    rem = cs - c1.astype(F32)
    c2 = rem.astype(BF16)
    c3 = (rem - c2.astype(F32)).astype(BF16)
    rin = lax.broadcasted_iota(jnp.int32, (LANES, LANES), 0)
    lout = lax.broadcasted_iota(jnp.int32, (LANES, LANES), 1)
    out = jnp.zeros((Q_TILE, LANES), F32)
    for part, cp in enumerate((c1, c2, c3)):
        place = jnp.where((rin < C_HEADS) & (lout == rin * 8 + part), 1.0, 0.0).astype(BF16)
        out = out + _dot(cp, place)
    return out.astype(BF16)


def _attn_kernel(*refs, kind):
    if kind == "moba":
        q_ref, k_ref, vt_ref, pen_ref, bias_ref, o_ref = refs
    elif kind == "mla":
        q_ref, k_ref, vt_ref, o_ref = refs
    else:
        q_ref, k_ref, vt_ref, c_ref, o_ref, kp_s = refs
    grp = pl.program_id(1)
    lane = lax.broadcasted_iota(jnp.int32, (1, LANES), 1)
    low = lane < A_HEAD_DIM
    kw = LANES if kind == "moba" else 2 * LANES
    qw = 2 * LANES if kind == "mla" else LANES
    k_cat = kp_s if kind == "fox" else k_ref

    if kind == "fox":
        rin = lax.broadcasted_iota(jnp.int32, (LANES, LANES), 0)
        lout = lax.broadcasted_iota(jnp.int32, (LANES, LANES), 1)

        def place(pair, off_a, off_b, val):
            base_a = 2 * pair * 8
            base_b = base_a + 8
            sel_a = (lout >= off_a) & (lout < off_a + 3) & (rin == base_a + lout - off_a)
            sel_b = (lout >= off_b) & (lout < off_b + 3) & (rin == base_b + lout - off_b)
            return jnp.where(sel_a | sel_b, val, 0.0).astype(BF16)

        def ones(off_a, off_b):
            in_a = (lane >= off_a) & (lane < off_a + 3)
            in_b = (lane >= off_b) & (lane < off_b + 3)
            return jnp.where(in_a | in_b, 1.0, 0.0)

        aq_all = []
        for pi in range(PAIR_GROUP):
            ak = (_dot(c_ref[...], place(grp * PAIR_GROUP + pi, A_HEAD_DIM + 3, 3, -1.0))
                  + ones(A_HEAD_DIM, 0)).astype(BF16)
            k = k_ref[:, pi * LANES:(pi + 1) * LANES]
            kp_s[:, pi * kw:pi * kw + LANES] = jnp.where(low, k, ak)
            kp_s[:, pi * kw + LANES:(pi + 1) * kw] = jnp.where(low, ak, k)
            aq_all.append((_dot(c_ref[...], place(grp * PAIR_GROUP + pi, A_HEAD_DIM, 0, 1.0))
                           + ones(A_HEAD_DIM + 3, 3)).astype(BF16))

    def q_operand(tile, pi):
        zero = jnp.zeros((Q_TILE, LANES), BF16)
        rows = slice(tile * Q_TILE, (tile + 1) * Q_TILE)
        q = q_ref[rows, pi * qw:(pi + 1) * qw]
        if kind == "moba":
            return jnp.concatenate([jnp.where(low, q, zero), jnp.where(low, zero, q)], axis=0)
        if kind == "fox":
            aq = aq_all[pi][rows]
            qa, qb = jnp.where(low, q, aq), jnp.where(low, aq, q)
        else:
            qa, qb = q[:, :LANES], q[:, LANES:]
        return jnp.concatenate([jnp.concatenate([qa, zero], axis=1),
                                jnp.concatenate([zero, qb], axis=1)], axis=0)

    def store_out(pi, tile, acc):
        ot2 = acc[:LANES] / acc[LANES:LANES + 1]
        ot = jnp.concatenate([ot2[:A_HEAD_DIM, :Q_TILE], ot2[A_HEAD_DIM:, Q_TILE:]], axis=0)
        o_ref[tile * Q_TILE:(tile + 1) * Q_TILE, pi * LANES:(pi + 1) * LANES] = ot.T.astype(BF16)

    def v_aug(pi, rows):
        width = rows.stop - rows.start
        return jnp.concatenate([vt_ref[pi * LANES:(pi + 1) * LANES, rows],
                                jnp.ones((BF16_ROWS, width), BF16)], axis=0)

    if kind == "moba":
        _moba_schedule(q_operand, k_ref, pen_ref, bias_ref, v_aug, store_out)
    else:
        _online_schedule(q_operand, k_cat, kw, v_aug, store_out)


def _online_schedule(q_operand, k_cat, kw, v_aug, store_out):
    key = lax.broadcasted_iota(jnp.int32, (Q_TILE, 2 * Q_TILE), 0)
    qry = lax.broadcasted_iota(jnp.int32, (Q_TILE, 2 * Q_TILE), 1) % Q_TILE

    def run(chains):
        n = len(chains)

        def rows_of(c, d):
            nb = chains[c][1]
            return slice((nb - 1 - d) * Q_TILE, (nb - d) * Q_TILE)

        def qk(c, d):
            pi = chains[c][0]
            return _dot_nt(k_cat[rows_of(c, d), pi * kw:(pi + 1) * kw], chains[c][2])

        s_next = [qk(c, 0) for c in range(n)]
        m = [None] * n
        acc = [None] * n
        for d in range(max(c[1] for c in chains)):
            live = [c for c in range(n) if d < chains[c][1]]
            s_cur = list(s_next)
            for c in live:
                if d + 1 < chains[c][1]:
                    s_next[c] = qk(c, d + 1)
            for c in live:
                sn = s_cur[c]
                if d == 0:
                    sn = jnp.where(key <= qry, sn, NEG_INF)
                bm = jnp.max(sn, axis=0, keepdims=True)
                m_new = bm if d == 0 else jnp.maximum(m[c], bm)
                pv = _dot(v_aug(chains[c][0], rows_of(c, d)), jnp.exp2(sn - m_new).astype(BF16))
                acc[c] = pv if d == 0 else acc[c] * jnp.exp2(m[c] - m_new) + pv
                m[c] = m_new
        return acc

    for jj in range(N_KBLK // 2):
        tiles = (N_KBLK - 1 - jj, jj)
        chains = [(pi, tile + 1, q_operand(tile, pi)) for pi in range(PAIR_GROUP) for tile in tiles]
        for (pi, nb, _), acc in zip(chains, run(chains)):
            store_out(pi, nb - 1, acc)


def _moba_schedule(q_operand, k_ref, pen_ref, bias_ref, v_aug, store_out):
    def new_chain(pi, nb):
        return {"pair": pi, "nb": nb, "q": q_operand(nb - 1, pi), "blocks": [], "shifts": [], "m": None}

    def score_block(ch, n):
        pi, nb = ch["pair"], ch["nb"]
        d = nb - 1 - n
        sn = _dot_nt(k_ref[n * Q_TILE:(n + 1) * Q_TILE, pi * LANES:(pi + 1) * LANES], ch["q"])
        shift = None
        if d < 2:
            sn = sn + bias_ref[pi, d]
        else:
            shift = bias_ref[pi, 2, 0:1, :]
        if d > 0:
            pen = pen_ref[0, nb - 1, pi * N_KBLK + d:pi * N_KBLK + d + 1, :]
            shift = pen if shift is None else shift + pen
        bm = jnp.max(sn, axis=0, keepdims=True)
        if shift is not None:
            bm = bm + shift
        ch["m"] = bm if ch["m"] is None else jnp.maximum(ch["m"], bm)
        ch["blocks"].append(sn)
        ch["shifts"].append(shift)

    def prob_block(ch, n):
        m, sh = ch["m"], ch["shifts"][n]
        return jnp.exp2(ch["blocks"][n] - (m if sh is None else m - sh)).astype(BF16)

    order = [(pi, nb) for nb in range(N_KBLK, 0, -1) for pi in range(PAIR_GROUP)]
    prev = None
    for item in order + [None]:
        cur = None if item is None else new_chain(*item)
        probs = []
        for n in range(max(cur["nb"] if cur else 0, prev["nb"] if prev else 0)):
            if cur is not None and n < cur["nb"]:
                score_block(cur, n)
            if prev is not None and n < prev["nb"]:
                probs.append(prob_block(prev, n))
        if prev is not None:
            nk = prev["nb"] * Q_TILE
            acc = _dot(v_aug(prev["pair"], slice(0, nk)), jnp.concatenate(probs, axis=0))
            store_out(prev["pair"], prev["nb"] - 1, acc)
        prev = cur


def _attention(kind, ins, in_specs, n_pairs, extra_scratch=()):
    width = PAIR_GROUP * LANES
    return pl.pallas_call(
        functools.partial(_attn_kernel, kind=kind),
        grid=(BATCH, n_pairs // PAIR_GROUP),
        in_specs=in_specs,
        out_specs=pl.BlockSpec((SEQ, width), lambda b, g: (b, g)),
        out_shape=jax.ShapeDtypeStruct((TOKENS, n_pairs * LANES), BF16),
        scratch_shapes=list(extra_scratch),
        compiler_params=pltpu.CompilerParams(
            dimension_semantics=("arbitrary", "arbitrary"), vmem_limit_bytes=VMEM_LIMIT),
        name="attn_" + kind,
    )(*ins)


def _seq_spec(pair_width, col0):
    return pl.BlockSpec((SEQ, PAIR_GROUP * pair_width), lambda b, g: (b, col0 + g))


def _vt_spec():
    return pl.BlockSpec((PAIR_GROUP * LANES, SEQ), lambda b, g: (g, b))


def _place_heads(w, n_heads, src_stride, src_off, width):
    per_head = w.reshape(w.shape[0], n_heads, src_stride)[:, :, src_off:src_off + width]
    per_head = jnp.pad(per_head, ((0, 0), (0, 0), (0, LANES - width)))
    return per_head.reshape(w.shape[0], n_heads * LANES)


def _row(v, width=None):
    v = v.reshape(1, -1).astype(F32)
    if width is not None and v.shape[1] < width:
        v = jnp.pad(v, ((0, 0), (0, width - v.shape[1])))
    return v


def kernel(x, p, t5_bias, ff1_norm, ff1_w_in, ff1_w_out, mix_norm, ff2_norm, ff2_w_in, ff2_w_out,
           ple_norm, ple_w_gate, ple_w_proj, ab_w_in, mla_q_norm, mla_w_uq, mla_kv_norm, mla_w_ukv,
           ab_w_out, fox_w_in, fox_b_f, fox_w_out, final_norm):
    xt = x.reshape(TOKENS, D_MODEL)

    ffw = {1: (ff1_norm, ff1_w_in.astype(BF16), ff1_w_out.astype(BF16)),
           2: (ff2_norm, ff2_w_in.astype(BF16), ff2_w_out.astype(BF16))}

    def ffn_args(which, layer):
        norm, w_in, w_out = ffw[which]
        args = [_row(norm[layer]), w_in, w_out]
        specs = [_wspec((1, D_MODEL)), _wspec((D_MODEL, 2 * D_FF), layer), _wspec((D_FF, D_MODEL), layer)]
        return args, specs

    ple_gate_bf = ple_w_gate.astype(BF16)
    ple_proj_bf = ple_w_proj.astype(BF16)

    def post(xin, mixes, w_outs, layer, final):
        fa, fs = ffn_args(2, layer)
        ws = [w.astype(BF16) for w in w_outs]
        ins = ([xin] + list(mixes) + ws + fa
               + [_row(ple_norm[layer]), ple_gate_bf, p.reshape(DEPTH, TOKENS, PLE_DIM), ple_proj_bf,
                  _row(final_norm)])
        specs = ([_tok_spec(D_MODEL)] + [_tok_spec(m.shape[1]) for m in mixes]
                 + [_wspec(w.shape) for w in ws] + fs
                 + [_wspec((1, D_MODEL)), _wspec((D_MODEL, D_MODEL), layer),
                    pl.BlockSpec((None, TOKEN_TILE, PLE_DIM), lambda t: (layer, t, 0)),
                    _wspec((PLE_DIM, D_MODEL), layer), _wspec((1, D_MODEL))])
        return _token_call(
            "post%d" % layer,
            functools.partial(_post_kernel, n_mix=len(mixes), final=final), ins, specs,
            jax.ShapeDtypeStruct((TOKENS, D_MODEL), F32), _tok_spec(D_MODEL))

    w_ab = ab_w_in[0]
    w_qk = w_ab[:, :2 * A_WIDTH].astype(BF16)
    w_vt = w_ab[:, 2 * A_WIDTH:3 * A_WIDTH].T.astype(BF16)
    c0 = 3 * A_WIDTH
    kr0 = MLA_Q_RANK + MLA_KV_RANK
    w_c = jnp.concatenate(
        [w_ab[:, c0:c0 + kr0], jnp.zeros((D_MODEL, MLA_NOPE), F32), w_ab[:, c0 + kr0:],
         jnp.zeros((D_MODEL, LANES - MLA_NOPE - MLA_ROPE), F32)], axis=1).astype(BF16)
    w_uq = _place_heads(mla_w_uq[0], B_HEADS, MLA_NOPE + MLA_ROPE, 0, MLA_NOPE + MLA_ROPE).astype(BF16)
    w_ukv_k = _place_heads(mla_w_ukv[0], B_HEADS, MLA_NOPE + MLA_V, 0, MLA_NOPE).astype(BF16)
    w_ukv = mla_w_ukv[0].reshape(MLA_KV_RANK, B_HEADS, MLA_NOPE + MLA_V)
    w_ukv_vt = w_ukv[:, :, MLA_NOPE:].reshape(MLA_KV_RANK, B_HEADS * MLA_V).T.astype(BF16)
    half = MLA_ROPE // 2
    inv = ROPE_THETA ** (-np.arange(half, dtype=np.float64) / half)
    inv_lane = np.zeros((1, LANES), np.float32)
    inv_lane[0, MLA_NOPE:MLA_NOPE + half] = inv
    inv_lane[0, MLA_NOPE + half:MLA_NOPE + MLA_ROPE] = inv
    inv_lane = jnp.asarray(inv_lane)

    fa, fs = ffn_args(1, 0)
    ins = ([xt] + fa + [_row(mix_norm[0]), w_qk, w_vt, w_c, _row(mla_q_norm[0]), w_uq,
                        _row(mla_kv_norm[0]), w_ukv_k, w_ukv_vt, inv_lane])
    specs = ([_tok_spec(D_MODEL)] + fs
             + [_wspec((1, D_MODEL)), _wspec(w_qk.shape), _wspec(w_vt.shape), _wspec(w_c.shape),
                _wspec((1, MLA_Q_RANK)), _wspec(w_uq.shape), _wspec((1, MLA_KV_RANK)),
                _wspec(w_ukv_k.shape), _wspec(w_ukv_vt.shape), _wspec((1, LANES))])
    pen_rows = A_HEADS // 2 * N_KBLK
    outs = (jax.ShapeDtypeStruct((TOKENS, D_MODEL), F32),
            jax.ShapeDtypeStruct((TOKENS, 2 * A_WIDTH), BF16),
            jax.ShapeDtypeStruct((A_WIDTH, TOKENS), BF16),
            jax.ShapeDtypeStruct((BATCH, N_KBLK, pen_rows, 2 * Q_TILE), F32),
            jax.ShapeDtypeStruct((TOKENS, B_HEADS * LANES), BF16),
            jax.ShapeDtypeStruct((TOKENS, B_HEADS * LANES), BF16),
            jax.ShapeDtypeStruct((B_HEADS * MLA_V, TOKENS), BF16))
    out_specs = (_tok_spec(D_MODEL), _tok_spec(2 * A_WIDTH), _tok_t_spec(A_WIDTH),
                 pl.BlockSpec((1, TILE_R, pen_rows, 2 * Q_TILE),
                              lambda t: (t // (N_KBLK // TILE_R), t % (N_KBLK // TILE_R), 0, 0)),
                 _tok_spec(B_HEADS * LANES), _tok_spec(B_HEADS * LANES),
                 _tok_t_spec(B_HEADS * MLA_V))
    x1, qk_a, vt_a, pen, q_mla, k_mla, vt_mla = _token_call(
        "pre0", _pre0_kernel, ins, specs, outs, out_specs,
        scratch=[pltpu.VMEM((N_KBLK, A_WIDTH), F32)])

    bias = pl.pallas_call(
        _t5_bias_kernel,
        grid=(A_HEADS // 2,),
        in_specs=[pl.BlockSpec(memory_space=pltpu.SMEM)],
        out_specs=pl.BlockSpec((1, 3, Q_TILE, 2 * Q_TILE), lambda h: (h, 0, 0, 0)),
        out_shape=jax.ShapeDtypeStruct((A_HEADS // 2, 3, Q_TILE, 2 * Q_TILE), F32),
        name="t5_bias_tiles",
    )(t5_bias.astype(F32))

    na = A_HEADS // 2
    o_a = _attention(
        "moba", [qk_a, qk_a, vt_a, pen, bias],
        [_seq_spec(LANES, 0), _seq_spec(LANES, na // PAIR_GROUP), _vt_spec(),
         pl.BlockSpec((1, N_KBLK, PAIR_GROUP * N_KBLK, 2 * Q_TILE), lambda b, g: (b, 0, g, 0)),
         pl.BlockSpec((PAIR_GROUP, 3, Q_TILE, 2 * Q_TILE), lambda b, g: (g, 0, 0, 0))], na)
    o_b = _attention(
        "mla", [q_mla, k_mla, vt_mla],
        [_seq_spec(2 * LANES, 0), _seq_spec(2 * LANES, 0), _vt_spec()], B_HEADS // 2)
    w_o = ab_w_out[0]
    xt = post(x1, [o_a, o_b], [w_o[:A_WIDTH], w_o[A_WIDTH:]], 0, DEPTH == 1)

    w_fox = fox_w_in[0]
    w_qk = w_fox[:, :2 * C_WIDTH].astype(BF16)
    w_vt = w_fox[:, 2 * C_WIDTH:3 * C_WIDTH].T.astype(BF16)
    w_f = jnp.pad(w_fox[:, 3 * C_WIDTH:], ((0, 0), (0, LANES - C_HEADS))).astype(BF16)
    fa, fs = ffn_args(1, 1)
    ins = [xt] + fa + [_row(mix_norm[1]), w_qk, w_vt, w_f, _row(fox_b_f[0], LANES)]
    specs = ([_tok_spec(D_MODEL)] + fs
             + [_wspec((1, D_MODEL)), _wspec(w_qk.shape), _wspec(w_vt.shape), _wspec(w_f.shape),
                _wspec((1, LANES))])
    outs = (jax.ShapeDtypeStruct((TOKENS, D_MODEL), F32),
            jax.ShapeDtypeStruct((TOKENS, 2 * C_WIDTH), BF16),
            jax.ShapeDtypeStruct((C_WIDTH, TOKENS), BF16),
            jax.ShapeDtypeStruct((TOKENS, LANES), BF16))
    out_specs = (_tok_spec(D_MODEL), _tok_spec(2 * C_WIDTH), _tok_t_spec(C_WIDTH), _tok_spec(LANES))
    x1, qk_c, vt_c, caug = _token_call("pre1", _pre1_kernel, ins, specs, outs, out_specs,
                                       scratch=[pltpu.VMEM((1, LANES), F32)])

    nc = C_HEADS // 2
    o_c = _attention(
        "fox", [qk_c, qk_c, vt_c, caug],
        [_seq_spec(LANES, 0), _seq_spec(LANES, nc // PAIR_GROUP), _vt_spec(),
         pl.BlockSpec((SEQ, LANES), lambda b, g: (b, 0))],
        nc, extra_scratch=[pltpu.VMEM((SEQ, PAIR_GROUP * 2 * LANES), BF16)])
    xt = post(x1, [o_c], [fox_w_out[0]], 1, True)
    return xt.reshape(BATCH, SEQ, D_MODEL)
```

```python
import functools
import math

import jax
import jax.numpy as jnp
import numpy as np
from jax import lax
from jax.experimental import pallas as pl
from jax.experimental.pallas import tpu as pltpu

F32 = jnp.float32
BF16 = jnp.bfloat16

D_MODEL = 1024
BATCH = 8
SEQ = 2048
DEPTH = 2
PLE_DIM = 256
D_FF = 2816
EPS = 1e-6

A_HEADS = 8
A_HEAD_DIM = 64
MOBA_BLOCK = 256
MOBA_TOPK = 3

B_HEADS = 8
MLA_Q_RANK = 256
MLA_KV_RANK = 128
MLA_NOPE = 64
MLA_ROPE = 32
MLA_V = 64
ROPE_THETA = 10000.0

T5_BUCKETS = 32
T5_MAX_DIST = 128

C_HEADS = 16
C_HEAD_DIM = 64

A_WIDTH = A_HEADS * A_HEAD_DIM
C_WIDTH = C_HEADS * C_HEAD_DIM

TOKENS = BATCH * SEQ
LANES = 128
BF16_ROWS = 16
PAIR_GROUP = 2
Q_TILE = MOBA_BLOCK
N_KBLK = SEQ // Q_TILE
TOKEN_TILE = 2 * Q_TILE
TILE_R = TOKEN_TILE // Q_TILE
FF_CHUNK = D_FF // 2
W_IN_CHUNK = 512
W_OUT_CHUNK = 256
VMEM_LIMIT = 56 * 1024 * 1024
NEG_INF = float("-inf")
LOG2E = math.log2(math.e)
QSCALE_64 = A_HEAD_DIM ** -0.5 * LOG2E
QSCALE_MLA = (MLA_NOPE + MLA_ROPE) ** -0.5 * LOG2E


def _wspec(shape, layer=None):
    nd = len(shape)
    if layer is None:
        return pl.BlockSpec(shape, lambda *_: (0,) * nd, pipeline_mode=pl.Buffered(1))
    return pl.BlockSpec((None,) + tuple(shape), lambda *_: (layer,) + (0,) * nd,
                        pipeline_mode=pl.Buffered(1))


def _dot(a, b):
    return jnp.dot(a, b, preferred_element_type=F32)


def _dot_nt(a, b):
    return lax.dot_general(a, b, (((1,), (1,)), ((), ())), preferred_element_type=F32)


def _rms(x, g):
    return x * lax.rsqrt(jnp.mean(x * x, axis=-1, keepdims=True) + EPS) * g


def _ffn_scratch():
    return [pltpu.VMEM((D_MODEL, 2 * D_FF), BF16), pltpu.VMEM((D_FF, D_MODEL), BF16),
            pltpu.VMEM((2, D_MODEL, W_IN_CHUNK), F32), pltpu.VMEM((2, W_OUT_CHUNK, D_MODEL), F32),
            pltpu.SemaphoreType.DMA((2, 2))]


def _load_ffn_weights(layer, win_hbm, wo_hbm, win_ref, wo_ref, stage_in, stage_out, sem):
    def copy_in(c, slot):
        return pltpu.make_async_copy(win_hbm.at[layer, :, pl.ds(c * W_IN_CHUNK, W_IN_CHUNK)],
                                     stage_in.at[slot], sem.at[0, slot])

    def copy_out(c, slot):
        return pltpu.make_async_copy(wo_hbm.at[layer, pl.ds(c * W_OUT_CHUNK, W_OUT_CHUNK), :],
                                     stage_out.at[slot], sem.at[1, slot])

    @pl.when(pl.program_id(0) == 0)
    def _():
        n_in = 2 * D_FF // W_IN_CHUNK
        n_out = D_FF // W_OUT_CHUNK
        copy_in(0, 0).start()
        copy_out(0, 0).start()
        for c in range(n_in):
            slot = c % 2
            if c + 1 < n_in:
                copy_in(c + 1, 1 - slot).start()
            copy_in(c, slot).wait()
            win_ref[:, c * W_IN_CHUNK:(c + 1) * W_IN_CHUNK] = stage_in[slot].astype(BF16)
        for c in range(n_out):
            slot = c % 2
            if c + 1 < n_out:
                copy_out(c + 1, 1 - slot).start()
            copy_out(c, slot).wait()
            wo_ref[c * W_OUT_CHUNK:(c + 1) * W_OUT_CHUNK, :] = stage_out[slot].astype(BF16)


def _ffn(x, g, win_ref, wo_ref):
    h = _rms(x, g).astype(BF16)
    acts = []
    for c in range(D_FF // FF_CHUNK):
        lo, hi = c * FF_CHUNK, (c + 1) * FF_CHUNK
        a = _dot(h, win_ref[:, lo:hi])
        u = _dot(h, win_ref[:, D_FF + lo:D_FF + hi])
        acts.append((a * jax.nn.sigmoid(a) * u).astype(BF16))
    return x + 0.5 * _dot(jnp.concatenate(acts, axis=1), wo_ref[...])


def _rope_tables(inv_lane, tile_idx):
    pos0 = (tile_idx * TOKEN_TILE) % SEQ
    pos = (pos0 + lax.broadcasted_iota(jnp.int32, (TOKEN_TILE, LANES), 0)).astype(F32)
    lane = lax.broadcasted_iota(jnp.int32, (TOKEN_TILE, LANES), 1)
    ang = pos * inv_lane
    is_x1 = (lane >= MLA_NOPE) & (lane < MLA_NOPE + MLA_ROPE // 2)
    is_x2 = (lane >= MLA_NOPE + MLA_ROPE // 2) & (lane < MLA_NOPE + MLA_ROPE)
    cos_t = jnp.where(is_x1 | is_x2, jnp.cos(ang), 1.0)
    sin = jnp.sin(ang)
    sin_t = jnp.where(is_x1, -sin, jnp.where(is_x2, sin, 0.0))
    return cos_t, sin_t, is_x1


def _rope_block(xb, cos_t, sin_t, is_x1):
    half = MLA_ROPE // 2
    partner = jnp.where(is_x1, pltpu.roll(xb, LANES - half, 1), pltpu.roll(xb, half, 1))
    return xb * cos_t + partner * sin_t


def _pre0_kernel(x_ref, g1_ref, win_hbm, wo_hbm, gmix_ref, wqk_ref, wvt_ref, wc_ref,
                 qn_ref, wuq_ref, kvn_ref, wukvk_ref, wukvvt_ref, inv_ref,
                 x1_ref, qk_ref, vt_ref, pen_ref, qm_ref, kmla_ref, vmt_ref, km_s, *ffn_s, layer):
    t = pl.program_id(0)
    _load_ffn_weights(layer, win_hbm, wo_hbm, *ffn_s)
    x1 = _ffn(x_ref[...], g1_ref[...], ffn_s[0], ffn_s[1])
    x1_ref[...] = x1
    h = _rms(x1, gmix_ref[...]).astype(BF16)
    qk = _dot(h, wqk_ref[...])
    q = qk[:, :A_WIDTH]
    k = qk[:, A_WIDTH:]
    qk_ref[:, :A_WIDTH] = (q * QSCALE_64).astype(BF16)
    qk_ref[:, A_WIDTH:] = k.astype(BF16)

    @pl.when(t == 0)
    def _():
        km_s[...] = jnp.zeros_like(km_s)

    c = _dot(h, wc_ref[...])
    gates = []
    for r in range(TILE_R):
        own = (t * TILE_R + r) % N_KBLK
        part = slice(r * Q_TILE, (r + 1) * Q_TILE)
        km_s[pl.ds(own, 1), :] = jnp.mean(k[part], axis=0, keepdims=True)
        gates.append((own, _moba_gate(q[part], km_s[...])))
    vt_ref[...] = _dot_nt(wvt_ref[...], h).astype(BF16)
    for r, (own, gate) in enumerate(gates):
        pen_ref[0, r] = _moba_select(gate, own)

    cq = c[:, :MLA_Q_RANK]
    ckv = c[:, MLA_Q_RANK:MLA_Q_RANK + MLA_KV_RANK]
    kr = c[:, MLA_Q_RANK + MLA_KV_RANK:]
    cqn = _rms(cq, qn_ref[...]).astype(BF16)
    ckvn = _rms(ckv, kvn_ref[...]).astype(BF16)
    qm = _dot(cqn, wuq_ref[...])
    kn = _dot(ckvn, wukvk_ref[...])
    vmt_ref[...] = _dot_nt(wukvvt_ref[...], ckvn).astype(BF16)

    cos_t, sin_t, is_x1 = _rope_tables(inv_ref[...], t)
    cos_q = cos_t * QSCALE_MLA
    sin_q = sin_t * QSCALE_MLA
    krr = _rope_block(kr, cos_t, sin_t, is_x1)
    for hb in range(B_HEADS):
        sl = slice(hb * LANES, (hb + 1) * LANES)
        qm_ref[:, sl] = _rope_block(qm[:, sl], cos_q, sin_q, is_x1).astype(BF16)
        kmla_ref[:, sl] = (kn[:, sl] + krr).astype(BF16)


def _pre1_kernel(x_ref, g1_ref, win_hbm, wo_hbm, gmix_ref, wqk_ref, wvt_ref, wf_ref, bf_ref,
                 x1_ref, qk_ref, vt_ref, caug_ref, carry_s, *ffn_s, layer):
    t = pl.program_id(0)
    _load_ffn_weights(layer, win_hbm, wo_hbm, *ffn_s)
    x1 = _ffn(x_ref[...], g1_ref[...], ffn_s[0], ffn_s[1])
    x1_ref[...] = x1
    h = _rms(x1, gmix_ref[...]).astype(BF16)

    @pl.when(t % (N_KBLK // TILE_R) == 0)
    def _():
        carry_s[...] = jnp.zeros_like(carry_s)

    z = _dot(h, wf_ref[...]) + bf_ref[...]
    qk = _dot(h, wqk_ref[...])
    qk_ref[:, :C_WIDTH] = (qk[:, :C_WIDTH] * QSCALE_64).astype(BF16)
    qk_ref[:, C_WIDTH:] = qk[:, C_WIDTH:].astype(BF16)
    cums = [_fox_cumsum(z[r * Q_TILE:(r + 1) * Q_TILE], carry_s) for r in range(TILE_R)]
    vt_ref[...] = _dot_nt(wvt_ref[...], h).astype(BF16)
    for r, cum in enumerate(cums):
        caug_ref[r * Q_TILE:(r + 1) * Q_TILE, :] = _fox_decay_parts(cum)


def _tok_spec(width):
    return pl.BlockSpec((TOKEN_TILE, width), lambda i: (i, 0))


def _tok_t_spec(height):
    return pl.BlockSpec((height, TOKEN_TILE), lambda i: (0, i))


def _token_call(name, body, ins, in_specs, outs, out_specs, scratch=()):
    return pl.pallas_call(
        body,
        grid=(TOKENS // TOKEN_TILE,),
        in_specs=in_specs,
        out_specs=out_specs,
        out_shape=outs,
        scratch_shapes=list(scratch) + _ffn_scratch(),
        compiler_params=pltpu.CompilerParams(
            dimension_semantics=("arbitrary",), vmem_limit_bytes=VMEM_LIMIT),
        name=name,
    )(*ins)


def _post_kernel(*refs, n_mix, final, layer):
    x_ref = refs[0]
    o_refs = refs[1:1 + n_mix]
    w_refs = refs[1 + n_mix:1 + 2 * n_mix]
    (g2_ref, win_hbm, wo_hbm, gple_ref, wg_ref, p_ref, wp_ref, gfin_ref,
     out_ref) = refs[1 + 2 * n_mix:10 + 2 * n_mix]
    ffn_s = refs[10 + 2 * n_mix:]
    _load_ffn_weights(layer, win_hbm, wo_hbm, *ffn_s)
    x = x_ref[...]
    for o_ref, w_ref in zip(o_refs, w_refs):
        x = x + _dot(o_ref[...], w_ref[...])
    x = _ffn(x, g2_ref[...], ffn_s[0], ffn_s[1])
    gate = jax.nn.sigmoid(_dot(_rms(x, gple_ref[...]).astype(BF16), wg_ref[...]))
    x = x + gate * _dot(p_ref[...].astype(BF16), wp_ref[...])
    if final:
        x = _rms(x, gfin_ref[...])
    out_ref[...] = x


def _split_bf16(x):
    hi = x.astype(BF16)
    lo = (x - hi.astype(F32)).astype(BF16)
    return hi, lo


def _moba_gate(q, km):
    rows = A_HEADS * N_KBLK
    gt = jnp.concatenate([km] * A_HEADS, axis=0)
    r = lax.broadcasted_iota(jnp.int32, (rows, A_WIDTH), 0)
    c = lax.broadcasted_iota(jnp.int32, (rows, A_WIDTH), 1)
    gt = jnp.where((r // N_KBLK) == (c // A_HEAD_DIM), gt, 0.0)
    g_hi, g_lo = _split_bf16(gt)
    q_hi, q_lo = _split_bf16(q)
    return _dot_nt(g_hi, q_hi) + _dot_nt(g_hi, q_lo) + _dot_nt(g_lo, q_hi)


def _moba_select(gate, own):
    n_idx = lax.broadcasted_iota(jnp.int32, (N_KBLK, Q_TILE), 0)
    pen_rows = []
    for h in range(A_HEADS):
        gh = gate[h * N_KBLK:(h + 1) * N_KBLK]
        rank = jnp.zeros((N_KBLK, Q_TILE), jnp.int32)
        for m in range(N_KBLK):
            gm = gh[m:m + 1]
            beats = (gm > gh) | ((gm == gh) & (m < n_idx))
            rank = rank + jnp.where(beats & (m < own), 1, 0)
        sel = (n_idx < own) & (rank < MOBA_TOPK)
        pen_n = jnp.where(sel, 0.0, NEG_INF)
        pen_d = jnp.full((N_KBLK, Q_TILE), NEG_INF, F32)
        for n in range(N_KBLK):
            pen_d = jnp.where(n_idx == own - n, pen_n[n:n + 1], pen_d)
        pen_rows.append(pen_d)
    pairs = [jnp.concatenate(pen_rows[2 * hp:2 * hp + 2], axis=1) for hp in range(A_HEADS // 2)]
    return jnp.concatenate(pairs, axis=0)


def _t5_bias_kernel(tbl_ref, o_ref):
    hp = pl.program_id(0)
    r = lax.broadcasted_iota(jnp.int32, (Q_TILE, Q_TILE), 0)
    c = lax.broadcasted_iota(jnp.int32, (Q_TILE, Q_TILE), 1)
    max_exact = T5_BUCKETS // 2
    for dd in range(3):
        dist = dd * Q_TILE + c - r
        dc = jnp.maximum(dist, 0)
        df = jnp.maximum(dc.astype(F32), 1.0)
        large = max_exact + (jnp.log(df / max_exact) / math.log(T5_MAX_DIST / max_exact)
                             * (T5_BUCKETS - max_exact)).astype(jnp.int32)
        large = jnp.minimum(large, T5_BUCKETS - 1)
        bucket = jnp.where(dc < max_exact, dc, large)
        for hh in range(2):
            bias = jnp.zeros((Q_TILE, Q_TILE), F32)
            for b in range(T5_BUCKETS):
                bias = jnp.where(bucket == b, tbl_ref[b, 2 * hp + hh], bias)
            bias = bias * LOG2E
            if dd == 0:
                bias = jnp.where(dist >= 0, bias, NEG_INF)
            o_ref[0, dd, :, hh * Q_TILE:(hh + 1) * Q_TILE] = bias


def _fox_cumsum(z, carry_ref):
    lane = lax.broadcasted_iota(jnp.int32, (Q_TILE, LANES), 1)
    logf = jnp.minimum(z, 0.0) - jnp.log1p(jnp.exp(-jnp.abs(z)))
    logf = jnp.where(lane < C_HEADS, logf, 0.0)
    r = lax.broadcasted_iota(jnp.int32, (Q_TILE, Q_TILE), 0)
    c = lax.broadcasted_iota(jnp.int32, (Q_TILE, Q_TILE), 1)
    tri = jnp.where(c <= r, 1.0, 0.0).astype(BF16)
    l1 = logf.astype(BF16)
    rem = logf - l1.astype(F32)
    l2 = rem.astype(BF16)
    l3 = (rem - l2.astype(F32)).astype(BF16)
    cum = _dot(tri, l1) + _dot(tri, l2) + _dot(tri, l3) + carry_ref[...]
    carry_ref[...] = cum[Q_TILE - 1:Q_TILE, :]
    return cum


def _fox_decay_parts(cum):
    cs = cum * LOG2E
    c1 = cs.astype(BF16)
    rem = cs - c1.astype(F32)
    c2 = rem.astype(BF16)
    c3 = (rem - c2.astype(F32)).astype(BF16)
    rin = lax.broadcasted_iota(jnp.int32, (LANES, LANES), 0)
    lout = lax.broadcasted_iota(jnp.int32, (LANES, LANES), 1)
    out = jnp.zeros((Q_TILE, LANES), F32)
    for part, cp in enumerate((c1, c2, c3)):
        place = jnp.where((rin < C_HEADS) & (lout == rin * 8 + part), 1.0, 0.0).astype(BF16)
        out = out + _dot(cp, place)
    return out.astype(BF16)


def _attn_kernel(*refs, kind):
    if kind == "moba":
        q_ref, k_ref, vt_ref, pen_ref, bias_ref, o_ref = refs
    elif kind == "mla":
        q_ref, k_ref, vt_ref, o_ref = refs
    else:
        q_ref, k_ref, vt_ref, c_ref, o_ref, kp_s = refs
    grp = pl.program_id(1)
    lane = lax.broadcasted_iota(jnp.int32, (1, LANES), 1)
    low = lane < A_HEAD_DIM
    kw = LANES if kind == "moba" else 2 * LANES
    qw = 2 * LANES if kind == "mla" else LANES
    k_cat = kp_s if kind == "fox" else k_ref

    if kind == "fox":
        rin = lax.broadcasted_iota(jnp.int32, (LANES, LANES), 0)
        lout = lax.broadcasted_iota(jnp.int32, (LANES, LANES), 1)

        def place(pair, off_a, off_b, val):
            base_a = 2 * pair * 8
            base_b = base_a + 8
            sel_a = (lout >= off_a) & (lout < off_a + 3) & (rin == base_a + lout - off_a)
            sel_b = (lout >= off_b) & (lout < off_b + 3) & (rin == base_b + lout - off_b)
            return jnp.where(sel_a | sel_b, val, 0.0).astype(BF16)

        def ones(off_a, off_b):
            in_a = (lane >= off_a) & (lane < off_a + 3)
            in_b = (lane >= off_b) & (lane < off_b + 3)
            return jnp.where(in_a | in_b, 1.0, 0.0)

        aq_all = []
        for pi in range(PAIR_GROUP):
            ak = (_dot(c_ref[...], place(grp * PAIR_GROUP + pi, A_HEAD_DIM + 3, 3, -1.0))
                  + ones(A_HEAD_DIM, 0)).astype(BF16)
            k = k_ref[:, pi * LANES:(pi + 1) * LANES]
            kp_s[:, pi * kw:pi * kw + LANES] = jnp.where(low, k, ak)
            kp_s[:, pi * kw + LANES:(pi + 1) * kw] = jnp.where(low, ak, k)
            aq_all.append((_dot(c_ref[...], place(grp * PAIR_GROUP + pi, A_HEAD_DIM, 0, 1.0))
                           + ones(A_HEAD_DIM + 3, 3)).astype(BF16))

    def q_operand(tile, pi):
        zero = jnp.zeros((Q_TILE, LANES), BF16)
        rows = slice(tile * Q_TILE, (tile + 1) * Q_TILE)
        q = q_ref[rows, pi * qw:(pi + 1) * qw]
        if kind == "moba":
            return jnp.concatenate([jnp.where(low, q, zero), jnp.where(low, zero, q)], axis=0)
        if kind == "fox":
            aq = aq_all[pi][rows]
            qa, qb = jnp.where(low, q, aq), jnp.where(low, aq, q)
        else:
            qa, qb = q[:, :LANES], q[:, LANES:]
        return jnp.concatenate([jnp.concatenate([qa, zero], axis=1),
                                jnp.concatenate([zero, qb], axis=1)], axis=0)

    def store_out(pi, tile, acc):
        ot2 = acc[:LANES] / acc[LANES:LANES + 1]
        ot = jnp.concatenate([ot2[:A_HEAD_DIM, :Q_TILE], ot2[A_HEAD_DIM:, Q_TILE:]], axis=0)
        o_ref[tile * Q_TILE:(tile + 1) * Q_TILE, pi * LANES:(pi + 1) * LANES] = ot.T.astype(BF16)

    def v_aug(pi, rows):
        width = rows.stop - rows.start
        return jnp.concatenate([vt_ref[pi * LANES:(pi + 1) * LANES, rows],
                                jnp.ones((BF16_ROWS, width), BF16)], axis=0)

    if kind == "moba":
        _moba_schedule(q_operand, k_ref, pen_ref, bias_ref, v_aug, store_out)
    else:
        _online_schedule(q_operand, k_cat, kw, v_aug, store_out)


def _online_schedule(q_operand, k_cat, kw, v_aug, store_out):
    key = lax.broadcasted_iota(jnp.int32, (Q_TILE, 2 * Q_TILE), 0)
    qry = lax.broadcasted_iota(jnp.int32, (Q_TILE, 2 * Q_TILE), 1) % Q_TILE

    def run(chains):
        n = len(chains)

        def rows_of(c, d):
            nb = chains[c][1]
            return slice((nb - 1 - d) * Q_TILE, (nb - d) * Q_TILE)

        def qk(c, d):
            pi = chains[c][0]
            return _dot_nt(k_cat[rows_of(c, d), pi * kw:(pi + 1) * kw], chains[c][2])

        s_next = [qk(c, 0) for c in range(n)]
        m = [None] * n
        acc = [None] * n
        for d in range(max(c[1] for c in chains)):
            live = [c for c in range(n) if d < chains[c][1]]
            s_cur = list(s_next)
            for c in live:
                if d + 1 < chains[c][1]:
                    s_next[c] = qk(c, d + 1)
            for c in live:
                sn = s_cur[c]
                if d == 0:
                    sn = jnp.where(key <= qry, sn, NEG_INF)
                bm = jnp.max(sn, axis=0, keepdims=True)
                m_new = bm if d == 0 else jnp.maximum(m[c], bm)
                pv = _dot(v_aug(chains[c][0], rows_of(c, d)), jnp.exp2(sn - m_new).astype(BF16))
                acc[c] = pv if d == 0 else acc[c] * jnp.exp2(m[c] - m_new) + pv
                m[c] = m_new
        return acc

    for jj in range(N_KBLK // 2):
        tiles = (N_KBLK - 1 - jj, jj)
        chains = [(pi, tile + 1, q_operand(tile, pi)) for pi in range(PAIR_GROUP) for tile in tiles]
        for (pi, nb, _), acc in zip(chains, run(chains)):
            store_out(pi, nb - 1, acc)


def _moba_schedule(q_operand, k_ref, pen_ref, bias_ref, v_aug, store_out):
    def new_chain(pi, nb):
        return {"pair": pi, "nb": nb, "q": q_operand(nb - 1, pi), "blocks": [], "shifts": [], "m": None}

    def score_block(ch, n):
        pi, nb = ch["pair"], ch["nb"]
        d = nb - 1 - n
        sn = _dot_nt(k_ref[n * Q_TILE:(n + 1) * Q_TILE, pi * LANES:(pi + 1) * LANES], ch["q"])
        shift = None
        if d < 2:
            sn = sn + bias_ref[pi, d]
        else:
            shift = bias_ref[pi, 2, 0:1, :]
        if d > 0:
            pen = pen_ref[0, nb - 1, pi * N_KBLK + d:pi * N_KBLK + d + 1, :]
            shift = pen if shift is None else shift + pen
        bm = jnp.max(sn, axis=0, keepdims=True)
        if shift is not None:
            bm = bm + shift
        ch["m"] = bm if ch["m"] is None else jnp.maximum(ch["m"], bm)
        ch["blocks"].append(sn)
        ch["shifts"].append(shift)

    def prob_block(ch, n):
        m, sh = ch["m"], ch["shifts"][n]
        return jnp.exp2(ch["blocks"][n] - (m if sh is None else m - sh)).astype(BF16)

    order = [(pi, nb) for nb in range(N_KBLK, 0, -1) for pi in range(PAIR_GROUP)]
    prev = None
    for item in order + [None]:
        cur = None if item is None else new_chain(*item)
        probs = []
        for n in range(max(cur["nb"] if cur else 0, prev["nb"] if prev else 0)):
            if cur is not None and n < cur["nb"]:
                score_block(cur, n)
            if prev is not None and n < prev["nb"]:
                probs.append(prob_block(prev, n))
        if prev is not None:
            nk = prev["nb"] * Q_TILE
            acc = _dot(v_aug(prev["pair"], slice(0, nk)), jnp.concatenate(probs, axis=0))
            store_out(prev["pair"], prev["nb"] - 1, acc)
        prev = cur


def _attention(kind, ins, in_specs, n_pairs, extra_scratch=()):
    width = PAIR_GROUP * LANES
    return pl.pallas_call(
        functools.partial(_attn_kernel, kind=kind),
        grid=(BATCH, n_pairs // PAIR_GROUP),
        in_specs=in_specs,
        out_specs=pl.BlockSpec((SEQ, width), lambda b, g: (b, g)),
        out_shape=jax.ShapeDtypeStruct((TOKENS, n_pairs * LANES), BF16),
        scratch_shapes=list(extra_scratch),
        compiler_params=pltpu.CompilerParams(
            dimension_semantics=("arbitrary", "arbitrary"), vmem_limit_bytes=VMEM_LIMIT),
        name="attn_" + kind,
    )(*ins)


def _seq_spec(pair_width, col0):
    return pl.BlockSpec((SEQ, PAIR_GROUP * pair_width), lambda b, g: (b, col0 + g))


def _vt_spec():
    return pl.BlockSpec((PAIR_GROUP * LANES, SEQ), lambda b, g: (g, b))


def _place_heads(w, n_heads, src_stride, src_off, width):
    per_head = w.reshape(w.shape[0], n_heads, src_stride)[:, :, src_off:src_off + width]
    per_head = jnp.pad(per_head, ((0, 0), (0, 0), (0, LANES - width)))
    return per_head.reshape(w.shape[0], n_heads * LANES)


def _row(v, width=None):
    v = v.reshape(1, -1).astype(F32)
    if width is not None and v.shape[1] < width:
        v = jnp.pad(v, ((0, 0), (0, width - v.shape[1])))
    return v


def kernel(x, p, t5_bias, ff1_norm, ff1_w_in, ff1_w_out, mix_norm, ff2_norm, ff2_w_in, ff2_w_out,
           ple_norm, ple_w_gate, ple_w_proj, ab_w_in, mla_q_norm, mla_w_uq, mla_kv_norm, mla_w_ukv,
           ab_w_out, fox_w_in, fox_b_f, fox_w_out, final_norm):
    xt = x.reshape(TOKENS, D_MODEL)

    ffw = {1: (ff1_norm, ff1_w_in, ff1_w_out), 2: (ff2_norm, ff2_w_in, ff2_w_out)}
    hbm = pl.BlockSpec(memory_space=pl.ANY)

    def ffn_args(which, layer):
        norm, w_in, w_out = ffw[which]
        return [_row(norm[layer]), w_in, w_out], [_wspec((1, D_MODEL)), hbm, hbm]

    ple_gate_bf = ple_w_gate.astype(BF16)
    ple_proj_bf = ple_w_proj.astype(BF16)

    def post(xin, mixes, w_outs, layer, final):
        fa, fs = ffn_args(2, layer)
        ws = [w.astype(BF16) for w in w_outs]
        ins = ([xin] + list(mixes) + ws + fa
               + [_row(ple_norm[layer]), ple_gate_bf, p.reshape(DEPTH, TOKENS, PLE_DIM), ple_proj_bf,
                  _row(final_norm)])
        specs = ([_tok_spec(D_MODEL)] + [_tok_spec(m.shape[1]) for m in mixes]
                 + [_wspec(w.shape) for w in ws] + fs
                 + [_wspec((1, D_MODEL)), _wspec((D_MODEL, D_MODEL), layer),
                    pl.BlockSpec((None, TOKEN_TILE, PLE_DIM), lambda t: (layer, t, 0)),
                    _wspec((PLE_DIM, D_MODEL), layer), _wspec((1, D_MODEL))])
        return _token_call(
            "post%d" % layer,
            functools.partial(_post_kernel, n_mix=len(mixes), final=final, layer=layer), ins, specs,
            jax.ShapeDtypeStruct((TOKENS, D_MODEL), F32), _tok_spec(D_MODEL))

    w_ab = ab_w_in[0]
    w_qk = w_ab[:, :2 * A_WIDTH].astype(BF16)
    w_vt = w_ab[:, 2 * A_WIDTH:3 * A_WIDTH].T.astype(BF16)
    c0 = 3 * A_WIDTH
    kr0 = MLA_Q_RANK + MLA_KV_RANK
    w_c = jnp.concatenate(
        [w_ab[:, c0:c0 + kr0], jnp.zeros((D_MODEL, MLA_NOPE), F32), w_ab[:, c0 + kr0:],
         jnp.zeros((D_MODEL, LANES - MLA_NOPE - MLA_ROPE), F32)], axis=1).astype(BF16)
    w_uq = _place_heads(mla_w_uq[0], B_HEADS, MLA_NOPE + MLA_ROPE, 0, MLA_NOPE + MLA_ROPE).astype(BF16)
    w_ukv_k = _place_heads(mla_w_ukv[0], B_HEADS, MLA_NOPE + MLA_V, 0, MLA_NOPE).astype(BF16)
    w_ukv = mla_w_ukv[0].reshape(MLA_KV_RANK, B_HEADS, MLA_NOPE + MLA_V)
    w_ukv_vt = w_ukv[:, :, MLA_NOPE:].reshape(MLA_KV_RANK, B_HEADS * MLA_V).T.astype(BF16)
    half = MLA_ROPE // 2
    inv = ROPE_THETA ** (-np.arange(half, dtype=np.float64) / half)
    inv_lane = np.zeros((1, LANES), np.float32)
    inv_lane[0, MLA_NOPE:MLA_NOPE + half] = inv
    inv_lane[0, MLA_NOPE + half:MLA_NOPE + MLA_ROPE] = inv
    inv_lane = jnp.asarray(inv_lane)

    fa, fs = ffn_args(1, 0)
    ins = ([xt] + fa + [_row(mix_norm[0]), w_qk, w_vt, w_c, _row(mla_q_norm[0]), w_uq,
                        _row(mla_kv_norm[0]), w_ukv_k, w_ukv_vt, inv_lane])
    specs = ([_tok_spec(D_MODEL)] + fs
             + [_wspec((1, D_MODEL)), _wspec(w_qk.shape), _wspec(w_vt.shape), _wspec(w_c.shape),
                _wspec((1, MLA_Q_RANK)), _wspec(w_uq.shape), _wspec((1, MLA_KV_RANK)),
                _wspec(w_ukv_k.shape), _wspec(w_ukv_vt.shape), _wspec((1, LANES))])
    pen_rows = A_HEADS // 2 * N_KBLK
    outs = (jax.ShapeDtypeStruct((TOKENS, D_MODEL), F32),
            jax.ShapeDtypeStruct((TOKENS, 2 * A_WIDTH), BF16),
            jax.ShapeDtypeStruct((A_WIDTH, TOKENS), BF16),
            jax.ShapeDtypeStruct((BATCH, N_KBLK, pen_rows, 2 * Q_TILE), F32),
            jax.ShapeDtypeStruct((TOKENS, B_HEADS * LANES), BF16),
            jax.ShapeDtypeStruct((TOKENS, B_HEADS * LANES), BF16),
            jax.ShapeDtypeStruct((B_HEADS * MLA_V, TOKENS), BF16))
    out_specs = (_tok_spec(D_MODEL), _tok_spec(2 * A_WIDTH), _tok_t_spec(A_WIDTH),
                 pl.BlockSpec((1, TILE_R, pen_rows, 2 * Q_TILE),
                              lambda t: (t // (N_KBLK // TILE_R), t % (N_KBLK // TILE_R), 0, 0)),
                 _tok_spec(B_HEADS * LANES), _tok_spec(B_HEADS * LANES),
                 _tok_t_spec(B_HEADS * MLA_V))
    x1, qk_a, vt_a, pen, q_mla, k_mla, vt_mla = _token_call(
        "pre0", functools.partial(_pre0_kernel, layer=0), ins, specs, outs, out_specs,
        scratch=[pltpu.VMEM((N_KBLK, A_WIDTH), F32)])

    bias = pl.pallas_call(
        _t5_bias_kernel,
        grid=(A_HEADS // 2,),
        in_specs=[pl.BlockSpec(memory_space=pltpu.SMEM)],
        out_specs=pl.BlockSpec((1, 3, Q_TILE, 2 * Q_TILE), lambda h: (h, 0, 0, 0)),
        out_shape=jax.ShapeDtypeStruct((A_HEADS // 2, 3, Q_TILE, 2 * Q_TILE), F32),
        name="t5_bias_tiles",
    )(t5_bias.astype(F32))

    na = A_HEADS // 2
    o_a = _attention(
        "moba", [qk_a, qk_a, vt_a, pen, bias],
        [_seq_spec(LANES, 0), _seq_spec(LANES, na // PAIR_GROUP), _vt_spec(),
         pl.BlockSpec((1, N_KBLK, PAIR_GROUP * N_KBLK, 2 * Q_TILE), lambda b, g: (b, 0, g, 0)),
         pl.BlockSpec((PAIR_GROUP, 3, Q_TILE, 2 * Q_TILE), lambda b, g: (g, 0, 0, 0))], na)
    o_b = _attention(
        "mla", [q_mla, k_mla, vt_mla],
        [_seq_spec(2 * LANES, 0), _seq_spec(2 * LANES, 0), _vt_spec()], B_HEADS // 2)
    w_o = ab_w_out[0]
    xt = post(x1, [o_a, o_b], [w_o[:A_WIDTH], w_o[A_WIDTH:]], 0, DEPTH == 1)

    w_fox = fox_w_in[0]
    w_qk = w_fox[:, :2 * C_WIDTH].astype(BF16)
    w_vt = w_fox[:, 2 * C_WIDTH:3 * C_WIDTH].T.astype(BF16)
    w_f = jnp.pad(w_fox[:, 3 * C_WIDTH:], ((0, 0), (0, LANES - C_HEADS))).astype(BF16)
    fa, fs = ffn_args(1, 1)
    ins = [xt] + fa + [_row(mix_norm[1]), w_qk, w_vt, w_f, _row(fox_b_f[0], LANES)]
    specs = ([_tok_spec(D_MODEL)] + fs
             + [_wspec((1, D_MODEL)), _wspec(w_qk.shape), _wspec(w_vt.shape), _wspec(w_f.shape),
                _wspec((1, LANES))])
    outs = (jax.ShapeDtypeStruct((TOKENS, D_MODEL), F32),
            jax.ShapeDtypeStruct((TOKENS, 2 * C_WIDTH), BF16),
            jax.ShapeDtypeStruct((C_WIDTH, TOKENS), BF16),
            jax.ShapeDtypeStruct((TOKENS, LANES), BF16))
    out_specs = (_tok_spec(D_MODEL), _tok_spec(2 * C_WIDTH), _tok_t_spec(C_WIDTH), _tok_spec(LANES))
    x1, qk_c, vt_c, caug = _token_call("pre1", functools.partial(_pre1_kernel, layer=1), ins, specs, outs, out_specs,
                                       scratch=[pltpu.VMEM((1, LANES), F32)])

    nc = C_HEADS // 2
    o_c = _attention(
        "fox", [qk_c, qk_c, vt_c, caug],
        [_seq_spec(LANES, 0), _seq_spec(LANES, nc // PAIR_GROUP), _vt_spec(),
         pl.BlockSpec((SEQ, LANES), lambda b, g: (b, 0))],
        nc, extra_scratch=[pltpu.VMEM((SEQ, PAIR_GROUP * 2 * LANES), BF16)])
    xt = post(x1, [o_c], [fox_w_out[0]], 1, True)
    return xt.reshape(BATCH, SEQ, D_MODEL)
```

```python
import functools
import math

import jax
import jax.numpy as jnp
import numpy as np
from jax import lax
from jax.experimental import pallas as pl
from jax.experimental.pallas import tpu as pltpu

F32 = jnp.float32
BF16 = jnp.bfloat16

D_MODEL = 1024
BATCH = 8
SEQ = 2048
DEPTH = 2
PLE_DIM = 256
D_FF = 2816
EPS = 1e-6

A_HEADS = 8
A_HEAD_DIM = 64
MOBA_BLOCK = 256
MOBA_TOPK = 3

B_HEADS = 8
MLA_Q_RANK = 256
MLA_KV_RANK = 128
MLA_NOPE = 64
MLA_ROPE = 32
MLA_V = 64
ROPE_THETA = 10000.0

T5_BUCKETS = 32
T5_MAX_DIST = 128

C_HEADS = 16
C_HEAD_DIM = 64

A_WIDTH = A_HEADS * A_HEAD_DIM
C_WIDTH = C_HEADS * C_HEAD_DIM

TOKENS = BATCH * SEQ
LANES = 128
BF16_ROWS = 16
PAIR_GROUP = 2
Q_TILE = MOBA_BLOCK
N_KBLK = SEQ // Q_TILE
TOKEN_TILE = 2 * Q_TILE
TILE_R = TOKEN_TILE // Q_TILE
FF_CHUNK = D_FF // 2
W_IN_CHUNK = 512
W_OUT_CHUNK = 256
VMEM_LIMIT = 56 * 1024 * 1024
NEG_INF = float("-inf")
LOG2E = math.log2(math.e)
QSCALE_64 = A_HEAD_DIM ** -0.5 * LOG2E
QSCALE_MLA = (MLA_NOPE + MLA_ROPE) ** -0.5 * LOG2E


def _wspec(shape, layer=None):
    nd = len(shape)
    if layer is None:
        return pl.BlockSpec(shape, lambda *_: (0,) * nd, pipeline_mode=pl.Buffered(1))
    return pl.BlockSpec((None,) + tuple(shape), lambda *_: (layer,) + (0,) * nd,
                        pipeline_mode=pl.Buffered(1))


def _dot(a, b):
    return jnp.dot(a, b, preferred_element_type=F32)


def _dot_nt(a, b):
    return lax.dot_general(a, b, (((1,), (1,)), ((), ())), preferred_element_type=F32)


def _rms(x, g):
    return x * lax.rsqrt(jnp.mean(x * x, axis=-1, keepdims=True) + EPS) * g


def _ffn_scratch():
    return [pltpu.VMEM((D_MODEL, 2 * D_FF), BF16), pltpu.VMEM((D_FF, D_MODEL), BF16),
            pltpu.VMEM((2, D_MODEL, W_IN_CHUNK), F32), pltpu.VMEM((2, W_OUT_CHUNK, D_MODEL), F32),
            pltpu.SemaphoreType.DMA((2, 2))]


def _load_ffn_weights(layer, win_hbm, wo_hbm, win_ref, wo_ref, stage_in, stage_out, sem):
    def copy_in(c, slot):
        return pltpu.make_async_copy(win_hbm.at[layer, :, pl.ds(c * W_IN_CHUNK, W_IN_CHUNK)],
                                     stage_in.at[slot], sem.at[0, slot])

    def copy_out(c, slot):
        return pltpu.make_async_copy(wo_hbm.at[layer, pl.ds(c * W_OUT_CHUNK, W_OUT_CHUNK), :],
                                     stage_out.at[slot], sem.at[1, slot])

    @pl.when(pl.program_id(0) == 0)
    def _():
        n_in = 2 * D_FF // W_IN_CHUNK
        n_out = D_FF // W_OUT_CHUNK
        copy_in(0, 0).start()
        copy_out(0, 0).start()
        for c in range(n_in):
            slot = c % 2
            if c + 1 < n_in:
                copy_in(c + 1, 1 - slot).start()
            copy_in(c, slot).wait()
            win_ref[:, c * W_IN_CHUNK:(c + 1) * W_IN_CHUNK] = stage_in[slot].astype(BF16)
        for c in range(n_out):
            slot = c % 2
            if c + 1 < n_out:
                copy_out(c + 1, 1 - slot).start()
            copy_out(c, slot).wait()
            wo_ref[c * W_OUT_CHUNK:(c + 1) * W_OUT_CHUNK, :] = stage_out[slot].astype(BF16)


def _ffn(x, g, win_ref, wo_ref):
    h = _rms(x, g).astype(BF16)
    acts = []
    for c in range(D_FF // FF_CHUNK):
        lo, hi = c * FF_CHUNK, (c + 1) * FF_CHUNK
        a = _dot(h, win_ref[:, lo:hi])
        u = _dot(h, win_ref[:, D_FF + lo:D_FF + hi])
        acts.append((a * jax.nn.sigmoid(a) * u).astype(BF16))
    return x + 0.5 * _dot(jnp.concatenate(acts, axis=1), wo_ref[...])


def _rope_table_kernel(inv_ref, o_ref):
    pos0 = pl.program_id(0) * TOKEN_TILE
    pos = (pos0 + lax.broadcasted_iota(jnp.int32, (TOKEN_TILE, LANES), 0)).astype(F32)
    lane = lax.broadcasted_iota(jnp.int32, (TOKEN_TILE, LANES), 1)
    ang = pos * inv_ref[...]
    is_x1 = (lane >= MLA_NOPE) & (lane < MLA_NOPE + MLA_ROPE // 2)
    is_x2 = (lane >= MLA_NOPE + MLA_ROPE // 2) & (lane < MLA_NOPE + MLA_ROPE)
    cos_t = jnp.where(is_x1 | is_x2, jnp.cos(ang), 1.0)
    sin = jnp.sin(ang)
    sin_t = jnp.where(is_x1, -sin, jnp.where(is_x2, sin, 0.0))
    o_ref[0] = cos_t
    o_ref[1] = sin_t
    o_ref[2] = cos_t * QSCALE_MLA
    o_ref[3] = sin_t * QSCALE_MLA


def _rope_block(xb, cos_t, sin_t):
    half = MLA_ROPE // 2
    lane = lax.broadcasted_iota(jnp.int32, xb.shape, 1)
    is_x1 = lane < MLA_NOPE + half
    partner = jnp.where(is_x1, pltpu.roll(xb, LANES - half, 1), pltpu.roll(xb, half, 1))
    return xb * cos_t + partner * sin_t


def _pre0_kernel(x_ref, g1_ref, win_hbm, wo_hbm, gmix_ref, wqk_ref, wvt_ref, wc_ref,
                 qn_ref, wuq_ref, wuqrot_ref, kvn_ref, wukvk_ref, wukvvt_ref, rope_ref,
                 x1_ref, qk_ref, vt_ref, pen_ref, qm_ref, kmla_ref, vmt_ref, km_s, *ffn_s, layer):
    t = pl.program_id(0)
    _load_ffn_weights(layer, win_hbm, wo_hbm, *ffn_s)
    x1 = _ffn(x_ref[...], g1_ref[...], ffn_s[0], ffn_s[1])
    x1_ref[...] = x1
    h = _rms(x1, gmix_ref[...]).astype(BF16)
    qk = _dot(h, wqk_ref[...])
    q = qk[:, :A_WIDTH]
    k = qk[:, A_WIDTH:]
    qk_ref[:, :A_WIDTH] = (q * QSCALE_64).astype(BF16)
    qk_ref[:, A_WIDTH:] = k.astype(BF16)

    @pl.when(t == 0)
    def _():
        km_s[...] = jnp.zeros_like(km_s)

    c = _dot(h, wc_ref[...])
    gates = []
    for r in range(TILE_R):
        own = (t * TILE_R + r) % N_KBLK
        part = slice(r * Q_TILE, (r + 1) * Q_TILE)
        km_s[pl.ds(own, 1), :] = jnp.mean(k[part], axis=0, keepdims=True)
        gates.append((own, _moba_gate(q[part], km_s[...])))
    vt_ref[...] = _dot_nt(wvt_ref[...], h).astype(BF16)
    for r, (own, gate) in enumerate(gates):
        pen_ref[0, r] = _moba_select(gate, own)

    cq = c[:, :MLA_Q_RANK]
    ckv = c[:, MLA_Q_RANK:MLA_Q_RANK + MLA_KV_RANK]
    kr = c[:, MLA_Q_RANK + MLA_KV_RANK:]
    cqn = _rms(cq, qn_ref[...]).astype(BF16)
    ckvn = _rms(ckv, kvn_ref[...]).astype(BF16)
    qm = _dot(cqn, wuq_ref[...])
    qm_partner = _dot(cqn, wuqrot_ref[...])
    kn = _dot(ckvn, wukvk_ref[...])
    vmt_ref[...] = _dot_nt(wukvvt_ref[...], ckvn).astype(BF16)

    krr = _rope_block(kr, rope_ref[0], rope_ref[1])
    for hb in range(B_HEADS):
        sl = slice(hb * LANES, (hb + 1) * LANES)
        qm_ref[:, sl] = (qm[:, sl] * rope_ref[2] + qm_partner[:, sl] * rope_ref[3]).astype(BF16)
        kmla_ref[:, sl] = (kn[:, sl] + krr).astype(BF16)


def _pre1_kernel(x_ref, g1_ref, win_hbm, wo_hbm, gmix_ref, wqk_ref, wvt_ref, wf_ref, bf_ref,
                 x1_ref, qk_ref, vt_ref, caug_ref, carry_s, *ffn_s, layer):
    t = pl.program_id(0)
    _load_ffn_weights(layer, win_hbm, wo_hbm, *ffn_s)
    x1 = _ffn(x_ref[...], g1_ref[...], ffn_s[0], ffn_s[1])
    x1_ref[...] = x1
    h = _rms(x1, gmix_ref[...]).astype(BF16)

    @pl.when(t % (N_KBLK // TILE_R) == 0)
    def _():
        carry_s[...] = jnp.zeros_like(carry_s)

    z = _dot(h, wf_ref[...]) + bf_ref[...]
    qk = _dot(h, wqk_ref[...])
    qk_ref[:, :C_WIDTH] = (qk[:, :C_WIDTH] * QSCALE_64).astype(BF16)
    qk_ref[:, C_WIDTH:] = qk[:, C_WIDTH:].astype(BF16)
    cums = [_fox_cumsum(z[r * Q_TILE:(r + 1) * Q_TILE], carry_s) for r in range(TILE_R)]
    vt_ref[...] = _dot_nt(wvt_ref[...], h).astype(BF16)
    for r, cum in enumerate(cums):
        caug_ref[r * Q_TILE:(r + 1) * Q_TILE, :] = _fox_decay_parts(cum)


def _tok_spec(width):
    return pl.BlockSpec((TOKEN_TILE, width), lambda i: (i, 0))


def _tok_t_spec(height):
    return pl.BlockSpec((height, TOKEN_TILE), lambda i: (0, i))


def _token_call(name, body, ins, in_specs, outs, out_specs, scratch=()):
    return pl.pallas_call(
        body,
        grid=(TOKENS // TOKEN_TILE,),
        in_specs=in_specs,
        out_specs=out_specs,
        out_shape=outs,
        scratch_shapes=list(scratch) + _ffn_scratch(),
        compiler_params=pltpu.CompilerParams(
            dimension_semantics=("arbitrary",), vmem_limit_bytes=VMEM_LIMIT),
        name=name,
    )(*ins)


def _post_kernel(*refs, n_mix, final, layer):
    x_ref = refs[0]
    o_refs = refs[1:1 + n_mix]
    w_refs = refs[1 + n_mix:1 + 2 * n_mix]
    (g2_ref, win_hbm, wo_hbm, gple_ref, wg_ref, p_ref, wp_ref, gfin_ref,
     out_ref) = refs[1 + 2 * n_mix:10 + 2 * n_mix]
    ffn_s = refs[10 + 2 * n_mix:]
    _load_ffn_weights(layer, win_hbm, wo_hbm, *ffn_s)
    x = x_ref[...]
    for o_ref, w_ref in zip(o_refs, w_refs):
        x = x + _dot(o_ref[...], w_ref[...])
    x = _ffn(x, g2_ref[...], ffn_s[0], ffn_s[1])
    gate = jax.nn.sigmoid(_dot(_rms(x, gple_ref[...]).astype(BF16), wg_ref[...]))
    x = x + gate * _dot(p_ref[...].astype(BF16), wp_ref[...])
    if final:
        x = _rms(x, gfin_ref[...])
    out_ref[...] = x


def _split_bf16(x):
    hi = x.astype(BF16)
    lo = (x - hi.astype(F32)).astype(BF16)
    return hi, lo


def _moba_gate(q, km):
    rows = A_HEADS * N_KBLK
    gt = jnp.concatenate([km] * A_HEADS, axis=0)
    r = lax.broadcasted_iota(jnp.int32, (rows, A_WIDTH), 0)
    c = lax.broadcasted_iota(jnp.int32, (rows, A_WIDTH), 1)
    gt = jnp.where((r // N_KBLK) == (c // A_HEAD_DIM), gt, 0.0)
    g_hi, g_lo = _split_bf16(gt)
    q_hi, q_lo = _split_bf16(q)
    return _dot_nt(g_hi, q_hi) + _dot_nt(g_hi, q_lo) + _dot_nt(g_lo, q_hi)


def _moba_select(gate, own):
    n_idx = lax.broadcasted_iota(jnp.int32, (N_KBLK, Q_TILE), 0)
    pen_rows = []
    for h in range(A_HEADS):
        gh = gate[h * N_KBLK:(h + 1) * N_KBLK]
        rank = jnp.zeros((N_KBLK, Q_TILE), jnp.int32)
        for m in range(N_KBLK):
            gm = gh[m:m + 1]
            beats = (gm > gh) | ((gm == gh) & (m < n_idx))
            rank = rank + jnp.where(beats & (m < own), 1, 0)
        sel = (n_idx < own) & (rank < MOBA_TOPK)
        pen_n = jnp.where(sel, 0.0, NEG_INF)
        pen_d = jnp.full((N_KBLK, Q_TILE), NEG_INF, F32)
        for n in range(N_KBLK):
            pen_d = jnp.where(n_idx == own - n, pen_n[n:n + 1], pen_d)
        pen_rows.append(pen_d)
    pairs = [jnp.concatenate(pen_rows[2 * hp:2 * hp + 2], axis=1) for hp in range(A_HEADS // 2)]
    return jnp.concatenate(pairs, axis=0)


def _t5_bias_kernel(tbl_ref, o_ref):
    hp = pl.program_id(0)
    r = lax.broadcasted_iota(jnp.int32, (Q_TILE, Q_TILE), 0)
    c = lax.broadcasted_iota(jnp.int32, (Q_TILE, Q_TILE), 1)
    max_exact = T5_BUCKETS // 2
    for dd in range(3):
        dist = dd * Q_TILE + c - r
        dc = jnp.maximum(dist, 0)
        df = jnp.maximum(dc.astype(F32), 1.0)
        large = max_exact + (jnp.log(df / max_exact) / math.log(T5_MAX_DIST / max_exact)
                             * (T5_BUCKETS - max_exact)).astype(jnp.int32)
        large = jnp.minimum(large, T5_BUCKETS - 1)
        bucket = jnp.where(dc < max_exact, dc, large)
        for hh in range(2):
            bias = jnp.zeros((Q_TILE, Q_TILE), F32)
            for b in range(T5_BUCKETS):
                bias = jnp.where(bucket == b, tbl_ref[b, 2 * hp + hh], bias)
            bias = bias * LOG2E
            if dd == 0:
                bias = jnp.where(dist >= 0, bias, NEG_INF)
            o_ref[0, dd, :, hh * Q_TILE:(hh + 1) * Q_TILE] = bias


def _fox_cumsum(z, carry_ref):
    lane = lax.broadcasted_iota(jnp.int32, (Q_TILE, LANES), 1)
    logf = jnp.minimum(z, 0.0) - jnp.log1p(jnp.exp(-jnp.abs(z)))
    logf = jnp.where(lane < C_HEADS, logf, 0.0)
    r = lax.broadcasted_iota(jnp.int32, (Q_TILE, Q_TILE), 0)
    c = lax.broadcasted_iota(jnp.int32, (Q_TILE, Q_TILE), 1)
    tri = jnp.where(c <= r, 1.0, 0.0).astype(BF16)
    l1 = logf.astype(BF16)
    rem = logf - l1.astype(F32)
    l2 = rem.astype(BF16)
    l3 = (rem - l2.astype(F32)).astype(BF16)
    cum = _dot(tri, l1) + _dot(tri, l2) + _dot(tri, l3) + carry_ref[...]
    carry_ref[...] = cum[Q_TILE - 1:Q_TILE, :]
    return cum


def _fox_decay_parts(cum):
    cs = cum * LOG2E
    c1 = cs.astype(BF16)
    rem = cs - c1.astype(F32)
    c2 = rem.astype(BF16)
    c3 = (rem - c2.astype(F32)).astype(BF16)
    rin = lax.broadcasted_iota(jnp.int32, (LANES, LANES), 0)
    lout = lax.broadcasted_iota(jnp.int32, (LANES, LANES), 1)
    out = jnp.zeros((Q_TILE, LANES), F32)
    for part, cp in enumerate((c1, c2, c3)):
        place = jnp.where((rin < C_HEADS) & (lout == rin * 8 + part), 1.0, 0.0).astype(BF16)
        out = out + _dot(cp, place)
    return out.astype(BF16)


def _attn_kernel(*refs, kind):
    if kind == "moba":
        q_ref, k_ref, vt_ref, pen_ref, bias_ref, o_ref = refs
    elif kind == "mla":
        q_ref, k_ref, vt_ref, o_ref = refs
    else:
        q_ref, k_ref, vt_ref, c_ref, o_ref, kp_s = refs
    grp = pl.program_id(1)
    lane = lax.broadcasted_iota(jnp.int32, (1, LANES), 1)
    low = lane < A_HEAD_DIM
    kw = LANES if kind == "moba" else 2 * LANES
    qw = 2 * LANES if kind == "mla" else LANES
    k_cat = kp_s if kind == "fox" else k_ref

    if kind == "fox":
        rin = lax.broadcasted_iota(jnp.int32, (LANES, LANES), 0)
        lout = lax.broadcasted_iota(jnp.int32, (LANES, LANES), 1)

        def place(pair, off_a, off_b, val):
            base_a = 2 * pair * 8
            base_b = base_a + 8
            sel_a = (lout >= off_a) & (lout < off_a + 3) & (rin == base_a + lout - off_a)
            sel_b = (lout >= off_b) & (lout < off_b + 3) & (rin == base_b + lout - off_b)
            return jnp.where(sel_a | sel_b, val, 0.0).astype(BF16)

        def ones(off_a, off_b):
            in_a = (lane >= off_a) & (lane < off_a + 3)
            in_b = (lane >= off_b) & (lane < off_b + 3)
            return jnp.where(in_a | in_b, 1.0, 0.0)

        aq_all = []
        for pi in range(PAIR_GROUP):
            ak = (_dot(c_ref[...], place(grp * PAIR_GROUP + pi, A_HEAD_DIM + 3, 3, -1.0))
                  + ones(A_HEAD_DIM, 0)).astype(BF16)
            k = k_ref[:, pi * LANES:(pi + 1) * LANES]
            kp_s[:, pi * kw:pi * kw + LANES] = jnp.where(low, k, ak)
            kp_s[:, pi * kw + LANES:(pi + 1) * kw] = jnp.where(low, ak, k)
            aq_all.append((_dot(c_ref[...], place(grp * PAIR_GROUP + pi, A_HEAD_DIM, 0, 1.0))
                           + ones(A_HEAD_DIM + 3, 3)).astype(BF16))

    def q_operand(tile, pi):
        zero = jnp.zeros((Q_TILE, LANES), BF16)
        rows = slice(tile * Q_TILE, (tile + 1) * Q_TILE)
        q = q_ref[rows, pi * qw:(pi + 1) * qw]
        if kind == "moba":
            return jnp.concatenate([jnp.where(low, q, zero), jnp.where(low, zero, q)], axis=0)
        if kind == "fox":
            aq = aq_all[pi][rows]
            qa, qb = jnp.where(low, q, aq), jnp.where(low, aq, q)
        else:
            qa, qb = q[:, :LANES], q[:, LANES:]
        return jnp.concatenate([jnp.concatenate([qa, zero], axis=1),
                                jnp.concatenate([zero, qb], axis=1)], axis=0)

    def store_out(pi, tile, acc):
        ot2 = acc[:LANES] / acc[LANES:LANES + 1]
        ot = jnp.concatenate([ot2[:A_HEAD_DIM, :Q_TILE], ot2[A_HEAD_DIM:, Q_TILE:]], axis=0)
        o_ref[tile * Q_TILE:(tile + 1) * Q_TILE, pi * LANES:(pi + 1) * LANES] = ot.T.astype(BF16)

    def v_aug(pi, rows):
        width = rows.stop - rows.start
        return jnp.concatenate([vt_ref[pi * LANES:(pi + 1) * LANES, rows],
                                jnp.ones((BF16_ROWS, width), BF16)], axis=0)

    if kind == "moba":
        _moba_schedule(q_operand, k_ref, pen_ref, bias_ref, v_aug, store_out)
    else:
        _online_schedule(q_operand, k_cat, kw, v_aug, store_out)


def _online_schedule(q_operand, k_cat, kw, v_aug, store_out):
    key = lax.broadcasted_iota(jnp.int32, (Q_TILE, 2 * Q_TILE), 0)
    qry = lax.broadcasted_iota(jnp.int32, (Q_TILE, 2 * Q_TILE), 1) % Q_TILE

    def run(chains):
        n = len(chains)

        def rows_of(c, d):
            nb = chains[c][1]
            return slice((nb - 1 - d) * Q_TILE, (nb - d) * Q_TILE)

        def qk(c, d):
            pi = chains[c][0]
            return _dot_nt(k_cat[rows_of(c, d), pi * kw:(pi + 1) * kw], chains[c][2])

        s_next = [qk(c, 0) for c in range(n)]
        m = [None] * n
        acc = [None] * n
        for d in range(max(c[1] for c in chains)):
            live = [c for c in range(n) if d < chains[c][1]]
            s_cur = list(s_next)
            for c in live:
                if d + 1 < chains[c][1]:
                    s_next[c] = qk(c, d + 1)
            for c in live:
                sn = s_cur[c]
                if d == 0:
                    sn = jnp.where(key <= qry, sn, NEG_INF)
                bm = jnp.max(sn, axis=0, keepdims=True)
                m_new = bm if d == 0 else jnp.maximum(m[c], bm)
                pv = _dot(v_aug(chains[c][0], rows_of(c, d)), jnp.exp2(sn - m_new).astype(BF16))
                acc[c] = pv if d == 0 else acc[c] * jnp.exp2(m[c] - m_new) + pv
                m[c] = m_new
        return acc

    for jj in range(N_KBLK // 2):
        tiles = (N_KBLK - 1 - jj, jj)
        chains = [(pi, tile + 1, q_operand(tile, pi)) for pi in range(PAIR_GROUP) for tile in tiles]
        for (pi, nb, _), acc in zip(chains, run(chains)):
            store_out(pi, nb - 1, acc)


def _moba_schedule(q_operand, k_ref, pen_ref, bias_ref, v_aug, store_out):
    def new_chain(pi, nb):
        return {"pair": pi, "nb": nb, "q": q_operand(nb - 1, pi), "blocks": [], "shifts": [], "m": None}

    def score_block(ch, n):
        pi, nb = ch["pair"], ch["nb"]
        d = nb - 1 - n
        sn = _dot_nt(k_ref[n * Q_TILE:(n + 1) * Q_TILE, pi * LANES:(pi + 1) * LANES], ch["q"])
        shift = None
        if d < 2:
            sn = sn + bias_ref[pi, d]
        else:
            shift = bias_ref[pi, 2, 0:1, :]
        if d > 0:
            pen = pen_ref[0, nb - 1, pi * N_KBLK + d:pi * N_KBLK + d + 1, :]
            shift = pen if shift is None else shift + pen
        bm = jnp.max(sn, axis=0, keepdims=True)
        if shift is not None:
            bm = bm + shift
        ch["m"] = bm if ch["m"] is None else jnp.maximum(ch["m"], bm)
        ch["blocks"].append(sn)
        ch["shifts"].append(shift)

    def prob_block(ch, n):
        m, sh = ch["m"], ch["shifts"][n]
        return jnp.exp2(ch["blocks"][n] - (m if sh is None else m - sh)).astype(BF16)

    order = [(pi, nb) for nb in range(N_KBLK, 0, -1) for pi in range(PAIR_GROUP)]
    prev = None
    for item in order + [None]:
        cur = None if item is None else new_chain(*item)
        probs = []
        for n in range(max(cur["nb"] if cur else 0, prev["nb"] if prev else 0)):
            if cur is not None and n < cur["nb"]:
                score_block(cur, n)
            if prev is not None and n < prev["nb"]:
                probs.append(prob_block(prev, n))
        if prev is not None:
            nk = prev["nb"] * Q_TILE
            acc = _dot(v_aug(prev["pair"], slice(0, nk)), jnp.concatenate(probs, axis=0))
            store_out(prev["pair"], prev["nb"] - 1, acc)
        prev = cur


def _attention(kind, ins, in_specs, n_pairs, extra_scratch=()):
    width = PAIR_GROUP * LANES
    return pl.pallas_call(
        functools.partial(_attn_kernel, kind=kind),
        grid=(BATCH, n_pairs // PAIR_GROUP),
        in_specs=in_specs,
        out_specs=pl.BlockSpec((SEQ, width), lambda b, g: (b, g)),
        out_shape=jax.ShapeDtypeStruct((TOKENS, n_pairs * LANES), BF16),
        scratch_shapes=list(extra_scratch),
        compiler_params=pltpu.CompilerParams(
            dimension_semantics=("arbitrary", "arbitrary"), vmem_limit_bytes=VMEM_LIMIT),
        name="attn_" + kind,
    )(*ins)


def _seq_spec(pair_width, col0):
    return pl.BlockSpec((SEQ, PAIR_GROUP * pair_width), lambda b, g: (b, col0 + g))


def _vt_spec():
    return pl.BlockSpec((PAIR_GROUP * LANES, SEQ), lambda b, g: (g, b))


def _place_heads(w, n_heads, src_stride, src_off, width):
    per_head = w.reshape(w.shape[0], n_heads, src_stride)[:, :, src_off:src_off + width]
    per_head = jnp.pad(per_head, ((0, 0), (0, 0), (0, LANES - width)))
    return per_head.reshape(w.shape[0], n_heads * LANES)


def _row(v, width=None):
    v = v.reshape(1, -1).astype(F32)
    if width is not None and v.shape[1] < width:
        v = jnp.pad(v, ((0, 0), (0, width - v.shape[1])))
    return v


def kernel(x, p, t5_bias, ff1_norm, ff1_w_in, ff1_w_out, mix_norm, ff2_norm, ff2_w_in, ff2_w_out,
           ple_norm, ple_w_gate, ple_w_proj, ab_w_in, mla_q_norm, mla_w_uq, mla_kv_norm, mla_w_ukv,
           ab_w_out, fox_w_in, fox_b_f, fox_w_out, final_norm):
    xt = x.reshape(TOKENS, D_MODEL)

    ffw = {1: (ff1_norm, ff1_w_in, ff1_w_out), 2: (ff2_norm, ff2_w_in, ff2_w_out)}
    hbm = pl.BlockSpec(memory_space=pl.ANY)

    def ffn_args(which, layer):
        norm, w_in, w_out = ffw[which]
        return [_row(norm[layer]), w_in, w_out], [_wspec((1, D_MODEL)), hbm, hbm]

    ple_gate_bf = ple_w_gate.astype(BF16)
    ple_proj_bf = ple_w_proj.astype(BF16)

    def post(xin, mixes, w_outs, layer, final):
        fa, fs = ffn_args(2, layer)
        ws = [w.astype(BF16) for w in w_outs]
        ins = ([xin] + list(mixes) + ws + fa
               + [_row(ple_norm[layer]), ple_gate_bf, p.reshape(DEPTH, TOKENS, PLE_DIM), ple_proj_bf,
                  _row(final_norm)])
        specs = ([_tok_spec(D_MODEL)] + [_tok_spec(m.shape[1]) for m in mixes]
                 + [_wspec(w.shape) for w in ws] + fs
                 + [_wspec((1, D_MODEL)), _wspec((D_MODEL, D_MODEL), layer),
                    pl.BlockSpec((None, TOKEN_TILE, PLE_DIM), lambda t: (layer, t, 0)),
                    _wspec((PLE_DIM, D_MODEL), layer), _wspec((1, D_MODEL))])
        return _token_call(
            "post%d" % layer,
            functools.partial(_post_kernel, n_mix=len(mixes), final=final, layer=layer), ins, specs,
            jax.ShapeDtypeStruct((TOKENS, D_MODEL), F32), _tok_spec(D_MODEL))

    w_ab = ab_w_in[0]
    w_qk = w_ab[:, :2 * A_WIDTH].astype(BF16)
    w_vt = w_ab[:, 2 * A_WIDTH:3 * A_WIDTH].T.astype(BF16)
    c0 = 3 * A_WIDTH
    kr0 = MLA_Q_RANK + MLA_KV_RANK
    w_c = jnp.concatenate(
        [w_ab[:, c0:c0 + kr0], jnp.zeros((D_MODEL, MLA_NOPE), F32), w_ab[:, c0 + kr0:],
         jnp.zeros((D_MODEL, LANES - MLA_NOPE - MLA_ROPE), F32)], axis=1).astype(BF16)
    w_uq = _place_heads(mla_w_uq[0], B_HEADS, MLA_NOPE + MLA_ROPE, 0, MLA_NOPE + MLA_ROPE).astype(BF16)
    per_head = mla_w_uq[0].reshape(MLA_Q_RANK, B_HEADS, MLA_NOPE + MLA_ROPE)
    x1_cols = per_head[:, :, MLA_NOPE:MLA_NOPE + MLA_ROPE // 2]
    x2_cols = per_head[:, :, MLA_NOPE + MLA_ROPE // 2:]
    partner_cols = jnp.concatenate([jnp.zeros_like(per_head[:, :, :MLA_NOPE]), x2_cols, x1_cols], axis=2)
    w_uq_rot = _place_heads(partner_cols.reshape(MLA_Q_RANK, -1), B_HEADS, MLA_NOPE + MLA_ROPE, 0,
                            MLA_NOPE + MLA_ROPE).astype(BF16)
    w_ukv_k = _place_heads(mla_w_ukv[0], B_HEADS, MLA_NOPE + MLA_V, 0, MLA_NOPE).astype(BF16)
    w_ukv = mla_w_ukv[0].reshape(MLA_KV_RANK, B_HEADS, MLA_NOPE + MLA_V)
    w_ukv_vt = w_ukv[:, :, MLA_NOPE:].reshape(MLA_KV_RANK, B_HEADS * MLA_V).T.astype(BF16)
    half = MLA_ROPE // 2
    inv = ROPE_THETA ** (-np.arange(half, dtype=np.float64) / half)
    inv_lane = np.zeros((1, LANES), np.float32)
    inv_lane[0, MLA_NOPE:MLA_NOPE + half] = inv
    inv_lane[0, MLA_NOPE + half:MLA_NOPE + MLA_ROPE] = inv
    rope_tbl = pl.pallas_call(
        _rope_table_kernel,
        grid=(SEQ // TOKEN_TILE,),
        in_specs=[pl.BlockSpec((1, LANES), lambda i: (0, 0))],
        out_specs=pl.BlockSpec((4, TOKEN_TILE, LANES), lambda i: (0, i, 0)),
        out_shape=jax.ShapeDtypeStruct((4, SEQ, LANES), F32),
        name="rope_tables",
    )(jnp.asarray(inv_lane))

    fa, fs = ffn_args(1, 0)
    ins = ([xt] + fa + [_row(mix_norm[0]), w_qk, w_vt, w_c, _row(mla_q_norm[0]), w_uq, w_uq_rot,
                        _row(mla_kv_norm[0]), w_ukv_k, w_ukv_vt, rope_tbl])
    specs = ([_tok_spec(D_MODEL)] + fs
             + [_wspec((1, D_MODEL)), _wspec(w_qk.shape), _wspec(w_vt.shape), _wspec(w_c.shape),
                _wspec((1, MLA_Q_RANK)), _wspec(w_uq.shape), _wspec(w_uq_rot.shape),
                _wspec((1, MLA_KV_RANK)),
                _wspec(w_ukv_k.shape), _wspec(w_ukv_vt.shape),
                pl.BlockSpec((4, TOKEN_TILE, LANES), lambda t: (0, t % (SEQ // TOKEN_TILE), 0))])
    pen_rows = A_HEADS // 2 * N_KBLK
    outs = (jax.ShapeDtypeStruct((TOKENS, D_MODEL), F32),
            jax.ShapeDtypeStruct((TOKENS, 2 * A_WIDTH), BF16),
            jax.ShapeDtypeStruct((A_WIDTH, TOKENS), BF16),
            jax.ShapeDtypeStruct((BATCH, N_KBLK, pen_rows, 2 * Q_TILE), F32),
            jax.ShapeDtypeStruct((TOKENS, B_HEADS * LANES), BF16),
            jax.ShapeDtypeStruct((TOKENS, B_HEADS * LANES), BF16),
            jax.ShapeDtypeStruct((B_HEADS * MLA_V, TOKENS), BF16))
    out_specs = (_tok_spec(D_MODEL), _tok_spec(2 * A_WIDTH), _tok_t_spec(A_WIDTH),
                 pl.BlockSpec((1, TILE_R, pen_rows, 2 * Q_TILE),
                              lambda t: (t // (N_KBLK // TILE_R), t % (N_KBLK // TILE_R), 0, 0)),
                 _tok_spec(B_HEADS * LANES), _tok_spec(B_HEADS * LANES),
                 _tok_t_spec(B_HEADS * MLA_V))
    x1, qk_a, vt_a, pen, q_mla, k_mla, vt_mla = _token_call(
        "pre0", functools.partial(_pre0_kernel, layer=0), ins, specs, outs, out_specs,
        scratch=[pltpu.VMEM((N_KBLK, A_WIDTH), F32)])

    bias = pl.pallas_call(
        _t5_bias_kernel,
        grid=(A_HEADS // 2,),
        in_specs=[pl.BlockSpec(memory_space=pltpu.SMEM)],
        out_specs=pl.BlockSpec((1, 3, Q_TILE, 2 * Q_TILE), lambda h: (h, 0, 0, 0)),
        out_shape=jax.ShapeDtypeStruct((A_HEADS // 2, 3, Q_TILE, 2 * Q_TILE), F32),
        name="t5_bias_tiles",
    )(t5_bias.astype(F32))

    na = A_HEADS // 2
    o_a = _attention(
        "moba", [qk_a, qk_a, vt_a, pen, bias],
        [_seq_spec(LANES, 0), _seq_spec(LANES, na // PAIR_GROUP), _vt_spec(),
         pl.BlockSpec((1, N_KBLK, PAIR_GROUP * N_KBLK, 2 * Q_TILE), lambda b, g: (b, 0, g, 0)),
         pl.BlockSpec((PAIR_GROUP, 3, Q_TILE, 2 * Q_TILE), lambda b, g: (g, 0, 0, 0))], na)
    o_b = _attention(
        "mla", [q_mla, k_mla, vt_mla],
        [_seq_spec(2 * LANES, 0), _seq_spec(2 * LANES, 0), _vt_spec()], B_HEADS // 2)
    w_o = ab_w_out[0]
    xt = post(x1, [o_a, o_b], [w_o[:A_WIDTH], w_o[A_WIDTH:]], 0, DEPTH == 1)

    w_fox = fox_w_in[0]
    w_qk = w_fox[:, :2 * C_WIDTH].astype(BF16)
    w_vt = w_fox[:, 2 * C_WIDTH:3 * C_WIDTH].T.astype(BF16)
    w_f = jnp.pad(w_fox[:, 3 * C_WIDTH:], ((0, 0), (0, LANES - C_HEADS))).astype(BF16)
    fa, fs = ffn_args(1, 1)
    ins = [xt] + fa + [_row(mix_norm[1]), w_qk, w_vt, w_f, _row(fox_b_f[0], LANES)]
    specs = ([_tok_spec(D_MODEL)] + fs
             + [_wspec((1, D_MODEL)), _wspec(w_qk.shape), _wspec(w_vt.shape), _wspec(w_f.shape),
                _wspec((1, LANES))])
    outs = (jax.ShapeDtypeStruct((TOKENS, D_MODEL), F32),
            jax.ShapeDtypeStruct((TOKENS, 2 * C_WIDTH), BF16),
            jax.ShapeDtypeStruct((C_WIDTH, TOKENS), BF16),
            jax.ShapeDtypeStruct((TOKENS, LANES), BF16))
    out_specs = (_tok_spec(D_MODEL), _tok_spec(2 * C_WIDTH), _tok_t_spec(C_WIDTH), _tok_spec(LANES))
    x1, qk_c, vt_c, caug = _token_call("pre1", functools.partial(_pre1_kernel, layer=1), ins, specs, outs, out_specs,
                                       scratch=[pltpu.VMEM((1, LANES), F32)])

    nc = C_HEADS // 2
    o_c = _attention(
        "fox", [qk_c, qk_c, vt_c, caug],
        [_seq_spec(LANES, 0), _seq_spec(LANES, nc // PAIR_GROUP), _vt_spec(),
         pl.BlockSpec((SEQ, LANES), lambda b, g: (b, 0))],
        nc, extra_scratch=[pltpu.VMEM((SEQ, PAIR_GROUP * 2 * LANES), BF16)])
    xt = post(x1, [o_c], [fox_w_out[0]], 1, True)
    return xt.reshape(BATCH, SEQ, D_MODEL)
```

```python
import functools
import math

import jax
import jax.numpy as jnp
import numpy as np
from jax import lax
from jax.experimental import pallas as pl
from jax.experimental.pallas import tpu as pltpu

F32 = jnp.float32
BF16 = jnp.bfloat16

D_MODEL = 1024
BATCH = 8
SEQ = 2048
DEPTH = 2
PLE_DIM = 256
D_FF = 2816
EPS = 1e-6

A_HEADS = 8
A_HEAD_DIM = 64
MOBA_BLOCK = 256
MOBA_TOPK = 3

B_HEADS = 8
MLA_Q_RANK = 256
MLA_KV_RANK = 128
MLA_NOPE = 64
MLA_ROPE = 32
MLA_V = 64
ROPE_THETA = 10000.0

T5_BUCKETS = 32
T5_MAX_DIST = 128

C_HEADS = 16
C_HEAD_DIM = 64

A_WIDTH = A_HEADS * A_HEAD_DIM
C_WIDTH = C_HEADS * C_HEAD_DIM

TOKENS = BATCH * SEQ
LANES = 128
BF16_ROWS = 16
PAIR_GROUP = 2
Q_TILE = MOBA_BLOCK
N_KBLK = SEQ // Q_TILE
TOKEN_TILE = 2 * Q_TILE
TILE_R = TOKEN_TILE // Q_TILE
FF_CHUNK = D_FF // 2
W_IN_CHUNK = 512
W_OUT_CHUNK = 256
VMEM_LIMIT = 56 * 1024 * 1024
NEG_INF = float("-inf")
LOG2E = math.log2(math.e)
QSCALE_64 = A_HEAD_DIM ** -0.5 * LOG2E
QSCALE_MLA = (MLA_NOPE + MLA_ROPE) ** -0.5 * LOG2E


def _wspec(shape, layer=None):
    nd = len(shape)
    if layer is None:
        return pl.BlockSpec(shape, lambda *_: (0,) * nd, pipeline_mode=pl.Buffered(1))
    return pl.BlockSpec((None,) + tuple(shape), lambda *_: (layer,) + (0,) * nd,
                        pipeline_mode=pl.Buffered(1))


def _dot(a, b):
    return jnp.dot(a, b, preferred_element_type=F32)


def _dot_nt(a, b):
    return lax.dot_general(a, b, (((1,), (1,)), ((), ())), preferred_element_type=F32)


def _rms(x, g):
    return x * lax.rsqrt(jnp.mean(x * x, axis=-1, keepdims=True) + EPS) * g


def _ffn_scratch():
    return [pltpu.VMEM((D_MODEL, 2 * D_FF), BF16), pltpu.VMEM((D_FF, D_MODEL), BF16),
            pltpu.VMEM((2, D_MODEL, W_IN_CHUNK), F32), pltpu.VMEM((2, W_OUT_CHUNK, D_MODEL), F32),
            pltpu.SemaphoreType.DMA((2, 2))]


def _load_ffn_weights(layer, win_hbm, wo_hbm, win_ref, wo_ref, stage_in, stage_out, sem):
    def copy_in(c, slot):
        return pltpu.make_async_copy(win_hbm.at[layer, :, pl.ds(c * W_IN_CHUNK, W_IN_CHUNK)],
                                     stage_in.at[slot], sem.at[0, slot])

    def copy_out(c, slot):
        return pltpu.make_async_copy(wo_hbm.at[layer, pl.ds(c * W_OUT_CHUNK, W_OUT_CHUNK), :],
                                     stage_out.at[slot], sem.at[1, slot])

    @pl.when(pl.program_id(0) == 0)
    def _():
        n_in = 2 * D_FF // W_IN_CHUNK
        n_out = D_FF // W_OUT_CHUNK
        copy_in(0, 0).start()
        copy_out(0, 0).start()
        for c in range(n_in):
            slot = c % 2
            if c + 1 < n_in:
                copy_in(c + 1, 1 - slot).start()
            copy_in(c, slot).wait()
            win_ref[:, c * W_IN_CHUNK:(c + 1) * W_IN_CHUNK] = stage_in[slot].astype(BF16)
        for c in range(n_out):
            slot = c % 2
            if c + 1 < n_out:
                copy_out(c + 1, 1 - slot).start()
            copy_out(c, slot).wait()
            wo_ref[c * W_OUT_CHUNK:(c + 1) * W_OUT_CHUNK, :] = stage_out[slot].astype(BF16)


def _ffn(x, g, win_ref, wo_ref):
    h = _rms(x, g).astype(BF16)
    acts = []
    for c in range(D_FF // FF_CHUNK):
        lo, hi = c * FF_CHUNK, (c + 1) * FF_CHUNK
        a = _dot(h, win_ref[:, lo:hi])
        u = _dot(h, win_ref[:, D_FF + lo:D_FF + hi])
        acts.append((a * jax.nn.sigmoid(a) * u).astype(BF16))
    return x + 0.5 * _dot(jnp.concatenate(acts, axis=1), wo_ref[...])


def _rope_table_kernel(inv_ref, o_ref):
    pos0 = pl.program_id(0) * TOKEN_TILE
    pos = (pos0 + lax.broadcasted_iota(jnp.int32, (TOKEN_TILE, LANES), 0)).astype(F32)
    lane = lax.broadcasted_iota(jnp.int32, (TOKEN_TILE, LANES), 1)
    ang = pos * inv_ref[...]
    is_x1 = (lane >= MLA_NOPE) & (lane < MLA_NOPE + MLA_ROPE // 2)
    is_x2 = (lane >= MLA_NOPE + MLA_ROPE // 2) & (lane < MLA_NOPE + MLA_ROPE)
    cos_t = jnp.where(is_x1 | is_x2, jnp.cos(ang), 1.0)
    sin = jnp.sin(ang)
    sin_t = jnp.where(is_x1, -sin, jnp.where(is_x2, sin, 0.0))
    o_ref[0] = cos_t
    o_ref[1] = sin_t
    o_ref[2] = cos_t * QSCALE_MLA
    o_ref[3] = sin_t * QSCALE_MLA


def _rope_block(xb, cos_t, sin_t):
    half = MLA_ROPE // 2
    lane = lax.broadcasted_iota(jnp.int32, xb.shape, 1)
    is_x1 = lane < MLA_NOPE + half
    partner = jnp.where(is_x1, pltpu.roll(xb, LANES - half, 1), pltpu.roll(xb, half, 1))
    return xb * cos_t + partner * sin_t


def _pre0_kernel(x_ref, g1_ref, win_hbm, wo_hbm, gmix_ref, wqk_ref, wvt_ref, wc_ref,
                 qn_ref, wuq_ref, wuqrot_ref, kvn_ref, wukvk_ref, wukvvt_ref, rope_ref,
                 x1_ref, qk_ref, vt_ref, pen_ref, qm_ref, kmla_ref, vmt_ref, km_s, *ffn_s, layer):
    t = pl.program_id(0)
    _load_ffn_weights(layer, win_hbm, wo_hbm, *ffn_s)
    x1 = _ffn(x_ref[...], g1_ref[...], ffn_s[0], ffn_s[1])
    x1_ref[...] = x1
    h = _rms(x1, gmix_ref[...]).astype(BF16)
    qk = _dot(h, wqk_ref[...])
    q = qk[:, :A_WIDTH]
    k = qk[:, A_WIDTH:]
    qk_ref[:, :A_WIDTH] = (q * QSCALE_64).astype(BF16)
    qk_ref[:, A_WIDTH:] = k.astype(BF16)

    @pl.when(t == 0)
    def _():
        km_s[...] = jnp.zeros_like(km_s)

    c = _dot(h, wc_ref[...])
    gates = []
    for r in range(TILE_R):
        own = (t * TILE_R + r) % N_KBLK
        part = slice(r * Q_TILE, (r + 1) * Q_TILE)
        km_s[pl.ds(own, 1), :] = jnp.mean(k[part], axis=0, keepdims=True)
        gates.append((own, _moba_gate(q[part], km_s[...])))
    vt_ref[...] = _dot_nt(wvt_ref[...], h).astype(BF16)
    for r, (own, gate) in enumerate(gates):
        pen_ref[0, r] = _moba_select(gate, own)

    cq = c[:, :MLA_Q_RANK]
    ckv = c[:, MLA_Q_RANK:MLA_Q_RANK + MLA_KV_RANK]
    kr = c[:, MLA_Q_RANK + MLA_KV_RANK:]
    cqn = _rms(cq, qn_ref[...]).astype(BF16)
    ckvn = _rms(ckv, kvn_ref[...]).astype(BF16)
    qm = _dot(cqn, wuq_ref[...])
    qm_partner = _dot(cqn, wuqrot_ref[...])
    kn = _dot(ckvn, wukvk_ref[...])
    vmt_ref[...] = _dot_nt(wukvvt_ref[...], ckvn).astype(BF16)

    krr = _rope_block(kr, rope_ref[0], rope_ref[1])
    for hb in range(B_HEADS):
        sl = slice(hb * LANES, (hb + 1) * LANES)
        qm_ref[:, sl] = (qm[:, sl] * rope_ref[2] + qm_partner[:, sl] * rope_ref[3]).astype(BF16)
        kmla_ref[:, sl] = (kn[:, sl] + krr).astype(BF16)


def _pre1_kernel(x_ref, g1_ref, win_hbm, wo_hbm, gmix_ref, wqk_ref, wvt_ref, wf_ref, bf_ref,
                 x1_ref, qk_ref, vt_ref, caug_ref, carry_s, *ffn_s, layer):
    t = pl.program_id(0)
    _load_ffn_weights(layer, win_hbm, wo_hbm, *ffn_s)
    x1 = _ffn(x_ref[...], g1_ref[...], ffn_s[0], ffn_s[1])
    x1_ref[...] = x1
    h = _rms(x1, gmix_ref[...]).astype(BF16)

    @pl.when(t % (N_KBLK // TILE_R) == 0)
    def _():
        carry_s[...] = jnp.zeros_like(carry_s)

    z = _dot(h, wf_ref[...]) + bf_ref[...]
    qk = _dot(h, wqk_ref[...])
    qk_ref[:, :C_WIDTH] = (qk[:, :C_WIDTH] * QSCALE_64).astype(BF16)
    qk_ref[:, C_WIDTH:] = qk[:, C_WIDTH:].astype(BF16)
    cums = [_fox_cumsum(z[r * Q_TILE:(r + 1) * Q_TILE], carry_s) for r in range(TILE_R)]
    vt_ref[...] = _dot_nt(wvt_ref[...], h).astype(BF16)
    for r, cum in enumerate(cums):
        caug_ref[r * Q_TILE:(r + 1) * Q_TILE, :] = _fox_decay_parts(cum)


def _tok_spec(width):
    return pl.BlockSpec((TOKEN_TILE, width), lambda i: (i, 0))


def _tok_t_spec(height):
    return pl.BlockSpec((height, TOKEN_TILE), lambda i: (0, i))


def _token_call(name, body, ins, in_specs, outs, out_specs, scratch=()):
    return pl.pallas_call(
        body,
        grid=(TOKENS // TOKEN_TILE,),
        in_specs=in_specs,
        out_specs=out_specs,
        out_shape=outs,
        scratch_shapes=list(scratch) + _ffn_scratch(),
        compiler_params=pltpu.CompilerParams(
            dimension_semantics=("arbitrary",), vmem_limit_bytes=VMEM_LIMIT),
        name=name,
    )(*ins)


def _post_kernel(*refs, n_mix, final, layer):
    x_ref = refs[0]
    o_refs = refs[1:1 + n_mix]
    w_refs = refs[1 + n_mix:1 + 2 * n_mix]
    (g2_ref, win_hbm, wo_hbm, gple_ref, wg_ref, p_ref, wp_ref, gfin_ref,
     out_ref) = refs[1 + 2 * n_mix:10 + 2 * n_mix]
    ffn_s = refs[10 + 2 * n_mix:]
    _load_ffn_weights(layer, win_hbm, wo_hbm, *ffn_s)
    x = x_ref[...]
    for o_ref, w_ref in zip(o_refs, w_refs):
        x = x + _dot(o_ref[...], w_ref[...])
    x = _ffn(x, g2_ref[...], ffn_s[0], ffn_s[1])
    gate = jax.nn.sigmoid(_dot(_rms(x, gple_ref[...]).astype(BF16), wg_ref[...]))
    x = x + gate * _dot(p_ref[...].astype(BF16), wp_ref[...])
    if final:
        x = _rms(x, gfin_ref[...])
    out_ref[...] = x


def _split_bf16(x):
    hi = x.astype(BF16)
    lo = (x - hi.astype(F32)).astype(BF16)
    return hi, lo


def _moba_gate(q, km):
    rows = A_HEADS * N_KBLK
    gt = jnp.concatenate([km] * A_HEADS, axis=0)
    r = lax.broadcasted_iota(jnp.int32, (rows, A_WIDTH), 0)
    c = lax.broadcasted_iota(jnp.int32, (rows, A_WIDTH), 1)
    gt = jnp.where((r // N_KBLK) == (c // A_HEAD_DIM), gt, 0.0)
    g_hi, g_lo = _split_bf16(gt)
    q_hi, q_lo = _split_bf16(q)
    return _dot_nt(g_hi, q_hi) + _dot_nt(g_hi, q_lo) + _dot_nt(g_lo, q_hi)


def _moba_select(gate, own):
    n_idx = lax.broadcasted_iota(jnp.int32, (N_KBLK, Q_TILE), 0)
    pen_rows = []
    for h in range(A_HEADS):
        gh = gate[h * N_KBLK:(h + 1) * N_KBLK]
        rank = jnp.zeros((N_KBLK, Q_TILE), jnp.int32)
        for m in range(N_KBLK):
            gm = gh[m:m + 1]
            beats = (gm > gh) | ((gm == gh) & (m < n_idx))
            rank = rank + jnp.where(beats & (m < own), 1, 0)
        sel = (n_idx < own) & (rank < MOBA_TOPK)
        pen_n = jnp.where(sel, 0.0, NEG_INF)
        pen_d = jnp.full((N_KBLK, Q_TILE), NEG_INF, F32)
        for n in range(N_KBLK):
            pen_d = jnp.where(n_idx == own - n, pen_n[n:n + 1], pen_d)
        pen_rows.append(pen_d)
    pairs = [jnp.concatenate(pen_rows[2 * hp:2 * hp + 2], axis=1) for hp in range(A_HEADS // 2)]
    return jnp.concatenate(pairs, axis=0)


def _t5_bias_kernel(tbl_ref, o_ref):
    hp = pl.program_id(0)
    r = lax.broadcasted_iota(jnp.int32, (Q_TILE, Q_TILE), 0)
    c = lax.broadcasted_iota(jnp.int32, (Q_TILE, Q_TILE), 1)
    max_exact = T5_BUCKETS // 2
    for dd in range(3):
        dist = dd * Q_TILE + c - r
        dc = jnp.maximum(dist, 0)
        df = jnp.maximum(dc.astype(F32), 1.0)
        large = max_exact + (jnp.log(df / max_exact) / math.log(T5_MAX_DIST / max_exact)
                             * (T5_BUCKETS - max_exact)).astype(jnp.int32)
        large = jnp.minimum(large, T5_BUCKETS - 1)
        bucket = jnp.where(dc < max_exact, dc, large)
        for hh in range(2):
            bias = jnp.zeros((Q_TILE, Q_TILE), F32)
            for b in range(T5_BUCKETS):
                bias = jnp.where(bucket == b, tbl_ref[b, 2 * hp + hh], bias)
            bias = bias * LOG2E
            if dd == 0:
                bias = jnp.where(dist >= 0, bias, NEG_INF)
            o_ref[0, dd, :, hh * Q_TILE:(hh + 1) * Q_TILE] = bias


def _fox_cumsum(z, carry_ref):
    lane = lax.broadcasted_iota(jnp.int32, (Q_TILE, LANES), 1)
    logf = jnp.minimum(z, 0.0) - jnp.log1p(jnp.exp(-jnp.abs(z)))
    logf = jnp.where(lane < C_HEADS, logf, 0.0)
    r = lax.broadcasted_iota(jnp.int32, (Q_TILE, Q_TILE), 0)
    c = lax.broadcasted_iota(jnp.int32, (Q_TILE, Q_TILE), 1)
    tri = jnp.where(c <= r, 1.0, 0.0).astype(BF16)
    l1 = logf.astype(BF16)
    rem = logf - l1.astype(F32)
    l2 = rem.astype(BF16)
    l3 = (rem - l2.astype(F32)).astype(BF16)
    cum = _dot(tri, l1) + _dot(tri, l2) + _dot(tri, l3) + carry_ref[...]
    carry_ref[...] = cum[Q_TILE - 1:Q_TILE, :]
    return cum


def _fox_decay_parts(cum):
    cs = cum * LOG2E
    c1 = cs.astype(BF16)
    rem = cs - c1.astype(F32)
    c2 = rem.astype(BF16)
    c3 = (rem - c2.astype(F32)).astype(BF16)
    rin = lax.broadcasted_iota(jnp.int32, (LANES, LANES), 0)
    lout = lax.broadcasted_iota(jnp.int32, (LANES, LANES), 1)
    out = jnp.zeros((Q_TILE, LANES), F32)
    for part, cp in enumerate((c1, c2, c3)):
        place = jnp.where((rin < C_HEADS) & (lout == rin * 8 + part), 1.0, 0.0).astype(BF16)
        out = out + _dot(cp, place)
    return out.astype(BF16)


def _attn_kernel(*refs, kind):
    if kind == "moba":
        q_ref, k_ref, vt_ref, pen_ref, bias_ref, o_ref = refs
    elif kind == "mla":
        q_ref, k_ref, vt_ref, o_ref = refs
    else:
        q_ref, k_ref, vt_ref, c_ref, o_ref, kp_s = refs
    grp = pl.program_id(1)
    lane = lax.broadcasted_iota(jnp.int32, (1, LANES), 1)
    low = lane < A_HEAD_DIM
    kw = LANES if kind == "moba" else 2 * LANES
    qw = 2 * LANES if kind == "mla" else LANES
    k_cat = kp_s if kind == "fox" else k_ref

    if kind == "fox":
        rin = lax.broadcasted_iota(jnp.int32, (LANES, LANES), 0)
        lout = lax.broadcasted_iota(jnp.int32, (LANES, LANES), 1)

        def place(pair, off_a, off_b, val):
            base_a = 2 * pair * 8
            base_b = base_a + 8
            sel_a = (lout >= off_a) & (lout < off_a + 3) & (rin == base_a + lout - off_a)
            sel_b = (lout >= off_b) & (lout < off_b + 3) & (rin == base_b + lout - off_b)
            return jnp.where(sel_a | sel_b, val, 0.0).astype(BF16)

        def ones(off_a, off_b):
            in_a = (lane >= off_a) & (lane < off_a + 3)
            in_b = (lane >= off_b) & (lane < off_b + 3)
            return jnp.where(in_a | in_b, 1.0, 0.0)

        aq_all = []
        for pi in range(PAIR_GROUP):
            ak = (_dot(c_ref[...], place(grp * PAIR_GROUP + pi, A_HEAD_DIM + 3, 3, -1.0))
                  + ones(A_HEAD_DIM, 0)).astype(BF16)
            k = k_ref[:, pi * LANES:(pi + 1) * LANES]
            kp_s[:, pi * kw:pi * kw + LANES] = jnp.where(low, k, ak)
            kp_s[:, pi * kw + LANES:(pi + 1) * kw] = jnp.where(low, ak, k)
            aq_all.append((_dot(c_ref[...], place(grp * PAIR_GROUP + pi, A_HEAD_DIM, 0, 1.0))
                           + ones(A_HEAD_DIM + 3, 3)).astype(BF16))

    def q_operand(tile, pi):
        zero = jnp.zeros((Q_TILE, LANES), BF16)
        rows = slice(tile * Q_TILE, (tile + 1) * Q_TILE)
        q = q_ref[rows, pi * qw:(pi + 1) * qw]
        if kind == "moba":
            return jnp.concatenate([jnp.where(low, q, zero), jnp.where(low, zero, q)], axis=0)
        if kind == "fox":
            aq = aq_all[pi][rows]
            qa, qb = jnp.where(low, q, aq), jnp.where(low, aq, q)
        else:
            qa, qb = q[:, :LANES], q[:, LANES:]
        return jnp.concatenate([jnp.concatenate([qa, zero], axis=1),
                                jnp.concatenate([zero, qb], axis=1)], axis=0)

    def store_out(pi, tile, accs):
        ot = jnp.concatenate([a[:A_HEAD_DIM] / a[A_HEAD_DIM:A_HEAD_DIM + 1] for a in accs], axis=0)
        o_ref[tile * Q_TILE:(tile + 1) * Q_TILE, pi * LANES:(pi + 1) * LANES] = ot.T.astype(BF16)

    def pv(pi, rows, p):
        ones_rows = jnp.ones((BF16_ROWS, rows.stop - rows.start), BF16)
        outs = []
        for hh in range(2):
            lo = pi * LANES + hh * A_HEAD_DIM
            v_aug = jnp.concatenate([vt_ref[lo:lo + A_HEAD_DIM, rows], ones_rows], axis=0)
            outs.append(_dot(v_aug, p[:, hh * Q_TILE:(hh + 1) * Q_TILE]))
        return outs

    if kind == "moba":
        _moba_schedule(q_operand, k_ref, pen_ref, bias_ref, pv, store_out)
    else:
        _online_schedule(q_operand, k_cat, kw, pv, store_out)


def _online_schedule(q_operand, k_cat, kw, pv, store_out):
    key = lax.broadcasted_iota(jnp.int32, (Q_TILE, 2 * Q_TILE), 0)
    qry = lax.broadcasted_iota(jnp.int32, (Q_TILE, 2 * Q_TILE), 1) % Q_TILE

    def run(chains):
        n = len(chains)

        def rows_of(c, d):
            nb = chains[c][1]
            return slice((nb - 1 - d) * Q_TILE, (nb - d) * Q_TILE)

        def qk(c, d):
            pi = chains[c][0]
            return _dot_nt(k_cat[rows_of(c, d), pi * kw:(pi + 1) * kw], chains[c][2])

        s_next = [qk(c, 0) for c in range(n)]
        m = [None] * n
        acc = [None] * n
        for d in range(max(c[1] for c in chains)):
            live = [c for c in range(n) if d < chains[c][1]]
            s_cur = list(s_next)
            for c in live:
                if d + 1 < chains[c][1]:
                    s_next[c] = qk(c, d + 1)
            for c in live:
                sn = s_cur[c]
                if d == 0:
                    sn = jnp.where(key <= qry, sn, NEG_INF)
                bm = jnp.max(sn, axis=0, keepdims=True)
                m_new = bm if d == 0 else jnp.maximum(m[c], bm)
                new = pv(chains[c][0], rows_of(c, d), jnp.exp2(sn - m_new).astype(BF16))
                if d > 0:
                    alpha = jnp.exp2(m[c] - m_new)
                    new = [a * alpha[:, hh * Q_TILE:(hh + 1) * Q_TILE] + n
                           for hh, (a, n) in enumerate(zip(acc[c], new))]
                acc[c] = new
                m[c] = m_new
        return acc

    for jj in range(N_KBLK // 2):
        tiles = (N_KBLK - 1 - jj, jj)
        chains = [(pi, tile + 1, q_operand(tile, pi)) for pi in range(PAIR_GROUP) for tile in tiles]
        for (pi, nb, _), acc in zip(chains, run(chains)):
            store_out(pi, nb - 1, acc)


def _moba_schedule(q_operand, k_ref, pen_ref, bias_ref, pv, store_out):
    def new_chain(pi, nb):
        return {"pair": pi, "nb": nb, "q": q_operand(nb - 1, pi), "blocks": [], "shifts": [], "m": None}

    def score_block(ch, n):
        pi, nb = ch["pair"], ch["nb"]
        d = nb - 1 - n
        sn = _dot_nt(k_ref[n * Q_TILE:(n + 1) * Q_TILE, pi * LANES:(pi + 1) * LANES], ch["q"])
        shift = None
        if d < 2:
            sn = sn + bias_ref[pi, d]
        else:
            shift = bias_ref[pi, 2, 0:1, :]
        if d > 0:
            pen = pen_ref[0, nb - 1, pi * N_KBLK + d:pi * N_KBLK + d + 1, :]
            shift = pen if shift is None else shift + pen
        bm = jnp.max(sn, axis=0, keepdims=True)
        if shift is not None:
            bm = bm + shift
        ch["m"] = bm if ch["m"] is None else jnp.maximum(ch["m"], bm)
        ch["blocks"].append(sn)
        ch["shifts"].append(shift)

    def prob_block(ch, n):
        m, sh = ch["m"], ch["shifts"][n]
        return jnp.exp2(ch["blocks"][n] - (m if sh is None else m - sh)).astype(BF16)

    order = [(pi, nb) for nb in range(N_KBLK, 0, -1) for pi in range(PAIR_GROUP)]
    prev = None
    for item in order + [None]:
        cur = None if item is None else new_chain(*item)
        probs = []
        for n in range(max(cur["nb"] if cur else 0, prev["nb"] if prev else 0)):
            if cur is not None and n < cur["nb"]:
                score_block(cur, n)
            if prev is not None and n < prev["nb"]:
                probs.append(prob_block(prev, n))
        if prev is not None:
            nk = prev["nb"] * Q_TILE
            accs = pv(prev["pair"], slice(0, nk), jnp.concatenate(probs, axis=0))
            store_out(prev["pair"], prev["nb"] - 1, accs)
        prev = cur


def _attention(kind, ins, in_specs, n_pairs, extra_scratch=()):
    width = PAIR_GROUP * LANES
    return pl.pallas_call(
        functools.partial(_attn_kernel, kind=kind),
        grid=(BATCH, n_pairs // PAIR_GROUP),
        in_specs=in_specs,
        out_specs=pl.BlockSpec((SEQ, width), lambda b, g: (b, g)),
        out_shape=jax.ShapeDtypeStruct((TOKENS, n_pairs * LANES), BF16),
        scratch_shapes=list(extra_scratch),
        compiler_params=pltpu.CompilerParams(
            dimension_semantics=("arbitrary", "arbitrary"), vmem_limit_bytes=VMEM_LIMIT),
        name="attn_" + kind,
    )(*ins)


def _seq_spec(pair_width, col0):
    return pl.BlockSpec((SEQ, PAIR_GROUP * pair_width), lambda b, g: (b, col0 + g))


def _vt_spec():
    return pl.BlockSpec((PAIR_GROUP * LANES, SEQ), lambda b, g: (g, b))


def _place_heads(w, n_heads, src_stride, src_off, width):
    per_head = w.reshape(w.shape[0], n_heads, src_stride)[:, :, src_off:src_off + width]
    per_head = jnp.pad(per_head, ((0, 0), (0, 0), (0, LANES - width)))
    return per_head.reshape(w.shape[0], n_heads * LANES)


def _row(v, width=None):
    v = v.reshape(1, -1).astype(F32)
    if width is not None and v.shape[1] < width:
        v = jnp.pad(v, ((0, 0), (0, width - v.shape[1])))
    return v


def kernel(x, p, t5_bias, ff1_norm, ff1_w_in, ff1_w_out, mix_norm, ff2_norm, ff2_w_in, ff2_w_out,
           ple_norm, ple_w_gate, ple_w_proj, ab_w_in, mla_q_norm, mla_w_uq, mla_kv_norm, mla_w_ukv,
           ab_w_out, fox_w_in, fox_b_f, fox_w_out, final_norm):
    xt = x.reshape(TOKENS, D_MODEL)

    ffw = {1: (ff1_norm, ff1_w_in, ff1_w_out), 2: (ff2_norm, ff2_w_in, ff2_w_out)}
    hbm = pl.BlockSpec(memory_space=pl.ANY)

    def ffn_args(which, layer):
        norm, w_in, w_out = ffw[which]
        return [_row(norm[layer]), w_in, w_out], [_wspec((1, D_MODEL)), hbm, hbm]

    ple_gate_bf = ple_w_gate.astype(BF16)
    ple_proj_bf = ple_w_proj.astype(BF16)

    def post(xin, mixes, w_outs, layer, final):
        fa, fs = ffn_args(2, layer)
        ws = [w.astype(BF16) for w in w_outs]
        ins = ([xin] + list(mixes) + ws + fa
               + [_row(ple_norm[layer]), ple_gate_bf, p.reshape(DEPTH, TOKENS, PLE_DIM), ple_proj_bf,
                  _row(final_norm)])
        specs = ([_tok_spec(D_MODEL)] + [_tok_spec(m.shape[1]) for m in mixes]
                 + [_wspec(w.shape) for w in ws] + fs
                 + [_wspec((1, D_MODEL)), _wspec((D_MODEL, D_MODEL), layer),
                    pl.BlockSpec((None, TOKEN_TILE, PLE_DIM), lambda t: (layer, t, 0)),
                    _wspec((PLE_DIM, D_MODEL), layer), _wspec((1, D_MODEL))])
        return _token_call(
            "post%d" % layer,
            functools.partial(_post_kernel, n_mix=len(mixes), final=final, layer=layer), ins, specs,
            jax.ShapeDtypeStruct((TOKENS, D_MODEL), F32), _tok_spec(D_MODEL))

    w_ab = ab_w_in[0]
    w_qk = w_ab[:, :2 * A_WIDTH].astype(BF16)
    w_vt = w_ab[:, 2 * A_WIDTH:3 * A_WIDTH].T.astype(BF16)
    c0 = 3 * A_WIDTH
    kr0 = MLA_Q_RANK + MLA_KV_RANK
    w_c = jnp.concatenate(
        [w_ab[:, c0:c0 + kr0], jnp.zeros((D_MODEL, MLA_NOPE), F32), w_ab[:, c0 + kr0:],
         jnp.zeros((D_MODEL, LANES - MLA_NOPE - MLA_ROPE), F32)], axis=1).astype(BF16)
    w_uq = _place_heads(mla_w_uq[0], B_HEADS, MLA_NOPE + MLA_ROPE, 0, MLA_NOPE + MLA_ROPE).astype(BF16)
    per_head = mla_w_uq[0].reshape(MLA_Q_RANK, B_HEADS, MLA_NOPE + MLA_ROPE)
    x1_cols = per_head[:, :, MLA_NOPE:MLA_NOPE + MLA_ROPE // 2]
    x2_cols = per_head[:, :, MLA_NOPE + MLA_ROPE // 2:]
    partner_cols = jnp.concatenate([jnp.zeros_like(per_head[:, :, :MLA_NOPE]), x2_cols, x1_cols], axis=2)
    w_uq_rot = _place_heads(partner_cols.reshape(MLA_Q_RANK, -1), B_HEADS, MLA_NOPE + MLA_ROPE, 0,
                            MLA_NOPE + MLA_ROPE).astype(BF16)
    w_ukv_k = _place_heads(mla_w_ukv[0], B_HEADS, MLA_NOPE + MLA_V, 0, MLA_NOPE).astype(BF16)
    w_ukv = mla_w_ukv[0].reshape(MLA_KV_RANK, B_HEADS, MLA_NOPE + MLA_V)
    w_ukv_vt = w_ukv[:, :, MLA_NOPE:].reshape(MLA_KV_RANK, B_HEADS * MLA_V).T.astype(BF16)
    half = MLA_ROPE // 2
    inv = ROPE_THETA ** (-np.arange(half, dtype=np.float64) / half)
    inv_lane = np.zeros((1, LANES), np.float32)
    inv_lane[0, MLA_NOPE:MLA_NOPE + half] = inv
    inv_lane[0, MLA_NOPE + half:MLA_NOPE + MLA_ROPE] = inv
    rope_tbl = pl.pallas_call(
        _rope_table_kernel,
        grid=(SEQ // TOKEN_TILE,),
        in_specs=[pl.BlockSpec((1, LANES), lambda i: (0, 0))],
        out_specs=pl.BlockSpec((4, TOKEN_TILE, LANES), lambda i: (0, i, 0)),
        out_shape=jax.ShapeDtypeStruct((4, SEQ, LANES), F32),
        name="rope_tables",
    )(jnp.asarray(inv_lane))

    fa, fs = ffn_args(1, 0)
    ins = ([xt] + fa + [_row(mix_norm[0]), w_qk, w_vt, w_c, _row(mla_q_norm[0]), w_uq, w_uq_rot,
                        _row(mla_kv_norm[0]), w_ukv_k, w_ukv_vt, rope_tbl])
    specs = ([_tok_spec(D_MODEL)] + fs
             + [_wspec((1, D_MODEL)), _wspec(w_qk.shape), _wspec(w_vt.shape), _wspec(w_c.shape),
                _wspec((1, MLA_Q_RANK)), _wspec(w_uq.shape), _wspec(w_uq_rot.shape),
                _wspec((1, MLA_KV_RANK)),
                _wspec(w_ukv_k.shape), _wspec(w_ukv_vt.shape),
                pl.BlockSpec((4, TOKEN_TILE, LANES), lambda t: (0, t % (SEQ // TOKEN_TILE), 0))])
    pen_rows = A_HEADS // 2 * N_KBLK
    outs = (jax.ShapeDtypeStruct((TOKENS, D_MODEL), F32),
            jax.ShapeDtypeStruct((TOKENS, 2 * A_WIDTH), BF16),
            jax.ShapeDtypeStruct((A_WIDTH, TOKENS), BF16),
            jax.ShapeDtypeStruct((BATCH, N_KBLK, pen_rows, 2 * Q_TILE), F32),
            jax.ShapeDtypeStruct((TOKENS, B_HEADS * LANES), BF16),
            jax.ShapeDtypeStruct((TOKENS, B_HEADS * LANES), BF16),
            jax.ShapeDtypeStruct((B_HEADS * MLA_V, TOKENS), BF16))
    out_specs = (_tok_spec(D_MODEL), _tok_spec(2 * A_WIDTH), _tok_t_spec(A_WIDTH),
                 pl.BlockSpec((1, TILE_R, pen_rows, 2 * Q_TILE),
                              lambda t: (t // (N_KBLK // TILE_R), t % (N_KBLK // TILE_R), 0, 0)),
                 _tok_spec(B_HEADS * LANES), _tok_spec(B_HEADS * LANES),
                 _tok_t_spec(B_HEADS * MLA_V))
    x1, qk_a, vt_a, pen, q_mla, k_mla, vt_mla = _token_call(
        "pre0", functools.partial(_pre0_kernel, layer=0), ins, specs, outs, out_specs,
        scratch=[pltpu.VMEM((N_KBLK, A_WIDTH), F32)])

    bias = pl.pallas_call(
        _t5_bias_kernel,
        grid=(A_HEADS // 2,),
        in_specs=[pl.BlockSpec(memory_space=pltpu.SMEM)],
        out_specs=pl.BlockSpec((1, 3, Q_TILE, 2 * Q_TILE), lambda h: (h, 0, 0, 0)),
        out_shape=jax.ShapeDtypeStruct((A_HEADS // 2, 3, Q_TILE, 2 * Q_TILE), F32),
        name="t5_bias_tiles",
    )(t5_bias.astype(F32))

    na = A_HEADS // 2
    o_a = _attention(
        "moba", [qk_a, qk_a, vt_a, pen, bias],
        [_seq_spec(LANES, 0), _seq_spec(LANES, na // PAIR_GROUP), _vt_spec(),
         pl.BlockSpec((1, N_KBLK, PAIR_GROUP * N_KBLK, 2 * Q_TILE), lambda b, g: (b, 0, g, 0)),
         pl.BlockSpec((PAIR_GROUP, 3, Q_TILE, 2 * Q_TILE), lambda b, g: (g, 0, 0, 0))], na)
    o_b = _attention(
        "mla", [q_mla, k_mla, vt_mla],
        [_seq_spec(2 * LANES, 0), _seq_spec(2 * LANES, 0), _vt_spec()], B_HEADS // 2)
    w_o = ab_w_out[0]
    xt = post(x1, [o_a, o_b], [w_o[:A_WIDTH], w_o[A_WIDTH:]], 0, DEPTH == 1)

    w_fox = fox_w_in[0]
    w_qk = w_fox[:, :2 * C_WIDTH].astype(BF16)
    w_vt = w_fox[:, 2 * C_WIDTH:3 * C_WIDTH].T.astype(BF16)
    w_f = jnp.pad(w_fox[:, 3 * C_WIDTH:], ((0, 0), (0, LANES - C_HEADS))).astype(BF16)
    fa, fs = ffn_args(1, 1)
    ins = [xt] + fa + [_row(mix_norm[1]), w_qk, w_vt, w_f, _row(fox_b_f[0], LANES)]
    specs = ([_tok_spec(D_MODEL)] + fs
             + [_wspec((1, D_MODEL)), _wspec(w_qk.shape), _wspec(w_vt.shape), _wspec(w_f.shape),
                _wspec((1, LANES))])
    outs = (jax.ShapeDtypeStruct((TOKENS, D_MODEL), F32),
            jax.ShapeDtypeStruct((TOKENS, 2 * C_WIDTH), BF16),
            jax.ShapeDtypeStruct((C_WIDTH, TOKENS), BF16),
            jax.ShapeDtypeStruct((TOKENS, LANES), BF16))
    out_specs = (_tok_spec(D_MODEL), _tok_spec(2 * C_WIDTH), _tok_t_spec(C_WIDTH), _tok_spec(LANES))
    x1, qk_c, vt_c, caug = _token_call("pre1", functools.partial(_pre1_kernel, layer=1), ins, specs, outs, out_specs,
                                       scratch=[pltpu.VMEM((1, LANES), F32)])

    nc = C_HEADS // 2
    o_c = _attention(
        "fox", [qk_c, qk_c, vt_c, caug],
        [_seq_spec(LANES, 0), _seq_spec(LANES, nc // PAIR_GROUP), _vt_spec(),
         pl.BlockSpec((SEQ, LANES), lambda b, g: (b, 0))],
        nc, extra_scratch=[pltpu.VMEM((SEQ, PAIR_GROUP * 2 * LANES), BF16)])
    xt = post(x1, [o_c], [fox_w_out[0]], 1, True)
    return xt.reshape(BATCH, SEQ, D_MODEL)
```

```python
import functools
import math

import jax
import jax.numpy as jnp
import numpy as np
from jax import lax
from jax.experimental import pallas as pl
from jax.experimental.pallas import tpu as pltpu

F32 = jnp.float32
BF16 = jnp.bfloat16

D_MODEL = 1024
BATCH = 8
SEQ = 2048
DEPTH = 2
PLE_DIM = 256
D_FF = 2816
EPS = 1e-6

A_HEADS = 8
A_HEAD_DIM = 64
MOBA_BLOCK = 256
MOBA_TOPK = 3

B_HEADS = 8
MLA_Q_RANK = 256
MLA_KV_RANK = 128
MLA_NOPE = 64
MLA_ROPE = 32
MLA_V = 64
ROPE_THETA = 10000.0

T5_BUCKETS = 32
T5_MAX_DIST = 128

C_HEADS = 16
C_HEAD_DIM = 64

A_WIDTH = A_HEADS * A_HEAD_DIM
C_WIDTH = C_HEADS * C_HEAD_DIM

TOKENS = BATCH * SEQ
LANES = 128
BF16_ROWS = 16
PAIR_GROUP = 2
Q_TILE = MOBA_BLOCK
N_KBLK = SEQ // Q_TILE
TOKEN_TILE = 2 * Q_TILE
TILE_R = TOKEN_TILE // Q_TILE
FF_CHUNK = D_FF // 2
W_IN_CHUNK = 512
W_OUT_CHUNK = 256
VMEM_LIMIT = 56 * 1024 * 1024
NEG_INF = float("-inf")
LOG2E = math.log2(math.e)
QSCALE_64 = A_HEAD_DIM ** -0.5 * LOG2E
QSCALE_MLA = (MLA_NOPE + MLA_ROPE) ** -0.5 * LOG2E


def _wspec(shape, layer=None):
    nd = len(shape)
    if layer is None:
        return pl.BlockSpec(shape, lambda *_: (0,) * nd, pipeline_mode=pl.Buffered(1))
    return pl.BlockSpec((None,) + tuple(shape), lambda *_: (layer,) + (0,) * nd,
                        pipeline_mode=pl.Buffered(1))


def _dot(a, b):
    return jnp.dot(a, b, preferred_element_type=F32)


def _dot_nt(a, b):
    return lax.dot_general(a, b, (((1,), (1,)), ((), ())), preferred_element_type=F32)


def _rms(x, g):
    return x * lax.rsqrt(jnp.mean(x * x, axis=-1, keepdims=True) + EPS) * g


def _ffn_scratch():
    return [pltpu.VMEM((D_MODEL, 2 * D_FF), BF16), pltpu.VMEM((D_FF, D_MODEL), BF16),
            pltpu.VMEM((2, D_MODEL, W_IN_CHUNK), F32), pltpu.VMEM((2, W_OUT_CHUNK, D_MODEL), F32),
            pltpu.SemaphoreType.DMA((2, 2))]


def _load_ffn_weights(layer, win_hbm, wo_hbm, win_ref, wo_ref, stage_in, stage_out, sem):
    def copy_in(c, slot):
        return pltpu.make_async_copy(win_hbm.at[layer, :, pl.ds(c * W_IN_CHUNK, W_IN_CHUNK)],
                                     stage_in.at[slot], sem.at[0, slot])

    def copy_out(c, slot):
        return pltpu.make_async_copy(wo_hbm.at[layer, pl.ds(c * W_OUT_CHUNK, W_OUT_CHUNK), :],
                                     stage_out.at[slot], sem.at[1, slot])

    @pl.when(pl.program_id(0) == 0)
    def _():
        n_in = 2 * D_FF // W_IN_CHUNK
        n_out = D_FF // W_OUT_CHUNK
        copy_in(0, 0).start()
        copy_out(0, 0).start()
        for c in range(n_in):
            slot = c % 2
            if c + 1 < n_in:
                copy_in(c + 1, 1 - slot).start()
            copy_in(c, slot).wait()
            win_ref[:, c * W_IN_CHUNK:(c + 1) * W_IN_CHUNK] = stage_in[slot].astype(BF16)
        for c in range(n_out):
            slot = c % 2
            if c + 1 < n_out:
                copy_out(c + 1, 1 - slot).start()
            copy_out(c, slot).wait()
            wo_ref[c * W_OUT_CHUNK:(c + 1) * W_OUT_CHUNK, :] = stage_out[slot].astype(BF16)


def _ffn(x, g, win_ref, wo_ref):
    h = _rms(x, g).astype(BF16)
    acts = []
    for c in range(D_FF // FF_CHUNK):
        lo, hi = c * FF_CHUNK, (c + 1) * FF_CHUNK
        a = _dot(h, win_ref[:, lo:hi])
        u = _dot(h, win_ref[:, D_FF + lo:D_FF + hi])
        acts.append((a * jax.nn.sigmoid(a) * u).astype(BF16))
    return x + 0.5 * _dot(jnp.concatenate(acts, axis=1), wo_ref[...])


def _rope_table_kernel(inv_ref, o_ref):
    pos0 = pl.program_id(0) * TOKEN_TILE
    pos = (pos0 + lax.broadcasted_iota(jnp.int32, (TOKEN_TILE, LANES), 0)).astype(F32)
    lane = lax.broadcasted_iota(jnp.int32, (TOKEN_TILE, LANES), 1)
    ang = pos * inv_ref[...]
    is_x1 = (lane >= MLA_NOPE) & (lane < MLA_NOPE + MLA_ROPE // 2)
    is_x2 = (lane >= MLA_NOPE + MLA_ROPE // 2) & (lane < MLA_NOPE + MLA_ROPE)
    cos_t = jnp.where(is_x1 | is_x2, jnp.cos(ang), 1.0)
    sin = jnp.sin(ang)
    sin_t = jnp.where(is_x1, -sin, jnp.where(is_x2, sin, 0.0))
    o_ref[0] = cos_t
    o_ref[1] = sin_t
    o_ref[2] = cos_t * QSCALE_MLA
    o_ref[3] = sin_t * QSCALE_MLA


def _rope_block(xb, cos_t, sin_t):
    half = MLA_ROPE // 2
    lane = lax.broadcasted_iota(jnp.int32, xb.shape, 1)
    is_x1 = lane < MLA_NOPE + half
    partner = jnp.where(is_x1, pltpu.roll(xb, LANES - half, 1), pltpu.roll(xb, half, 1))
    return xb * cos_t + partner * sin_t


def _pre0_kernel(x_ref, g1_ref, win_hbm, wo_hbm, gmix_ref, wqk_ref, wvt_ref, wc_ref,
                 qn_ref, wuq_ref, wuqrot_ref, kvn_ref, wukvk_ref, wukvvt_ref, rope_ref,
                 x1_ref, qk_ref, vt_ref, pen_ref, qm_ref, kmla_ref, vmt_ref, km_s, *ffn_s, layer):
    t = pl.program_id(0)
    _load_ffn_weights(layer, win_hbm, wo_hbm, *ffn_s)
    x1 = _ffn(x_ref[...], g1_ref[...], ffn_s[0], ffn_s[1])
    x1_ref[...] = x1
    h = _rms(x1, gmix_ref[...]).astype(BF16)
    qk = _dot(h, wqk_ref[...])
    q = qk[:, :A_WIDTH]
    k = qk[:, A_WIDTH:]
    qk_ref[:, :A_WIDTH] = (q * QSCALE_64).astype(BF16)
    qk_ref[:, A_WIDTH:] = k.astype(BF16)

    @pl.when(t == 0)
    def _():
        km_s[...] = jnp.zeros_like(km_s)

    c = _dot(h, wc_ref[...])
    gates = []
    for r in range(TILE_R):
        own = (t * TILE_R + r) % N_KBLK
        part = slice(r * Q_TILE, (r + 1) * Q_TILE)
        km_s[pl.ds(own, 1), :] = jnp.mean(k[part], axis=0, keepdims=True)
        gates.append((own, _moba_gate(q[part], km_s[...])))
    vt_ref[...] = _dot_nt(wvt_ref[...], h).astype(BF16)
    for r, (own, gate) in enumerate(gates):
        pen_ref[0, r] = _moba_select(gate, own)

    cq = c[:, :MLA_Q_RANK]
    ckv = c[:, MLA_Q_RANK:MLA_Q_RANK + MLA_KV_RANK]
    kr = c[:, MLA_Q_RANK + MLA_KV_RANK:]
    cqn = _rms(cq, qn_ref[...]).astype(BF16)
    ckvn = _rms(ckv, kvn_ref[...]).astype(BF16)
    qm = _dot(cqn, wuq_ref[...])
    qm_partner = _dot(cqn, wuqrot_ref[...])
    kn = _dot(ckvn, wukvk_ref[...])
    vmt_ref[...] = _dot_nt(wukvvt_ref[...], ckvn).astype(BF16)

    krr = _rope_block(kr, rope_ref[0], rope_ref[1])
    for hb in range(B_HEADS):
        sl = slice(hb * LANES, (hb + 1) * LANES)
        qm_ref[:, sl] = (qm[:, sl] * rope_ref[2] + qm_partner[:, sl] * rope_ref[3]).astype(BF16)
        kmla_ref[:, sl] = (kn[:, sl] + krr).astype(BF16)


def _pre1_kernel(x_ref, g1_ref, win_hbm, wo_hbm, gmix_ref, wqk_ref, wvt_ref, wf_ref, bf_ref,
                 x1_ref, qk_ref, vt_ref, caug_ref, carry_s, *ffn_s, layer):
    t = pl.program_id(0)
    _load_ffn_weights(layer, win_hbm, wo_hbm, *ffn_s)
    x1 = _ffn(x_ref[...], g1_ref[...], ffn_s[0], ffn_s[1])
    x1_ref[...] = x1
    h = _rms(x1, gmix_ref[...]).astype(BF16)

    @pl.when(t % (N_KBLK // TILE_R) == 0)
    def _():
        carry_s[...] = jnp.zeros_like(carry_s)

    z = _dot(h, wf_ref[...]) + bf_ref[...]
    qk = _dot(h, wqk_ref[...])
    qk_ref[:, :C_WIDTH] = (qk[:, :C_WIDTH] * QSCALE_64).astype(BF16)
    qk_ref[:, C_WIDTH:] = qk[:, C_WIDTH:].astype(BF16)
    cums = [_fox_cumsum(z[r * Q_TILE:(r + 1) * Q_TILE], carry_s) for r in range(TILE_R)]
    vt_ref[...] = _dot_nt(wvt_ref[...], h).astype(BF16)
    for r, cum in enumerate(cums):
        caug_ref[r * Q_TILE:(r + 1) * Q_TILE, :] = _fox_decay_parts(cum)


def _tok_spec(width):
    return pl.BlockSpec((TOKEN_TILE, width), lambda i: (i, 0))


def _tok_t_spec(height):
    return pl.BlockSpec((height, TOKEN_TILE), lambda i: (0, i))


def _token_call(name, body, ins, in_specs, outs, out_specs, scratch=()):
    return pl.pallas_call(
        body,
        grid=(TOKENS // TOKEN_TILE,),
        in_specs=in_specs,
        out_specs=out_specs,
        out_shape=outs,
        scratch_shapes=list(scratch) + _ffn_scratch(),
        compiler_params=pltpu.CompilerParams(
            dimension_semantics=("arbitrary",), vmem_limit_bytes=VMEM_LIMIT),
        name=name,
    )(*ins)


def _post_kernel(*refs, n_mix, final, layer):
    x_ref = refs[0]
    o_refs = refs[1:1 + n_mix]
    w_refs = refs[1 + n_mix:1 + 2 * n_mix]
    (g2_ref, win_hbm, wo_hbm, gple_ref, wg_ref, p_ref, wp_ref, gfin_ref,
     out_ref) = refs[1 + 2 * n_mix:10 + 2 * n_mix]
    ffn_s = refs[10 + 2 * n_mix:]
    _load_ffn_weights(layer, win_hbm, wo_hbm, *ffn_s)
    x = x_ref[...]
    for o_ref, w_ref in zip(o_refs, w_refs):
        x = x + _dot(o_ref[...], w_ref[...])
    x = _ffn(x, g2_ref[...], ffn_s[0], ffn_s[1])
    gate = jax.nn.sigmoid(_dot(_rms(x, gple_ref[...]).astype(BF16), wg_ref[...]))
    x = x + gate * _dot(p_ref[...].astype(BF16), wp_ref[...])
    if final:
        x = _rms(x, gfin_ref[...])
    out_ref[...] = x


def _split_bf16(x):
    hi = x.astype(BF16)
    lo = (x - hi.astype(F32)).astype(BF16)
    return hi, lo


def _moba_gate(q, km):
    rows = A_HEADS * N_KBLK
    gt = jnp.concatenate([km] * A_HEADS, axis=0)
    r = lax.broadcasted_iota(jnp.int32, (rows, A_WIDTH), 0)
    c = lax.broadcasted_iota(jnp.int32, (rows, A_WIDTH), 1)
    gt = jnp.where((r // N_KBLK) == (c // A_HEAD_DIM), gt, 0.0)
    g_hi, g_lo = _split_bf16(gt)
    q_hi, q_lo = _split_bf16(q)
    return _dot_nt(g_hi, q_hi) + _dot_nt(g_hi, q_lo) + _dot_nt(g_lo, q_hi)


def _moba_select(gate, own):
    n_idx = lax.broadcasted_iota(jnp.int32, (N_KBLK, Q_TILE), 0)
    pen_rows = []
    for h in range(A_HEADS):
        gh = gate[h * N_KBLK:(h + 1) * N_KBLK]
        rank = jnp.zeros((N_KBLK, Q_TILE), jnp.int32)
        for m in range(N_KBLK):
            gm = gh[m:m + 1]
            beats = (gm > gh) | ((gm == gh) & (m < n_idx))
            rank = rank + jnp.where(beats & (m < own), 1, 0)
        sel = (n_idx < own) & (rank < MOBA_TOPK)
        pen_n = jnp.where(sel, 0.0, NEG_INF)
        pen_d = jnp.full((N_KBLK, Q_TILE), NEG_INF, F32)
        for n in range(N_KBLK):
            pen_d = jnp.where(n_idx == own - n, pen_n[n:n + 1], pen_d)
        pen_rows.append(pen_d)
    pairs = [jnp.concatenate(pen_rows[2 * hp:2 * hp + 2], axis=1) for hp in range(A_HEADS // 2)]
    return jnp.concatenate(pairs, axis=0)


def _t5_bias_kernel(tbl_ref, o_ref):
    hp = pl.program_id(0)
    r = lax.broadcasted_iota(jnp.int32, (Q_TILE, Q_TILE), 0)
    c = lax.broadcasted_iota(jnp.int32, (Q_TILE, Q_TILE), 1)
    max_exact = T5_BUCKETS // 2
    for dd in range(3):
        dist = dd * Q_TILE + c - r
        dc = jnp.maximum(dist, 0)
        df = jnp.maximum(dc.astype(F32), 1.0)
        large = max_exact + (jnp.log(df / max_exact) / math.log(T5_MAX_DIST / max_exact)
                             * (T5_BUCKETS - max_exact)).astype(jnp.int32)
        large = jnp.minimum(large, T5_BUCKETS - 1)
        bucket = jnp.where(dc < max_exact, dc, large)
        for hh in range(2):
            bias = jnp.zeros((Q_TILE, Q_TILE), F32)
            for b in range(T5_BUCKETS):
                bias = jnp.where(bucket == b, tbl_ref[b, 2 * hp + hh], bias)
            bias = bias * LOG2E
            if dd == 0:
                bias = jnp.where(dist >= 0, bias, NEG_INF)
            o_ref[0, dd, :, hh * Q_TILE:(hh + 1) * Q_TILE] = bias


def _fox_cumsum(z, carry_ref):
    lane = lax.broadcasted_iota(jnp.int32, (Q_TILE, LANES), 1)
    logf = jnp.minimum(z, 0.0) - jnp.log1p(jnp.exp(-jnp.abs(z)))
    logf = jnp.where(lane < C_HEADS, logf, 0.0)
    r = lax.broadcasted_iota(jnp.int32, (Q_TILE, Q_TILE), 0)
    c = lax.broadcasted_iota(jnp.int32, (Q_TILE, Q_TILE), 1)
    tri = jnp.where(c <= r, 1.0, 0.0).astype(BF16)
    l1 = logf.astype(BF16)
    rem = logf - l1.astype(F32)
    l2 = rem.astype(BF16)
    l3 = (rem - l2.astype(F32)).astype(BF16)
    cum = _dot(tri, l1) + _dot(tri, l2) + _dot(tri, l3) + carry_ref[...]
    carry_ref[...] = cum[Q_TILE - 1:Q_TILE, :]
    return cum


def _fox_decay_parts(cum):
    cs = cum * LOG2E
    c1 = cs.astype(BF16)
    rem = cs - c1.astype(F32)
    c2 = rem.astype(BF16)
    c3 = (rem - c2.astype(F32)).astype(BF16)
    rin = lax.broadcasted_iota(jnp.int32, (LANES, LANES), 0)
    lout = lax.broadcasted_iota(jnp.int32, (LANES, LANES), 1)
    out = jnp.zeros((Q_TILE, LANES), F32)
    for part, cp in enumerate((c1, c2, c3)):
        place = jnp.where((rin < C_HEADS) & (lout == rin * 8 + part), 1.0, 0.0).astype(BF16)
        out = out + _dot(cp, place)
    return out.astype(BF16)


def _attn_kernel(*refs, kind):
    if kind == "moba":
        q_ref, k_ref, vt_ref, pen_ref, bias_ref, o_ref = refs
    elif kind == "mla":
        q_ref, k_ref, vt_ref, o_ref = refs
    else:
        q_ref, k_ref, vt_ref, c_ref, o_ref, kp_s = refs
    grp = pl.program_id(1)
    lane = lax.broadcasted_iota(jnp.int32, (1, LANES), 1)
    low = lane < A_HEAD_DIM
    kw = LANES if kind == "moba" else 2 * LANES
    qw = 2 * LANES if kind == "mla" else LANES
    k_cat = kp_s if kind == "fox" else k_ref

    if kind == "fox":
        rin = lax.broadcasted_iota(jnp.int32, (LANES, LANES), 0)
        lout = lax.broadcasted_iota(jnp.int32, (LANES, LANES), 1)

        def place(pair, off_a, off_b, val):
            base_a = 2 * pair * 8
            base_b = base_a + 8
            sel_a = (lout >= off_a) & (lout < off_a + 3) & (rin == base_a + lout - off_a)
            sel_b = (lout >= off_b) & (lout < off_b + 3) & (rin == base_b + lout - off_b)
            return jnp.where(sel_a | sel_b, val, 0.0).astype(BF16)

        def ones(off_a, off_b):
            in_a = (lane >= off_a) & (lane < off_a + 3)
            in_b = (lane >= off_b) & (lane < off_b + 3)
            return jnp.where(in_a | in_b, 1.0, 0.0)

        aq_all = []
        for pi in range(PAIR_GROUP):
            ak = (_dot(c_ref[...], place(grp * PAIR_GROUP + pi, A_HEAD_DIM + 3, 3, -1.0))
                  + ones(A_HEAD_DIM, 0)).astype(BF16)
            k = k_ref[:, pi * LANES:(pi + 1) * LANES]
            kp_s[:, pi * kw:pi * kw + LANES] = jnp.where(low, k, ak)
            kp_s[:, pi * kw + LANES:(pi + 1) * kw] = jnp.where(low, ak, k)
            aq_all.append((_dot(c_ref[...], place(grp * PAIR_GROUP + pi, A_HEAD_DIM, 0, 1.0))
                           + ones(A_HEAD_DIM + 3, 3)).astype(BF16))

    def q_operand(tile, pi):
        zero = jnp.zeros((Q_TILE, LANES), BF16)
        rows = slice(tile * Q_TILE, (tile + 1) * Q_TILE)
        q = q_ref[rows, pi * qw:(pi + 1) * qw]
        if kind == "moba":
            return jnp.concatenate([jnp.where(low, q, zero), jnp.where(low, zero, q)], axis=0)
        if kind == "fox":
            aq = aq_all[pi][rows]
            qa, qb = jnp.where(low, q, aq), jnp.where(low, aq, q)
        else:
            qa, qb = q[:, :LANES], q[:, LANES:]
        return jnp.concatenate([jnp.concatenate([qa, zero], axis=1),
                                jnp.concatenate([zero, qb], axis=1)], axis=0)

    def store_out(pi, tile, accs):
        ot = jnp.concatenate([a[:A_HEAD_DIM] / a[A_HEAD_DIM:A_HEAD_DIM + 1] for a in accs], axis=0)
        o_ref[tile * Q_TILE:(tile + 1) * Q_TILE, pi * LANES:(pi + 1) * LANES] = ot.T.astype(BF16)

    def pv(pi, rows, p):
        ones_rows = jnp.ones((BF16_ROWS, rows.stop - rows.start), BF16)
        outs = []
        for hh in range(2):
            lo = pi * LANES + hh * A_HEAD_DIM
            v_aug = jnp.concatenate([vt_ref[lo:lo + A_HEAD_DIM, rows], ones_rows], axis=0)
            outs.append(_dot(v_aug, p[:, hh * Q_TILE:(hh + 1) * Q_TILE]))
        return outs

    if kind == "moba":
        _moba_schedule(q_operand, k_ref, pen_ref, bias_ref, pv, store_out)
    else:
        _online_schedule(q_operand, k_cat, kw, pv, store_out)


def _online_schedule(q_operand, k_cat, kw, pv, store_out):
    key = lax.broadcasted_iota(jnp.int32, (Q_TILE, 2 * Q_TILE), 0)
    qry = lax.broadcasted_iota(jnp.int32, (Q_TILE, 2 * Q_TILE), 1) % Q_TILE

    def run(chains):
        n = len(chains)

        def rows_of(c, d):
            nb = chains[c][1]
            return slice((nb - 1 - d) * Q_TILE, (nb - d) * Q_TILE)

        def qk(c, d):
            pi = chains[c][0]
            return _dot_nt(k_cat[rows_of(c, d), pi * kw:(pi + 1) * kw], chains[c][2])

        s_next = [qk(c, 0) for c in range(n)]
        m = [None] * n
        acc = [None] * n
        for d in range(max(c[1] for c in chains)):
            live = [c for c in range(n) if d < chains[c][1]]
            s_cur = list(s_next)
            for c in live:
                if d + 1 < chains[c][1]:
                    s_next[c] = qk(c, d + 1)
            for c in live:
                sn = s_cur[c]
                if d == 0:
                    sn = jnp.where(key <= qry, sn, NEG_INF)
                bm = jnp.max(sn, axis=0, keepdims=True)
                m_new = bm if d == 0 else jnp.maximum(m[c], bm)
                new = pv(chains[c][0], rows_of(c, d), jnp.exp2(sn - m_new).astype(BF16))
                if d > 0:
                    alpha = jnp.exp2(m[c] - m_new)
                    new = [a * alpha[:, hh * Q_TILE:(hh + 1) * Q_TILE] + n
                           for hh, (a, n) in enumerate(zip(acc[c], new))]
                acc[c] = new
                m[c] = m_new
        return acc

    for jj in range(N_KBLK // 2):
        tiles = (N_KBLK - 1 - jj, jj)
        chains = [(pi, tile + 1, q_operand(tile, pi)) for pi in range(PAIR_GROUP) for tile in tiles]
        for (pi, nb, _), acc in zip(chains, run(chains)):
            store_out(pi, nb - 1, acc)


def _moba_schedule(q_operand, k_ref, pen_ref, bias_ref, pv, store_out):
    def new_chain(pi, nb):
        return {"pair": pi, "nb": nb, "q": q_operand(nb - 1, pi), "blocks": [], "shifts": [], "m": None}

    def score_block(ch, n):
        pi, nb = ch["pair"], ch["nb"]
        d = nb - 1 - n
        sn = _dot_nt(k_ref[n * Q_TILE:(n + 1) * Q_TILE, pi * LANES:(pi + 1) * LANES], ch["q"])
        shift = None
        if d < 2:
            sn = sn + bias_ref[pi, d]
        else:
            shift = bias_ref[pi, 2, 0:1, :]
        if d > 0:
            pen = pen_ref[0, nb - 1, pi * N_KBLK + d:pi * N_KBLK + d + 1, :]
            shift = pen if shift is None else shift + pen
        bm = jnp.max(sn, axis=0, keepdims=True)
        if shift is not None:
            bm = bm + shift
        ch["m"] = bm if ch["m"] is None else jnp.maximum(ch["m"], bm)
        ch["blocks"].append(sn)
        ch["shifts"].append(shift)

    def prob_block(ch, n):
        m, sh = ch["m"], ch["shifts"][n]
        return jnp.exp2(ch["blocks"][n] - (m if sh is None else m - sh)).astype(BF16)

    prev = []
    for nb in list(range(N_KBLK, 0, -1)) + [0]:
        cur = [new_chain(pi, nb) for pi in range(PAIR_GROUP)] if nb else []
        probs = [[] for _ in prev]
        for n in range(max(nb, prev[0]["nb"] if prev else 0)):
            for ch in cur:
                if n < nb:
                    score_block(ch, n)
            for ch, pr in zip(prev, probs):
                if n < ch["nb"]:
                    pr.append(prob_block(ch, n))
        for ch, pr in zip(prev, probs):
            accs = pv(ch["pair"], slice(0, ch["nb"] * Q_TILE), jnp.concatenate(pr, axis=0))
            store_out(ch["pair"], ch["nb"] - 1, accs)
        prev = cur


def _attention(kind, ins, in_specs, n_pairs, extra_scratch=()):
    width = PAIR_GROUP * LANES
    return pl.pallas_call(
        functools.partial(_attn_kernel, kind=kind),
        grid=(BATCH, n_pairs // PAIR_GROUP),
        in_specs=in_specs,
        out_specs=pl.BlockSpec((SEQ, width), lambda b, g: (b, g)),
        out_shape=jax.ShapeDtypeStruct((TOKENS, n_pairs * LANES), BF16),
        scratch_shapes=list(extra_scratch),
        compiler_params=pltpu.CompilerParams(
            dimension_semantics=("arbitrary", "arbitrary"), vmem_limit_bytes=VMEM_LIMIT),
        name="attn_" + kind,
    )(*ins)


def _seq_spec(pair_width, col0):
    return pl.BlockSpec((SEQ, PAIR_GROUP * pair_width), lambda b, g: (b, col0 + g))


def _vt_spec():
    return pl.BlockSpec((PAIR_GROUP * LANES, SEQ), lambda b, g: (g, b))


def _place_heads(w, n_heads, src_stride, src_off, width):
    per_head = w.reshape(w.shape[0], n_heads, src_stride)[:, :, src_off:src_off + width]
    per_head = jnp.pad(per_head, ((0, 0), (0, 0), (0, LANES - width)))
    return per_head.reshape(w.shape[0], n_heads * LANES)


def _row(v, width=None):
    v = v.reshape(1, -1).astype(F32)
    if width is not None and v.shape[1] < width:
        v = jnp.pad(v, ((0, 0), (0, width - v.shape[1])))
    return v


def kernel(x, p, t5_bias, ff1_norm, ff1_w_in, ff1_w_out, mix_norm, ff2_norm, ff2_w_in, ff2_w_out,
           ple_norm, ple_w_gate, ple_w_proj, ab_w_in, mla_q_norm, mla_w_uq, mla_kv_norm, mla_w_ukv,
           ab_w_out, fox_w_in, fox_b_f, fox_w_out, final_norm):
    xt = x.reshape(TOKENS, D_MODEL)

    ffw = {1: (ff1_norm, ff1_w_in, ff1_w_out), 2: (ff2_norm, ff2_w_in, ff2_w_out)}
    hbm = pl.BlockSpec(memory_space=pl.ANY)

    def ffn_args(which, layer):
        norm, w_in, w_out = ffw[which]
        return [_row(norm[layer]), w_in, w_out], [_wspec((1, D_MODEL)), hbm, hbm]

    ple_gate_bf = ple_w_gate.astype(BF16)
    ple_proj_bf = ple_w_proj.astype(BF16)

    def post(xin, mixes, w_outs, layer, final):
        fa, fs = ffn_args(2, layer)
        ws = [w.astype(BF16) for w in w_outs]
        ins = ([xin] + list(mixes) + ws + fa
               + [_row(ple_norm[layer]), ple_gate_bf, p.reshape(DEPTH, TOKENS, PLE_DIM), ple_proj_bf,
                  _row(final_norm)])
        specs = ([_tok_spec(D_MODEL)] + [_tok_spec(m.shape[1]) for m in mixes]
                 + [_wspec(w.shape) for w in ws] + fs
                 + [_wspec((1, D_MODEL)), _wspec((D_MODEL, D_MODEL), layer),
                    pl.BlockSpec((None, TOKEN_TILE, PLE_DIM), lambda t: (layer, t, 0)),
                    _wspec((PLE_DIM, D_MODEL), layer), _wspec((1, D_MODEL))])
        return _token_call(
            "post%d" % layer,
            functools.partial(_post_kernel, n_mix=len(mixes), final=final, layer=layer), ins, specs,
            jax.ShapeDtypeStruct((TOKENS, D_MODEL), F32), _tok_spec(D_MODEL))

    w_ab = ab_w_in[0]
    w_qk = w_ab[:, :2 * A_WIDTH].astype(BF16)
    w_vt = w_ab[:, 2 * A_WIDTH:3 * A_WIDTH].T.astype(BF16)
    c0 = 3 * A_WIDTH
    kr0 = MLA_Q_RANK + MLA_KV_RANK
    w_c = jnp.concatenate(
        [w_ab[:, c0:c0 + kr0], jnp.zeros((D_MODEL, MLA_NOPE), F32), w_ab[:, c0 + kr0:],
         jnp.zeros((D_MODEL, LANES - MLA_NOPE - MLA_ROPE), F32)], axis=1).astype(BF16)
    w_uq = _place_heads(mla_w_uq[0], B_HEADS, MLA_NOPE + MLA_ROPE, 0, MLA_NOPE + MLA_ROPE).astype(BF16)
    per_head = mla_w_uq[0].reshape(MLA_Q_RANK, B_HEADS, MLA_NOPE + MLA_ROPE)
    x1_cols = per_head[:, :, MLA_NOPE:MLA_NOPE + MLA_ROPE // 2]
    x2_cols = per_head[:, :, MLA_NOPE + MLA_ROPE // 2:]
    partner_cols = jnp.concatenate([jnp.zeros_like(per_head[:, :, :MLA_NOPE]), x2_cols, x1_cols], axis=2)
    w_uq_rot = _place_heads(partner_cols.reshape(MLA_Q_RANK, -1), B_HEADS, MLA_NOPE + MLA_ROPE, 0,
                            MLA_NOPE + MLA_ROPE).astype(BF16)
    w_ukv_k = _place_heads(mla_w_ukv[0], B_HEADS, MLA_NOPE + MLA_V, 0, MLA_NOPE).astype(BF16)
    w_ukv = mla_w_ukv[0].reshape(MLA_KV_RANK, B_HEADS, MLA_NOPE + MLA_V)
    w_ukv_vt = w_ukv[:, :, MLA_NOPE:].reshape(MLA_KV_RANK, B_HEADS * MLA_V).T.astype(BF16)
    half = MLA_ROPE // 2
    inv = ROPE_THETA ** (-np.arange(half, dtype=np.float64) / half)
    inv_lane = np.zeros((1, LANES), np.float32)
    inv_lane[0, MLA_NOPE:MLA_NOPE + half] = inv
    inv_lane[0, MLA_NOPE + half:MLA_NOPE + MLA_ROPE] = inv
    rope_tbl = pl.pallas_call(
        _rope_table_kernel,
        grid=(SEQ // TOKEN_TILE,),
        in_specs=[pl.BlockSpec((1, LANES), lambda i: (0, 0))],
        out_specs=pl.BlockSpec((4, TOKEN_TILE, LANES), lambda i: (0, i, 0)),
        out_shape=jax.ShapeDtypeStruct((4, SEQ, LANES), F32),
        name="rope_tables",
    )(jnp.asarray(inv_lane))

    fa, fs = ffn_args(1, 0)
    ins = ([xt] + fa + [_row(mix_norm[0]), w_qk, w_vt, w_c, _row(mla_q_norm[0]), w_uq, w_uq_rot,
                        _row(mla_kv_norm[0]), w_ukv_k, w_ukv_vt, rope_tbl])
    specs = ([_tok_spec(D_MODEL)] + fs
             + [_wspec((1, D_MODEL)), _wspec(w_qk.shape), _wspec(w_vt.shape), _wspec(w_c.shape),
                _wspec((1, MLA_Q_RANK)), _wspec(w_uq.shape), _wspec(w_uq_rot.shape),
                _wspec((1, MLA_KV_RANK)),
                _wspec(w_ukv_k.shape), _wspec(w_ukv_vt.shape),
                pl.BlockSpec((4, TOKEN_TILE, LANES), lambda t: (0, t % (SEQ // TOKEN_TILE), 0))])
    pen_rows = A_HEADS // 2 * N_KBLK
    outs = (jax.ShapeDtypeStruct((TOKENS, D_MODEL), F32),
            jax.ShapeDtypeStruct((TOKENS, 2 * A_WIDTH), BF16),
            jax.ShapeDtypeStruct((A_WIDTH, TOKENS), BF16),
            jax.ShapeDtypeStruct((BATCH, N_KBLK, pen_rows, 2 * Q_TILE), F32),
            jax.ShapeDtypeStruct((TOKENS, B_HEADS * LANES), BF16),
            jax.ShapeDtypeStruct((TOKENS, B_HEADS * LANES), BF16),
            jax.ShapeDtypeStruct((B_HEADS * MLA_V, TOKENS), BF16))
    out_specs = (_tok_spec(D_MODEL), _tok_spec(2 * A_WIDTH), _tok_t_spec(A_WIDTH),
                 pl.BlockSpec((1, TILE_R, pen_rows, 2 * Q_TILE),
                              lambda t: (t // (N_KBLK // TILE_R), t % (N_KBLK // TILE_R), 0, 0)),
                 _tok_spec(B_HEADS * LANES), _tok_spec(B_HEADS * LANES),
                 _tok_t_spec(B_HEADS * MLA_V))
    x1, qk_a, vt_a, pen, q_mla, k_mla, vt_mla = _token_call(
        "pre0", functools.partial(_pre0_kernel, layer=0), ins, specs, outs, out_specs,
        scratch=[pltpu.VMEM((N_KBLK, A_WIDTH), F32)])

    bias = pl.pallas_call(
        _t5_bias_kernel,
        grid=(A_HEADS // 2,),
        in_specs=[pl.BlockSpec(memory_space=pltpu.SMEM)],
        out_specs=pl.BlockSpec((1, 3, Q_TILE, 2 * Q_TILE), lambda h: (h, 0, 0, 0)),
        out_shape=jax.ShapeDtypeStruct((A_HEADS // 2, 3, Q_TILE, 2 * Q_TILE), F32),
        name="t5_bias_tiles",
    )(t5_bias.astype(F32))

    na = A_HEADS // 2
    o_a = _attention(
        "moba", [qk_a, qk_a, vt_a, pen, bias],
        [_seq_spec(LANES, 0), _seq_spec(LANES, na // PAIR_GROUP), _vt_spec(),
         pl.BlockSpec((1, N_KBLK, PAIR_GROUP * N_KBLK, 2 * Q_TILE), lambda b, g: (b, 0, g, 0)),
         pl.BlockSpec((PAIR_GROUP, 3, Q_TILE, 2 * Q_TILE), lambda b, g: (g, 0, 0, 0))], na)
    o_b = _attention(
        "mla", [q_mla, k_mla, vt_mla],
        [_seq_spec(2 * LANES, 0), _seq_spec(2 * LANES, 0), _vt_spec()], B_HEADS // 2)
    w_o = ab_w_out[0]
    xt = post(x1, [o_a, o_b], [w_o[:A_WIDTH], w_o[A_WIDTH:]], 0, DEPTH == 1)

    w_fox = fox_w_in[0]
    w_qk = w_fox[:, :2 * C_WIDTH].astype(BF16)
    w_vt = w_fox[:, 2 * C_WIDTH:3 * C_WIDTH].T.astype(BF16)
    w_f = jnp.pad(w_fox[:, 3 * C_WIDTH:], ((0, 0), (0, LANES - C_HEADS))).astype(BF16)
    fa, fs = ffn_args(1, 1)
    ins = [xt] + fa + [_row(mix_norm[1]), w_qk, w_vt, w_f, _row(fox_b_f[0], LANES)]
    specs = ([_tok_spec(D_MODEL)] + fs
             + [_wspec((1, D_MODEL)), _wspec(w_qk.shape), _wspec(w_vt.shape), _wspec(w_f.shape),
                _wspec((1, LANES))])
    outs = (jax.ShapeDtypeStruct((TOKENS, D_MODEL), F32),
            jax.ShapeDtypeStruct((TOKENS, 2 * C_WIDTH), BF16),
            jax.ShapeDtypeStruct((C_WIDTH, TOKENS), BF16),
            jax.ShapeDtypeStruct((TOKENS, LANES), BF16))
    out_specs = (_tok_spec(D_MODEL), _tok_spec(2 * C_WIDTH), _tok_t_spec(C_WIDTH), _tok_spec(LANES))
    x1, qk_c, vt_c, caug = _token_call("pre1", functools.partial(_pre1_kernel, layer=1), ins, specs, outs, out_specs,
                                       scratch=[pltpu.VMEM((1, LANES), F32)])

    nc = C_HEADS // 2
    o_c = _attention(
        "fox", [qk_c, qk_c, vt_c, caug],
        [_seq_spec(LANES, 0), _seq_spec(LANES, nc // PAIR_GROUP), _vt_spec(),
         pl.BlockSpec((SEQ, LANES), lambda b, g: (b, 0))],
        nc, extra_scratch=[pltpu.VMEM((SEQ, PAIR_GROUP * 2 * LANES), BF16)])
    xt = post(x1, [o_c], [fox_w_out[0]], 1, True)
    return xt.reshape(BATCH, SEQ, D_MODEL)
```

```python
import functools
import math

import jax
import jax.numpy as jnp
import numpy as np
from jax import lax
from jax.experimental import pallas as pl
from jax.experimental.pallas import tpu as pltpu

F32 = jnp.float32
BF16 = jnp.bfloat16

D_MODEL = 1024
BATCH = 8
SEQ = 2048
DEPTH = 2
PLE_DIM = 256
D_FF = 2816
EPS = 1e-6

A_HEADS = 8
A_HEAD_DIM = 64
MOBA_BLOCK = 256
MOBA_TOPK = 3

B_HEADS = 8
MLA_Q_RANK = 256
MLA_KV_RANK = 128
MLA_NOPE = 64
MLA_ROPE = 32
MLA_V = 64
ROPE_THETA = 10000.0

T5_BUCKETS = 32
T5_MAX_DIST = 128

C_HEADS = 16
C_HEAD_DIM = 64

A_WIDTH = A_HEADS * A_HEAD_DIM
C_WIDTH = C_HEADS * C_HEAD_DIM

TOKENS = BATCH * SEQ
LANES = 128
BF16_ROWS = 16
PAIR_GROUP = 2
Q_TILE = MOBA_BLOCK
N_KBLK = SEQ // Q_TILE
TOKEN_TILE = 2 * Q_TILE
TILE_R = TOKEN_TILE // Q_TILE
FF_CHUNK = D_FF // 2
W_IN_CHUNK = 256
W_OUT_CHUNK = 128
W_SLOTS = 4
VMEM_LIMIT = 56 * 1024 * 1024
NEG_INF = float("-inf")
LOG2E = math.log2(math.e)
QSCALE_64 = A_HEAD_DIM ** -0.5 * LOG2E
QSCALE_MLA = (MLA_NOPE + MLA_ROPE) ** -0.5 * LOG2E


def _wspec(shape, layer=None):
    nd = len(shape)
    if layer is None:
        return pl.BlockSpec(shape, lambda *_: (0,) * nd, pipeline_mode=pl.Buffered(1))
    return pl.BlockSpec((None,) + tuple(shape), lambda *_: (layer,) + (0,) * nd,
                        pipeline_mode=pl.Buffered(1))


def _dot(a, b):
    return jnp.dot(a, b, preferred_element_type=F32)


def _dot_nt(a, b):
    return lax.dot_general(a, b, (((1,), (1,)), ((), ())), preferred_element_type=F32)


def _rms(x, g):
    return x * lax.rsqrt(jnp.mean(x * x, axis=-1, keepdims=True) + EPS) * g


def _ffn_scratch():
    return [pltpu.VMEM((D_MODEL, 2 * D_FF), BF16), pltpu.VMEM((D_FF, D_MODEL), BF16),
            pltpu.VMEM((W_SLOTS, D_MODEL, W_IN_CHUNK), F32),
            pltpu.VMEM((W_SLOTS, W_OUT_CHUNK, D_MODEL), F32),
            pltpu.SemaphoreType.DMA((2, W_SLOTS))]


def _load_ffn_weights(layer, win_hbm, wo_hbm, win_ref, wo_ref, stage_in, stage_out, sem):
    def copy_in(c):
        return pltpu.make_async_copy(win_hbm.at[layer, :, pl.ds(c * W_IN_CHUNK, W_IN_CHUNK)],
                                     stage_in.at[c % W_SLOTS], sem.at[0, c % W_SLOTS])

    def copy_out(c):
        return pltpu.make_async_copy(wo_hbm.at[layer, pl.ds(c * W_OUT_CHUNK, W_OUT_CHUNK), :],
                                     stage_out.at[c % W_SLOTS], sem.at[1, c % W_SLOTS])

    def stream(copy, n_chunks, consume):
        for c in range(min(W_SLOTS - 1, n_chunks)):
            copy(c).start()
        for c in range(n_chunks):
            copy(c).wait()
            consume(c)
            if c + W_SLOTS - 1 < n_chunks:
                copy(c + W_SLOTS - 1).start()

    def cast_in(c):
        win_ref[:, c * W_IN_CHUNK:(c + 1) * W_IN_CHUNK] = stage_in[c % W_SLOTS].astype(BF16)

    def cast_out(c):
        wo_ref[c * W_OUT_CHUNK:(c + 1) * W_OUT_CHUNK, :] = stage_out[c % W_SLOTS].astype(BF16)

    @pl.when(pl.program_id(0) == 0)
    def _():
        stream(copy_in, 2 * D_FF // W_IN_CHUNK, cast_in)
        stream(copy_out, D_FF // W_OUT_CHUNK, cast_out)


def _ffn(x, g, win_ref, wo_ref):
    h = _rms(x, g).astype(BF16)
    acts = []
    for c in range(D_FF // FF_CHUNK):
        lo, hi = c * FF_CHUNK, (c + 1) * FF_CHUNK
        a = _dot(h, win_ref[:, lo:hi])
        u = _dot(h, win_ref[:, D_FF + lo:D_FF + hi])
        acts.append((a * jax.nn.sigmoid(a) * u).astype(BF16))
    return x + 0.5 * _dot(jnp.concatenate(acts, axis=1), wo_ref[...])


def _rope_table_kernel(inv_ref, o_ref):
    pos0 = pl.program_id(0) * TOKEN_TILE
    pos = (pos0 + lax.broadcasted_iota(jnp.int32, (TOKEN_TILE, LANES), 0)).astype(F32)
    lane = lax.broadcasted_iota(jnp.int32, (TOKEN_TILE, LANES), 1)
    ang = pos * inv_ref[...]
    is_x1 = (lane >= MLA_NOPE) & (lane < MLA_NOPE + MLA_ROPE // 2)
    is_x2 = (lane >= MLA_NOPE + MLA_ROPE // 2) & (lane < MLA_NOPE + MLA_ROPE)
    cos_t = jnp.where(is_x1 | is_x2, jnp.cos(ang), 1.0)
    sin = jnp.sin(ang)
    sin_t = jnp.where(is_x1, -sin, jnp.where(is_x2, sin, 0.0))
    o_ref[0] = cos_t
    o_ref[1] = sin_t
    o_ref[2] = cos_t * QSCALE_MLA
    o_ref[3] = sin_t * QSCALE_MLA


def _rope_block(xb, cos_t, sin_t):
    half = MLA_ROPE // 2
    lane = lax.broadcasted_iota(jnp.int32, xb.shape, 1)
    is_x1 = lane < MLA_NOPE + half
    partner = jnp.where(is_x1, pltpu.roll(xb, LANES - half, 1), pltpu.roll(xb, half, 1))
    return xb * cos_t + partner * sin_t


def _pre0_kernel(x_ref, g1_ref, win_hbm, wo_hbm, gmix_ref, wqk_ref, wvt_ref, wc_ref,
                 qn_ref, wuq_ref, wuqrot_ref, kvn_ref, wukvk_ref, wukvvt_ref, rope_ref,
                 x1_ref, qk_ref, vt_ref, pen_ref, qm_ref, kmla_ref, vmt_ref, km_s, *ffn_s, layer):
    t = pl.program_id(0)
    _load_ffn_weights(layer, win_hbm, wo_hbm, *ffn_s)
    x1 = _ffn(x_ref[...], g1_ref[...], ffn_s[0], ffn_s[1])
    x1_ref[...] = x1
    h = _rms(x1, gmix_ref[...]).astype(BF16)
    qk = _dot(h, wqk_ref[...])
    q = qk[:, :A_WIDTH]
    k = qk[:, A_WIDTH:]
    qk_ref[:, :A_WIDTH] = (q * QSCALE_64).astype(BF16)
    qk_ref[:, A_WIDTH:] = k.astype(BF16)

    @pl.when(t == 0)
    def _():
        km_s[...] = jnp.zeros_like(km_s)

    c = _dot(h, wc_ref[...])
    gates = []
    for r in range(TILE_R):
        own = (t * TILE_R + r) % N_KBLK
        part = slice(r * Q_TILE, (r + 1) * Q_TILE)
        km_s[pl.ds(own, 1), :] = jnp.mean(k[part], axis=0, keepdims=True)
        gates.append((own, _moba_gate(q[part], km_s[...])))
    vt_ref[...] = _dot_nt(wvt_ref[...], h).astype(BF16)
    for r, (own, gate) in enumerate(gates):
        pen_ref[0, r] = _moba_select(gate, own)

    cq = c[:, :MLA_Q_RANK]
    ckv = c[:, MLA_Q_RANK:MLA_Q_RANK + MLA_KV_RANK]
    kr = c[:, MLA_Q_RANK + MLA_KV_RANK:]
    cqn = _rms(cq, qn_ref[...]).astype(BF16)
    ckvn = _rms(ckv, kvn_ref[...]).astype(BF16)
    qm = _dot(cqn, wuq_ref[...])
    qm_partner = _dot(cqn, wuqrot_ref[...])
    kn = _dot(ckvn, wukvk_ref[...])
    vmt_ref[...] = _dot_nt(wukvvt_ref[...], ckvn).astype(BF16)

    krr = _rope_block(kr, rope_ref[0], rope_ref[1])
    for hb in range(B_HEADS):
        sl = slice(hb * LANES, (hb + 1) * LANES)
        qm_ref[:, sl] = (qm[:, sl] * rope_ref[2] + qm_partner[:, sl] * rope_ref[3]).astype(BF16)
        kmla_ref[:, sl] = (kn[:, sl] + krr).astype(BF16)


def _pre1_kernel(x_ref, g1_ref, win_hbm, wo_hbm, gmix_ref, wqk_ref, wvt_ref, wf_ref, bf_ref,
                 x1_ref, qk_ref, vt_ref, caug_ref, carry_s, *ffn_s, layer):
    t = pl.program_id(0)
    _load_ffn_weights(layer, win_hbm, wo_hbm, *ffn_s)
    x1 = _ffn(x_ref[...], g1_ref[...], ffn_s[0], ffn_s[1])
    x1_ref[...] = x1
    h = _rms(x1, gmix_ref[...]).astype(BF16)

    @pl.when(t % (N_KBLK // TILE_R) == 0)
    def _():
        carry_s[...] = jnp.zeros_like(carry_s)

    z = _dot(h, wf_ref[...]) + bf_ref[...]
    qk = _dot(h, wqk_ref[...])
    qk_ref[:, :C_WIDTH] = (qk[:, :C_WIDTH] * QSCALE_64).astype(BF16)
    qk_ref[:, C_WIDTH:] = qk[:, C_WIDTH:].astype(BF16)
    cums = [_fox_cumsum(z[r * Q_TILE:(r + 1) * Q_TILE], carry_s) for r in range(TILE_R)]
    vt_ref[...] = _dot_nt(wvt_ref[...], h).astype(BF16)
    for r, cum in enumerate(cums):
        caug_ref[r * Q_TILE:(r + 1) * Q_TILE, :] = _fox_decay_parts(cum)


def _tok_spec(width):
    return pl.BlockSpec((TOKEN_TILE, width), lambda i: (i, 0))


def _tok_t_spec(height):
    return pl.BlockSpec((height, TOKEN_TILE), lambda i: (0, i))


def _token_call(name, body, ins, in_specs, outs, out_specs, scratch=()):
    return pl.pallas_call(
        body,
        grid=(TOKENS // TOKEN_TILE,),
        in_specs=in_specs,
        out_specs=out_specs,
        out_shape=outs,
        scratch_shapes=list(scratch) + _ffn_scratch(),
        compiler_params=pltpu.CompilerParams(
            dimension_semantics=("arbitrary",), vmem_limit_bytes=VMEM_LIMIT),
        name=name,
    )(*ins)


def _post_kernel(*refs, n_mix, final, layer):
    x_ref = refs[0]
    o_refs = refs[1:1 + n_mix]
    w_refs = refs[1 + n_mix:1 + 2 * n_mix]
    (g2_ref, win_hbm, wo_hbm, gple_ref, wg_ref, p_ref, wp_ref, gfin_ref,
     out_ref) = refs[1 + 2 * n_mix:10 + 2 * n_mix]
    ffn_s = refs[10 + 2 * n_mix:]
    _load_ffn_weights(layer, win_hbm, wo_hbm, *ffn_s)
    x = x_ref[...]
    for o_ref, w_ref in zip(o_refs, w_refs):
        x = x + _dot(o_ref[...], w_ref[...])
    x = _ffn(x, g2_ref[...], ffn_s[0], ffn_s[1])
    gate = jax.nn.sigmoid(_dot(_rms(x, gple_ref[...]).astype(BF16), wg_ref[...]))
    x = x + gate * _dot(p_ref[...].astype(BF16), wp_ref[...])
    if final:
        x = _rms(x, gfin_ref[...])
    out_ref[...] = x


def _split_bf16(x):
    hi = x.astype(BF16)
    lo = (x - hi.astype(F32)).astype(BF16)
    return hi, lo


def _moba_gate(q, km):
    rows = A_HEADS * N_KBLK
    gt = jnp.concatenate([km] * A_HEADS, axis=0)
    r = lax.broadcasted_iota(jnp.int32, (rows, A_WIDTH), 0)
    c = lax.broadcasted_iota(jnp.int32, (rows, A_WIDTH), 1)
    gt = jnp.where((r // N_KBLK) == (c // A_HEAD_DIM), gt, 0.0)
    g_hi, g_lo = _split_bf16(gt)
    q_hi, q_lo = _split_bf16(q)
    return _dot_nt(g_hi, q_hi) + _dot_nt(g_hi, q_lo) + _dot_nt(g_lo, q_hi)


def _moba_select(gate, own):
    n_idx = lax.broadcasted_iota(jnp.int32, (N_KBLK, Q_TILE), 0)
    pen_rows = []
    for h in range(A_HEADS):
        gh = gate[h * N_KBLK:(h + 1) * N_KBLK]
        rank = jnp.zeros((N_KBLK, Q_TILE), jnp.int32)
        for m in range(N_KBLK):
            gm = gh[m:m + 1]
            beats = (gm > gh) | ((gm == gh) & (m < n_idx))
            rank = rank + jnp.where(beats & (m < own), 1, 0)
        sel = (n_idx < own) & (rank < MOBA_TOPK)
        pen_n = jnp.where(sel, 0.0, NEG_INF)
        pen_d = jnp.full((N_KBLK, Q_TILE), NEG_INF, F32)
        for n in range(N_KBLK):
            pen_d = jnp.where(n_idx == own - n, pen_n[n:n + 1], pen_d)
        pen_rows.append(pen_d)
    pairs = [jnp.concatenate(pen_rows[2 * hp:2 * hp + 2], axis=1) for hp in range(A_HEADS // 2)]
    return jnp.concatenate(pairs, axis=0)


def _t5_bias_kernel(tbl_ref, o_ref):
    hp = pl.program_id(0)
    r = lax.broadcasted_iota(jnp.int32, (Q_TILE, Q_TILE), 0)
    c = lax.broadcasted_iota(jnp.int32, (Q_TILE, Q_TILE), 1)
    max_exact = T5_BUCKETS // 2
    for dd in range(3):
        dist = dd * Q_TILE + c - r
        dc = jnp.maximum(dist, 0)
        df = jnp.maximum(dc.astype(F32), 1.0)
        large = max_exact + (jnp.log(df / max_exact) / math.log(T5_MAX_DIST / max_exact)
                             * (T5_BUCKETS - max_exact)).astype(jnp.int32)
        large = jnp.minimum(large, T5_BUCKETS - 1)
        bucket = jnp.where(dc < max_exact, dc, large)
        for hh in range(2):
            bias = jnp.zeros((Q_TILE, Q_TILE), F32)
            for b in range(T5_BUCKETS):
                bias = jnp.where(bucket == b, tbl_ref[b, 2 * hp + hh], bias)
            bias = bias * LOG2E
            if dd == 0:
                bias = jnp.where(dist >= 0, bias, NEG_INF)
            o_ref[0, dd, :, hh * Q_TILE:(hh + 1) * Q_TILE] = bias


def _fox_cumsum(z, carry_ref):
    lane = lax.broadcasted_iota(jnp.int32, (Q_TILE, LANES), 1)
    logf = jnp.minimum(z, 0.0) - jnp.log1p(jnp.exp(-jnp.abs(z)))
    logf = jnp.where(lane < C_HEADS, logf, 0.0)
    r = lax.broadcasted_iota(jnp.int32, (Q_TILE, Q_TILE), 0)
    c = lax.broadcasted_iota(jnp.int32, (Q_TILE, Q_TILE), 1)
    tri = jnp.where(c <= r, 1.0, 0.0).astype(BF16)
    l1 = logf.astype(BF16)
    rem = logf - l1.astype(F32)
    l2 = rem.astype(BF16)
    l3 = (rem - l2.astype(F32)).astype(BF16)
    cum = _dot(tri, l1) + _dot(tri, l2) + _dot(tri, l3) + carry_ref[...]
    carry_ref[...] = cum[Q_TILE - 1:Q_TILE, :]
    return cum


def _fox_decay_parts(cum):
    cs = cum * LOG2E
    c1 = cs.astype(BF16)
    rem = cs - c1.astype(F32)
    c2 = rem.astype(BF16)
    c3 = (rem - c2.astype(F32)).astype(BF16)
    rin = lax.broadcasted_iota(jnp.int32, (LANES, LANES), 0)
    lout = lax.broadcasted_iota(jnp.int32, (LANES, LANES), 1)
    out = jnp.zeros((Q_TILE, LANES), F32)
    for part, cp in enumerate((c1, c2, c3)):
        place = jnp.where((rin < C_HEADS) & (lout == rin * 8 + part), 1.0, 0.0).astype(BF16)
        out = out + _dot(cp, place)
    return out.astype(BF16)


def _attn_kernel(*refs, kind):
    if kind == "moba":
        q_ref, k_ref, vt_ref, pen_ref, bias_ref, o_ref = refs
    elif kind == "mla":
        q_ref, k_ref, vt_ref, o_ref = refs
    else:
        q_ref, k_ref, vt_ref, c_ref, o_ref, kp_s = refs
    grp = pl.program_id(1)
    lane = lax.broadcasted_iota(jnp.int32, (1, LANES), 1)
    low = lane < A_HEAD_DIM
    kw = LANES if kind == "moba" else 2 * LANES
    qw = 2 * LANES if kind == "mla" else LANES
    k_cat = kp_s if kind == "fox" else k_ref

    if kind == "fox":
        rin = lax.broadcasted_iota(jnp.int32, (LANES, LANES), 0)
        lout = lax.broadcasted_iota(jnp.int32, (LANES, LANES), 1)

        def place(pair, off_a, off_b, val):
            base_a = 2 * pair * 8
            base_b = base_a + 8
            sel_a = (lout >= off_a) & (lout < off_a + 3) & (rin == base_a + lout - off_a)
            sel_b = (lout >= off_b) & (lout < off_b + 3) & (rin == base_b + lout - off_b)
            return jnp.where(sel_a | sel_b, val, 0.0).astype(BF16)

        def ones(off_a, off_b):
            in_a = (lane >= off_a) & (lane < off_a + 3)
            in_b = (lane >= off_b) & (lane < off_b + 3)
            return jnp.where(in_a | in_b, 1.0, 0.0)

        aq_all = []
        for pi in range(PAIR_GROUP):
            ak = (_dot(c_ref[...], place(grp * PAIR_GROUP + pi, A_HEAD_DIM + 3, 3, -1.0))
                  + ones(A_HEAD_DIM, 0)).astype(BF16)
            k = k_ref[:, pi * LANES:(pi + 1) * LANES]
            kp_s[:, pi * kw:pi * kw + LANES] = jnp.where(low, k, ak)
            kp_s[:, pi * kw + LANES:(pi + 1) * kw] = jnp.where(low, ak, k)
            aq_all.append((_dot(c_ref[...], place(grp * PAIR_GROUP + pi, A_HEAD_DIM, 0, 1.0))
                           + ones(A_HEAD_DIM + 3, 3)).astype(BF16))

    def q_operand(tile, pi):
        zero = jnp.zeros((Q_TILE, LANES), BF16)
        rows = slice(tile * Q_TILE, (tile + 1) * Q_TILE)
        q = q_ref[rows, pi * qw:(pi + 1) * qw]
        if kind == "moba":
            return jnp.concatenate([jnp.where(low, q, zero), jnp.where(low, zero, q)], axis=0)
        if kind == "fox":
            aq = aq_all[pi][rows]
            qa, qb = jnp.where(low, q, aq), jnp.where(low, aq, q)
        else:
            qa, qb = q[:, :LANES], q[:, LANES:]
        return jnp.concatenate([jnp.concatenate([qa, zero], axis=1),
                                jnp.concatenate([zero, qb], axis=1)], axis=0)

    def store_out(pi, tile, accs):
        ot = jnp.concatenate([a[:A_HEAD_DIM] / a[A_HEAD_DIM:A_HEAD_DIM + 1] for a in accs], axis=0)
        o_ref[tile * Q_TILE:(tile + 1) * Q_TILE, pi * LANES:(pi + 1) * LANES] = ot.T.astype(BF16)

    def pv(pi, rows, p):
        ones_rows = jnp.ones((BF16_ROWS, rows.stop - rows.start), BF16)
        outs = []
        for hh in range(2):
            lo = pi * LANES + hh * A_HEAD_DIM
            v_aug = jnp.concatenate([vt_ref[lo:lo + A_HEAD_DIM, rows], ones_rows], axis=0)
            outs.append(_dot(v_aug, p[:, hh * Q_TILE:(hh + 1) * Q_TILE]))
        return outs

    if kind == "moba":
        _moba_schedule(q_operand, k_ref, pen_ref, bias_ref, pv, store_out)
    else:
        _online_schedule(q_operand, k_cat, kw, pv, store_out)


def _online_schedule(q_operand, k_cat, kw, pv, store_out):
    key = lax.broadcasted_iota(jnp.int32, (Q_TILE, 2 * Q_TILE), 0)
    qry = lax.broadcasted_iota(jnp.int32, (Q_TILE, 2 * Q_TILE), 1) % Q_TILE

    def key_rows(nb, d):
        return slice((nb - 1 - d) * Q_TILE, (nb - d) * Q_TILE)

    def score(jj):
        out = []
        for pi in range(PAIR_GROUP):
            for nb in (N_KBLK - jj, jj + 1):
                q = q_operand(nb - 1, pi)
                out.append((pi, nb, [_dot_nt(k_cat[key_rows(nb, d), pi * kw:(pi + 1) * kw], q)
                                     for d in range(nb)]))
        return out

    def fold(chains):
        state = {}
        for d in range(max(nb for _, nb, _ in chains)):
            for c, (pi, nb, logits) in enumerate(chains):
                if d >= nb:
                    continue
                sn = logits[d]
                if d == 0:
                    sn = jnp.where(key <= qry, sn, NEG_INF)
                    m_new = jnp.max(sn, axis=0, keepdims=True)
                else:
                    m_old, acc_old = state[c]
                    m_new = jnp.maximum(m_old, jnp.max(sn, axis=0, keepdims=True))
                new = pv(pi, key_rows(nb, d), jnp.exp2(sn - m_new).astype(BF16))
                if d > 0:
                    alpha = jnp.exp2(m_old - m_new)
                    new = [a * alpha[:, hh * Q_TILE:(hh + 1) * Q_TILE] + n
                           for hh, (a, n) in enumerate(zip(acc_old, new))]
                state[c] = (m_new, new)
        for c, (pi, nb, _) in enumerate(chains):
            store_out(pi, nb - 1, state[c][1])

    groups = N_KBLK // 2
    scored = score(0)
    for jj in range(groups):
        upcoming = score(jj + 1) if jj + 1 < groups else None
        fold(scored)
        scored = upcoming


def _moba_schedule(q_operand, k_ref, pen_ref, bias_ref, pv, store_out):
    def new_chain(pi, nb):
        return {"pair": pi, "nb": nb, "q": q_operand(nb - 1, pi), "blocks": [], "shifts": [], "m": None}

    def score_block(ch, n):
        pi, nb = ch["pair"], ch["nb"]
        d = nb - 1 - n
        sn = _dot_nt(k_ref[n * Q_TILE:(n + 1) * Q_TILE, pi * LANES:(pi + 1) * LANES], ch["q"])
        shift = None
        if d < 2:
            sn = sn + bias_ref[pi, d]
        else:
            shift = bias_ref[pi, 2, 0:1, :]
        if d > 0:
            pen = pen_ref[0, nb - 1, pi * N_KBLK + d:pi * N_KBLK + d + 1, :]
            shift = pen if shift is None else shift + pen
        bm = jnp.max(sn, axis=0, keepdims=True)
        if shift is not None:
            bm = bm + shift
        ch["m"] = bm if ch["m"] is None else jnp.maximum(ch["m"], bm)
        ch["blocks"].append(sn)
        ch["shifts"].append(shift)

    def prob_block(ch, n):
        m, sh = ch["m"], ch["shifts"][n]
        return jnp.exp2(ch["blocks"][n] - (m if sh is None else m - sh)).astype(BF16)

    groups = N_KBLK // 2
    prev = []
    for jj in list(range(groups)) + [None]:
        cur = [] if jj is None else [new_chain(pi, nb) for pi in range(PAIR_GROUP)
                                     for nb in (N_KBLK - jj, jj + 1)]
        probs = [[] for _ in prev]
        for n in range(max(ch["nb"] for ch in cur + prev)):
            for ch in cur:
                if n < ch["nb"]:
                    score_block(ch, n)
            for ch, pr in zip(prev, probs):
                if n < ch["nb"]:
                    pr.append(prob_block(ch, n))
        for ch, pr in zip(prev, probs):
            accs = pv(ch["pair"], slice(0, ch["nb"] * Q_TILE), jnp.concatenate(pr, axis=0))
            store_out(ch["pair"], ch["nb"] - 1, accs)
        prev = cur


def _attention(kind, ins, in_specs, n_pairs, extra_scratch=()):
    width = PAIR_GROUP * LANES
    return pl.pallas_call(
        functools.partial(_attn_kernel, kind=kind),
        grid=(BATCH, n_pairs // PAIR_GROUP),
        in_specs=in_specs,
        out_specs=pl.BlockSpec((SEQ, width), lambda b, g: (b, g)),
        out_shape=jax.ShapeDtypeStruct((TOKENS, n_pairs * LANES), BF16),
        scratch_shapes=list(extra_scratch),
        compiler_params=pltpu.CompilerParams(
            dimension_semantics=("arbitrary", "arbitrary"), vmem_limit_bytes=VMEM_LIMIT),
        name="attn_" + kind,
    )(*ins)


def _seq_spec(pair_width, col0):
    return pl.BlockSpec((SEQ, PAIR_GROUP * pair_width), lambda b, g: (b, col0 + g))


def _vt_spec():
    return pl.BlockSpec((PAIR_GROUP * LANES, SEQ), lambda b, g: (g, b))


def _place_heads(w, n_heads, src_stride, src_off, width):
    per_head = w.reshape(w.shape[0], n_heads, src_stride)[:, :, src_off:src_off + width]
    per_head = jnp.pad(per_head, ((0, 0), (0, 0), (0, LANES - width)))
    return per_head.reshape(w.shape[0], n_heads * LANES)


def _row(v, width=None):
    v = v.reshape(1, -1).astype(F32)
    if width is not None and v.shape[1] < width:
        v = jnp.pad(v, ((0, 0), (0, width - v.shape[1])))
    return v


def kernel(x, p, t5_bias, ff1_norm, ff1_w_in, ff1_w_out, mix_norm, ff2_norm, ff2_w_in, ff2_w_out,
           ple_norm, ple_w_gate, ple_w_proj, ab_w_in, mla_q_norm, mla_w_uq, mla_kv_norm, mla_w_ukv,
           ab_w_out, fox_w_in, fox_b_f, fox_w_out, final_norm):
    xt = x.reshape(TOKENS, D_MODEL)

    ffw = {1: (ff1_norm, ff1_w_in, ff1_w_out), 2: (ff2_norm, ff2_w_in, ff2_w_out)}
    hbm = pl.BlockSpec(memory_space=pl.ANY)

    def ffn_args(which, layer):
        norm, w_in, w_out = ffw[which]
        return [_row(norm[layer]), w_in, w_out], [_wspec((1, D_MODEL)), hbm, hbm]

    ple_gate_bf = ple_w_gate.astype(BF16)
    ple_proj_bf = ple_w_proj.astype(BF16)

    def post(xin, mixes, w_outs, layer, final):
        fa, fs = ffn_args(2, layer)
        ws = [w.astype(BF16) for w in w_outs]
        ins = ([xin] + list(mixes) + ws + fa
               + [_row(ple_norm[layer]), ple_gate_bf, p.reshape(DEPTH, TOKENS, PLE_DIM), ple_proj_bf,
                  _row(final_norm)])
        specs = ([_tok_spec(D_MODEL)] + [_tok_spec(m.shape[1]) for m in mixes]
                 + [_wspec(w.shape) for w in ws] + fs
                 + [_wspec((1, D_MODEL)), _wspec((D_MODEL, D_MODEL), layer),
                    pl.BlockSpec((None, TOKEN_TILE, PLE_DIM), lambda t: (layer, t, 0)),
                    _wspec((PLE_DIM, D_MODEL), layer), _wspec((1, D_MODEL))])
        return _token_call(
            "post%d" % layer,
            functools.partial(_post_kernel, n_mix=len(mixes), final=final, layer=layer), ins, specs,
            jax.ShapeDtypeStruct((TOKENS, D_MODEL), F32), _tok_spec(D_MODEL))

    w_ab = ab_w_in[0]
    w_qk = w_ab[:, :2 * A_WIDTH].astype(BF16)
    w_vt = w_ab[:, 2 * A_WIDTH:3 * A_WIDTH].T.astype(BF16)
    c0 = 3 * A_WIDTH
    kr0 = MLA_Q_RANK + MLA_KV_RANK
    w_c = jnp.concatenate(
        [w_ab[:, c0:c0 + kr0], jnp.zeros((D_MODEL, MLA_NOPE), F32), w_ab[:, c0 + kr0:],
         jnp.zeros((D_MODEL, LANES - MLA_NOPE - MLA_ROPE), F32)], axis=1).astype(BF16)
    w_uq = _place_heads(mla_w_uq[0], B_HEADS, MLA_NOPE + MLA_ROPE, 0, MLA_NOPE + MLA_ROPE).astype(BF16)
    per_head = mla_w_uq[0].reshape(MLA_Q_RANK, B_HEADS, MLA_NOPE + MLA_ROPE)
    x1_cols = per_head[:, :, MLA_NOPE:MLA_NOPE + MLA_ROPE // 2]
    x2_cols = per_head[:, :, MLA_NOPE + MLA_ROPE // 2:]
    partner_cols = jnp.concatenate([jnp.zeros_like(per_head[:, :, :MLA_NOPE]), x2_cols, x1_cols], axis=2)
    w_uq_rot = _place_heads(partner_cols.reshape(MLA_Q_RANK, -1), B_HEADS, MLA_NOPE + MLA_ROPE, 0,
                            MLA_NOPE + MLA_ROPE).astype(BF16)
    w_ukv_k = _place_heads(mla_w_ukv[0], B_HEADS, MLA_NOPE + MLA_V, 0, MLA_NOPE).astype(BF16)
    w_ukv = mla_w_ukv[0].reshape(MLA_KV_RANK, B_HEADS, MLA_NOPE + MLA_V)
    w_ukv_vt = w_ukv[:, :, MLA_NOPE:].reshape(MLA_KV_RANK, B_HEADS * MLA_V).T.astype(BF16)
    half = MLA_ROPE // 2
    inv = ROPE_THETA ** (-np.arange(half, dtype=np.float64) / half)
    inv_lane = np.zeros((1, LANES), np.float32)
    inv_lane[0, MLA_NOPE:MLA_NOPE + half] = inv
    inv_lane[0, MLA_NOPE + half:MLA_NOPE + MLA_ROPE] = inv
    rope_tbl = pl.pallas_call(
        _rope_table_kernel,
        grid=(SEQ // TOKEN_TILE,),
        in_specs=[pl.BlockSpec((1, LANES), lambda i: (0, 0))],
        out_specs=pl.BlockSpec((4, TOKEN_TILE, LANES), lambda i: (0, i, 0)),
        out_shape=jax.ShapeDtypeStruct((4, SEQ, LANES), F32),
        name="rope_tables",
    )(jnp.asarray(inv_lane))

    fa, fs = ffn_args(1, 0)
    ins = ([xt] + fa + [_row(mix_norm[0]), w_qk, w_vt, w_c, _row(mla_q_norm[0]), w_uq, w_uq_rot,
                        _row(mla_kv_norm[0]), w_ukv_k, w_ukv_vt, rope_tbl])
    specs = ([_tok_spec(D_MODEL)] + fs
             + [_wspec((1, D_MODEL)), _wspec(w_qk.shape), _wspec(w_vt.shape), _wspec(w_c.shape),
                _wspec((1, MLA_Q_RANK)), _wspec(w_uq.shape), _wspec(w_uq_rot.shape),
                _wspec((1, MLA_KV_RANK)),
                _wspec(w_ukv_k.shape), _wspec(w_ukv_vt.shape),
                pl.BlockSpec((4, TOKEN_TILE, LANES), lambda t: (0, t % (SEQ // TOKEN_TILE), 0))])
    pen_rows = A_HEADS // 2 * N_KBLK
    outs = (jax.ShapeDtypeStruct((TOKENS, D_MODEL), F32),
            jax.ShapeDtypeStruct((TOKENS, 2 * A_WIDTH), BF16),
            jax.ShapeDtypeStruct((A_WIDTH, TOKENS), BF16),
            jax.ShapeDtypeStruct((BATCH, N_KBLK, pen_rows, 2 * Q_TILE), F32),
            jax.ShapeDtypeStruct((TOKENS, B_HEADS * LANES), BF16),
            jax.ShapeDtypeStruct((TOKENS, B_HEADS * LANES), BF16),
            jax.ShapeDtypeStruct((B_HEADS * MLA_V, TOKENS), BF16))
    out_specs = (_tok_spec(D_MODEL), _tok_spec(2 * A_WIDTH), _tok_t_spec(A_WIDTH),
                 pl.BlockSpec((1, TILE_R, pen_rows, 2 * Q_TILE),
                              lambda t: (t // (N_KBLK // TILE_R), t % (N_KBLK // TILE_R), 0, 0)),
                 _tok_spec(B_HEADS * LANES), _tok_spec(B_HEADS * LANES),
                 _tok_t_spec(B_HEADS * MLA_V))
    x1, qk_a, vt_a, pen, q_mla, k_mla, vt_mla = _token_call(
        "pre0", functools.partial(_pre0_kernel, layer=0), ins, specs, outs, out_specs,
        scratch=[pltpu.VMEM((N_KBLK, A_WIDTH), F32)])

    bias = pl.pallas_call(
        _t5_bias_kernel,
        grid=(A_HEADS // 2,),
        in_specs=[pl.BlockSpec(memory_space=pltpu.SMEM)],
        out_specs=pl.BlockSpec((1, 3, Q_TILE, 2 * Q_TILE), lambda h: (h, 0, 0, 0)),
        out_shape=jax.ShapeDtypeStruct((A_HEADS // 2, 3, Q_TILE, 2 * Q_TILE), F32),
        name="t5_bias_tiles",
    )(t5_bias.astype(F32))

    na = A_HEADS // 2
    o_a = _attention(
        "moba", [qk_a, qk_a, vt_a, pen, bias],
        [_seq_spec(LANES, 0), _seq_spec(LANES, na // PAIR_GROUP), _vt_spec(),
         pl.BlockSpec((1, N_KBLK, PAIR_GROUP * N_KBLK, 2 * Q_TILE), lambda b, g: (b, 0, g, 0)),
         pl.BlockSpec((PAIR_GROUP, 3, Q_TILE, 2 * Q_TILE), lambda b, g: (g, 0, 0, 0))], na)
    o_b = _attention(
        "mla", [q_mla, k_mla, vt_mla],
        [_seq_spec(2 * LANES, 0), _seq_spec(2 * LANES, 0), _vt_spec()], B_HEADS // 2)
    w_o = ab_w_out[0]
    xt = post(x1, [o_a, o_b], [w_o[:A_WIDTH], w_o[A_WIDTH:]], 0, DEPTH == 1)

    w_fox = fox_w_in[0]
    w_qk = w_fox[:, :2 * C_WIDTH].astype(BF16)
    w_vt = w_fox[:, 2 * C_WIDTH:3 * C_WIDTH].T.astype(BF16)
    w_f = jnp.pad(w_fox[:, 3 * C_WIDTH:], ((0, 0), (0, LANES - C_HEADS))).astype(BF16)
    fa, fs = ffn_args(1, 1)
    ins = [xt] + fa + [_row(mix_norm[1]), w_qk, w_vt, w_f, _row(fox_b_f[0], LANES)]
    specs = ([_tok_spec(D_MODEL)] + fs
             + [_wspec((1, D_MODEL)), _wspec(w_qk.shape), _wspec(w_vt.shape), _wspec(w_f.shape),
                _wspec((1, LANES))])
    outs = (jax.ShapeDtypeStruct((TOKENS, D_MODEL), F32),
            jax.ShapeDtypeStruct((TOKENS, 2 * C_WIDTH), BF16),
            jax.ShapeDtypeStruct((C_WIDTH, TOKENS), BF16),
            jax.ShapeDtypeStruct((TOKENS, LANES), BF16))
    out_specs = (_tok_spec(D_MODEL), _tok_spec(2 * C_WIDTH), _tok_t_spec(C_WIDTH), _tok_spec(LANES))
    x1, qk_c, vt_c, caug = _token_call("pre1", functools.partial(_pre1_kernel, layer=1), ins, specs, outs, out_specs,
                                       scratch=[pltpu.VMEM((1, LANES), F32)])

    nc = C_HEADS // 2
    o_c = _attention(
        "fox", [qk_c, qk_c, vt_c, caug],
        [_seq_spec(LANES, 0), _seq_spec(LANES, nc // PAIR_GROUP), _vt_spec(),
         pl.BlockSpec((SEQ, LANES), lambda b, g: (b, 0))],
        nc, extra_scratch=[pltpu.VMEM((SEQ, PAIR_GROUP * 2 * LANES), BF16)])
    xt = post(x1, [o_c], [fox_w_out[0]], 1, True)
    return xt.reshape(BATCH, SEQ, D_MODEL)
```

```python
import functools
import math

import jax
import jax.numpy as jnp
import numpy as np
from jax import lax
from jax.experimental import pallas as pl
from jax.experimental.pallas import tpu as pltpu

F32 = jnp.float32
BF16 = jnp.bfloat16

D_MODEL = 1024
BATCH = 8
SEQ = 2048
DEPTH = 2
PLE_DIM = 256
D_FF = 2816
EPS = 1e-6

A_HEADS = 8
A_HEAD_DIM = 64
MOBA_BLOCK = 256
MOBA_TOPK = 3

B_HEADS = 8
MLA_Q_RANK = 256
MLA_KV_RANK = 128
MLA_NOPE = 64
MLA_ROPE = 32
MLA_V = 64
ROPE_THETA = 10000.0

T5_BUCKETS = 32
T5_MAX_DIST = 128

C_HEADS = 16
C_HEAD_DIM = 64

A_WIDTH = A_HEADS * A_HEAD_DIM
C_WIDTH = C_HEADS * C_HEAD_DIM

TOKENS = BATCH * SEQ
LANES = 128
BF16_ROWS = 16
PAIR_GROUP = 2
Q_TILE = MOBA_BLOCK
N_KBLK = SEQ // Q_TILE
TOKEN_TILE = 2 * Q_TILE
TILE_R = TOKEN_TILE // Q_TILE
FF_CHUNK = D_FF // 2
W_IN_CHUNK = 512
W_OUT_CHUNK = 256
VMEM_LIMIT = 56 * 1024 * 1024
NEG_INF = float("-inf")
LOG2E = math.log2(math.e)
QSCALE_64 = A_HEAD_DIM ** -0.5 * LOG2E
QSCALE_MLA = (MLA_NOPE + MLA_ROPE) ** -0.5 * LOG2E

assert T5_MAX_DIST <= Q_TILE + 1
assert SEQ % TOKEN_TILE == 0 and TOKEN_TILE % Q_TILE == 0 and N_KBLK % 2 == 0


def _wspec(shape, layer=None):
    nd = len(shape)
    if layer is None:
        return pl.BlockSpec(shape, lambda *_: (0,) * nd, pipeline_mode=pl.Buffered(1))
    return pl.BlockSpec((None,) + tuple(shape), lambda *_: (layer,) + (0,) * nd,
                        pipeline_mode=pl.Buffered(1))


def _dot(a, b):
    return jnp.dot(a, b, preferred_element_type=F32)


def _dot_nt(a, b):
    return lax.dot_general(a, b, (((1,), (1,)), ((), ())), preferred_element_type=F32)


def _rms(x, g):
    return x * lax.rsqrt(jnp.mean(x * x, axis=-1, keepdims=True) + EPS) * g


def _ffn_scratch():
    return [pltpu.VMEM((D_MODEL, 2 * D_FF), BF16), pltpu.VMEM((D_FF, D_MODEL), BF16),
            pltpu.VMEM((2, D_MODEL, W_IN_CHUNK), F32), pltpu.VMEM((2, W_OUT_CHUNK, D_MODEL), F32),
            pltpu.SemaphoreType.DMA((2, 2))]


def _load_ffn_weights(layer, win_hbm, wo_hbm, win_ref, wo_ref, stage_in, stage_out, sem):
    def copy_in(c, slot):
        return pltpu.make_async_copy(win_hbm.at[layer, :, pl.ds(c * W_IN_CHUNK, W_IN_CHUNK)],
                                     stage_in.at[slot], sem.at[0, slot])

    def copy_out(c, slot):
        return pltpu.make_async_copy(wo_hbm.at[layer, pl.ds(c * W_OUT_CHUNK, W_OUT_CHUNK), :],
                                     stage_out.at[slot], sem.at[1, slot])

    @pl.when(pl.program_id(0) == 0)
    def _():
        n_in = 2 * D_FF // W_IN_CHUNK
        n_out = D_FF // W_OUT_CHUNK
        copy_in(0, 0).start()
        copy_out(0, 0).start()
        for c in range(n_in):
            slot = c % 2
            if c + 1 < n_in:
                copy_in(c + 1, 1 - slot).start()
            copy_in(c, slot).wait()
            win_ref[:, c * W_IN_CHUNK:(c + 1) * W_IN_CHUNK] = stage_in[slot].astype(BF16)
        for c in range(n_out):
            slot = c % 2
            if c + 1 < n_out:
                copy_out(c + 1, 1 - slot).start()
            copy_out(c, slot).wait()
            wo_ref[c * W_OUT_CHUNK:(c + 1) * W_OUT_CHUNK, :] = stage_out[slot].astype(BF16)


def _ffn(x, g, win_ref, wo_ref):
    h = _rms(x, g).astype(BF16)
    acts = []
    for c in range(D_FF // FF_CHUNK):
        lo, hi = c * FF_CHUNK, (c + 1) * FF_CHUNK
        a = _dot(h, win_ref[:, lo:hi])
        u = _dot(h, win_ref[:, D_FF + lo:D_FF + hi])
        acts.append((a * jax.nn.sigmoid(a) * u).astype(BF16))
    return x + 0.5 * _dot(jnp.concatenate(acts, axis=1), wo_ref[...])


def _rope_table_kernel(inv_ref, o_ref):
    pos0 = pl.program_id(0) * TOKEN_TILE
    pos = (pos0 + lax.broadcasted_iota(jnp.int32, (TOKEN_TILE, LANES), 0)).astype(F32)
    lane = lax.broadcasted_iota(jnp.int32, (TOKEN_TILE, LANES), 1)
    ang = pos * inv_ref[...]
    is_x1 = (lane >= MLA_NOPE) & (lane < MLA_NOPE + MLA_ROPE // 2)
    is_x2 = (lane >= MLA_NOPE + MLA_ROPE // 2) & (lane < MLA_NOPE + MLA_ROPE)
    cos_t = jnp.where(is_x1 | is_x2, jnp.cos(ang), 1.0)
    sin = jnp.sin(ang)
    sin_t = jnp.where(is_x1, -sin, jnp.where(is_x2, sin, 0.0))
    o_ref[0] = cos_t
    o_ref[1] = sin_t
    o_ref[2] = cos_t * QSCALE_MLA
    o_ref[3] = sin_t * QSCALE_MLA


def _rope_block(xb, cos_t, sin_t):
    half = MLA_ROPE // 2
    lane = lax.broadcasted_iota(jnp.int32, xb.shape, 1)
    is_x1 = lane < MLA_NOPE + half
    partner = jnp.where(is_x1, pltpu.roll(xb, LANES - half, 1), pltpu.roll(xb, half, 1))
    return xb * cos_t + partner * sin_t


def _pre0_kernel(x_ref, g1_ref, win_hbm, wo_hbm, gmix_ref, wqk_ref, wvt_ref, wc_ref,
                 qn_ref, wuq_ref, wuqrot_ref, kvn_ref, wukvk_ref, wukvvt_ref, rope_ref,
                 x1_ref, qk_ref, vt_ref, pen_ref, qm_ref, kmla_ref, vmt_ref, km_s, *ffn_s, layer):
    t = pl.program_id(0)
    _load_ffn_weights(layer, win_hbm, wo_hbm, *ffn_s)
    x1 = _ffn(x_ref[...], g1_ref[...], ffn_s[0], ffn_s[1])
    x1_ref[...] = x1
    h = _rms(x1, gmix_ref[...]).astype(BF16)
    qk = _dot(h, wqk_ref[...])
    q = qk[:, :A_WIDTH]
    k = qk[:, A_WIDTH:]
    qk_ref[:, :A_WIDTH] = (q * QSCALE_64).astype(BF16)
    qk_ref[:, A_WIDTH:] = k.astype(BF16)

    @pl.when(t == 0)
    def _():
        km_s[...] = jnp.zeros_like(km_s)

    c = _dot(h, wc_ref[...])
    gates = []
    for r in range(TILE_R):
        own = (t * TILE_R + r) % N_KBLK
        part = slice(r * Q_TILE, (r + 1) * Q_TILE)
        km_s[pl.ds(own, 1), :] = jnp.mean(k[part], axis=0, keepdims=True)
        gates.append((own, _moba_gate(q[part], km_s[...])))
    vt_ref[...] = _dot_nt(wvt_ref[...], h).astype(BF16)
    for r, (own, gate) in enumerate(gates):
        pen_ref[0, r] = _moba_select(gate, own)

    cq = c[:, :MLA_Q_RANK]
    ckv = c[:, MLA_Q_RANK:MLA_Q_RANK + MLA_KV_RANK]
    kr = c[:, MLA_Q_RANK + MLA_KV_RANK:]
    cqn = _rms(cq, qn_ref[...]).astype(BF16)
    ckvn = _rms(ckv, kvn_ref[...]).astype(BF16)
    qm = _dot(cqn, wuq_ref[...])
    qm_partner = _dot(cqn, wuqrot_ref[...])
    kn = _dot(ckvn, wukvk_ref[...])
    vmt_ref[...] = _dot_nt(wukvvt_ref[...], ckvn).astype(BF16)

    krr = _rope_block(kr, rope_ref[0], rope_ref[1])
    for hb in range(B_HEADS):
        sl = slice(hb * LANES, (hb + 1) * LANES)
        qm_ref[:, sl] = (qm[:, sl] * rope_ref[2] + qm_partner[:, sl] * rope_ref[3]).astype(BF16)
        kmla_ref[:, sl] = (kn[:, sl] + krr).astype(BF16)


def _pre1_kernel(x_ref, g1_ref, win_hbm, wo_hbm, gmix_ref, wqk_ref, wvt_ref, wf_ref, bf_ref,
                 x1_ref, qk_ref, vt_ref, caug_ref, carry_s, *ffn_s, layer):
    t = pl.program_id(0)
    _load_ffn_weights(layer, win_hbm, wo_hbm, *ffn_s)
    x1 = _ffn(x_ref[...], g1_ref[...], ffn_s[0], ffn_s[1])
    x1_ref[...] = x1
    h = _rms(x1, gmix_ref[...]).astype(BF16)

    @pl.when(t % (N_KBLK // TILE_R) == 0)
    def _():
        carry_s[...] = jnp.zeros_like(carry_s)

    z = _dot(h, wf_ref[...]) + bf_ref[...]
    qk = _dot(h, wqk_ref[...])
    qk_ref[:, :C_WIDTH] = (qk[:, :C_WIDTH] * QSCALE_64).astype(BF16)
    qk_ref[:, C_WIDTH:] = qk[:, C_WIDTH:].astype(BF16)
    cums = [_fox_cumsum(z[r * Q_TILE:(r + 1) * Q_TILE], carry_s) for r in range(TILE_R)]
    vt_ref[...] = _dot_nt(wvt_ref[...], h).astype(BF16)
    for r, cum in enumerate(cums):
        caug_ref[r * Q_TILE:(r + 1) * Q_TILE, :] = _fox_decay_parts(cum)


def _tok_spec(width):
    return pl.BlockSpec((TOKEN_TILE, width), lambda i: (i, 0))


def _tok_t_spec(height):
    return pl.BlockSpec((height, TOKEN_TILE), lambda i: (0, i))


def _token_call(name, body, ins, in_specs, outs, out_specs, scratch=()):
    return pl.pallas_call(
        body,
        grid=(TOKENS // TOKEN_TILE,),
        in_specs=in_specs,
        out_specs=out_specs,
        out_shape=outs,
        scratch_shapes=list(scratch) + _ffn_scratch(),
        compiler_params=pltpu.CompilerParams(
            dimension_semantics=("arbitrary",), vmem_limit_bytes=VMEM_LIMIT),
        name=name,
    )(*ins)


def _post_kernel(*refs, n_mix, final, layer):
    x_ref = refs[0]
    o_refs = refs[1:1 + n_mix]
    w_refs = refs[1 + n_mix:1 + 2 * n_mix]
    (g2_ref, win_hbm, wo_hbm, gple_ref, wg_ref, p_ref, wp_ref, gfin_ref,
     out_ref) = refs[1 + 2 * n_mix:10 + 2 * n_mix]
    ffn_s = refs[10 + 2 * n_mix:]
    _load_ffn_weights(layer, win_hbm, wo_hbm, *ffn_s)
    x = x_ref[...]
    for o_ref, w_ref in zip(o_refs, w_refs):
        x = x + _dot(o_ref[...], w_ref[...])
    x = _ffn(x, g2_ref[...], ffn_s[0], ffn_s[1])
    gate = jax.nn.sigmoid(_dot(_rms(x, gple_ref[...]).astype(BF16), wg_ref[...]))
    x = x + gate * _dot(p_ref[...].astype(BF16), wp_ref[...])
    if final:
        x = _rms(x, gfin_ref[...])
    out_ref[...] = x


def _split_bf16(x):
    hi = x.astype(BF16)
    lo = (x - hi.astype(F32)).astype(BF16)
    return hi, lo


def _moba_gate(q, km):
    rows = A_HEADS * N_KBLK
    gt = jnp.concatenate([km] * A_HEADS, axis=0)
    r = lax.broadcasted_iota(jnp.int32, (rows, A_WIDTH), 0)
    c = lax.broadcasted_iota(jnp.int32, (rows, A_WIDTH), 1)
    gt = jnp.where((r // N_KBLK) == (c // A_HEAD_DIM), gt, 0.0)
    g_hi, g_lo = _split_bf16(gt)
    q_hi, q_lo = _split_bf16(q)
    return _dot_nt(g_hi, q_hi) + _dot_nt(g_hi, q_lo) + _dot_nt(g_lo, q_hi)


def _moba_select(gate, own):
    n_idx = lax.broadcasted_iota(jnp.int32, (N_KBLK, Q_TILE), 0)
    pen_rows = []
    for h in range(A_HEADS):
        gh = gate[h * N_KBLK:(h + 1) * N_KBLK]
        rank = jnp.zeros((N_KBLK, Q_TILE), jnp.int32)
        for m in range(N_KBLK):
            gm = gh[m:m + 1]
            beats = (gm > gh) | ((gm == gh) & (m < n_idx))
            rank = rank + jnp.where(beats & (m < own), 1, 0)
        sel = (n_idx < own) & (rank < MOBA_TOPK)
        pen_n = jnp.where(sel, 0.0, NEG_INF)
        pen_d = jnp.full((N_KBLK, Q_TILE), NEG_INF, F32)
        for n in range(N_KBLK):
            pen_d = jnp.where(n_idx == own - n, pen_n[n:n + 1], pen_d)
        pen_rows.append(pen_d)
    pairs = [jnp.concatenate(pen_rows[2 * hp:2 * hp + 2], axis=1) for hp in range(A_HEADS // 2)]
    return jnp.concatenate(pairs, axis=0)


def _t5_bias_kernel(tbl_ref, o_ref):
    hp = pl.program_id(0)
    r = lax.broadcasted_iota(jnp.int32, (Q_TILE, Q_TILE), 0)
    c = lax.broadcasted_iota(jnp.int32, (Q_TILE, Q_TILE), 1)
    max_exact = T5_BUCKETS // 2
    for dd in range(3):
        dist = dd * Q_TILE + c - r
        dc = jnp.maximum(dist, 0)
        df = jnp.maximum(dc.astype(F32), 1.0)
        large = max_exact + (jnp.log(df / max_exact) / math.log(T5_MAX_DIST / max_exact)
                             * (T5_BUCKETS - max_exact)).astype(jnp.int32)
        large = jnp.minimum(large, T5_BUCKETS - 1)
        bucket = jnp.where(dc < max_exact, dc, large)
        for hh in range(2):
            bias = jnp.zeros((Q_TILE, Q_TILE), F32)
            for b in range(T5_BUCKETS):
                bias = jnp.where(bucket == b, tbl_ref[b, 2 * hp + hh], bias)
            bias = bias * LOG2E
            if dd == 0:
                bias = jnp.where(dist >= 0, bias, NEG_INF)
            o_ref[0, dd, :, hh * Q_TILE:(hh + 1) * Q_TILE] = bias


def _fox_cumsum(z, carry_ref):
    lane = lax.broadcasted_iota(jnp.int32, (Q_TILE, LANES), 1)
    logf = jnp.minimum(z, 0.0) - jnp.log1p(jnp.exp(-jnp.abs(z)))
    logf = jnp.where(lane < C_HEADS, logf, 0.0)
    r = lax.broadcasted_iota(jnp.int32, (Q_TILE, Q_TILE), 0)
    c = lax.broadcasted_iota(jnp.int32, (Q_TILE, Q_TILE), 1)
    tri = jnp.where(c <= r, 1.0, 0.0).astype(BF16)
    l1 = logf.astype(BF16)
    rem = logf - l1.astype(F32)
    l2 = rem.astype(BF16)
    l3 = (rem - l2.astype(F32)).astype(BF16)
    cum = _dot(tri, l1) + _dot(tri, l2) + _dot(tri, l3) + carry_ref[...]
    carry_ref[...] = cum[Q_TILE - 1:Q_TILE, :]
    return cum


def _fox_decay_parts(cum):
    cs = cum * LOG2E
    c1 = cs.astype(BF16)
    rem = cs - c1.astype(F32)
    c2 = rem.astype(BF16)
    c3 = (rem - c2.astype(F32)).astype(BF16)
    rin = lax.broadcasted_iota(jnp.int32, (LANES, LANES), 0)
    lout = lax.broadcasted_iota(jnp.int32, (LANES, LANES), 1)
    out = jnp.zeros((Q_TILE, LANES), F32)
    for part, cp in enumerate((c1, c2, c3)):
        place = jnp.where((rin < C_HEADS) & (lout == rin * 8 + part), 1.0, 0.0).astype(BF16)
        out = out + _dot(cp, place)
    return out.astype(BF16)


def _attn_kernel(*refs, kind):
    if kind == "moba":
        q_ref, k_ref, vt_ref, pen_ref, bias_ref, o_ref = refs
    elif kind == "mla":
        q_ref, k_ref, vt_ref, o_ref = refs
    else:
        q_ref, k_ref, vt_ref, c_ref, o_ref, kp_s = refs
    grp = pl.program_id(1)
    lane = lax.broadcasted_iota(jnp.int32, (1, LANES), 1)
    low = lane < A_HEAD_DIM
    kw = LANES if kind == "moba" else 2 * LANES
    qw = 2 * LANES if kind == "mla" else LANES
    k_cat = kp_s if kind == "fox" else k_ref

    if kind == "fox":
        rin = lax.broadcasted_iota(jnp.int32, (LANES, LANES), 0)
        lout = lax.broadcasted_iota(jnp.int32, (LANES, LANES), 1)

        def place(pair, off_a, off_b, val):
            base_a = 2 * pair * 8
            base_b = base_a + 8
            sel_a = (lout >= off_a) & (lout < off_a + 3) & (rin == base_a + lout - off_a)
            sel_b = (lout >= off_b) & (lout < off_b + 3) & (rin == base_b + lout - off_b)
            return jnp.where(sel_a | sel_b, val, 0.0).astype(BF16)

        def ones(off_a, off_b):
            in_a = (lane >= off_a) & (lane < off_a + 3)
            in_b = (lane >= off_b) & (lane < off_b + 3)
            return jnp.where(in_a | in_b, 1.0, 0.0)

        aq_all = []
        for pi in range(PAIR_GROUP):
            ak = (_dot(c_ref[...], place(grp * PAIR_GROUP + pi, A_HEAD_DIM + 3, 3, -1.0))
                  + ones(A_HEAD_DIM, 0)).astype(BF16)
            k = k_ref[:, pi * LANES:(pi + 1) * LANES]
            kp_s[:, pi * kw:pi * kw + LANES] = jnp.where(low, k, ak)
            kp_s[:, pi * kw + LANES:(pi + 1) * kw] = jnp.where(low, ak, k)
            aq_all.append((_dot(c_ref[...], place(grp * PAIR_GROUP + pi, A_HEAD_DIM, 0, 1.0))
                           + ones(A_HEAD_DIM + 3, 3)).astype(BF16))

    def q_operand(tile, pi):
        zero = jnp.zeros((Q_TILE, LANES), BF16)
        rows = slice(tile * Q_TILE, (tile + 1) * Q_TILE)
        q = q_ref[rows, pi * qw:(pi + 1) * qw]
        if kind == "moba":
            return jnp.concatenate([jnp.where(low, q, zero), jnp.where(low, zero, q)], axis=0)
        if kind == "fox":
            aq = aq_all[pi][rows]
            qa, qb = jnp.where(low, q, aq), jnp.where(low, aq, q)
        else:
            qa, qb = q[:, :LANES], q[:, LANES:]
        return jnp.concatenate([jnp.concatenate([qa, zero], axis=1),
                                jnp.concatenate([zero, qb], axis=1)], axis=0)

    def store_out(pi, tile, accs):
        ot = jnp.concatenate([a[:A_HEAD_DIM] / a[A_HEAD_DIM:A_HEAD_DIM + 1] for a in accs], axis=0)
        o_ref[tile * Q_TILE:(tile + 1) * Q_TILE, pi * LANES:(pi + 1) * LANES] = ot.T.astype(BF16)

    def pv(pi, rows, p):
        ones_rows = jnp.ones((BF16_ROWS, rows.stop - rows.start), BF16)
        outs = []
        for hh in range(2):
            lo = pi * LANES + hh * A_HEAD_DIM
            v_aug = jnp.concatenate([vt_ref[lo:lo + A_HEAD_DIM, rows], ones_rows], axis=0)
            outs.append(_dot(v_aug, p[:, hh * Q_TILE:(hh + 1) * Q_TILE]))
        return outs

    if kind == "moba":
        _moba_schedule(q_operand, k_ref, pen_ref, bias_ref, pv, store_out)
    else:
        _online_schedule(q_operand, k_cat, kw, pv, store_out)


def _online_schedule(q_operand, k_cat, kw, pv, store_out):
    key = lax.broadcasted_iota(jnp.int32, (Q_TILE, 2 * Q_TILE), 0)
    qry = lax.broadcasted_iota(jnp.int32, (Q_TILE, 2 * Q_TILE), 1) % Q_TILE

    def key_rows(nb, d):
        return slice((nb - 1 - d) * Q_TILE, (nb - d) * Q_TILE)

    def score(jj):
        out = []
        for pi in range(PAIR_GROUP):
            for nb in (N_KBLK - jj, jj + 1):
                q = q_operand(nb - 1, pi)
                out.append((pi, nb, [_dot_nt(k_cat[key_rows(nb, d), pi * kw:(pi + 1) * kw], q)
                                     for d in range(nb)]))
        return out

    def fold(chains):
        state = {}
        for d in range(max(nb for _, nb, _ in chains)):
            for c, (pi, nb, logits) in enumerate(chains):
                if d >= nb:
                    continue
                sn = logits[d]
                if d == 0:
                    sn = jnp.where(key <= qry, sn, NEG_INF)
                    m_new = jnp.max(sn, axis=0, keepdims=True)
                else:
                    m_old, acc_old = state[c]
                    m_new = jnp.maximum(m_old, jnp.max(sn, axis=0, keepdims=True))
                new = pv(pi, key_rows(nb, d), jnp.exp2(sn - m_new).astype(BF16))
                if d > 0:
                    alpha = jnp.exp2(m_old - m_new)
                    new = [a * alpha[:, hh * Q_TILE:(hh + 1) * Q_TILE] + n
                           for hh, (a, n) in enumerate(zip(acc_old, new))]
                state[c] = (m_new, new)
        for c, (pi, nb, _) in enumerate(chains):
            store_out(pi, nb - 1, state[c][1])

    groups = N_KBLK // 2
    scored = score(0)
    for jj in range(groups):
        upcoming = score(jj + 1) if jj + 1 < groups else None
        fold(scored)
        scored = upcoming


def _moba_schedule(q_operand, k_ref, pen_ref, bias_ref, pv, store_out):
    def new_chain(pi, nb):
        return {"pair": pi, "nb": nb, "q": q_operand(nb - 1, pi), "blocks": [], "shifts": [], "m": None}

    def score_block(ch, n):
        pi, nb = ch["pair"], ch["nb"]
        d = nb - 1 - n
        sn = _dot_nt(k_ref[n * Q_TILE:(n + 1) * Q_TILE, pi * LANES:(pi + 1) * LANES], ch["q"])
        shift = None
        if d < 2:
            sn = sn + bias_ref[pi, d]
        else:
            shift = bias_ref[pi, 2, 0:1, :]
        if d > 0:
            pen = pen_ref[0, nb - 1, pi * N_KBLK + d:pi * N_KBLK + d + 1, :]
            shift = pen if shift is None else shift + pen
        bm = jnp.max(sn, axis=0, keepdims=True)
        if shift is not None:
            bm = bm + shift
        ch["m"] = bm if ch["m"] is None else jnp.maximum(ch["m"], bm)
        ch["blocks"].append(sn)
        ch["shifts"].append(shift)

    def prob_block(ch, n):
        m, sh = ch["m"], ch["shifts"][n]
        return jnp.exp2(ch["blocks"][n] - (m if sh is None else m - sh)).astype(BF16)

    groups = N_KBLK // 2
    prev = []
    for jj in list(range(groups)) + [None]:
        cur = [] if jj is None else [new_chain(pi, nb) for pi in range(PAIR_GROUP)
                                     for nb in (N_KBLK - jj, jj + 1)]
        probs = [[] for _ in prev]
        for n in range(max(ch["nb"] for ch in cur + prev)):
            for ch in cur:
                if n < ch["nb"]:
                    score_block(ch, n)
            for ch, pr in zip(prev, probs):
                if n < ch["nb"]:
                    pr.append(prob_block(ch, n))
        for ch, pr in zip(prev, probs):
            accs = pv(ch["pair"], slice(0, ch["nb"] * Q_TILE), jnp.concatenate(pr, axis=0))
            store_out(ch["pair"], ch["nb"] - 1, accs)
        prev = cur


def _attention(kind, ins, in_specs, n_pairs, extra_scratch=()):
    width = PAIR_GROUP * LANES
    return pl.pallas_call(
        functools.partial(_attn_kernel, kind=kind),
        grid=(BATCH, n_pairs // PAIR_GROUP),
        in_specs=in_specs,
        out_specs=pl.BlockSpec((SEQ, width), lambda b, g: (b, g)),
        out_shape=jax.ShapeDtypeStruct((TOKENS, n_pairs * LANES), BF16),
        scratch_shapes=list(extra_scratch),
        compiler_params=pltpu.CompilerParams(
            dimension_semantics=("arbitrary", "arbitrary"), vmem_limit_bytes=VMEM_LIMIT),
        name="attn_" + kind,
    )(*ins)


def _seq_spec(pair_width, col0):
    return pl.BlockSpec((SEQ, PAIR_GROUP * pair_width), lambda b, g: (b, col0 + g))


def _vt_spec():
    return pl.BlockSpec((PAIR_GROUP * LANES, SEQ), lambda b, g: (g, b))


def _place_heads(w, n_heads, src_stride, src_off, width):
    per_head = w.reshape(w.shape[0], n_heads, src_stride)[:, :, src_off:src_off + width]
    per_head = jnp.pad(per_head, ((0, 0), (0, 0), (0, LANES - width)))
    return per_head.reshape(w.shape[0], n_heads * LANES)


def _row(v, width=None):
    v = v.reshape(1, -1).astype(F32)
    if width is not None and v.shape[1] < width:
        v = jnp.pad(v, ((0, 0), (0, width - v.shape[1])))
    return v


def kernel(x, p, t5_bias, ff1_norm, ff1_w_in, ff1_w_out, mix_norm, ff2_norm, ff2_w_in, ff2_w_out,
           ple_norm, ple_w_gate, ple_w_proj, ab_w_in, mla_q_norm, mla_w_uq, mla_kv_norm, mla_w_ukv,
           ab_w_out, fox_w_in, fox_b_f, fox_w_out, final_norm):
    xt = x.reshape(TOKENS, D_MODEL)

    ffw = {1: (ff1_norm, ff1_w_in, ff1_w_out), 2: (ff2_norm, ff2_w_in, ff2_w_out)}
    hbm = pl.BlockSpec(memory_space=pl.ANY)

    def ffn_args(which, layer):
        norm, w_in, w_out = ffw[which]
        return [_row(norm[layer]), w_in, w_out], [_wspec((1, D_MODEL)), hbm, hbm]

    ple_gate_bf = ple_w_gate.astype(BF16)
    ple_proj_bf = ple_w_proj.astype(BF16)

    def post(xin, mixes, w_outs, layer, final):
        fa, fs = ffn_args(2, layer)
        ws = [w.astype(BF16) for w in w_outs]
        ins = ([xin] + list(mixes) + ws + fa
               + [_row(ple_norm[layer]), ple_gate_bf, p.reshape(DEPTH, TOKENS, PLE_DIM), ple_proj_bf,
                  _row(final_norm)])
        specs = ([_tok_spec(D_MODEL)] + [_tok_spec(m.shape[1]) for m in mixes]
                 + [_wspec(w.shape) for w in ws] + fs
                 + [_wspec((1, D_MODEL)), _wspec((D_MODEL, D_MODEL), layer),
                    pl.BlockSpec((None, TOKEN_TILE, PLE_DIM), lambda t: (layer, t, 0)),
                    _wspec((PLE_DIM, D_MODEL), layer), _wspec((1, D_MODEL))])
        return _token_call(
            "post%d" % layer,
            functools.partial(_post_kernel, n_mix=len(mixes), final=final, layer=layer), ins, specs,
            jax.ShapeDtypeStruct((TOKENS, D_MODEL), F32), _tok_spec(D_MODEL))

    w_ab = ab_w_in[0]
    w_qk = w_ab[:, :2 * A_WIDTH].astype(BF16)
    w_vt = w_ab[:, 2 * A_WIDTH:3 * A_WIDTH].T.astype(BF16)
    c0 = 3 * A_WIDTH
    kr0 = MLA_Q_RANK + MLA_KV_RANK
    w_c = jnp.concatenate(
        [w_ab[:, c0:c0 + kr0], jnp.zeros((D_MODEL, MLA_NOPE), F32), w_ab[:, c0 + kr0:],
         jnp.zeros((D_MODEL, LANES - MLA_NOPE - MLA_ROPE), F32)], axis=1).astype(BF16)
    w_uq = _place_heads(mla_w_uq[0], B_HEADS, MLA_NOPE + MLA_ROPE, 0, MLA_NOPE + MLA_ROPE).astype(BF16)
    per_head = mla_w_uq[0].reshape(MLA_Q_RANK, B_HEADS, MLA_NOPE + MLA_ROPE)
    x1_cols = per_head[:, :, MLA_NOPE:MLA_NOPE + MLA_ROPE // 2]
    x2_cols = per_head[:, :, MLA_NOPE + MLA_ROPE // 2:]
    partner_cols = jnp.concatenate([jnp.zeros_like(per_head[:, :, :MLA_NOPE]), x2_cols, x1_cols], axis=2)
    w_uq_rot = _place_heads(partner_cols.reshape(MLA_Q_RANK, -1), B_HEADS, MLA_NOPE + MLA_ROPE, 0,
                            MLA_NOPE + MLA_ROPE).astype(BF16)
    w_ukv_k = _place_heads(mla_w_ukv[0], B_HEADS, MLA_NOPE + MLA_V, 0, MLA_NOPE).astype(BF16)
    w_ukv = mla_w_ukv[0].reshape(MLA_KV_RANK, B_HEADS, MLA_NOPE + MLA_V)
    w_ukv_vt = w_ukv[:, :, MLA_NOPE:].reshape(MLA_KV_RANK, B_HEADS * MLA_V).T.astype(BF16)
    half = MLA_ROPE // 2
    inv = ROPE_THETA ** (-np.arange(half, dtype=np.float64) / half)
    inv_lane = np.zeros((1, LANES), np.float32)
    inv_lane[0, MLA_NOPE:MLA_NOPE + half] = inv
    inv_lane[0, MLA_NOPE + half:MLA_NOPE + MLA_ROPE] = inv
    rope_tbl = pl.pallas_call(
        _rope_table_kernel,
        grid=(SEQ // TOKEN_TILE,),
        in_specs=[pl.BlockSpec((1, LANES), lambda i: (0, 0))],
        out_specs=pl.BlockSpec((4, TOKEN_TILE, LANES), lambda i: (0, i, 0)),
        out_shape=jax.ShapeDtypeStruct((4, SEQ, LANES), F32),
        name="rope_tables",
    )(jnp.asarray(inv_lane))

    fa, fs = ffn_args(1, 0)
    ins = ([xt] + fa + [_row(mix_norm[0]), w_qk, w_vt, w_c, _row(mla_q_norm[0]), w_uq, w_uq_rot,
                        _row(mla_kv_norm[0]), w_ukv_k, w_ukv_vt, rope_tbl])
    specs = ([_tok_spec(D_MODEL)] + fs
             + [_wspec((1, D_MODEL)), _wspec(w_qk.shape), _wspec(w_vt.shape), _wspec(w_c.shape),
                _wspec((1, MLA_Q_RANK)), _wspec(w_uq.shape), _wspec(w_uq_rot.shape),
                _wspec((1, MLA_KV_RANK)),
                _wspec(w_ukv_k.shape), _wspec(w_ukv_vt.shape),
                pl.BlockSpec((4, TOKEN_TILE, LANES), lambda t: (0, t % (SEQ // TOKEN_TILE), 0))])
    pen_rows = A_HEADS // 2 * N_KBLK
    outs = (jax.ShapeDtypeStruct((TOKENS, D_MODEL), F32),
            jax.ShapeDtypeStruct((TOKENS, 2 * A_WIDTH), BF16),
            jax.ShapeDtypeStruct((A_WIDTH, TOKENS), BF16),
            jax.ShapeDtypeStruct((BATCH, N_KBLK, pen_rows, 2 * Q_TILE), F32),
            jax.ShapeDtypeStruct((TOKENS, B_HEADS * LANES), BF16),
            jax.ShapeDtypeStruct((TOKENS, B_HEADS * LANES), BF16),
            jax.ShapeDtypeStruct((B_HEADS * MLA_V, TOKENS), BF16))
    out_specs = (_tok_spec(D_MODEL), _tok_spec(2 * A_WIDTH), _tok_t_spec(A_WIDTH),
                 pl.BlockSpec((1, TILE_R, pen_rows, 2 * Q_TILE),
                              lambda t: (t // (N_KBLK // TILE_R), t % (N_KBLK // TILE_R), 0, 0)),
                 _tok_spec(B_HEADS * LANES), _tok_spec(B_HEADS * LANES),
                 _tok_t_spec(B_HEADS * MLA_V))
    x1, qk_a, vt_a, pen, q_mla, k_mla, vt_mla = _token_call(
        "pre0", functools.partial(_pre0_kernel, layer=0), ins, specs, outs, out_specs,
        scratch=[pltpu.VMEM((N_KBLK, A_WIDTH), F32)])

    bias = pl.pallas_call(
        _t5_bias_kernel,
        grid=(A_HEADS // 2,),
        in_specs=[pl.BlockSpec(memory_space=pltpu.SMEM)],
        out_specs=pl.BlockSpec((1, 3, Q_TILE, 2 * Q_TILE), lambda h: (h, 0, 0, 0)),
        out_shape=jax.ShapeDtypeStruct((A_HEADS // 2, 3, Q_TILE, 2 * Q_TILE), F32),
        name="t5_bias_tiles",
    )(t5_bias.astype(F32))

    na = A_HEADS // 2
    o_a = _attention(
        "moba", [qk_a, qk_a, vt_a, pen, bias],
        [_seq_spec(LANES, 0), _seq_spec(LANES, na // PAIR_GROUP), _vt_spec(),
         pl.BlockSpec((1, N_KBLK, PAIR_GROUP * N_KBLK, 2 * Q_TILE), lambda b, g: (b, 0, g, 0)),
         pl.BlockSpec((PAIR_GROUP, 3, Q_TILE, 2 * Q_TILE), lambda b, g: (g, 0, 0, 0))], na)
    o_b = _attention(
        "mla", [q_mla, k_mla, vt_mla],
        [_seq_spec(2 * LANES, 0), _seq_spec(2 * LANES, 0), _vt_spec()], B_HEADS // 2)
    w_o = ab_w_out[0]
    xt = post(x1, [o_a, o_b], [w_o[:A_WIDTH], w_o[A_WIDTH:]], 0, DEPTH == 1)

    w_fox = fox_w_in[0]
    w_qk = w_fox[:, :2 * C_WIDTH].astype(BF16)
    w_vt = w_fox[:, 2 * C_WIDTH:3 * C_WIDTH].T.astype(BF16)
    w_f = jnp.pad(w_fox[:, 3 * C_WIDTH:], ((0, 0), (0, LANES - C_HEADS))).astype(BF16)
    fa, fs = ffn_args(1, 1)
    ins = [xt] + fa + [_row(mix_norm[1]), w_qk, w_vt, w_f, _row(fox_b_f[0], LANES)]
    specs = ([_tok_spec(D_MODEL)] + fs
             + [_wspec((1, D_MODEL)), _wspec(w_qk.shape), _wspec(w_vt.shape), _wspec(w_f.shape),
                _wspec((1, LANES))])
    outs = (jax.ShapeDtypeStruct((TOKENS, D_MODEL), F32),
            jax.ShapeDtypeStruct((TOKENS, 2 * C_WIDTH), BF16),
            jax.ShapeDtypeStruct((C_WIDTH, TOKENS), BF16),
            jax.ShapeDtypeStruct((TOKENS, LANES), BF16))
    out_specs = (_tok_spec(D_MODEL), _tok_spec(2 * C_WIDTH), _tok_t_spec(C_WIDTH), _tok_spec(LANES))
    x1, qk_c, vt_c, caug = _token_call("pre1", functools.partial(_pre1_kernel, layer=1), ins, specs, outs, out_specs,
                                       scratch=[pltpu.VMEM((1, LANES), F32)])

    nc = C_HEADS // 2
    o_c = _attention(
        "fox", [qk_c, qk_c, vt_c, caug],
        [_seq_spec(LANES, 0), _seq_spec(LANES, nc // PAIR_GROUP), _vt_spec(),
         pl.BlockSpec((SEQ, LANES), lambda b, g: (b, 0))],
        nc, extra_scratch=[pltpu.VMEM((SEQ, PAIR_GROUP * 2 * LANES), BF16)])
    xt = post(x1, [o_c], [fox_w_out[0]], 1, True)
    return xt.reshape(BATCH, SEQ, D_MODEL)
```

```python
import functools
import math

import jax
import jax.numpy as jnp
import numpy as np
from jax import lax
from jax.experimental import pallas as pl
from jax.experimental.pallas import tpu as pltpu

F32 = jnp.float32
BF16 = jnp.bfloat16

D_MODEL = 1024
BATCH = 8
SEQ = 2048
DEPTH = 2
PLE_DIM = 256
D_FF = 2816
EPS = 1e-6

A_HEADS = 8
A_HEAD_DIM = 64
MOBA_BLOCK = 256
MOBA_TOPK = 3

B_HEADS = 8
MLA_Q_RANK = 256
MLA_KV_RANK = 128
MLA_NOPE = 64
MLA_ROPE = 32
MLA_V = 64
ROPE_THETA = 10000.0

T5_BUCKETS = 32
T5_MAX_DIST = 128

C_HEADS = 16
C_HEAD_DIM = 64

A_WIDTH = A_HEADS * A_HEAD_DIM
C_WIDTH = C_HEADS * C_HEAD_DIM

TOKENS = BATCH * SEQ
LANES = 128
BF16_ROWS = 16
PAIR_GROUP = 2
Q_TILE = MOBA_BLOCK
N_KBLK = SEQ // Q_TILE
TOKEN_TILE = 2 * Q_TILE
TILE_R = TOKEN_TILE // Q_TILE
FF_CHUNK = D_FF // 2
W_IN_CHUNK = 512
W_OUT_CHUNK = 256
VMEM_LIMIT = 56 * 1024 * 1024
NEG_INF = float("-inf")
LOG2E = math.log2(math.e)
QSCALE_64 = A_HEAD_DIM ** -0.5 * LOG2E
QSCALE_MLA = (MLA_NOPE + MLA_ROPE) ** -0.5 * LOG2E

assert T5_MAX_DIST <= Q_TILE + 1
assert SEQ % TOKEN_TILE == 0 and TOKEN_TILE % Q_TILE == 0 and N_KBLK % 2 == 0


def _wspec(shape, layer=None):
    nd = len(shape)
    if layer is None:
        return pl.BlockSpec(shape, lambda *_: (0,) * nd, pipeline_mode=pl.Buffered(1))
    return pl.BlockSpec((None,) + tuple(shape), lambda *_: (layer,) + (0,) * nd,
                        pipeline_mode=pl.Buffered(1))


def _dot(a, b):
    return jnp.dot(a, b, preferred_element_type=F32)


def _dot_nt(a, b):
    return lax.dot_general(a, b, (((1,), (1,)), ((), ())), preferred_element_type=F32)


def _rms(x, g):
    return x * lax.rsqrt(jnp.mean(x * x, axis=-1, keepdims=True) + EPS) * g


def _ffn_scratch():
    return [pltpu.VMEM((D_MODEL, 2 * D_FF), BF16), pltpu.VMEM((D_FF, D_MODEL), BF16),
            pltpu.VMEM((2, D_MODEL, W_IN_CHUNK), F32), pltpu.VMEM((2, W_OUT_CHUNK, D_MODEL), F32),
            pltpu.SemaphoreType.DMA((2, 2))]


def _load_ffn_weights(layer, win_hbm, wo_hbm, win_ref, wo_ref, stage_in, stage_out, sem):
    def copy_in(c, slot):
        return pltpu.make_async_copy(win_hbm.at[layer, :, pl.ds(c * W_IN_CHUNK, W_IN_CHUNK)],
                                     stage_in.at[slot], sem.at[0, slot])

    def copy_out(c, slot):
        return pltpu.make_async_copy(wo_hbm.at[layer, pl.ds(c * W_OUT_CHUNK, W_OUT_CHUNK), :],
                                     stage_out.at[slot], sem.at[1, slot])

    @pl.when(pl.program_id(0) == 0)
    def _():
        n_in = 2 * D_FF // W_IN_CHUNK
        n_out = D_FF // W_OUT_CHUNK
        copy_in(0, 0).start()
        copy_out(0, 0).start()
        for c in range(n_in):
            slot = c % 2
            if c + 1 < n_in:
                copy_in(c + 1, 1 - slot).start()
            copy_in(c, slot).wait()
            win_ref[:, c * W_IN_CHUNK:(c + 1) * W_IN_CHUNK] = stage_in[slot].astype(BF16)
        for c in range(n_out):
            slot = c % 2
            if c + 1 < n_out:
                copy_out(c + 1, 1 - slot).start()
            copy_out(c, slot).wait()
            wo_ref[c * W_OUT_CHUNK:(c + 1) * W_OUT_CHUNK, :] = stage_out[slot].astype(BF16)


def _ffn(x, g, win_ref, wo_ref):
    return _ffn_normed(x, _rms(x, g).astype(BF16), win_ref, wo_ref)


def _ffn_normed(x, h, win_ref, wo_ref):
    acts = []
    for c in range(D_FF // FF_CHUNK):
        lo, hi = c * FF_CHUNK, (c + 1) * FF_CHUNK
        a = _dot(h, win_ref[:, lo:hi])
        u = _dot(h, win_ref[:, D_FF + lo:D_FF + hi])
        acts.append((a * jax.nn.sigmoid(a) * u).astype(BF16))
    return x + 0.5 * _dot(jnp.concatenate(acts, axis=1), wo_ref[...])


def _rope_table_kernel(inv_ref, o_ref):
    pos0 = pl.program_id(0) * TOKEN_TILE
    pos = (pos0 + lax.broadcasted_iota(jnp.int32, (TOKEN_TILE, LANES), 0)).astype(F32)
    lane = lax.broadcasted_iota(jnp.int32, (TOKEN_TILE, LANES), 1)
    ang = pos * inv_ref[...]
    is_x1 = (lane >= MLA_NOPE) & (lane < MLA_NOPE + MLA_ROPE // 2)
    is_x2 = (lane >= MLA_NOPE + MLA_ROPE // 2) & (lane < MLA_NOPE + MLA_ROPE)
    cos_t = jnp.where(is_x1 | is_x2, jnp.cos(ang), 1.0)
    sin = jnp.sin(ang)
    sin_t = jnp.where(is_x1, -sin, jnp.where(is_x2, sin, 0.0))
    o_ref[0] = cos_t
    o_ref[1] = sin_t
    o_ref[2] = cos_t * QSCALE_MLA
    o_ref[3] = sin_t * QSCALE_MLA


def _rope_block(xb, cos_t, sin_t):
    half = MLA_ROPE // 2
    lane = lax.broadcasted_iota(jnp.int32, xb.shape, 1)
    is_x1 = lane < MLA_NOPE + half
    partner = jnp.where(is_x1, pltpu.roll(xb, LANES - half, 1), pltpu.roll(xb, half, 1))
    return xb * cos_t + partner * sin_t


def _pre0_kernel(x_ref, g1_ref, win_hbm, wo_hbm, gmix_ref, wqk_ref, wvt_ref, wc_ref,
                 qn_ref, wuq_ref, wuqrot_ref, kvn_ref, wukvk_ref, wukvvt_ref, rope_ref,
                 x1_ref, qk_ref, vt_ref, pen_ref, qm_ref, kmla_ref, vmt_ref, km_s, *ffn_s, layer):
    t = pl.program_id(0)
    _load_ffn_weights(layer, win_hbm, wo_hbm, *ffn_s)
    x1 = _ffn(x_ref[...], g1_ref[...], ffn_s[0], ffn_s[1])
    x1_ref[...] = x1
    h = _rms(x1, gmix_ref[...]).astype(BF16)
    qk = _dot(h, wqk_ref[...])
    q = qk[:, :A_WIDTH]
    k = qk[:, A_WIDTH:]
    qk_ref[:, :A_WIDTH] = (q * QSCALE_64).astype(BF16)
    qk_ref[:, A_WIDTH:] = k.astype(BF16)

    @pl.when(t == 0)
    def _():
        km_s[...] = jnp.zeros_like(km_s)

    c = _dot(h, wc_ref[...])
    gates = []
    for r in range(TILE_R):
        own = (t * TILE_R + r) % N_KBLK
        part = slice(r * Q_TILE, (r + 1) * Q_TILE)
        km_s[pl.ds(own, 1), :] = jnp.mean(k[part], axis=0, keepdims=True)
        gates.append((own, _moba_gate(q[part], km_s[...])))
    vt_ref[...] = _dot_nt(wvt_ref[...], h).astype(BF16)
    for r, (own, gate) in enumerate(gates):
        pen_ref[0, r] = _moba_select(gate, own)

    cq = c[:, :MLA_Q_RANK]
    ckv = c[:, MLA_Q_RANK:MLA_Q_RANK + MLA_KV_RANK]
    kr = c[:, MLA_Q_RANK + MLA_KV_RANK:]
    cqn = _rms(cq, qn_ref[...]).astype(BF16)
    ckvn = _rms(ckv, kvn_ref[...]).astype(BF16)
    qm = _dot(cqn, wuq_ref[...])
    qm_partner = _dot(cqn, wuqrot_ref[...])
    kn = _dot(ckvn, wukvk_ref[...])
    vmt_ref[...] = _dot_nt(wukvvt_ref[...], ckvn).astype(BF16)

    krr = _rope_block(kr, rope_ref[0], rope_ref[1])
    for hb in range(B_HEADS):
        sl = slice(hb * LANES, (hb + 1) * LANES)
        qm_ref[:, sl] = (qm[:, sl] * rope_ref[2] + qm_partner[:, sl] * rope_ref[3]).astype(BF16)
        kmla_ref[:, sl] = (kn[:, sl] + krr).astype(BF16)


def _pre1_kernel(x_ref, xnext_ref, g1_ref, win_hbm, wo_hbm, gmix_ref, wqk_ref, wvt_ref, wf_ref, bf_ref,
                 x1_ref, qk_ref, vt_ref, caug_ref, carry_s, h_s, *ffn_s, layer):
    t = pl.program_id(0)
    _load_ffn_weights(layer, win_hbm, wo_hbm, *ffn_s)

    @pl.when(t == 0)
    def _():
        h_s[0] = _rms(x_ref[...], g1_ref[...]).astype(BF16)

    x1 = _ffn_normed(x_ref[...], h_s[t % 2], ffn_s[0], ffn_s[1])
    x1_ref[...] = x1
    h = _rms(x1, gmix_ref[...]).astype(BF16)

    @pl.when(t % (N_KBLK // TILE_R) == 0)
    def _():
        carry_s[...] = jnp.zeros_like(carry_s)

    z = _dot(h, wf_ref[...]) + bf_ref[...]
    qk = _dot(h, wqk_ref[...])
    qk_ref[:, :C_WIDTH] = (qk[:, :C_WIDTH] * QSCALE_64).astype(BF16)
    qk_ref[:, C_WIDTH:] = qk[:, C_WIDTH:].astype(BF16)
    cums = [_fox_cumsum(z[r * Q_TILE:(r + 1) * Q_TILE], carry_s) for r in range(TILE_R)]
    vt_ref[...] = _dot_nt(wvt_ref[...], h).astype(BF16)
    for r, cum in enumerate(cums):
        caug_ref[r * Q_TILE:(r + 1) * Q_TILE, :] = _fox_decay_parts(cum)
    h_s[(t + 1) % 2] = _rms(xnext_ref[...], g1_ref[...]).astype(BF16)


def _tok_spec(width):
    return pl.BlockSpec((TOKEN_TILE, width), lambda i: (i, 0))


def _tok_t_spec(height):
    return pl.BlockSpec((height, TOKEN_TILE), lambda i: (0, i))


def _token_call(name, body, ins, in_specs, outs, out_specs, scratch=()):
    return pl.pallas_call(
        body,
        grid=(TOKENS // TOKEN_TILE,),
        in_specs=in_specs,
        out_specs=out_specs,
        out_shape=outs,
        scratch_shapes=list(scratch) + _ffn_scratch(),
        compiler_params=pltpu.CompilerParams(
            dimension_semantics=("arbitrary",), vmem_limit_bytes=VMEM_LIMIT),
        name=name,
    )(*ins)


def _post_kernel(*refs, n_mix, final, layer):
    x_ref = refs[0]
    o_refs = refs[1:1 + n_mix]
    w_refs = refs[1 + n_mix:1 + 2 * n_mix]
    (g2_ref, win_hbm, wo_hbm, gple_ref, wg_ref, p_ref, wp_ref, gfin_ref,
     out_ref) = refs[1 + 2 * n_mix:10 + 2 * n_mix]
    ffn_s = refs[10 + 2 * n_mix:]
    _load_ffn_weights(layer, win_hbm, wo_hbm, *ffn_s)
    x = x_ref[...]
    for o_ref, w_ref in zip(o_refs, w_refs):
        x = x + _dot(o_ref[...], w_ref[...])
    x = _ffn(x, g2_ref[...], ffn_s[0], ffn_s[1])
    gate = jax.nn.sigmoid(_dot(_rms(x, gple_ref[...]).astype(BF16), wg_ref[...]))
    x = x + gate * _dot(p_ref[...].astype(BF16), wp_ref[...])
    if final:
        x = _rms(x, gfin_ref[...])
    out_ref[...] = x


def _split_bf16(x):
    hi = x.astype(BF16)
    lo = (x - hi.astype(F32)).astype(BF16)
    return hi, lo


def _moba_gate(q, km):
    rows = A_HEADS * N_KBLK
    gt = jnp.concatenate([km] * A_HEADS, axis=0)
    r = lax.broadcasted_iota(jnp.int32, (rows, A_WIDTH), 0)
    c = lax.broadcasted_iota(jnp.int32, (rows, A_WIDTH), 1)
    gt = jnp.where((r // N_KBLK) == (c // A_HEAD_DIM), gt, 0.0)
    g_hi, g_lo = _split_bf16(gt)
    q_hi, q_lo = _split_bf16(q)
    return _dot_nt(g_hi, q_hi) + _dot_nt(g_hi, q_lo) + _dot_nt(g_lo, q_hi)


def _moba_select(gate, own):
    n_idx = lax.broadcasted_iota(jnp.int32, (N_KBLK, Q_TILE), 0)
    pen_rows = []
    for h in range(A_HEADS):
        gh = gate[h * N_KBLK:(h + 1) * N_KBLK]
        rank = jnp.zeros((N_KBLK, Q_TILE), jnp.int32)
        for m in range(N_KBLK):
            gm = gh[m:m + 1]
            beats = (gm > gh) | ((gm == gh) & (m < n_idx))
            rank = rank + jnp.where(beats & (m < own), 1, 0)
        sel = (n_idx < own) & (rank < MOBA_TOPK)
        pen_n = jnp.where(sel, 0.0, NEG_INF)
        pen_d = jnp.full((N_KBLK, Q_TILE), NEG_INF, F32)
        for n in range(N_KBLK):
            pen_d = jnp.where(n_idx == own - n, pen_n[n:n + 1], pen_d)
        pen_rows.append(pen_d)
    pairs = [jnp.concatenate(pen_rows[2 * hp:2 * hp + 2], axis=1) for hp in range(A_HEADS // 2)]
    return jnp.concatenate(pairs, axis=0)


def _t5_bias_kernel(tbl_ref, o_ref):
    hp = pl.program_id(0)
    r = lax.broadcasted_iota(jnp.int32, (Q_TILE, Q_TILE), 0)
    c = lax.broadcasted_iota(jnp.int32, (Q_TILE, Q_TILE), 1)
    max_exact = T5_BUCKETS // 2
    for dd in range(3):
        dist = dd * Q_TILE + c - r
        dc = jnp.maximum(dist, 0)
        df = jnp.maximum(dc.astype(F32), 1.0)
        large = max_exact + (jnp.log(df / max_exact) / math.log(T5_MAX_DIST / max_exact)
                             * (T5_BUCKETS - max_exact)).astype(jnp.int32)
        large = jnp.minimum(large, T5_BUCKETS - 1)
        bucket = jnp.where(dc < max_exact, dc, large)
        for hh in range(2):
            bias = jnp.zeros((Q_TILE, Q_TILE), F32)
            for b in range(T5_BUCKETS):
                bias = jnp.where(bucket == b, tbl_ref[b, 2 * hp + hh], bias)
            bias = bias * LOG2E
            if dd == 0:
                bias = jnp.where(dist >= 0, bias, NEG_INF)
            o_ref[0, dd, :, hh * Q_TILE:(hh + 1) * Q_TILE] = bias


def _fox_cumsum(z, carry_ref):
    lane = lax.broadcasted_iota(jnp.int32, (Q_TILE, LANES), 1)
    logf = jnp.minimum(z, 0.0) - jnp.log1p(jnp.exp(-jnp.abs(z)))
    logf = jnp.where(lane < C_HEADS, logf, 0.0)
    r = lax.broadcasted_iota(jnp.int32, (Q_TILE, Q_TILE), 0)
    c = lax.broadcasted_iota(jnp.int32, (Q_TILE, Q_TILE), 1)
    tri = jnp.where(c <= r, 1.0, 0.0).astype(BF16)
    l1 = logf.astype(BF16)
    rem = logf - l1.astype(F32)
    l2 = rem.astype(BF16)
    l3 = (rem - l2.astype(F32)).astype(BF16)
    cum = _dot(tri, l1) + _dot(tri, l2) + _dot(tri, l3) + carry_ref[...]
    carry_ref[...] = cum[Q_TILE - 1:Q_TILE, :]
    return cum


def _fox_decay_parts(cum):
    cs = cum * LOG2E
    c1 = cs.astype(BF16)
    rem = cs - c1.astype(F32)
    c2 = rem.astype(BF16)
    c3 = (rem - c2.astype(F32)).astype(BF16)
    rin = lax.broadcasted_iota(jnp.int32, (LANES, LANES), 0)
    lout = lax.broadcasted_iota(jnp.int32, (LANES, LANES), 1)
    out = jnp.zeros((Q_TILE, LANES), F32)
    for part, cp in enumerate((c1, c2, c3)):
        place = jnp.where((rin < C_HEADS) & (lout == rin * 8 + part), 1.0, 0.0).astype(BF16)
        out = out + _dot(cp, place)
    return out.astype(BF16)


def _attn_kernel(*refs, kind):
    if kind == "moba":
        q_ref, k_ref, vt_ref, pen_ref, bias_ref, o_ref = refs
    elif kind == "mla":
        q_ref, k_ref, vt_ref, o_ref = refs
    else:
        q_ref, k_ref, vt_ref, c_ref, o_ref, kp_s = refs
    grp = pl.program_id(1)
    lane = lax.broadcasted_iota(jnp.int32, (1, LANES), 1)
    low = lane < A_HEAD_DIM
    kw = LANES if kind == "moba" else 2 * LANES
    qw = 2 * LANES if kind == "mla" else LANES
    k_cat = kp_s if kind == "fox" else k_ref

    if kind == "fox":
        rin = lax.broadcasted_iota(jnp.int32, (LANES, LANES), 0)
        lout = lax.broadcasted_iota(jnp.int32, (LANES, LANES), 1)

        def place(pair, off_a, off_b, val):
            base_a = 2 * pair * 8
            base_b = base_a + 8
            sel_a = (lout >= off_a) & (lout < off_a + 3) & (rin == base_a + lout - off_a)
            sel_b = (lout >= off_b) & (lout < off_b + 3) & (rin == base_b + lout - off_b)
            return jnp.where(sel_a | sel_b, val, 0.0).astype(BF16)

        def ones(off_a, off_b):
            in_a = (lane >= off_a) & (lane < off_a + 3)
            in_b = (lane >= off_b) & (lane < off_b + 3)
            return jnp.where(in_a | in_b, 1.0, 0.0)

        aq_all = []
        for pi in range(PAIR_GROUP):
            ak = (_dot(c_ref[...], place(grp * PAIR_GROUP + pi, A_HEAD_DIM + 3, 3, -1.0))
                  + ones(A_HEAD_DIM, 0)).astype(BF16)
            k = k_ref[:, pi * LANES:(pi + 1) * LANES]
            kp_s[:, pi * kw:pi * kw + LANES] = jnp.where(low, k, ak)
            kp_s[:, pi * kw + LANES:(pi + 1) * kw] = jnp.where(low, ak, k)
            aq_all.append((_dot(c_ref[...], place(grp * PAIR_GROUP + pi, A_HEAD_DIM, 0, 1.0))
                           + ones(A_HEAD_DIM + 3, 3)).astype(BF16))

    def q_operand(tile, pi):
        zero = jnp.zeros((Q_TILE, LANES), BF16)
        rows = slice(tile * Q_TILE, (tile + 1) * Q_TILE)
        q = q_ref[rows, pi * qw:(pi + 1) * qw]
        if kind == "moba":
            return jnp.concatenate([jnp.where(low, q, zero), jnp.where(low, zero, q)], axis=0)
        if kind == "fox":
            aq = aq_all[pi][rows]
            qa, qb = jnp.where(low, q, aq), jnp.where(low, aq, q)
        else:
            qa, qb = q[:, :LANES], q[:, LANES:]
        return jnp.concatenate([jnp.concatenate([qa, zero], axis=1),
                                jnp.concatenate([zero, qb], axis=1)], axis=0)

    def store_out(pi, tile, accs):
        ot = jnp.concatenate([a[:A_HEAD_DIM] / a[A_HEAD_DIM:A_HEAD_DIM + 1] for a in accs], axis=0)
        o_ref[tile * Q_TILE:(tile + 1) * Q_TILE, pi * LANES:(pi + 1) * LANES] = ot.T.astype(BF16)

    def pv(pi, rows, p):
        ones_rows = jnp.ones((BF16_ROWS, rows.stop - rows.start), BF16)
        outs = []
        for hh in range(2):
            lo = pi * LANES + hh * A_HEAD_DIM
            v_aug = jnp.concatenate([vt_ref[lo:lo + A_HEAD_DIM, rows], ones_rows], axis=0)
            outs.append(_dot(v_aug, p[:, hh * Q_TILE:(hh + 1) * Q_TILE]))
        return outs

    if kind == "moba":
        _moba_schedule(q_operand, k_ref, pen_ref, bias_ref, pv, store_out)
    else:
        _online_schedule(q_operand, k_cat, kw, pv, store_out)


def _online_schedule(q_operand, k_cat, kw, pv, store_out):
    key = lax.broadcasted_iota(jnp.int32, (Q_TILE, 2 * Q_TILE), 0)
    qry = lax.broadcasted_iota(jnp.int32, (Q_TILE, 2 * Q_TILE), 1) % Q_TILE

    def key_rows(nb, d):
        return slice((nb - 1 - d) * Q_TILE, (nb - d) * Q_TILE)

    def score(jj):
        out = []
        for pi in range(PAIR_GROUP):
            for nb in (N_KBLK - jj, jj + 1):
                q = q_operand(nb - 1, pi)
                out.append((pi, nb, [_dot_nt(k_cat[key_rows(nb, d), pi * kw:(pi + 1) * kw], q)
                                     for d in range(nb)]))
        return out

    def fold(chains):
        state = {}
        for d in range(max(nb for _, nb, _ in chains)):
            for c, (pi, nb, logits) in enumerate(chains):
                if d >= nb:
                    continue
                sn = logits[d]
                if d == 0:
                    sn = jnp.where(key <= qry, sn, NEG_INF)
                    m_new = jnp.max(sn, axis=0, keepdims=True)
                else:
                    m_old, acc_old = state[c]
                    m_new = jnp.maximum(m_old, jnp.max(sn, axis=0, keepdims=True))
                new = pv(pi, key_rows(nb, d), jnp.exp2(sn - m_new).astype(BF16))
                if d > 0:
                    alpha = jnp.exp2(m_old - m_new)
                    new = [a * alpha[:, hh * Q_TILE:(hh + 1) * Q_TILE] + n
                           for hh, (a, n) in enumerate(zip(acc_old, new))]
                state[c] = (m_new, new)
        for c, (pi, nb, _) in enumerate(chains):
            store_out(pi, nb - 1, state[c][1])

    groups = N_KBLK // 2
    scored = score(0)
    for jj in range(groups):
        upcoming = score(jj + 1) if jj + 1 < groups else None
        fold(scored)
        scored = upcoming


def _moba_schedule(q_operand, k_ref, pen_ref, bias_ref, pv, store_out):
    def new_chain(pi, nb):
        return {"pair": pi, "nb": nb, "q": q_operand(nb - 1, pi), "blocks": [], "shifts": [], "m": None}

    def score_block(ch, n):
        pi, nb = ch["pair"], ch["nb"]
        d = nb - 1 - n
        sn = _dot_nt(k_ref[n * Q_TILE:(n + 1) * Q_TILE, pi * LANES:(pi + 1) * LANES], ch["q"])
        shift = None
        if d < 2:
            sn = sn + bias_ref[pi, d]
        else:
            shift = bias_ref[pi, 2, 0:1, :]
        if d > 0:
            pen = pen_ref[0, nb - 1, pi * N_KBLK + d:pi * N_KBLK + d + 1, :]
            shift = pen if shift is None else shift + pen
        bm = jnp.max(sn, axis=0, keepdims=True)
        if shift is not None:
            bm = bm + shift
        ch["m"] = bm if ch["m"] is None else jnp.maximum(ch["m"], bm)
        ch["blocks"].append(sn)
        ch["shifts"].append(shift)

    def prob_block(ch, n):
        m, sh = ch["m"], ch["shifts"][n]
        return jnp.exp2(ch["blocks"][n] - (m if sh is None else m - sh)).astype(BF16)

    groups = N_KBLK // 2
    prev = []
    for jj in list(range(groups)) + [None]:
        cur = [] if jj is None else [new_chain(pi, nb) for pi in range(PAIR_GROUP)
                                     for nb in (N_KBLK - jj, jj + 1)]
        probs = [[] for _ in prev]
        for n in range(max(ch["nb"] for ch in cur + prev)):
            for ch in cur:
                if n < ch["nb"]:
                    score_block(ch, n)
            for ch, pr in zip(prev, probs):
                if n < ch["nb"]:
                    pr.append(prob_block(ch, n))
        for ch, pr in zip(prev, probs):
            accs = pv(ch["pair"], slice(0, ch["nb"] * Q_TILE), jnp.concatenate(pr, axis=0))
            store_out(ch["pair"], ch["nb"] - 1, accs)
        prev = cur


def _attention(kind, ins, in_specs, n_pairs, extra_scratch=()):
    width = PAIR_GROUP * LANES
    return pl.pallas_call(
        functools.partial(_attn_kernel, kind=kind),
        grid=(BATCH, n_pairs // PAIR_GROUP),
        in_specs=in_specs,
        out_specs=pl.BlockSpec((SEQ, width), lambda b, g: (b, g)),
        out_shape=jax.ShapeDtypeStruct((TOKENS, n_pairs * LANES), BF16),
        scratch_shapes=list(extra_scratch),
        compiler_params=pltpu.CompilerParams(
            dimension_semantics=("arbitrary", "arbitrary"), vmem_limit_bytes=VMEM_LIMIT),
        name="attn_" + kind,
    )(*ins)


def _seq_spec(pair_width, col0):
    return pl.BlockSpec((SEQ, PAIR_GROUP * pair_width), lambda b, g: (b, col0 + g))


def _vt_spec():
    return pl.BlockSpec((PAIR_GROUP * LANES, SEQ), lambda b, g: (g, b))


def _place_heads(w, n_heads, src_stride, src_off, width):
    per_head = w.reshape(w.shape[0], n_heads, src_stride)[:, :, src_off:src_off + width]
    per_head = jnp.pad(per_head, ((0, 0), (0, 0), (0, LANES - width)))
    return per_head.reshape(w.shape[0], n_heads * LANES)


def _row(v, width=None):
    v = v.reshape(1, -1).astype(F32)
    if width is not None and v.shape[1] < width:
        v = jnp.pad(v, ((0, 0), (0, width - v.shape[1])))
    return v


def kernel(x, p, t5_bias, ff1_norm, ff1_w_in, ff1_w_out, mix_norm, ff2_norm, ff2_w_in, ff2_w_out,
           ple_norm, ple_w_gate, ple_w_proj, ab_w_in, mla_q_norm, mla_w_uq, mla_kv_norm, mla_w_ukv,
           ab_w_out, fox_w_in, fox_b_f, fox_w_out, final_norm):
    xt = x.reshape(TOKENS, D_MODEL)

    ffw = {1: (ff1_norm, ff1_w_in, ff1_w_out), 2: (ff2_norm, ff2_w_in, ff2_w_out)}
    hbm = pl.BlockSpec(memory_space=pl.ANY)

    def ffn_args(which, layer):
        norm, w_in, w_out = ffw[which]
        return [_row(norm[layer]), w_in, w_out], [_wspec((1, D_MODEL)), hbm, hbm]

    ple_gate_bf = ple_w_gate.astype(BF16)
    ple_proj_bf = ple_w_proj.astype(BF16)

    def post(xin, mixes, w_outs, layer, final):
        fa, fs = ffn_args(2, layer)
        ws = [w.astype(BF16) for w in w_outs]
        ins = ([xin] + list(mixes) + ws + fa
               + [_row(ple_norm[layer]), ple_gate_bf, p.reshape(DEPTH, TOKENS, PLE_DIM), ple_proj_bf,
                  _row(final_norm)])
        specs = ([_tok_spec(D_MODEL)] + [_tok_spec(m.shape[1]) for m in mixes]
                 + [_wspec(w.shape) for w in ws] + fs
                 + [_wspec((1, D_MODEL)), _wspec((D_MODEL, D_MODEL), layer),
                    pl.BlockSpec((None, TOKEN_TILE, PLE_DIM), lambda t: (layer, t, 0)),
                    _wspec((PLE_DIM, D_MODEL), layer), _wspec((1, D_MODEL))])
        return _token_call(
            "post%d" % layer,
            functools.partial(_post_kernel, n_mix=len(mixes), final=final, layer=layer), ins, specs,
            jax.ShapeDtypeStruct((TOKENS, D_MODEL), F32), _tok_spec(D_MODEL))

    w_ab = ab_w_in[0]
    w_qk = w_ab[:, :2 * A_WIDTH].astype(BF16)
    w_vt = w_ab[:, 2 * A_WIDTH:3 * A_WIDTH].T.astype(BF16)
    c0 = 3 * A_WIDTH
    kr0 = MLA_Q_RANK + MLA_KV_RANK
    w_c = jnp.concatenate(
        [w_ab[:, c0:c0 + kr0], jnp.zeros((D_MODEL, MLA_NOPE), F32), w_ab[:, c0 + kr0:],
         jnp.zeros((D_MODEL, LANES - MLA_NOPE - MLA_ROPE), F32)], axis=1).astype(BF16)
    w_uq = _place_heads(mla_w_uq[0], B_HEADS, MLA_NOPE + MLA_ROPE, 0, MLA_NOPE + MLA_ROPE).astype(BF16)
    per_head = mla_w_uq[0].reshape(MLA_Q_RANK, B_HEADS, MLA_NOPE + MLA_ROPE)
    x1_cols = per_head[:, :, MLA_NOPE:MLA_NOPE + MLA_ROPE // 2]
    x2_cols = per_head[:, :, MLA_NOPE + MLA_ROPE // 2:]
    partner_cols = jnp.concatenate([jnp.zeros_like(per_head[:, :, :MLA_NOPE]), x2_cols, x1_cols], axis=2)
    w_uq_rot = _place_heads(partner_cols.reshape(MLA_Q_RANK, -1), B_HEADS, MLA_NOPE + MLA_ROPE, 0,
                            MLA_NOPE + MLA_ROPE).astype(BF16)
    w_ukv_k = _place_heads(mla_w_ukv[0], B_HEADS, MLA_NOPE + MLA_V, 0, MLA_NOPE).astype(BF16)
    w_ukv = mla_w_ukv[0].reshape(MLA_KV_RANK, B_HEADS, MLA_NOPE + MLA_V)
    w_ukv_vt = w_ukv[:, :, MLA_NOPE:].reshape(MLA_KV_RANK, B_HEADS * MLA_V).T.astype(BF16)
    half = MLA_ROPE // 2
    inv = ROPE_THETA ** (-np.arange(half, dtype=np.float64) / half)
    inv_lane = np.zeros((1, LANES), np.float32)
    inv_lane[0, MLA_NOPE:MLA_NOPE + half] = inv
    inv_lane[0, MLA_NOPE + half:MLA_NOPE + MLA_ROPE] = inv
    rope_tbl = pl.pallas_call(
        _rope_table_kernel,
        grid=(SEQ // TOKEN_TILE,),
        in_specs=[pl.BlockSpec((1, LANES), lambda i: (0, 0))],
        out_specs=pl.BlockSpec((4, TOKEN_TILE, LANES), lambda i: (0, i, 0)),
        out_shape=jax.ShapeDtypeStruct((4, SEQ, LANES), F32),
        name="rope_tables",
    )(jnp.asarray(inv_lane))

    fa, fs = ffn_args(1, 0)
    ins = ([xt] + fa + [_row(mix_norm[0]), w_qk, w_vt, w_c, _row(mla_q_norm[0]), w_uq, w_uq_rot,
                        _row(mla_kv_norm[0]), w_ukv_k, w_ukv_vt, rope_tbl])
    specs = ([_tok_spec(D_MODEL)] + fs
             + [_wspec((1, D_MODEL)), _wspec(w_qk.shape), _wspec(w_vt.shape), _wspec(w_c.shape),
                _wspec((1, MLA_Q_RANK)), _wspec(w_uq.shape), _wspec(w_uq_rot.shape),
                _wspec((1, MLA_KV_RANK)),
                _wspec(w_ukv_k.shape), _wspec(w_ukv_vt.shape),
                pl.BlockSpec((4, TOKEN_TILE, LANES), lambda t: (0, t % (SEQ // TOKEN_TILE), 0))])
    pen_rows = A_HEADS // 2 * N_KBLK
    outs = (jax.ShapeDtypeStruct((TOKENS, D_MODEL), F32),
            jax.ShapeDtypeStruct((TOKENS, 2 * A_WIDTH), BF16),
            jax.ShapeDtypeStruct((A_WIDTH, TOKENS), BF16),
            jax.ShapeDtypeStruct((BATCH, N_KBLK, pen_rows, 2 * Q_TILE), F32),
            jax.ShapeDtypeStruct((TOKENS, B_HEADS * LANES), BF16),
            jax.ShapeDtypeStruct((TOKENS, B_HEADS * LANES), BF16),
            jax.ShapeDtypeStruct((B_HEADS * MLA_V, TOKENS), BF16))
    out_specs = (_tok_spec(D_MODEL), _tok_spec(2 * A_WIDTH), _tok_t_spec(A_WIDTH),
                 pl.BlockSpec((1, TILE_R, pen_rows, 2 * Q_TILE),
                              lambda t: (t // (N_KBLK // TILE_R), t % (N_KBLK // TILE_R), 0, 0)),
                 _tok_spec(B_HEADS * LANES), _tok_spec(B_HEADS * LANES),
                 _tok_t_spec(B_HEADS * MLA_V))
    x1, qk_a, vt_a, pen, q_mla, k_mla, vt_mla = _token_call(
        "pre0", functools.partial(_pre0_kernel, layer=0), ins, specs, outs, out_specs,
        scratch=[pltpu.VMEM((N_KBLK, A_WIDTH), F32)])

    bias = pl.pallas_call(
        _t5_bias_kernel,
        grid=(A_HEADS // 2,),
        in_specs=[pl.BlockSpec(memory_space=pltpu.SMEM)],
        out_specs=pl.BlockSpec((1, 3, Q_TILE, 2 * Q_TILE), lambda h: (h, 0, 0, 0)),
        out_shape=jax.ShapeDtypeStruct((A_HEADS // 2, 3, Q_TILE, 2 * Q_TILE), F32),
        name="t5_bias_tiles",
    )(t5_bias.astype(F32))

    na = A_HEADS // 2
    o_a = _attention(
        "moba", [qk_a, qk_a, vt_a, pen, bias],
        [_seq_spec(LANES, 0), _seq_spec(LANES, na // PAIR_GROUP), _vt_spec(),
         pl.BlockSpec((1, N_KBLK, PAIR_GROUP * N_KBLK, 2 * Q_TILE), lambda b, g: (b, 0, g, 0)),
         pl.BlockSpec((PAIR_GROUP, 3, Q_TILE, 2 * Q_TILE), lambda b, g: (g, 0, 0, 0))], na)
    o_b = _attention(
        "mla", [q_mla, k_mla, vt_mla],
        [_seq_spec(2 * LANES, 0), _seq_spec(2 * LANES, 0), _vt_spec()], B_HEADS // 2)
    w_o = ab_w_out[0]
    xt = post(x1, [o_a, o_b], [w_o[:A_WIDTH], w_o[A_WIDTH:]], 0, DEPTH == 1)

    w_fox = fox_w_in[0]
    w_qk = w_fox[:, :2 * C_WIDTH].astype(BF16)
    w_vt = w_fox[:, 2 * C_WIDTH:3 * C_WIDTH].T.astype(BF16)
    w_f = jnp.pad(w_fox[:, 3 * C_WIDTH:], ((0, 0), (0, LANES - C_HEADS))).astype(BF16)
    fa, fs = ffn_args(1, 1)
    last = TOKENS // TOKEN_TILE - 1
    ins = [xt, xt] + fa + [_row(mix_norm[1]), w_qk, w_vt, w_f, _row(fox_b_f[0], LANES)]
    specs = ([_tok_spec(D_MODEL),
              pl.BlockSpec((TOKEN_TILE, D_MODEL), lambda t: (jnp.minimum(t + 1, last), 0))] + fs
             + [_wspec((1, D_MODEL)), _wspec(w_qk.shape), _wspec(w_vt.shape), _wspec(w_f.shape),
                _wspec((1, LANES))])
    outs = (jax.ShapeDtypeStruct((TOKENS, D_MODEL), F32),
            jax.ShapeDtypeStruct((TOKENS, 2 * C_WIDTH), BF16),
            jax.ShapeDtypeStruct((C_WIDTH, TOKENS), BF16),
            jax.ShapeDtypeStruct((TOKENS, LANES), BF16))
    out_specs = (_tok_spec(D_MODEL), _tok_spec(2 * C_WIDTH), _tok_t_spec(C_WIDTH), _tok_spec(LANES))
    x1, qk_c, vt_c, caug = _token_call("pre1", functools.partial(_pre1_kernel, layer=1), ins, specs, outs, out_specs,
                                       scratch=[pltpu.VMEM((1, LANES), F32),
                                                pltpu.VMEM((2, TOKEN_TILE, D_MODEL), BF16)])

    nc = C_HEADS // 2
    o_c = _attention(
        "fox", [qk_c, qk_c, vt_c, caug],
        [_seq_spec(LANES, 0), _seq_spec(LANES, nc // PAIR_GROUP), _vt_spec(),
         pl.BlockSpec((SEQ, LANES), lambda b, g: (b, 0))],
        nc, extra_scratch=[pltpu.VMEM((SEQ, PAIR_GROUP * 2 * LANES), BF16)])
    xt = post(x1, [o_c], [fox_w_out[0]], 1, True)
    return xt.reshape(BATCH, SEQ, D_MODEL)
```

```python
import functools
import math

import jax
import jax.numpy as jnp
import numpy as np
from jax import lax
from jax.experimental import pallas as pl
from jax.experimental.pallas import tpu as pltpu

F32 = jnp.float32
BF16 = jnp.bfloat16

D_MODEL = 1024
BATCH = 8
SEQ = 2048
DEPTH = 2
PLE_DIM = 256
D_FF = 2816
EPS = 1e-6

A_HEADS = 8
A_HEAD_DIM = 64
MOBA_BLOCK = 256
MOBA_TOPK = 3

B_HEADS = 8
MLA_Q_RANK = 256
MLA_KV_RANK = 128
MLA_NOPE = 64
MLA_ROPE = 32
MLA_V = 64
ROPE_THETA = 10000.0

T5_BUCKETS = 32
T5_MAX_DIST = 128

C_HEADS = 16
C_HEAD_DIM = 64

A_WIDTH = A_HEADS * A_HEAD_DIM
C_WIDTH = C_HEADS * C_HEAD_DIM

TOKENS = BATCH * SEQ
LANES = 128
BF16_ROWS = 16
PAIR_GROUP = 2
Q_TILE = MOBA_BLOCK
N_KBLK = SEQ // Q_TILE
TOKEN_TILE = 2 * Q_TILE
TILE_R = TOKEN_TILE // Q_TILE
FF_CHUNK = D_FF // 2
W_IN_CHUNK = 512
W_OUT_CHUNK = 256
VMEM_LIMIT = 56 * 1024 * 1024
NEG_INF = float("-inf")
LOG2E = math.log2(math.e)
QSCALE_64 = A_HEAD_DIM ** -0.5 * LOG2E
QSCALE_MLA = (MLA_NOPE + MLA_ROPE) ** -0.5 * LOG2E

assert T5_MAX_DIST <= Q_TILE + 1
assert SEQ % TOKEN_TILE == 0 and TOKEN_TILE % Q_TILE == 0 and N_KBLK % 2 == 0


def _wspec(shape, layer=None):
    nd = len(shape)
    if layer is None:
        return pl.BlockSpec(shape, lambda *_: (0,) * nd, pipeline_mode=pl.Buffered(1))
    return pl.BlockSpec((None,) + tuple(shape), lambda *_: (layer,) + (0,) * nd,
                        pipeline_mode=pl.Buffered(1))


def _dot(a, b):
    return jnp.dot(a, b, preferred_element_type=F32)


def _dot_nt(a, b):
    return lax.dot_general(a, b, (((1,), (1,)), ((), ())), preferred_element_type=F32)


def _rms(x, g):
    return x * lax.rsqrt(jnp.mean(x * x, axis=-1, keepdims=True) + EPS) * g


def _ffn_scratch():
    return [pltpu.VMEM((D_MODEL, 2 * D_FF), BF16), pltpu.VMEM((D_FF, D_MODEL), BF16),
            pltpu.VMEM((2, D_MODEL, W_IN_CHUNK), F32), pltpu.VMEM((2, W_OUT_CHUNK, D_MODEL), F32),
            pltpu.SemaphoreType.DMA((2, 2))]


def _load_ffn_weights(layer, win_hbm, wo_hbm, win_ref, wo_ref, stage_in, stage_out, sem):
    def copy_in(c, slot):
        return pltpu.make_async_copy(win_hbm.at[layer, :, pl.ds(c * W_IN_CHUNK, W_IN_CHUNK)],
                                     stage_in.at[slot], sem.at[0, slot])

    def copy_out(c, slot):
        return pltpu.make_async_copy(wo_hbm.at[layer, pl.ds(c * W_OUT_CHUNK, W_OUT_CHUNK), :],
                                     stage_out.at[slot], sem.at[1, slot])

    @pl.when(pl.program_id(0) == 0)
    def _():
        n_in = 2 * D_FF // W_IN_CHUNK
        n_out = D_FF // W_OUT_CHUNK
        copy_in(0, 0).start()
        copy_out(0, 0).start()
        for c in range(n_in):
            slot = c % 2
            if c + 1 < n_in:
                copy_in(c + 1, 1 - slot).start()
            copy_in(c, slot).wait()
            win_ref[:, c * W_IN_CHUNK:(c + 1) * W_IN_CHUNK] = stage_in[slot].astype(BF16)
        for c in range(n_out):
            slot = c % 2
            if c + 1 < n_out:
                copy_out(c + 1, 1 - slot).start()
            copy_out(c, slot).wait()
            wo_ref[c * W_OUT_CHUNK:(c + 1) * W_OUT_CHUNK, :] = stage_out[slot].astype(BF16)


def _ffn(x, g, win_ref, wo_ref):
    h = _rms(x, g).astype(BF16)
    acts = []
    for c in range(D_FF // FF_CHUNK):
        lo, hi = c * FF_CHUNK, (c + 1) * FF_CHUNK
        a = _dot(h, win_ref[:, lo:hi])
        u = _dot(h, win_ref[:, D_FF + lo:D_FF + hi])
        acts.append((a * jax.nn.sigmoid(a) * u).astype(BF16))
    return x + 0.5 * _dot(jnp.concatenate(acts, axis=1), wo_ref[...])


def _rope_table_kernel(inv_ref, o_ref):
    pos0 = pl.program_id(0) * TOKEN_TILE
    pos = (pos0 + lax.broadcasted_iota(jnp.int32, (TOKEN_TILE, LANES), 0)).astype(F32)
    lane = lax.broadcasted_iota(jnp.int32, (TOKEN_TILE, LANES), 1)
    ang = pos * inv_ref[...]
    is_x1 = (lane >= MLA_NOPE) & (lane < MLA_NOPE + MLA_ROPE // 2)
    is_x2 = (lane >= MLA_NOPE + MLA_ROPE // 2) & (lane < MLA_NOPE + MLA_ROPE)
    cos_t = jnp.where(is_x1 | is_x2, jnp.cos(ang), 1.0)
    sin = jnp.sin(ang)
    sin_t = jnp.where(is_x1, -sin, jnp.where(is_x2, sin, 0.0))
    o_ref[0] = cos_t
    o_ref[1] = sin_t
    o_ref[2] = cos_t * QSCALE_MLA
    o_ref[3] = sin_t * QSCALE_MLA


def _rope_block(xb, cos_t, sin_t):
    half = MLA_ROPE // 2
    lane = lax.broadcasted_iota(jnp.int32, xb.shape, 1)
    is_x1 = lane < MLA_NOPE + half
    partner = jnp.where(is_x1, pltpu.roll(xb, LANES - half, 1), pltpu.roll(xb, half, 1))
    return xb * cos_t + partner * sin_t


def _pre0_kernel(x_ref, g1_ref, win_hbm, wo_hbm, gmix_ref, wqk_ref, wvt_ref, wc_ref,
                 qn_ref, wuq_ref, wuqrot_ref, kvn_ref, wukvk_ref, wukvvt_ref, rope_ref,
                 x1_ref, qk_ref, vt_ref, pen_ref, qm_ref, kmla_ref, vmt_ref, km_s, *ffn_s, layer):
    t = pl.program_id(0)
    _load_ffn_weights(layer, win_hbm, wo_hbm, *ffn_s)
    x1 = _ffn(x_ref[...], g1_ref[...], ffn_s[0], ffn_s[1])
    x1_ref[...] = x1
    h = _rms(x1, gmix_ref[...]).astype(BF16)
    qk = _dot(h, wqk_ref[...])
    q = qk[:, :A_WIDTH]
    k = qk[:, A_WIDTH:]
    qk_ref[:, :A_WIDTH] = (q * QSCALE_64).astype(BF16)
    qk_ref[:, A_WIDTH:] = k.astype(BF16)

    @pl.when(t == 0)
    def _():
        km_s[...] = jnp.zeros_like(km_s)

    c = _dot(h, wc_ref[...])
    gates = []
    for r in range(TILE_R):
        own = (t * TILE_R + r) % N_KBLK
        part = slice(r * Q_TILE, (r + 1) * Q_TILE)
        km_s[pl.ds(own, 1), :] = jnp.mean(k[part], axis=0, keepdims=True)
        gates.append((own, _moba_gate(q[part], km_s[...])))
    vt_ref[...] = _dot_nt(wvt_ref[...], h).astype(BF16)
    for r, (own, gate) in enumerate(gates):
        pen_ref[0, r] = _moba_select(gate, own)

    cq = c[:, :MLA_Q_RANK]
    ckv = c[:, MLA_Q_RANK:MLA_Q_RANK + MLA_KV_RANK]
    kr = c[:, MLA_Q_RANK + MLA_KV_RANK:]
    cqn = _rms(cq, qn_ref[...]).astype(BF16)
    ckvn = _rms(ckv, kvn_ref[...]).astype(BF16)
    qm = _dot(cqn, wuq_ref[...])
    qm_partner = _dot(cqn, wuqrot_ref[...])
    kn = _dot(ckvn, wukvk_ref[...])
    vmt_ref[...] = _dot_nt(wukvvt_ref[...], ckvn).astype(BF16)

    krr = _rope_block(kr, rope_ref[0], rope_ref[1])
    for hb in range(B_HEADS):
        sl = slice(hb * LANES, (hb + 1) * LANES)
        qm_ref[:, sl] = (qm[:, sl] * rope_ref[2] + qm_partner[:, sl] * rope_ref[3]).astype(BF16)
        kmla_ref[:, sl] = (kn[:, sl] + krr).astype(BF16)


def _pre1_kernel(x_ref, g1_ref, win_hbm, wo_hbm, gmix_ref, wqk_ref, wvt_ref, wf_ref, bf_ref,
                 x1_ref, qk_ref, vt_ref, caug_ref, carry_s, *ffn_s, layer):
    t = pl.program_id(0)
    _load_ffn_weights(layer, win_hbm, wo_hbm, *ffn_s)
    x1 = _ffn(x_ref[...], g1_ref[...], ffn_s[0], ffn_s[1])
    x1_ref[...] = x1
    h = _rms(x1, gmix_ref[...]).astype(BF16)

    @pl.when(t % (N_KBLK // TILE_R) == 0)
    def _():
        carry_s[...] = jnp.zeros_like(carry_s)

    z = _dot(h, wf_ref[...]) + bf_ref[...]
    qk = _dot(h, wqk_ref[...])
    qk_ref[:, :C_WIDTH] = (qk[:, :C_WIDTH] * QSCALE_64).astype(BF16)
    qk_ref[:, C_WIDTH:] = qk[:, C_WIDTH:].astype(BF16)
    cums = [_fox_cumsum(z[r * Q_TILE:(r + 1) * Q_TILE], carry_s) for r in range(TILE_R)]
    vt_ref[...] = _dot_nt(wvt_ref[...], h).astype(BF16)
    for r, cum in enumerate(cums):
        caug_ref[r * Q_TILE:(r + 1) * Q_TILE, :] = _fox_decay_parts(cum)


def _tok_spec(width):
    return pl.BlockSpec((TOKEN_TILE, width), lambda i: (i, 0))


def _tok_t_spec(height):
    return pl.BlockSpec((height, TOKEN_TILE), lambda i: (0, i))


def _token_call(name, body, ins, in_specs, outs, out_specs, scratch=()):
    return pl.pallas_call(
        body,
        grid=(TOKENS // TOKEN_TILE,),
        in_specs=in_specs,
        out_specs=out_specs,
        out_shape=outs,
        scratch_shapes=list(scratch) + _ffn_scratch(),
        compiler_params=pltpu.CompilerParams(
            dimension_semantics=("arbitrary",), vmem_limit_bytes=VMEM_LIMIT),
        name=name,
    )(*ins)


def _post_kernel(*refs, n_mix, final, layer):
    x_ref = refs[0]
    o_refs = refs[1:1 + n_mix]
    w_refs = refs[1 + n_mix:1 + 2 * n_mix]
    (g2_ref, win_hbm, wo_hbm, gple_ref, wg_ref, p_ref, wp_ref, gfin_ref,
     out_ref) = refs[1 + 2 * n_mix:10 + 2 * n_mix]
    ffn_s = refs[10 + 2 * n_mix:]
    _load_ffn_weights(layer, win_hbm, wo_hbm, *ffn_s)
    x = x_ref[...]
    for o_ref, w_ref in zip(o_refs, w_refs):
        x = x + _dot(o_ref[...], w_ref[...])
    x = _ffn(x, g2_ref[...], ffn_s[0], ffn_s[1])
    gate = jax.nn.sigmoid(_dot(_rms(x, gple_ref[...]).astype(BF16), wg_ref[...]))
    x = x + gate * _dot(p_ref[...].astype(BF16), wp_ref[...])
    if final:
        x = _rms(x, gfin_ref[...])
    out_ref[...] = x


def _split_bf16(x):
    hi = x.astype(BF16)
    lo = (x - hi.astype(F32)).astype(BF16)
    return hi, lo


def _moba_gate(q, km):
    rows = A_HEADS * N_KBLK
    gt = jnp.concatenate([km] * A_HEADS, axis=0)
    r = lax.broadcasted_iota(jnp.int32, (rows, A_WIDTH), 0)
    c = lax.broadcasted_iota(jnp.int32, (rows, A_WIDTH), 1)
    gt = jnp.where((r // N_KBLK) == (c // A_HEAD_DIM), gt, 0.0)
    g_hi, g_lo = _split_bf16(gt)
    q_hi, q_lo = _split_bf16(q)
    return _dot_nt(g_hi, q_hi) + _dot_nt(g_hi, q_lo) + _dot_nt(g_lo, q_hi)


def _moba_select(gate, own):
    n_idx = lax.broadcasted_iota(jnp.int32, (N_KBLK, Q_TILE), 0)
    pen_rows = []
    for h in range(A_HEADS):
        gh = gate[h * N_KBLK:(h + 1) * N_KBLK]
        rank = jnp.zeros((N_KBLK, Q_TILE), jnp.int32)
        for m in range(N_KBLK):
            gm = gh[m:m + 1]
            beats = (gm > gh) | ((gm == gh) & (m < n_idx))
            rank = rank + jnp.where(beats & (m < own), 1, 0)
        sel = (n_idx < own) & (rank < MOBA_TOPK)
        pen_n = jnp.where(sel, 0.0, NEG_INF)
        pen_d = jnp.full((N_KBLK, Q_TILE), NEG_INF, F32)
        for n in range(N_KBLK):
            pen_d = jnp.where(n_idx == own - n, pen_n[n:n + 1], pen_d)
        pen_rows.append(pen_d)
    pairs = [jnp.concatenate(pen_rows[2 * hp:2 * hp + 2], axis=1) for hp in range(A_HEADS // 2)]
    return jnp.concatenate(pairs, axis=0)


def _t5_bias_kernel(tbl_ref, o_ref):
    hp = pl.program_id(0)
    r = lax.broadcasted_iota(jnp.int32, (Q_TILE, Q_TILE), 0)
    c = lax.broadcasted_iota(jnp.int32, (Q_TILE, Q_TILE), 1)
    max_exact = T5_BUCKETS // 2
    for dd in range(3):
        dist = dd * Q_TILE + c - r
        dc = jnp.maximum(dist, 0)
        df = jnp.maximum(dc.astype(F32), 1.0)
        large = max_exact + (jnp.log(df / max_exact) / math.log(T5_MAX_DIST / max_exact)
                             * (T5_BUCKETS - max_exact)).astype(jnp.int32)
        large = jnp.minimum(large, T5_BUCKETS - 1)
        bucket = jnp.where(dc < max_exact, dc, large)
        for hh in range(2):
            bias = jnp.zeros((Q_TILE, Q_TILE), F32)
            for b in range(T5_BUCKETS):
                bias = jnp.where(bucket == b, tbl_ref[b, 2 * hp + hh], bias)
            bias = bias * LOG2E
            if dd == 0:
                bias = jnp.where(dist >= 0, bias, NEG_INF)
            o_ref[0, dd, :, hh * Q_TILE:(hh + 1) * Q_TILE] = bias


def _fox_cumsum(z, carry_ref):
    lane = lax.broadcasted_iota(jnp.int32, (Q_TILE, LANES), 1)
    logf = jnp.minimum(z, 0.0) - jnp.log1p(jnp.exp(-jnp.abs(z)))
    logf = jnp.where(lane < C_HEADS, logf, 0.0)
    r = lax.broadcasted_iota(jnp.int32, (Q_TILE, Q_TILE), 0)
    c = lax.broadcasted_iota(jnp.int32, (Q_TILE, Q_TILE), 1)
    tri = jnp.where(c <= r, 1.0, 0.0).astype(BF16)
    l1 = logf.astype(BF16)
    rem = logf - l1.astype(F32)
    l2 = rem.astype(BF16)
    l3 = (rem - l2.astype(F32)).astype(BF16)
    cum = _dot(tri, l1) + _dot(tri, l2) + _dot(tri, l3) + carry_ref[...]
    carry_ref[...] = cum[Q_TILE - 1:Q_TILE, :]
    return cum


def _fox_decay_parts(cum):
    cs = cum * LOG2E
    c1 = cs.astype(BF16)
    rem = cs - c1.astype(F32)
    c2 = rem.astype(BF16)
    c3 = (rem - c2.astype(F32)).astype(BF16)
    rin = lax.broadcasted_iota(jnp.int32, (LANES, LANES), 0)
    lout = lax.broadcasted_iota(jnp.int32, (LANES, LANES), 1)
    out = jnp.zeros((Q_TILE, LANES), F32)
    for part, cp in enumerate((c1, c2, c3)):
        place = jnp.where((rin < C_HEADS) & (lout == rin * 8 + part), 1.0, 0.0).astype(BF16)
        out = out + _dot(cp, place)
    return out.astype(BF16)


def _attn_kernel(*refs, kind):
    if kind == "moba":
        q_ref, k_ref, vt_ref, pen_ref, bias_ref, o_ref = refs
    elif kind == "mla":
        q_ref, k_ref, vt_ref, o_ref = refs
    else:
        q_ref, k_ref, vt_ref, c_ref, o_ref, kp_s = refs
    grp = pl.program_id(1)
    lane = lax.broadcasted_iota(jnp.int32, (1, LANES), 1)
    low = lane < A_HEAD_DIM
    kw = LANES if kind == "moba" else 2 * LANES
    qw = 2 * LANES if kind == "mla" else LANES
    k_cat = kp_s if kind == "fox" else k_ref

    if kind == "fox":
        rin = lax.broadcasted_iota(jnp.int32, (LANES, LANES), 0)
        lout = lax.broadcasted_iota(jnp.int32, (LANES, LANES), 1)

        def place(pair, off_a, off_b, val):
            base_a = 2 * pair * 8
            base_b = base_a + 8
            sel_a = (lout >= off_a) & (lout < off_a + 3) & (rin == base_a + lout - off_a)
            sel_b = (lout >= off_b) & (lout < off_b + 3) & (rin == base_b + lout - off_b)
            return jnp.where(sel_a | sel_b, val, 0.0).astype(BF16)

        def ones(off_a, off_b):
            in_a = (lane >= off_a) & (lane < off_a + 3)
            in_b = (lane >= off_b) & (lane < off_b + 3)
            return jnp.where(in_a | in_b, 1.0, 0.0)

        aq_all = []
        for pi in range(PAIR_GROUP):
            ak = (_dot(c_ref[...], place(grp * PAIR_GROUP + pi, A_HEAD_DIM + 3, 3, -1.0))
                  + ones(A_HEAD_DIM, 0)).astype(BF16)
            k = k_ref[:, pi * LANES:(pi + 1) * LANES]
            kp_s[:, pi * kw:pi * kw + LANES] = jnp.where(low, k, ak)
            kp_s[:, pi * kw + LANES:(pi + 1) * kw] = jnp.where(low, ak, k)
            aq_all.append((_dot(c_ref[...], place(grp * PAIR_GROUP + pi, A_HEAD_DIM, 0, 1.0))
                           + ones(A_HEAD_DIM + 3, 3)).astype(BF16))

    def q_operand(tile, pi):
        zero = jnp.zeros((Q_TILE, LANES), BF16)
        rows = slice(tile * Q_TILE, (tile + 1) * Q_TILE)
        q = q_ref[rows, pi * qw:(pi + 1) * qw]
        if kind == "moba":
            return jnp.concatenate([jnp.where(low, q, zero), jnp.where(low, zero, q)], axis=0)
        if kind == "fox":
            aq = aq_all[pi][rows]
            qa, qb = jnp.where(low, q, aq), jnp.where(low, aq, q)
        else:
            qa, qb = q[:, :LANES], q[:, LANES:]
        return jnp.concatenate([jnp.concatenate([qa, zero], axis=1),
                                jnp.concatenate([zero, qb], axis=1)], axis=0)

    def store_out(pi, tile, accs):
        ot = jnp.concatenate([a[:A_HEAD_DIM] / a[A_HEAD_DIM:A_HEAD_DIM + 1] for a in accs], axis=0)
        o_ref[tile * Q_TILE:(tile + 1) * Q_TILE, pi * LANES:(pi + 1) * LANES] = ot.T.astype(BF16)

    def pv(pi, rows, p):
        ones_rows = jnp.ones((BF16_ROWS, rows.stop - rows.start), BF16)
        outs = []
        for hh in range(2):
            lo = pi * LANES + hh * A_HEAD_DIM
            v_aug = jnp.concatenate([vt_ref[lo:lo + A_HEAD_DIM, rows], ones_rows], axis=0)
            outs.append(_dot(v_aug, p[:, hh * Q_TILE:(hh + 1) * Q_TILE]))
        return outs

    if kind == "moba":
        _moba_schedule(q_operand, k_ref, pen_ref, bias_ref, pv, store_out)
    else:
        _online_schedule(q_operand, k_cat, kw, pv, store_out)


def _online_schedule(q_operand, k_cat, kw, pv, store_out):
    key = lax.broadcasted_iota(jnp.int32, (Q_TILE, 2 * Q_TILE), 0)
    qry = lax.broadcasted_iota(jnp.int32, (Q_TILE, 2 * Q_TILE), 1) % Q_TILE

    def key_rows(nb, d):
        return slice((nb - 1 - d) * Q_TILE, (nb - d) * Q_TILE)

    def score(jj):
        out = []
        for pi in range(PAIR_GROUP):
            for nb in (N_KBLK - jj, jj + 1):
                q = q_operand(nb - 1, pi)
                out.append((pi, nb, [_dot_nt(k_cat[key_rows(nb, d), pi * kw:(pi + 1) * kw], q)
                                     for d in range(nb)]))
        return out

    def fold(chains):
        state = {}
        for d in range(max(nb for _, nb, _ in chains)):
            for c, (pi, nb, logits) in enumerate(chains):
                if d >= nb:
                    continue
                sn = logits[d]
                if d == 0:
                    sn = jnp.where(key <= qry, sn, NEG_INF)
                    m_new = jnp.max(sn, axis=0, keepdims=True)
                else:
                    m_old, acc_old = state[c]
                    m_new = jnp.maximum(m_old, jnp.max(sn, axis=0, keepdims=True))
                new = pv(pi, key_rows(nb, d), jnp.exp2(sn - m_new).astype(BF16))
                if d > 0:
                    alpha = jnp.exp2(m_old - m_new)
                    new = [a * alpha[:, hh * Q_TILE:(hh + 1) * Q_TILE] + n
                           for hh, (a, n) in enumerate(zip(acc_old, new))]
                state[c] = (m_new, new)
        for c, (pi, nb, _) in enumerate(chains):
            store_out(pi, nb - 1, state[c][1])

    groups = N_KBLK // 2
    scored = score(0)
    for jj in range(groups):
        upcoming = score(jj + 1) if jj + 1 < groups else None
        fold(scored)
        scored = upcoming


def _moba_schedule(q_operand, k_ref, pen_ref, bias_ref, pv, store_out):
    def new_chain(pi, nb):
        return {"pair": pi, "nb": nb, "q": q_operand(nb - 1, pi), "blocks": [], "shifts": [], "m": None}

    def score_block(ch, n):
        pi, nb = ch["pair"], ch["nb"]
        d = nb - 1 - n
        sn = _dot_nt(k_ref[n * Q_TILE:(n + 1) * Q_TILE, pi * LANES:(pi + 1) * LANES], ch["q"])
        shift = None
        if d < 2:
            sn = sn + bias_ref[pi, d]
        else:
            shift = bias_ref[pi, 2, 0:1, :]
        if d > 0:
            pen = pen_ref[0, nb - 1, pi * N_KBLK + d:pi * N_KBLK + d + 1, :]
            shift = pen if shift is None else shift + pen
        bm = jnp.max(sn, axis=0, keepdims=True)
        if shift is not None:
            bm = bm + shift
        ch["m"] = bm if ch["m"] is None else jnp.maximum(ch["m"], bm)
        ch["blocks"].append(sn)
        ch["shifts"].append(shift)

    def prob_block(ch, n):
        m, sh = ch["m"], ch["shifts"][n]
        return jnp.exp2(ch["blocks"][n] - (m if sh is None else m - sh)).astype(BF16)

    groups = N_KBLK // 2
    prev = []
    for jj in list(range(groups)) + [None]:
        cur = [] if jj is None else [new_chain(pi, nb) for pi in range(PAIR_GROUP)
                                     for nb in (N_KBLK - jj, jj + 1)]
        probs = [[] for _ in prev]
        for n in range(max(ch["nb"] for ch in cur + prev)):
            for ch in cur:
                if n < ch["nb"]:
                    score_block(ch, n)
            for ch, pr in zip(prev, probs):
                if n < ch["nb"]:
                    pr.append(prob_block(ch, n))
        for ch, pr in zip(prev, probs):
            accs = pv(ch["pair"], slice(0, ch["nb"] * Q_TILE), jnp.concatenate(pr, axis=0))
            store_out(ch["pair"], ch["nb"] - 1, accs)
        prev = cur


def _attention(kind, ins, in_specs, n_pairs, extra_scratch=()):
    width = PAIR_GROUP * LANES
    return pl.pallas_call(
        functools.partial(_attn_kernel, kind=kind),
        grid=(BATCH, n_pairs // PAIR_GROUP),
        in_specs=in_specs,
        out_specs=pl.BlockSpec((SEQ, width), lambda b, g: (b, g)),
        out_shape=jax.ShapeDtypeStruct((TOKENS, n_pairs * LANES), BF16),
        scratch_shapes=list(extra_scratch),
        compiler_params=pltpu.CompilerParams(
            dimension_semantics=("arbitrary", "arbitrary"), vmem_limit_bytes=VMEM_LIMIT),
        name="attn_" + kind,
    )(*ins)


def _attn_layer0_kernel(qa_ref, ka_ref, vta_ref, pen_ref, bias_ref, qm_ref, km_ref, vtm_ref,
                        oa_ref, ob_ref):
    _attn_kernel(qa_ref, ka_ref, vta_ref, pen_ref, bias_ref, oa_ref, kind="moba")
    _attn_kernel(qm_ref, km_ref, vtm_ref, ob_ref, kind="mla")


def _attention_layer0(ins, in_specs, n_pairs):
    width = PAIR_GROUP * LANES
    out = jax.ShapeDtypeStruct((TOKENS, n_pairs * LANES), BF16)
    out_spec = pl.BlockSpec((SEQ, width), lambda b, g: (b, g))
    return pl.pallas_call(
        _attn_layer0_kernel,
        grid=(BATCH, n_pairs // PAIR_GROUP),
        in_specs=in_specs,
        out_specs=(out_spec, out_spec),
        out_shape=(out, out),
        compiler_params=pltpu.CompilerParams(
            dimension_semantics=("arbitrary", "arbitrary"), vmem_limit_bytes=VMEM_LIMIT),
        name="attn_moba_mla",
    )(*ins)


def _seq_spec(pair_width, col0):
    return pl.BlockSpec((SEQ, PAIR_GROUP * pair_width), lambda b, g: (b, col0 + g))


def _vt_spec():
    return pl.BlockSpec((PAIR_GROUP * LANES, SEQ), lambda b, g: (g, b))


def _place_heads(w, n_heads, src_stride, src_off, width):
    per_head = w.reshape(w.shape[0], n_heads, src_stride)[:, :, src_off:src_off + width]
    per_head = jnp.pad(per_head, ((0, 0), (0, 0), (0, LANES - width)))
    return per_head.reshape(w.shape[0], n_heads * LANES)


def _row(v, width=None):
    v = v.reshape(1, -1).astype(F32)
    if width is not None and v.shape[1] < width:
        v = jnp.pad(v, ((0, 0), (0, width - v.shape[1])))
    return v


def kernel(x, p, t5_bias, ff1_norm, ff1_w_in, ff1_w_out, mix_norm, ff2_norm, ff2_w_in, ff2_w_out,
           ple_norm, ple_w_gate, ple_w_proj, ab_w_in, mla_q_norm, mla_w_uq, mla_kv_norm, mla_w_ukv,
           ab_w_out, fox_w_in, fox_b_f, fox_w_out, final_norm):
    xt = x.reshape(TOKENS, D_MODEL)

    ffw = {1: (ff1_norm, ff1_w_in, ff1_w_out), 2: (ff2_norm, ff2_w_in, ff2_w_out)}
    hbm = pl.BlockSpec(memory_space=pl.ANY)

    def ffn_args(which, layer):
        norm, w_in, w_out = ffw[which]
        return [_row(norm[layer]), w_in, w_out], [_wspec((1, D_MODEL)), hbm, hbm]

    ple_gate_bf = ple_w_gate.astype(BF16)
    ple_proj_bf = ple_w_proj.astype(BF16)

    def post(xin, mixes, w_outs, layer, final):
        fa, fs = ffn_args(2, layer)
        ws = [w.astype(BF16) for w in w_outs]
        ins = ([xin] + list(mixes) + ws + fa
               + [_row(ple_norm[layer]), ple_gate_bf, p.reshape(DEPTH, TOKENS, PLE_DIM), ple_proj_bf,
                  _row(final_norm)])
        specs = ([_tok_spec(D_MODEL)] + [_tok_spec(m.shape[1]) for m in mixes]
                 + [_wspec(w.shape) for w in ws] + fs
                 + [_wspec((1, D_MODEL)), _wspec((D_MODEL, D_MODEL), layer),
                    pl.BlockSpec((None, TOKEN_TILE, PLE_DIM), lambda t: (layer, t, 0)),
                    _wspec((PLE_DIM, D_MODEL), layer), _wspec((1, D_MODEL))])
        return _token_call(
            "post%d" % layer,
            functools.partial(_post_kernel, n_mix=len(mixes), final=final, layer=layer), ins, specs,
            jax.ShapeDtypeStruct((TOKENS, D_MODEL), F32), _tok_spec(D_MODEL))

    w_ab = ab_w_in[0]
    w_qk = w_ab[:, :2 * A_WIDTH].astype(BF16)
    w_vt = w_ab[:, 2 * A_WIDTH:3 * A_WIDTH].T.astype(BF16)
    c0 = 3 * A_WIDTH
    kr0 = MLA_Q_RANK + MLA_KV_RANK
    w_c = jnp.concatenate(
        [w_ab[:, c0:c0 + kr0], jnp.zeros((D_MODEL, MLA_NOPE), F32), w_ab[:, c0 + kr0:],
         jnp.zeros((D_MODEL, LANES - MLA_NOPE - MLA_ROPE), F32)], axis=1).astype(BF16)
    w_uq = _place_heads(mla_w_uq[0], B_HEADS, MLA_NOPE + MLA_ROPE, 0, MLA_NOPE + MLA_ROPE).astype(BF16)
    per_head = mla_w_uq[0].reshape(MLA_Q_RANK, B_HEADS, MLA_NOPE + MLA_ROPE)
    x1_cols = per_head[:, :, MLA_NOPE:MLA_NOPE + MLA_ROPE // 2]
    x2_cols = per_head[:, :, MLA_NOPE + MLA_ROPE // 2:]
    partner_cols = jnp.concatenate([jnp.zeros_like(per_head[:, :, :MLA_NOPE]), x2_cols, x1_cols], axis=2)
    w_uq_rot = _place_heads(partner_cols.reshape(MLA_Q_RANK, -1), B_HEADS, MLA_NOPE + MLA_ROPE, 0,
                            MLA_NOPE + MLA_ROPE).astype(BF16)
    w_ukv_k = _place_heads(mla_w_ukv[0], B_HEADS, MLA_NOPE + MLA_V, 0, MLA_NOPE).astype(BF16)
    w_ukv = mla_w_ukv[0].reshape(MLA_KV_RANK, B_HEADS, MLA_NOPE + MLA_V)
    w_ukv_vt = w_ukv[:, :, MLA_NOPE:].reshape(MLA_KV_RANK, B_HEADS * MLA_V).T.astype(BF16)
    half = MLA_ROPE // 2
    inv = ROPE_THETA ** (-np.arange(half, dtype=np.float64) / half)
    inv_lane = np.zeros((1, LANES), np.float32)
    inv_lane[0, MLA_NOPE:MLA_NOPE + half] = inv
    inv_lane[0, MLA_NOPE + half:MLA_NOPE + MLA_ROPE] = inv
    rope_tbl = pl.pallas_call(
        _rope_table_kernel,
        grid=(SEQ // TOKEN_TILE,),
        in_specs=[pl.BlockSpec((1, LANES), lambda i: (0, 0))],
        out_specs=pl.BlockSpec((4, TOKEN_TILE, LANES), lambda i: (0, i, 0)),
        out_shape=jax.ShapeDtypeStruct((4, SEQ, LANES), F32),
        name="rope_tables",
    )(jnp.asarray(inv_lane))

    fa, fs = ffn_args(1, 0)
    ins = ([xt] + fa + [_row(mix_norm[0]), w_qk, w_vt, w_c, _row(mla_q_norm[0]), w_uq, w_uq_rot,
                        _row(mla_kv_norm[0]), w_ukv_k, w_ukv_vt, rope_tbl])
    specs = ([_tok_spec(D_MODEL)] + fs
             + [_wspec((1, D_MODEL)), _wspec(w_qk.shape), _wspec(w_vt.shape), _wspec(w_c.shape),
                _wspec((1, MLA_Q_RANK)), _wspec(w_uq.shape), _wspec(w_uq_rot.shape),
                _wspec((1, MLA_KV_RANK)),
                _wspec(w_ukv_k.shape), _wspec(w_ukv_vt.shape),
                pl.BlockSpec((4, TOKEN_TILE, LANES), lambda t: (0, t % (SEQ // TOKEN_TILE), 0))])
    pen_rows = A_HEADS // 2 * N_KBLK
    outs = (jax.ShapeDtypeStruct((TOKENS, D_MODEL), F32),
            jax.ShapeDtypeStruct((TOKENS, 2 * A_WIDTH), BF16),
            jax.ShapeDtypeStruct((A_WIDTH, TOKENS), BF16),
            jax.ShapeDtypeStruct((BATCH, N_KBLK, pen_rows, 2 * Q_TILE), F32),
            jax.ShapeDtypeStruct((TOKENS, B_HEADS * LANES), BF16),
            jax.ShapeDtypeStruct((TOKENS, B_HEADS * LANES), BF16),
            jax.ShapeDtypeStruct((B_HEADS * MLA_V, TOKENS), BF16))
    out_specs = (_tok_spec(D_MODEL), _tok_spec(2 * A_WIDTH), _tok_t_spec(A_WIDTH),
                 pl.BlockSpec((1, TILE_R, pen_rows, 2 * Q_TILE),
                              lambda t: (t // (N_KBLK // TILE_R), t % (N_KBLK // TILE_R), 0, 0)),
                 _tok_spec(B_HEADS * LANES), _tok_spec(B_HEADS * LANES),
                 _tok_t_spec(B_HEADS * MLA_V))
    x1, qk_a, vt_a, pen, q_mla, k_mla, vt_mla = _token_call(
        "pre0", functools.partial(_pre0_kernel, layer=0), ins, specs, outs, out_specs,
        scratch=[pltpu.VMEM((N_KBLK, A_WIDTH), F32)])

    bias = pl.pallas_call(
        _t5_bias_kernel,
        grid=(A_HEADS // 2,),
        in_specs=[pl.BlockSpec(memory_space=pltpu.SMEM)],
        out_specs=pl.BlockSpec((1, 3, Q_TILE, 2 * Q_TILE), lambda h: (h, 0, 0, 0)),
        out_shape=jax.ShapeDtypeStruct((A_HEADS // 2, 3, Q_TILE, 2 * Q_TILE), F32),
        name="t5_bias_tiles",
    )(t5_bias.astype(F32))

    na = A_HEADS // 2
    assert A_HEADS == B_HEADS
    o_a, o_b = _attention_layer0(
        [qk_a, qk_a, vt_a, pen, bias, q_mla, k_mla, vt_mla],
        [_seq_spec(LANES, 0), _seq_spec(LANES, na // PAIR_GROUP), _vt_spec(),
         pl.BlockSpec((1, N_KBLK, PAIR_GROUP * N_KBLK, 2 * Q_TILE), lambda b, g: (b, 0, g, 0)),
         pl.BlockSpec((PAIR_GROUP, 3, Q_TILE, 2 * Q_TILE), lambda b, g: (g, 0, 0, 0)),
         _seq_spec(2 * LANES, 0), _seq_spec(2 * LANES, 0), _vt_spec()], na)
    w_o = ab_w_out[0]
    xt = post(x1, [o_a, o_b], [w_o[:A_WIDTH], w_o[A_WIDTH:]], 0, DEPTH == 1)

    w_fox = fox_w_in[0]
    w_qk = w_fox[:, :2 * C_WIDTH].astype(BF16)
    w_vt = w_fox[:, 2 * C_WIDTH:3 * C_WIDTH].T.astype(BF16)
    w_f = jnp.pad(w_fox[:, 3 * C_WIDTH:], ((0, 0), (0, LANES - C_HEADS))).astype(BF16)
    fa, fs = ffn_args(1, 1)
    ins = [xt] + fa + [_row(mix_norm[1]), w_qk, w_vt, w_f, _row(fox_b_f[0], LANES)]
    specs = ([_tok_spec(D_MODEL)] + fs
             + [_wspec((1, D_MODEL)), _wspec(w_qk.shape), _wspec(w_vt.shape), _wspec(w_f.shape),
                _wspec((1, LANES))])
    outs = (jax.ShapeDtypeStruct((TOKENS, D_MODEL), F32),
            jax.ShapeDtypeStruct((TOKENS, 2 * C_WIDTH), BF16),
            jax.ShapeDtypeStruct((C_WIDTH, TOKENS), BF16),
            jax.ShapeDtypeStruct((TOKENS, LANES), BF16))
    out_specs = (_tok_spec(D_MODEL), _tok_spec(2 * C_WIDTH), _tok_t_spec(C_WIDTH), _tok_spec(LANES))
    x1, qk_c, vt_c, caug = _token_call("pre1", functools.partial(_pre1_kernel, layer=1), ins, specs, outs, out_specs,
                                       scratch=[pltpu.VMEM((1, LANES), F32)])

    nc = C_HEADS // 2
    o_c = _attention(
        "fox", [qk_c, qk_c, vt_c, caug],
        [_seq_spec(LANES, 0), _seq_spec(LANES, nc // PAIR_GROUP), _vt_spec(),
         pl.BlockSpec((SEQ, LANES), lambda b, g: (b, 0))],
        nc, extra_scratch=[pltpu.VMEM((SEQ, PAIR_GROUP * 2 * LANES), BF16)])
    xt = post(x1, [o_c], [fox_w_out[0]], 1, True)
    return xt.reshape(BATCH, SEQ, D_MODEL)
```
